```python
import jax, jax.numpy as jnp
from jax import lax
import numpy as np

D_MODEL = 2048
BATCH = 8
SEQ = 2048
DEPTH = 2

GRID_W = 64
CTX_LEN = 256
N_MIXERS = 2
N_RWKV = (DEPTH + 1) // 2
N_CONV = DEPTH // 2
HEAD_SIZE = 64
N_HEADS = D_MODEL // HEAD_SIZE
LORA_DECAY = max(32, int(round(1.8 * D_MODEL ** 0.5 / 32)) * 32)
LORA_A = max(32, int(round(1.8 * D_MODEL ** 0.5 / 32)) * 32)
LORA_GATE = max(32, int(round(0.6 * D_MODEL ** 0.8 / 32)) * 32)
CONV_WIDTH = 3
D_FF = ((8 * D_MODEL // 3 + 255) // 256) * 256
NORM_EPS = 1e-6
GN_EPS = 64e-5

kernel_name = "rwkv7_shortconv_hybrid_dit"


def rmsnorm(h, g):
    hf = h.astype(jnp.float32)
    hn = hf * lax.rsqrt(jnp.mean(hf * hf, axis=-1, keepdims=True) + NORM_EPS)
    return hn.astype(h.dtype) * g


def modulate(h, shift, scale):
    return h * (1 + scale) + shift


def split_heads(t):
    return t.reshape(t.shape[0], t.shape[1], N_HEADS, HEAD_SIZE)


def grid_shift(h):
    Bn, T, D = h.shape
    rows = T // GRID_W
    g = h.reshape(Bn, rows, GRID_W, D)
    q = D // 4
    left = jnp.pad(g[:, :, :-1, :q], ((0, 0), (0, 0), (1, 0), (0, 0)))
    right = jnp.pad(g[:, :, 1:, q:2 * q], ((0, 0), (0, 0), (0, 1), (0, 0)))
    up = jnp.pad(g[:, :-1, :, 2 * q:3 * q], ((0, 0), (1, 0), (0, 0), (0, 0)))
    down = jnp.pad(g[:, 1:, :, 3 * q:], ((0, 0), (0, 1), (0, 0), (0, 0)))
    return jnp.concatenate([left, right, up, down], axis=-1).reshape(Bn, T, D)


def seq_shift(h):
    half = h.shape[-1] // 2
    prev = jnp.pad(h[:, :-1, :half], ((0, 0), (1, 0), (0, 0)))
    nxt = jnp.pad(h[:, 1:, half:], ((0, 0), (0, 1), (0, 0)))
    return jnp.concatenate([prev, nxt], axis=-1)


def wkv_scan(state0, r, decay, k, v, a, b, reverse):
    def step(S, inp):
        r_t, w_t, k_t, v_t, a_t, b_t = inp
        sa = jnp.einsum('bhvk,bhk->bhv', S, a_t)
        S = (S * w_t[:, :, None, :] + sa[..., None] * b_t[:, :, None, :]
             + v_t[..., None] * k_t[:, :, None, :])
        return S, jnp.einsum('bhvk,bhk->bhv', S, r_t)
    xs = tuple(jnp.swapaxes(t.astype(jnp.float32), 0, 1) for t in (r, decay, k, v, a, b))
    state, ys = lax.scan(step, state0, xs, reverse=reverse)
    return state, jnp.swapaxes(ys, 0, 1)


def rwkv_shared(h, h_shift, mix, wr, wk, wv, g1, g2, k_k):
    xx = h_shift - h
    xr, xw, xk, xv, xa, xg = [h + xx * mix[m] for m in range(6)]
    r = split_heads(xr @ wr)
    k = xk @ wk
    v = split_heads(xv @ wv)
    g = jax.nn.sigmoid(xg @ g1) @ g2
    kk = split_heads((k * k_k).astype(jnp.float32))
    kk = kk / jnp.maximum(jnp.sqrt(jnp.sum(kk * kk, axis=-1, keepdims=True)), 1e-12)
    return r, k, v, kk, g, xw, xa


def rwkv_direction(xw, xa, k, kk, w0, w1, w2, a0, a1, a2, k_a):
    log_w = -jax.nn.softplus(-(w0 + jnp.tanh(xw @ w1) @ w2).astype(jnp.float32)) - 0.5
    decay = jnp.exp(-jnp.exp(log_w))
    a = jax.nn.sigmoid(a0 + (xa @ a1) @ a2)
    k_d = k * (1 + (a - 1) * k_a)
    a_h = split_heads(a).astype(jnp.float32)
    return split_heads(decay), split_heads(k_d), -kk, kk * a_h


def rwkv_readout(y, r, ksum, v, g, r_k, ln_w, ln_b, wo, dtype):
    Bn, T = y.shape[0], y.shape[1]
    mu = jnp.mean(y, axis=-1, keepdims=True)
    var = jnp.mean(jnp.square(y - mu), axis=-1, keepdims=True)
    o = ((y - mu) * lax.rsqrt(var + GN_EPS)).reshape(Bn, T, D_MODEL) * ln_w + ln_b
    bonus = jnp.sum(r * ksum * r_k, axis=-1, keepdims=True) * v
    o = o + bonus.reshape(Bn, T, D_MODEL)
    return (o.astype(dtype) * g) @ wo


def rwkv7_mix(hx, hc, need_ctx, mix, wr, wk, wv, wo, w0, w1, w2, a0, a1, a2,
              g1, g2, k_k, k_a, r_k, ln_w, ln_b):
    r_c, k_c, v_c, kk_c, g_c, xw_c, xa_c = rwkv_shared(hc, seq_shift(hc), mix, wr, wk, wv, g1, g2, k_k)
    r_x, k_x, v_x, kk_x, g_x, xw_x, xa_x = rwkv_shared(hx, grid_shift(hx), mix, wr, wk, wv, g1, g2, k_k)
    state0 = jnp.zeros((hx.shape[0], N_HEADS, HEAD_SIZE, HEAD_SIZE), jnp.float32)
    y_c = 0.0
    y_x = 0.0
    ksum_c = 0.0
    ksum_x = 0.0
    for d, rev in enumerate((False, True)):
        dec_c, kd_c, aa_c, bb_c = rwkv_direction(xw_c, xa_c, k_c, kk_c, w0[d], w1[d], w2[d], a0[d], a1[d], a2[d], k_a)
        state_c, yd_c = wkv_scan(state0, r_c, dec_c, kd_c, v_c, aa_c, bb_c, rev)
        dec_x, kd_x, aa_x, bb_x = rwkv_direction(xw_x, xa_x, k_x, kk_x, w0[d], w1[d], w2[d], a0[d], a1[d], a2[d], k_a)
        _, yd_x = wkv_scan(state_c, r_x, dec_x, kd_x, v_x, aa_x, bb_x, rev)
        y_x = y_x + yd_x
        ksum_x = ksum_x + kd_x
        if need_ctx:
            y_c = y_c + yd_c
            ksum_c = ksum_c + kd_c
    out_x = rwkv_readout(y_x, r_x, ksum_x, v_x, g_x, r_k, ln_w, ln_b, wo, hx.dtype)
    out_c = rwkv_readout(y_c, r_c, ksum_c, v_c, g_c, r_k, ln_w, ln_b, wo, hc.dtype) if need_ctx else None
    return out_x, out_c


def short_conv(h, w_in, conv_w, w_out):
    gb, gc, u = jnp.split(h @ w_in, 3, axis=-1)
    z = jnp.pad(gc * u, ((0, 0), (1, 1), (0, 0)))
    conv = z[:, :-2] * conv_w[0] + z[:, 1:-1] * conv_w[1] + z[:, 2:] * conv_w[2]
    return (gb * conv) @ w_out


def swiglu(h, w13, w2):
    a, b = jnp.split(h @ w13, 2, axis=-1)
    return (jax.nn.silu(a) * b) @ w2


def setup_inputs(seed: int = 0) -> dict:
    key = jax.random.key(seed)
    ks = iter(jax.random.split(key, 48))

    def nrm(shape, scale):
        return jax.random.normal(next(ks), shape, jnp.float32) * scale

    def unif(shape, lo, hi):
        return jax.random.uniform(next(ks), shape, jnp.float32, lo, hi)

    D, F = D_MODEL, D_FF
    inv = D ** -0.5
    return {
        "x": nrm((BATCH, SEQ, D), 1.0),
        "c": nrm((BATCH, D), 1.0),
        "ctx": nrm((BATCH, CTX_LEN, D), 1.0),
        "c_ctx": nrm((D,), 1.0),
        "norm1_g": 1.0 + nrm((DEPTH, D), 0.02),
        "norm2_g": 1.0 + nrm((DEPTH, D), 0.02),
        "ada_w": nrm((DEPTH, D, 6 * D), 0.5 * inv),
        "ada_b": nrm((DEPTH, 6 * D), 0.02),
        "rw_mix": unif((N_RWKV, 6, D), 0.0, 1.0),
        "rw_wr": nrm((N_RWKV, D, D), inv),
        "rw_wk": nrm((N_RWKV, D, D), inv),
        "rw_wv": nrm((N_RWKV, D, D), inv),
        "rw_wo": nrm((N_RWKV, D, D), inv),
        "rw_w0": unif((N_RWKV, 2, D), -6.5, -1.5),
        "rw_w1": nrm((N_RWKV, 2, D, LORA_DECAY), inv),
        "rw_w2": nrm((N_RWKV, 2, LORA_DECAY, D), 0.5 * LORA_DECAY ** -0.5),
        "rw_a0": nrm((N_RWKV, 2, D), 0.1),
        "rw_a1": nrm((N_RWKV, 2, D, LORA_A), inv),
        "rw_a2": nrm((N_RWKV, 2, LORA_A, D), 0.5 * LORA_A ** -0.5),
        "rw_g1": nrm((N_RWKV, D, LORA_GATE), inv),
        "rw_g2": nrm((N_RWKV, LORA_GATE, D), LORA_GATE ** -0.5),
        "rw_kk": 0.85 + nrm((N_RWKV, D), 0.05),
        "rw_ka": 1.0 + nrm((N_RWKV, D), 0.05),
        "rw_rk": nrm((N_RWKV, N_HEADS, HEAD_SIZE), 0.1),
        "rw_lnw": 1.0 + nrm((N_RWKV, D), 0.02),
        "rw_lnb": nrm((N_RWKV, D), 0.02),
        "sc_win": nrm((N_CONV, D, 3 * D), inv),
        "sc_conv": nrm((N_CONV, CONV_WIDTH, D), CONV_WIDTH ** -0.5),
        "sc_wout": nrm((N_CONV, D, D), inv),
        "ffn_w13": nrm((DEPTH, D, 2 * F), inv),
        "ffn_w2": nrm((DEPTH, F, D), F ** -0.5),
        "final_g": 1.0 + nrm((D,), 0.02),
    }


def reference(x, c, ctx, c_ctx, norm1_g, norm2_g, ada_w, ada_b,
              rw_mix, rw_wr, rw_wk, rw_wv, rw_wo, rw_w0, rw_w1, rw_w2,
              rw_a0, rw_a1, rw_a2, rw_g1, rw_g2, rw_kk, rw_ka, rw_rk, rw_lnw, rw_lnb,
              sc_win, sc_conv, sc_wout, ffn_w13, ffn_w2, final_g):
    cond_x = jax.nn.silu(c)
    cond_c = jax.nn.silu(c_ctx)
    for i in range(DEPTH):
        last = i == DEPTH - 1
        is_rwkv = i % N_MIXERS == 0
        j = i // N_MIXERS
        mod_x = (cond_x @ ada_w[i] + ada_b[i])[:, None, :]
        sh1_x, sc1_x, gt1_x, sh2_x, sc2_x, gt2_x = jnp.split(mod_x, 6, axis=-1)
        hx = modulate(rmsnorm(x, norm1_g[i]), sh1_x, sc1_x)
        ctx_used = is_rwkv or not last
        if ctx_used:
            mod_c = (cond_c @ ada_w[i] + ada_b[i])[None, None, :]
            sh1_c, sc1_c, gt1_c, sh2_c, sc2_c, gt2_c = jnp.split(mod_c, 6, axis=-1)
            hc = modulate(rmsnorm(ctx, norm1_g[i]), sh1_c, sc1_c)
        if is_rwkv:
            yx, yc = rwkv7_mix(hx, hc, not last, rw_mix[j], rw_wr[j], rw_wk[j], rw_wv[j], rw_wo[j],
                               rw_w0[j], rw_w1[j], rw_w2[j], rw_a0[j], rw_a1[j], rw_a2[j],
                               rw_g1[j], rw_g2[j], rw_kk[j], rw_ka[j], rw_rk[j], rw_lnw[j], rw_lnb[j])
        else:
            yx = short_conv(hx, sc_win[j], sc_conv[j], sc_wout[j])
            yc = short_conv(hc, sc_win[j], sc_conv[j], sc_wout[j]) if not last else None
        x = x + gt1_x * yx
        x = x + gt2_x * swiglu(modulate(rmsnorm(x, norm2_g[i]), sh2_x, sc2_x), ffn_w13[i], ffn_w2[i])
        if not last:
            ctx = ctx + gt1_c * yc
            ctx = ctx + gt2_c * swiglu(modulate(rmsnorm(ctx, norm2_g[i]), sh2_c, sc2_c), ffn_w13[i], ffn_w2[i])
    return rmsnorm(x, final_g)
```

```python
import functools

import jax
import jax.numpy as jnp
from jax import lax
from jax.experimental import pallas as pl
from jax.experimental.pallas import tpu as pltpu

HEAD = 64
GRID_W = 64
CHUNK = 64
NORM_EPS = 1e-6
GN_EPS = 64e-5
LANES = 128
VMEM_LIMIT = 56 * 1024 * 1024

F32 = jnp.float32
BF16 = jnp.bfloat16
HI = lax.Precision.HIGHEST


def _params(*sem):
    return pltpu.CompilerParams(dimension_semantics=sem, vmem_limit_bytes=VMEM_LIMIT)


def _tile(n, pref, mult):
    t = min(pref, n)
    t -= t % mult
    while t >= mult:
        if n % t == 0:
            return t
        t -= mult
    return n


def _sigmoid(x):
    return 1.0 / (1.0 + jnp.exp(-x))


def _norm_mod(x, g, shift, scale):
    hn = x * lax.rsqrt(jnp.mean(x * x, axis=-1, keepdims=True) + NORM_EPS)
    return (hn * g) * (1.0 + scale) + shift


def _dot(a, b, precision=None):
    return jnp.dot(a, b, preferred_element_type=F32, precision=precision)


def _dot_t(a, b, ca, cb, precision=None):
    return lax.dot_general(a, b, (((ca,), (cb,)), ((), ())), preferred_element_type=F32,
                           precision=precision)


def _ada_kernel(c_ref, w_ref, b_ref, o_ref):
    c = c_ref[...]
    s = c * _sigmoid(c)
    o_ref[...] = _dot(s.astype(BF16), w_ref[...].astype(BF16)) + b_ref[...]


def _ada(cond, w, b):
    R, D = cond.shape
    N = w.shape[1]
    tn = _tile(N, 1024, LANES)
    return pl.pallas_call(
        _ada_kernel,
        grid=(N // tn,),
        in_specs=[pl.BlockSpec((R, D), lambda j: (0, 0)),
                  pl.BlockSpec((D, tn), lambda j: (0, j)),
                  pl.BlockSpec((1, tn), lambda j: (0, j))],
        out_specs=pl.BlockSpec((R, tn), lambda j: (0, j)),
        out_shape=jax.ShapeDtypeStruct((R, N), F32),
        compiler_params=_params("parallel"),
        name="ada_mod",
    )(cond, w, b[None])


def _norm_kernel(x_ref, g_ref, sh_ref, sc_ref, o_ref):
    o_ref[0] = _norm_mod(x_ref[0], g_ref[...], sh_ref[0], sc_ref[0])


def _bsel(arr):
    if arr.shape[0] == 1:
        return lambda b, *_: (0, 0, 0)
    return lambda b, *_: (b, 0, 0)


def _norm(x, g, shift, scale):
    Bn, T, D = x.shape
    tm = _tile(T, 512, 8)
    return pl.pallas_call(
        _norm_kernel,
        grid=(Bn, T // tm),
        in_specs=[pl.BlockSpec((1, tm, D), lambda b, i: (b, i, 0)),
                  pl.BlockSpec((1, D), lambda b, i: (0, 0)),
                  pl.BlockSpec((1, 1, D), _bsel(shift)),
                  pl.BlockSpec((1, 1, D), _bsel(scale))],
        out_specs=pl.BlockSpec((1, tm, D), lambda b, i: (b, i, 0)),
        out_shape=jax.ShapeDtypeStruct((Bn, T, D), F32),
        compiler_params=_params("parallel", "parallel"),
        name="norm",
    )(x, g[None], shift, scale)


def _write_mix(out_refs, mix_ref, h, shifted, c0, c1):
    xx = shifted - h
    for m, o_ref in enumerate(out_refs):
        o_ref[0, :, c0:c1] = (h + xx * mix_ref[m:m + 1, c0:c1]).astype(o_ref.dtype)


def _prep_latent_kernel(x_ref, xu_ref, xd_ref, g_ref, sh_ref, sc_ref, mix_ref, *out_refs):
    i = pl.program_id(1)
    n = pl.num_programs(1)
    g, sh, sc = g_ref[...], sh_ref[0], sc_ref[0]
    h = _norm_mod(x_ref[0], g, sh, sc)
    tm, D = h.shape
    q = D // 4
    hu = _norm_mod(xu_ref[0], g, sh, sc)[:, 2 * q:3 * q]
    hd = _norm_mod(xd_ref[0], g, sh, sc)[:, 3 * q:]
    hu = jnp.where(i > 0, hu, 0.0)
    hd = jnp.where(i < n - 1, hd, 0.0)
    col = lax.broadcasted_iota(jnp.int32, (tm, 1), 0) & (GRID_W - 1)
    h0, h1, h2, h3 = h[:, :q], h[:, q:2 * q], h[:, 2 * q:3 * q], h[:, 3 * q:]
    left = jnp.where(col == 0, 0.0, pltpu.roll(h0, 1, 0))
    right = jnp.where(col == GRID_W - 1, 0.0, pltpu.roll(h1, tm - 1, 0))
    if tm > GRID_W:
        up = jnp.concatenate([hu, h2[:tm - GRID_W]], axis=0)
        down = jnp.concatenate([h3[GRID_W:], hd], axis=0)
    else:
        up, down = hu, hd
    _write_mix(out_refs, mix_ref, h0, left, 0, q)
    _write_mix(out_refs, mix_ref, h1, right, q, 2 * q)
    _write_mix(out_refs, mix_ref, h2, up, 2 * q, 3 * q)
    _write_mix(out_refs, mix_ref, h3, down, 3 * q, D)


def _prep_latent(x, g, shift, scale, mix):
    Bn, T, D = x.shape
    rows_per_tile = _tile(T // GRID_W, 4, 1)
    tm = rows_per_tile * GRID_W
    nrow = T // GRID_W
    return pl.pallas_call(
        _prep_latent_kernel,
        grid=(Bn, T // tm),
        in_specs=[pl.BlockSpec((1, tm, D), lambda b, i: (b, i, 0)),
                  pl.BlockSpec((1, GRID_W, D),
                               lambda b, i: (b, jnp.maximum(i * rows_per_tile - 1, 0), 0)),
                  pl.BlockSpec((1, GRID_W, D),
                               lambda b, i: (b, jnp.minimum((i + 1) * rows_per_tile, nrow - 1), 0)),
                  pl.BlockSpec((1, D), lambda b, i: (0, 0)),
                  pl.BlockSpec((1, 1, D), _bsel(shift)),
                  pl.BlockSpec((1, 1, D), _bsel(scale)),
                  pl.BlockSpec((6, D), lambda b, i: (0, 0))],
        out_specs=[pl.BlockSpec((1, tm, D), lambda b, i: (b, i, 0))] * 6,
        out_shape=[jax.ShapeDtypeStruct((Bn, T, D), BF16)] * 6,
        compiler_params=_params("parallel", "parallel"),
        name="prep_latent",
    )(x, x, x, g[None], shift, scale, mix)


def _prep_ctx_kernel(x_ref, g_ref, sh_ref, sc_ref, mix_ref, *out_refs):
    h = _norm_mod(x_ref[0], g_ref[...], sh_ref[0], sc_ref[0])
    L, D = h.shape
    half = D // 2
    t = lax.broadcasted_iota(jnp.int32, (L, 1), 0)
    h0, h1 = h[:, :half], h[:, half:]
    prev = jnp.where(t == 0, 0.0, pltpu.roll(h0, 1, 0))
    nxt = jnp.where(t == L - 1, 0.0, pltpu.roll(h1, L - 1, 0))
    _write_mix(out_refs, mix_ref, h0, prev, 0, half)
    _write_mix(out_refs, mix_ref, h1, nxt, half, D)


def _prep_ctx(x, g, shift, scale, mix):
    Bn, L, D = x.shape
    return pl.pallas_call(
        _prep_ctx_kernel,
        grid=(Bn,),
        in_specs=[pl.BlockSpec((1, L, D), lambda b: (b, 0, 0)),
                  pl.BlockSpec((1, D), lambda b: (0, 0)),
                  pl.BlockSpec((1, 1, D), _bsel(shift)),
                  pl.BlockSpec((1, 1, D), _bsel(scale)),
                  pl.BlockSpec((6, D), lambda b: (0, 0))],
        out_specs=[pl.BlockSpec((1, L, D), lambda b: (b, 0, 0))] * 6,
        out_shape=[jax.ShapeDtypeStruct((Bn, L, D), BF16)] * 6,
        compiler_params=_params("parallel"),
        name="prep_ctx",
    )(x, g[None], shift, scale, mix)


def _mm_kernel(a_ref, w_ref, o_ref, *, act):
    acc = _dot(a_ref[0], w_ref[...])
    if act == "tanh":
        acc = jnp.tanh(acc)
    elif act == "sigmoid":
        acc = _sigmoid(acc)
    o_ref[0] = acc.astype(o_ref.dtype)


def _mm(a, w, out_dtype, act=None, name="mm"):
    Bn, T, K = a.shape
    N = w.shape[1]
    tm = _tile(T, 1024, 16)
    tn = _tile(N, 1024, LANES)
    return pl.pallas_call(
        functools.partial(_mm_kernel, act=act),
        grid=(Bn, T // tm, N // tn),
        in_specs=[pl.BlockSpec((1, tm, K), lambda b, i, j: (b, i, 0)),
                  pl.BlockSpec((K, tn), lambda b, i, j: (0, j))],
        out_specs=pl.BlockSpec((1, tm, tn), lambda b, i, j: (b, i, j)),
        out_shape=jax.ShapeDtypeStruct((Bn, T, N), out_dtype),
        compiler_params=_params("parallel", "parallel", "parallel"),
        name=name,
    )(a, w)


def _mm_res_kernel(a_ref, w_ref, res_ref, gate_ref, o_ref):
    o_ref[0] = res_ref[0] + gate_ref[0] * _dot(a_ref[0], w_ref[...])


def _mm_res(a, w, res, gate, name="mm_res"):
    Bn, T, K = a.shape
    N = w.shape[1]
    tm = _tile(T, 1024, 16)
    tn = _tile(N, 1024 if K <= 2048 else 512, LANES)
    gsel = _bsel(gate)
    return pl.pallas_call(
        _mm_res_kernel,
        grid=(Bn, T // tm, N // tn),
        in_specs=[pl.BlockSpec((1, tm, K), lambda b, i, j: (b, i, 0)),
                  pl.BlockSpec((K, tn), lambda b, i, j: (0, j)),
                  pl.BlockSpec((1, tm, tn), lambda b, i, j: (b, i, j)),
                  pl.BlockSpec((1, 1, tn), lambda b, i, j: gsel(b)[:2] + (j,))],
        out_specs=pl.BlockSpec((1, tm, tn), lambda b, i, j: (b, i, j)),
        out_shape=jax.ShapeDtypeStruct((Bn, T, N), F32),
        compiler_params=_params("parallel", "parallel", "parallel"),
        name=name,
    )(a, w, res, gate)


def _swiglu_combine(a, b):
    return (a * _sigmoid(a) * b,)


def _conv_in_combine(gb, gc, u):
    return gb, gc * u


def _norm_mm_kernel(x_ref, g_ref, sh_ref, sc_ref, *rest, nw, combine):
    w_refs, out_refs, h_scr = rest[:nw], rest[nw:-1], rest[-1]

    @pl.when(pl.program_id(2) == 0)
    def _():
        h_scr[...] = _norm_mod(x_ref[0], g_ref[...], sh_ref[0], sc_ref[0]).astype(BF16)

    h = h_scr[...]
    outs = combine(*[_dot(h, w_ref[...]) for w_ref in w_refs])
    for o_ref, o in zip(out_refs, outs):
        o_ref[0] = o.astype(o_ref.dtype)


def _norm_mm(x, g, shift, scale, w, nw, combine, out_dtypes, name):
    Bn, T, D = x.shape
    N = w.shape[1] // nw
    tm = _tile(T, 1024, 16)
    tn = _tile(N, 512, LANES)
    nj = N // tn
    w_specs = [pl.BlockSpec((D, tn), functools.partial(lambda b, i, j, m: (0, j + m * nj), m=m))
               for m in range(nw)]
    return pl.pallas_call(
        functools.partial(_norm_mm_kernel, nw=nw, combine=combine),
        grid=(Bn, T // tm, nj),
        in_specs=[pl.BlockSpec((1, tm, D), lambda b, i, j: (b, i, 0)),
                  pl.BlockSpec((1, D), lambda b, i, j: (0, 0)),
                  pl.BlockSpec((1, 1, D), _bsel(shift)),
                  pl.BlockSpec((1, 1, D), _bsel(scale))] + w_specs,
        out_specs=[pl.BlockSpec((1, tm, tn), lambda b, i, j: (b, i, j))] * len(out_dtypes),
        out_shape=[jax.ShapeDtypeStruct((Bn, T, N), dt) for dt in out_dtypes],
        scratch_shapes=[pltpu.VMEM((tm, D), BF16)],
        compiler_params=_params("parallel", "parallel", "arbitrary"),
        name=name,
    )(x, g[None], shift, scale, *([w] * nw))


def _seg_ones(width):
    shift = HEAD.bit_length() - 1
    r = lax.shift_right_logical(lax.broadcasted_iota(jnp.int32, (width, width), 0), shift)
    c = lax.shift_right_logical(lax.broadcasted_iota(jnp.int32, (width, width), 1), shift)
    return (r == c).astype(F32)


def _wkv_kernel(r_ref, k_ref, v_ref, tw_ref, aw_ref, w2_ref, a2_ref, w0_ref, a0_ref, kk_ref, ka_ref,
                s0_ref, y_ref, sout_ref, s_scr, *, reverse, hb, precision):
    c = pl.program_id(2)
    C = CHUNK

    @pl.when(c == 0)
    def _():
        s_scr[...] = s0_ref[0]

    r, k, v = r_ref[0], k_ref[0], v_ref[0]
    z = w0_ref[...] + _dot(tw_ref[0], w2_ref[...])
    softplus = jnp.maximum(-z, 0.0) + jnp.log(1.0 + jnp.exp(-jnp.abs(z)))
    lw = -jnp.exp(-softplus - 0.5)
    a_sig = _sigmoid(a0_ref[...] + _dot(aw_ref[0], a2_ref[...]))
    kd = k * (1.0 + (a_sig - 1.0) * ka_ref[...])
    kkv = k * kk_ref[...]
    ss = _dot(kkv * kkv, _seg_ones(hb * HEAD), HI)
    kkn = kkv / jnp.maximum(jnp.sqrt(ss), 1e-12)
    aa = -kkn
    bb = kkn * a_sig

    t_i = lax.broadcasted_iota(jnp.int32, (C, C), 0)
    s_i = lax.broadcasted_iota(jnp.int32, (C, C), 1)
    before = (s_i > t_i) if reverse else (s_i < t_i)
    upto = before | (s_i == t_i)
    cum = _dot(upto.astype(F32), lw, HI)
    e_neg = jnp.exp(-cum)
    at = aa * jnp.exp(cum - lw)
    rt = r * jnp.exp(cum)
    bt = bb * e_neg
    kt = kd * e_neg
    last = 0 if reverse else C - 1
    w_tot = jnp.exp(cum[last:last + 1, :])

    t2 = lax.broadcasted_iota(jnp.int32, (C, 2 * C), 0)
    s2 = lax.broadcasted_iota(jnp.int32, (C, 2 * C), 1) & (C - 1)
    before2 = (s2 > t2) if reverse else (s2 < t2)
    upto2 = before2 | (s2 == t2)
    zeros_cv = jnp.zeros((C, HEAD), F32)

    for h in range(hb):
        sl = slice(h * HEAD, (h + 1) * HEAD)
        X = jnp.concatenate([at[:, sl], rt[:, sl]], axis=0)
        Z = jnp.concatenate([bt[:, sl], kt[:, sl]], axis=0)
        vh = v[:, sl]
        S = s_scr[h]
        G = _dot_t(X, Z, 1, 1, precision)
        XS = _dot_t(X, S, 1, 1, precision)
        GA = jnp.where(before2, G[:C], 0.0)
        GR = jnp.where(upto2, G[C:], 0.0)
        u = XS[:C] + _dot(GA, jnp.concatenate([zeros_cv, vh], axis=0), precision)
        P = GA[:, :C]
        n_sq = C.bit_length() - 1
        for j in range(n_sq):
            u = u + _dot(P, u, precision)
            if j < n_sq - 1:
                P = _dot(P, P, precision)
        UV = jnp.concatenate([u, vh], axis=0)
        y_ref[0, :, sl] = XS[C:] + _dot(GR, UV, precision)
        s_scr[h] = (S + _dot_t(UV, Z, 0, 0, precision)) * w_tot[:, sl]

    @pl.when(c == pl.num_programs(2) - 1)
    def _():
        sout_ref[0] = s_scr[...]


def _wkv(r, k, v, tw, aw, w2p, a2p, w0, a0, kk, ka, s0, d, precision=HI):
    Bn, T, D = r.shape
    H = D // HEAD
    hb = _tile(H, 4, 1)
    hw = hb * HEAD
    nc = T // CHUNK
    reverse = d == 1
    cidx = (lambda c: nc - 1 - c) if reverse else (lambda c: c)
    tok = pl.BlockSpec((1, CHUNK, hw), lambda b, g, c: (b, cidx(c), g))
    lora = pl.BlockSpec((1, CHUNK, LANES), lambda b, g, c: (b, cidx(c), d))
    lw2 = pl.BlockSpec((LANES, hw), lambda b, g, c: (0, g))
    vec = pl.BlockSpec((1, hw), lambda b, g, c: (0, g))
    st = pl.BlockSpec((1, hb, HEAD, HEAD), lambda b, g, c: (b, g, 0, 0))
    return pl.pallas_call(
        functools.partial(_wkv_kernel, reverse=reverse, hb=hb, precision=precision),
        grid=(Bn, H // hb, nc),
        in_specs=[tok, tok, tok, lora, lora, lw2, lw2, vec, vec, vec, vec, st],
        out_specs=[tok, st],
        out_shape=[jax.ShapeDtypeStruct((Bn, T, D), F32),
                   jax.ShapeDtypeStruct((Bn, H, HEAD, HEAD), F32)],
        scratch_shapes=[pltpu.VMEM((hb, HEAD, HEAD), F32)],
        compiler_params=_params("parallel", "parallel", "arbitrary"),
        name="wkv_rev" if reverse else "wkv_fwd",
    )(r, k, v, tw, aw, w2p, a2p, w0, a0, kk, ka, s0)


def _readout_kernel(yf_ref, yb_ref, r_ref, k_ref, v_ref, gs_ref, aw_ref, a2f_ref, a2b_ref, g2_ref,
                    a0_ref, ka_ref, rk_ref, lnw_ref, lnb_ref, o_ref):
    tn = o_ref.shape[-1]
    seg = _seg_ones(tn)
    aw = aw_ref[0]
    a_f = _sigmoid(a0_ref[0:1, :] + _dot(aw[:, :LANES], a2f_ref[...]))
    a_b = _sigmoid(a0_ref[1:2, :] + _dot(aw[:, LANES:], a2b_ref[...]))
    k = k_ref[0]
    ka = ka_ref[...]
    ksum = k * (1.0 + (a_f - 1.0) * ka) + k * (1.0 + (a_b - 1.0) * ka)
    y = yf_ref[0] + yb_ref[0]
    mu = _dot(y, seg, HI) * (1.0 / HEAD)
    yc = y - mu
    var = _dot(yc * yc, seg, HI) * (1.0 / HEAD)
    o = yc * lax.rsqrt(var + GN_EPS) * lnw_ref[...] + lnb_ref[...]
    bonus = _dot(r_ref[0] * ksum * rk_ref[...], seg, HI) * v_ref[0]
    g = _dot(gs_ref[0], g2_ref[...])
    o_ref[0] = ((o + bonus) * g).astype(o_ref.dtype)


def _readout(yf, yb, r, k, v, gs, aw, a2fp, a2bp, g2, a0, ka, rk, lnw, lnb):
    Bn, T, D = r.shape
    tm = _tile(T, 512, 16)
    tn = _tile(D, 512, LANES)
    G = gs.shape[-1]
    tok = pl.BlockSpec((1, tm, tn), lambda b, i, j: (b, i, j))
    vec = pl.BlockSpec((1, tn), lambda b, i, j: (0, j))
    return pl.pallas_call(
        _readout_kernel,
        grid=(Bn, T // tm, D // tn),
        in_specs=[tok, tok, tok, tok, tok,
                  pl.BlockSpec((1, tm, G), lambda b, i, j: (b, i, 0)),
                  pl.BlockSpec((1, tm, 2 * LANES), lambda b, i, j: (b, i, 0)),
                  pl.BlockSpec((LANES, tn), lambda b, i, j: (0, j)),
                  pl.BlockSpec((LANES, tn), lambda b, i, j: (0, j)),
                  pl.BlockSpec((G, tn), lambda b, i, j: (0, j)),
                  pl.BlockSpec((2, tn), lambda b, i, j: (0, j)),
                  vec, vec, vec, vec],
        out_specs=tok,
        out_shape=jax.ShapeDtypeStruct((Bn, T, D), BF16),
        compiler_params=_params("parallel", "parallel", "parallel"),
        name="rwkv_readout",
    )(yf, yb, r, k, v, gs, aw, a2fp, a2bp, g2, a0, ka, rk, lnw, lnb)


def _conv_kernel(gb_ref, z_ref, cw_ref, o_ref):
    z = z_ref[0]
    T = z.shape[0]
    t = lax.broadcasted_iota(jnp.int32, (T, 1), 0)
    zp = jnp.where(t == 0, 0.0, pltpu.roll(z, 1, 0))
    zn = jnp.where(t == T - 1, 0.0, pltpu.roll(z, T - 1, 0))
    conv = zp * cw_ref[0:1, :] + z * cw_ref[1:2, :] + zn * cw_ref[2:3, :]
    o_ref[0] = (gb_ref[0] * conv).astype(o_ref.dtype)


def _conv(gb, z, cw):
    Bn, T, D = z.shape
    tn = _tile(D, 256, LANES)
    tok = pl.BlockSpec((1, T, tn), lambda b, j: (b, 0, j))
    return pl.pallas_call(
        _conv_kernel,
        grid=(Bn, D // tn),
        in_specs=[tok, tok, pl.BlockSpec((3, tn), lambda b, j: (0, j))],
        out_specs=tok,
        out_shape=jax.ShapeDtypeStruct((Bn, T, D), BF16),
        compiler_params=_params("parallel", "parallel"),
        name="short_conv",
    )(gb, z, cw)


def _pad_rows(w, rows):
    return jnp.pad(w, ((0, rows - w.shape[0]), (0, 0)))


def _pad_cols(w, cols):
    return jnp.pad(w, ((0, 0), (0, cols - w.shape[1])))


def _split_mod(mod_rows, D):
    return [mod_rows[:, m * D:(m + 1) * D][:, None, :] for m in range(6)]


def _ffn(x, g, sh, sc, gt, w13, w2, tag):
    act = _norm_mm(x, g, sh, sc, w13, 2, _swiglu_combine, (BF16,), name="ffn_up_" + tag)[0]
    return _mm_res(act, w2, x, gt, name="ffn_down_" + tag)


def _rwkv_layer(x, ctx, mods_x, mods_c, g1, g2n, mix, wr, wk, wv, wo, w0, w1, w2, a0, a1, a2,
                lg1, lg2, k_k, k_a, r_k, ln_w, ln_b, w13, wdn):
    D = x.shape[-1]
    H = D // HEAD
    w1cat = jnp.concatenate([_pad_cols(w1[0], LANES), _pad_cols(w1[1], LANES)], axis=1).astype(BF16)
    a1cat = jnp.concatenate([_pad_cols(a1[0], LANES), _pad_cols(a1[1], LANES)], axis=1).astype(BF16)
    w2p = [_pad_rows(w2[d], LANES).astype(BF16) for d in range(2)]
    a2p = [_pad_rows(a2[d], LANES).astype(BF16) for d in range(2)]
    wr, wk, wv, wo = (t.astype(BF16) for t in (wr, wk, wv, wo))
    lg1, lg2 = lg1.astype(BF16), lg2.astype(BF16)
    rk = r_k.reshape(1, D)

    sets = {}
    for tag, tok, mods, prep in (("c", ctx, mods_c, _prep_ctx), ("x", x, mods_x, _prep_latent)):
        xr, xw, xk, xv, xa, xg = prep(tok, g1, mods[0], mods[1], mix)
        sets[tag] = dict(
            r=_mm(xr, wr, F32, name="proj_r_" + tag),
            k=_mm(xk, wk, F32, name="proj_k_" + tag),
            v=_mm(xv, wv, F32, name="proj_v_" + tag),
            tw=_mm(xw, w1cat, BF16, act="tanh", name="lora_w_" + tag),
            aw=_mm(xa, a1cat, BF16, name="lora_a_" + tag),
            gs=_mm(xg, lg1, BF16, act="sigmoid", name="lora_g_" + tag))

    ys = {"c": [], "x": []}
    zero_state = jnp.zeros((x.shape[0], H, HEAD, HEAD), F32)
    for d in range(2):
        state = zero_state
        for tag in ("c", "x"):
            s = sets[tag]
            y, state = _wkv(s["r"], s["k"], s["v"], s["tw"], s["aw"], w2p[d], a2p[d],
                            w0[d][None], a0[d][None], k_k[None], k_a[None], state, d)
            ys[tag].append(y)

    outs = []
    for tag, tok, mods in (("c", ctx, mods_c), ("x", x, mods_x)):
        s = sets[tag]
        og = _readout(ys[tag][0], ys[tag][1], s["r"], s["k"], s["v"], s["gs"], s["aw"],
                      a2p[0], a2p[1], lg2, a0, k_a[None], rk, ln_w[None], ln_b[None])
        t1 = _mm_res(og, wo, tok, mods[2], name="proj_o_" + tag)
        outs.append(_ffn(t1, g2n, mods[3], mods[4], mods[5], w13, wdn, tag))
    return outs[1], outs[0]


def _conv_layer(x, mods, g1, g2n, w_in, conv_w, w_out, w13, wdn):
    gb, z = _norm_mm(x, g1, mods[0], mods[1], w_in.astype(BF16), 3, _conv_in_combine, (F32, F32),
                     name="conv_in")
    p = _conv(gb, z, conv_w)
    t1 = _mm_res(p, w_out.astype(BF16), x, mods[2], name="conv_out")
    return _ffn(t1, g2n, mods[3], mods[4], mods[5], w13, wdn, "x")


def kernel(x, c, ctx, c_ctx, norm1_g, norm2_g, ada_w, ada_b, rw_mix, rw_wr, rw_wk, rw_wv, rw_wo,
           rw_w0, rw_w1, rw_w2, rw_a0, rw_a1, rw_a2, rw_g1, rw_g2, rw_kk, rw_ka, rw_rk, rw_lnw,
           rw_lnb, sc_win, sc_conv, sc_wout, ffn_w13, ffn_w2, final_g):
    B, T, D = x.shape
    depth = norm1_g.shape[0]
    rows = -(-(B + 1) // 8) * 8
    cond = jnp.zeros((rows, D), F32).at[:B].set(c).at[B].set(c_ctx)
    for i in range(depth):
        last = i == depth - 1
        j = i // 2
        mod = _ada(cond, ada_w[i], ada_b[i])
        mods_x = _split_mod(mod[:B], D)
        mods_c = _split_mod(mod[B:B + 1], D)
        w13 = ffn_w13[i].astype(BF16)
        wdn = ffn_w2[i].astype(BF16)
        if i % 2 == 0:
            x, ctx_new = _rwkv_layer(
                x, ctx, mods_x, mods_c, norm1_g[i], norm2_g[i], rw_mix[j], rw_wr[j], rw_wk[j],
                rw_wv[j], rw_wo[j], rw_w0[j], rw_w1[j], rw_w2[j], rw_a0[j], rw_a1[j], rw_a2[j],
                rw_g1[j], rw_g2[j], rw_kk[j], rw_ka[j], rw_rk[j], rw_lnw[j], rw_lnb[j], w13, wdn)
            ctx = ctx_new
        else:
            x = _conv_layer(x, mods_x, norm1_g[i], norm2_g[i], sc_win[j], sc_conv[j], sc_wout[j],
                            w13, wdn)
            if not last:
                ctx = _conv_layer(ctx, mods_c, norm1_g[i], norm2_g[i], sc_win[j], sc_conv[j],
                                  sc_wout[j], w13, wdn)
    zeros = jnp.zeros((1, 1, D), F32)
    return _norm(x, final_g, zeros, zeros)
```

```python
import functools

import jax
import jax.numpy as jnp
from jax import lax
from jax.experimental import pallas as pl
from jax.experimental.pallas import tpu as pltpu

HEAD = 64
GRID_W = 64
CHUNK = 64
NORM_EPS = 1e-6
GN_EPS = 64e-5
LANES = 128
VMEM_LIMIT = 56 * 1024 * 1024

F32 = jnp.float32
BF16 = jnp.bfloat16
HI = lax.Precision.HIGHEST


def _params(*sem):
    return pltpu.CompilerParams(dimension_semantics=sem, vmem_limit_bytes=VMEM_LIMIT)


def _tile(n, pref, mult):
    t = min(pref, n)
    t -= t % mult
    while t >= mult:
        if n % t == 0:
            return t
        t -= mult
    return n


def _sigmoid(x):
    return 1.0 / (1.0 + jnp.exp(-x))


def _norm_mod(x, g, shift, scale):
    hn = x * lax.rsqrt(jnp.mean(x * x, axis=-1, keepdims=True) + NORM_EPS)
    return (hn * g) * (1.0 + scale) + shift


def _dot(a, b, precision=None):
    return jnp.dot(a, b, preferred_element_type=F32, precision=precision)


def _dot_t(a, b, ca, cb, precision=None):
    return lax.dot_general(a, b, (((ca,), (cb,)), ((), ())), preferred_element_type=F32,
                           precision=precision)


def _ada_kernel(c_ref, w_ref, b_ref, o_ref):
    c = c_ref[...]
    s = c * _sigmoid(c)
    o_ref[...] = _dot(s.astype(BF16), w_ref[...].astype(BF16)) + b_ref[...]


def _ada(cond, w, b):
    R, D = cond.shape
    N = w.shape[1]
    tn = _tile(N, 1024, LANES)
    return pl.pallas_call(
        _ada_kernel,
        grid=(N // tn,),
        in_specs=[pl.BlockSpec((R, D), lambda j: (0, 0)),
                  pl.BlockSpec((D, tn), lambda j: (0, j)),
                  pl.BlockSpec((1, tn), lambda j: (0, j))],
        out_specs=pl.BlockSpec((R, tn), lambda j: (0, j)),
        out_shape=jax.ShapeDtypeStruct((R, N), F32),
        compiler_params=_params("parallel"),
        name="ada_mod",
    )(cond, w, b[None])


def _norm_kernel(x_ref, g_ref, sh_ref, sc_ref, o_ref):
    o_ref[0] = _norm_mod(x_ref[0], g_ref[...], sh_ref[0], sc_ref[0])


def _bsel(arr):
    if arr.shape[0] == 1:
        return lambda b, *_: (0, 0, 0)
    return lambda b, *_: (b, 0, 0)


def _norm(x, g, shift, scale):
    Bn, T, D = x.shape
    tm = _tile(T, 512, 8)
    return pl.pallas_call(
        _norm_kernel,
        grid=(Bn, T // tm),
        in_specs=[pl.BlockSpec((1, tm, D), lambda b, i: (b, i, 0)),
                  pl.BlockSpec((1, D), lambda b, i: (0, 0)),
                  pl.BlockSpec((1, 1, D), _bsel(shift)),
                  pl.BlockSpec((1, 1, D), _bsel(scale))],
        out_specs=pl.BlockSpec((1, tm, D), lambda b, i: (b, i, 0)),
        out_shape=jax.ShapeDtypeStruct((Bn, T, D), F32),
        compiler_params=_params("parallel", "parallel"),
        name="norm",
    )(x, g[None], shift, scale)


def _write_mix(out_refs, mix_ref, h, shifted, c0, c1):
    xx = shifted - h
    for m, o_ref in enumerate(out_refs):
        o_ref[0, :, c0:c1] = (h + xx * mix_ref[m:m + 1, c0:c1]).astype(o_ref.dtype)


def _prep_latent_kernel(x_ref, xu_ref, xd_ref, g_ref, sh_ref, sc_ref, mix_ref, *out_refs):
    i = pl.program_id(1)
    n = pl.num_programs(1)
    g, sh, sc = g_ref[...], sh_ref[0], sc_ref[0]
    h = _norm_mod(x_ref[0], g, sh, sc)
    tm, D = h.shape
    q = D // 4
    hu = _norm_mod(xu_ref[0], g, sh, sc)[:, 2 * q:3 * q]
    hd = _norm_mod(xd_ref[0], g, sh, sc)[:, 3 * q:]
    hu = jnp.where(i > 0, hu, 0.0)
    hd = jnp.where(i < n - 1, hd, 0.0)
    col = lax.broadcasted_iota(jnp.int32, (tm, 1), 0) & (GRID_W - 1)
    h0, h1, h2, h3 = h[:, :q], h[:, q:2 * q], h[:, 2 * q:3 * q], h[:, 3 * q:]
    left = jnp.where(col == 0, 0.0, pltpu.roll(h0, 1, 0))
    right = jnp.where(col == GRID_W - 1, 0.0, pltpu.roll(h1, tm - 1, 0))
    if tm > GRID_W:
        up = jnp.concatenate([hu, h2[:tm - GRID_W]], axis=0)
        down = jnp.concatenate([h3[GRID_W:], hd], axis=0)
    else:
        up, down = hu, hd
    _write_mix(out_refs, mix_ref, h0, left, 0, q)
    _write_mix(out_refs, mix_ref, h1, right, q, 2 * q)
    _write_mix(out_refs, mix_ref, h2, up, 2 * q, 3 * q)
    _write_mix(out_refs, mix_ref, h3, down, 3 * q, D)


def _prep_latent(x, g, shift, scale, mix):
    Bn, T, D = x.shape
    rows_per_tile = _tile(T // GRID_W, 4, 1)
    tm = rows_per_tile * GRID_W
    nrow = T // GRID_W
    return pl.pallas_call(
        _prep_latent_kernel,
        grid=(Bn, T // tm),
        in_specs=[pl.BlockSpec((1, tm, D), lambda b, i: (b, i, 0)),
                  pl.BlockSpec((1, GRID_W, D),
                               lambda b, i: (b, jnp.maximum(i * rows_per_tile - 1, 0), 0)),
                  pl.BlockSpec((1, GRID_W, D),
                               lambda b, i: (b, jnp.minimum((i + 1) * rows_per_tile, nrow - 1), 0)),
                  pl.BlockSpec((1, D), lambda b, i: (0, 0)),
                  pl.BlockSpec((1, 1, D), _bsel(shift)),
                  pl.BlockSpec((1, 1, D), _bsel(scale)),
                  pl.BlockSpec((6, D), lambda b, i: (0, 0))],
        out_specs=[pl.BlockSpec((1, tm, D), lambda b, i: (b, i, 0))] * 6,
        out_shape=[jax.ShapeDtypeStruct((Bn, T, D), BF16)] * 6,
        compiler_params=_params("parallel", "parallel"),
        name="prep_latent",
    )(x, x, x, g[None], shift, scale, mix)


def _prep_ctx_kernel(x_ref, g_ref, sh_ref, sc_ref, mix_ref, *out_refs):
    h = _norm_mod(x_ref[0], g_ref[...], sh_ref[0], sc_ref[0])
    L, D = h.shape
    half = D // 2
    t = lax.broadcasted_iota(jnp.int32, (L, 1), 0)
    h0, h1 = h[:, :half], h[:, half:]
    prev = jnp.where(t == 0, 0.0, pltpu.roll(h0, 1, 0))
    nxt = jnp.where(t == L - 1, 0.0, pltpu.roll(h1, L - 1, 0))
    _write_mix(out_refs, mix_ref, h0, prev, 0, half)
    _write_mix(out_refs, mix_ref, h1, nxt, half, D)


def _prep_ctx(x, g, shift, scale, mix):
    Bn, L, D = x.shape
    return pl.pallas_call(
        _prep_ctx_kernel,
        grid=(Bn,),
        in_specs=[pl.BlockSpec((1, L, D), lambda b: (b, 0, 0)),
                  pl.BlockSpec((1, D), lambda b: (0, 0)),
                  pl.BlockSpec((1, 1, D), _bsel(shift)),
                  pl.BlockSpec((1, 1, D), _bsel(scale)),
                  pl.BlockSpec((6, D), lambda b: (0, 0))],
        out_specs=[pl.BlockSpec((1, L, D), lambda b: (b, 0, 0))] * 6,
        out_shape=[jax.ShapeDtypeStruct((Bn, L, D), BF16)] * 6,
        compiler_params=_params("parallel"),
        name="prep_ctx",
    )(x, g[None], shift, scale, mix)


def _mm_kernel(a_ref, w_ref, o_ref, *, act):
    acc = _dot(a_ref[0], w_ref[...])
    if act == "tanh":
        acc = jnp.tanh(acc)
    elif act == "sigmoid":
        acc = _sigmoid(acc)
    o_ref[0] = acc.astype(o_ref.dtype)


def _mm(a, w, out_dtype, act=None, name="mm"):
    Bn, T, K = a.shape
    N = w.shape[1]
    tm = _tile(T, 1024, 16)
    tn = _tile(N, 1024, LANES)
    return pl.pallas_call(
        functools.partial(_mm_kernel, act=act),
        grid=(Bn, T // tm, N // tn),
        in_specs=[pl.BlockSpec((1, tm, K), lambda b, i, j: (b, i, 0)),
                  pl.BlockSpec((K, tn), lambda b, i, j: (0, j))],
        out_specs=pl.BlockSpec((1, tm, tn), lambda b, i, j: (b, i, j)),
        out_shape=jax.ShapeDtypeStruct((Bn, T, N), out_dtype),
        compiler_params=_params("parallel", "parallel", "parallel"),
        name=name,
    )(a, w)


def _mm_res_kernel(a_ref, w_ref, res_ref, gate_ref, o_ref):
    o_ref[0] = res_ref[0] + gate_ref[0] * _dot(a_ref[0], w_ref[...])


def _mm_res(a, w, res, gate, name="mm_res"):
    Bn, T, K = a.shape
    N = w.shape[1]
    tm = _tile(T, 1024, 16)
    tn = _tile(N, 1024 if K <= 2048 else 512, LANES)
    gsel = _bsel(gate)
    return pl.pallas_call(
        _mm_res_kernel,
        grid=(Bn, T // tm, N // tn),
        in_specs=[pl.BlockSpec((1, tm, K), lambda b, i, j: (b, i, 0)),
                  pl.BlockSpec((K, tn), lambda b, i, j: (0, j)),
                  pl.BlockSpec((1, tm, tn), lambda b, i, j: (b, i, j)),
                  pl.BlockSpec((1, 1, tn), lambda b, i, j: gsel(b)[:2] + (j,))],
        out_specs=pl.BlockSpec((1, tm, tn), lambda b, i, j: (b, i, j)),
        out_shape=jax.ShapeDtypeStruct((Bn, T, N), F32),
        compiler_params=_params("parallel", "parallel", "parallel"),
        name=name,
    )(a, w, res, gate)


def _swiglu_combine(a, b):
    return (a * _sigmoid(a) * b,)


def _conv_in_combine(gb, gc, u):
    return gb, gc * u


def _norm_mm_kernel(x_ref, g_ref, sh_ref, sc_ref, *rest, nw, combine):
    w_refs, out_refs, h_scr = rest[:nw], rest[nw:-1], rest[-1]

    @pl.when(pl.program_id(2) == 0)
    def _():
        h_scr[...] = _norm_mod(x_ref[0], g_ref[...], sh_ref[0], sc_ref[0]).astype(BF16)

    h = h_scr[...]
    outs = combine(*[_dot(h, w_ref[...]) for w_ref in w_refs])
    for o_ref, o in zip(out_refs, outs):
        o_ref[0] = o.astype(o_ref.dtype)


def _norm_mm(x, g, shift, scale, w, nw, combine, out_dtypes, name):
    Bn, T, D = x.shape
    N = w.shape[1] // nw
    tm = _tile(T, 1024, 16)
    tn = _tile(N, 512, LANES)
    nj = N // tn
    w_specs = [pl.BlockSpec((D, tn), functools.partial(lambda b, i, j, m: (0, j + m * nj), m=m))
               for m in range(nw)]
    return pl.pallas_call(
        functools.partial(_norm_mm_kernel, nw=nw, combine=combine),
        grid=(Bn, T // tm, nj),
        in_specs=[pl.BlockSpec((1, tm, D), lambda b, i, j: (b, i, 0)),
                  pl.BlockSpec((1, D), lambda b, i, j: (0, 0)),
                  pl.BlockSpec((1, 1, D), _bsel(shift)),
                  pl.BlockSpec((1, 1, D), _bsel(scale))] + w_specs,
        out_specs=[pl.BlockSpec((1, tm, tn), lambda b, i, j: (b, i, j))] * len(out_dtypes),
        out_shape=[jax.ShapeDtypeStruct((Bn, T, N), dt) for dt in out_dtypes],
        scratch_shapes=[pltpu.VMEM((tm, D), BF16)],
        compiler_params=_params("parallel", "parallel", "arbitrary"),
        name=name,
    )(x, g[None], shift, scale, *([w] * nw))


def _seg_ones(width):
    shift = HEAD.bit_length() - 1
    r = lax.shift_right_logical(lax.broadcasted_iota(jnp.int32, (width, width), 0), shift)
    c = lax.shift_right_logical(lax.broadcasted_iota(jnp.int32, (width, width), 1), shift)
    return (r == c).astype(F32)


def _split3(x):
    hi = x.astype(BF16)
    r1 = x - hi.astype(F32)
    mid = r1.astype(BF16)
    lo = (r1 - mid.astype(F32)).astype(BF16)
    return hi, mid, lo


def _wkv_kernel(r_ref, k_ref, v_ref, tw_ref, aw_ref, w2_ref, a2_ref, w0_ref, a0_ref, kk_ref, ka_ref,
                s0_ref, y_ref, sout_ref, s_scr, *, reverse, npair):
    c = pl.program_id(2)
    C = CHUNK
    PW = 2 * HEAD

    @pl.when(c == 0)
    def _():
        s_scr[...] = s0_ref[0]

    rr = lax.shift_right_logical(lax.broadcasted_iota(jnp.int32, (PW, PW), 0), HEAD.bit_length() - 1)
    cc = lax.shift_right_logical(lax.broadcasted_iota(jnp.int32, (PW, PW), 1), HEAD.bit_length() - 1)
    same = rr == cc
    same_bf = same.astype(BF16)

    def bd(x):
        xb = x.astype(BF16)
        return jnp.concatenate([xb, xb], axis=0) * same_bf

    r, k, v = r_ref[0], k_ref[0], v_ref[0]
    z = w0_ref[...] + _dot(tw_ref[0], w2_ref[...])
    softplus = jnp.maximum(-z, 0.0) + jnp.log(1.0 + jnp.exp(-jnp.abs(z)))
    lw = -jnp.exp(-softplus - 0.5)
    a_sig = _sigmoid(a0_ref[...] + _dot(aw_ref[0], a2_ref[...]))
    kd = k * (1.0 + (a_sig - 1.0) * ka_ref[...])
    kkv = k * kk_ref[...]
    kk2 = kkv * kkv

    t_i = lax.broadcasted_iota(jnp.int32, (C, C), 0)
    s_i = lax.broadcasted_iota(jnp.int32, (C, C), 1)
    tri = ((s_i >= t_i) if reverse else (s_i <= t_i)).astype(BF16)
    lw3 = _split3(lw)
    cum = (_dot(tri, lw3[0]) + _dot(tri, lw3[1])) + _dot(tri, lw3[2])
    e_pos = jnp.exp(cum)
    e_neg = jnp.exp(-cum)
    e_prev = jnp.exp(cum - lw)
    last = 0 if reverse else C - 1
    w_tot = e_pos[last:last + 1, :]

    t2 = lax.broadcasted_iota(jnp.int32, (C, PW), 0)
    s2 = lax.broadcasted_iota(jnp.int32, (C, PW), 1) & (HEAD - 1)
    before = (s2 > t2) if reverse else (s2 < t2)
    upto = before | (s2 == t2)

    pairs = range(npair)
    sls = [slice(p * PW, (p + 1) * PW) for p in pairs]
    q3 = [_split3(kk2[:, sl]) for sl in sls]
    ss = [(_dot(q[0], same_bf) + _dot(q[1], same_bf)) + _dot(q[2], same_bf) for q in q3]
    kkn = [kkv[:, sl] * lax.rsqrt(jnp.maximum(s, 1e-24)) for sl, s in zip(sls, ss)]
    at = [(-kn) * e_prev[:, sl] for sl, kn in zip(sls, kkn)]
    bt = [(kn * a_sig[:, sl]) * e_neg[:, sl] for sl, kn in zip(sls, kkn)]
    rt = [r[:, sl] * e_pos[:, sl] for sl in sls]
    kt = [kd[:, sl] * e_neg[:, sl] for sl in sls]
    vp = [v[:, sl] for sl in sls]
    S = [s_scr[p] for p in pairs]
    X = [jnp.concatenate([at[p], rt[p]], axis=0).astype(BF16) for p in pairs]
    Gb = [_dot_t(X[p], bd(bt[p]), 1, 1) for p in pairs]
    Gk = [_dot_t(X[p], bd(kt[p]), 1, 1) for p in pairs]
    XS = [_dot_t(X[p], S[p].astype(BF16), 1, 1) for p in pairs]
    bdv = [bd(vp[p]) for p in pairs]
    u = [XS[p][:C] + _dot(jnp.where(before, Gk[p][:C], 0.0).astype(BF16), bdv[p]) for p in pairs]
    P = [jnp.where(before, Gb[p][:C], 0.0) for p in pairs]
    n_sq = C.bit_length() - 1
    for j in range(n_sq):
        Pb = [P[p].astype(BF16) for p in pairs]
        u = [u[p] + _dot(Pb[p], bd(u[p])) for p in pairs]
        if j < n_sq - 1:
            P = [_dot(Pb[p], bd(P[p])) for p in pairs]
    for p in pairs:
        Rb = jnp.where(upto, Gb[p][C:], 0.0).astype(BF16)
        Rk = jnp.where(upto, Gk[p][C:], 0.0).astype(BF16)
        y_ref[0, :, sls[p]] = (XS[p][C:] + _dot(Rb, bd(u[p]))) + _dot(Rk, bdv[p])
    for p in pairs:
        UV = jnp.concatenate([u[p], vp[p]], axis=0).astype(BF16)
        Z = jnp.concatenate([bt[p], kt[p]], axis=0).astype(BF16)
        dS = _dot_t(UV, Z, 0, 0)
        s_scr[p] = (S[p] + jnp.where(same, dS, 0.0)) * w_tot[:, sls[p]]

    @pl.when(c == pl.num_programs(2) - 1)
    def _():
        sout_ref[0] = s_scr[...]


def _wkv(r, k, v, tw, aw, w2p, a2p, w0, a0, kk, ka, s0, d):
    Bn, T, D = r.shape
    PW = 2 * HEAD
    npairs = D // PW
    npair = _tile(npairs, 8, 1)
    hw = npair * PW
    nc = T // CHUNK
    reverse = d == 1
    cidx = (lambda c: nc - 1 - c) if reverse else (lambda c: c)
    tok = pl.BlockSpec((1, CHUNK, hw), lambda b, g, c: (b, cidx(c), g))
    lora = pl.BlockSpec((1, CHUNK, LANES), lambda b, g, c: (b, cidx(c), d))
    lw2 = pl.BlockSpec((LANES, hw), lambda b, g, c: (0, g))
    vec = pl.BlockSpec((1, hw), lambda b, g, c: (0, g))
    st = pl.BlockSpec((1, npair, PW, PW), lambda b, g, c: (b, g, 0, 0))
    return pl.pallas_call(
        functools.partial(_wkv_kernel, reverse=reverse, npair=npair),
        grid=(Bn, npairs // npair, nc),
        in_specs=[tok, tok, tok, lora, lora, lw2, lw2, vec, vec, vec, vec, st],
        out_specs=[tok, st],
        out_shape=[jax.ShapeDtypeStruct((Bn, T, D), F32),
                   jax.ShapeDtypeStruct((Bn, npairs, PW, PW), F32)],
        scratch_shapes=[pltpu.VMEM((npair, PW, PW), F32)],
        compiler_params=_params("parallel", "parallel", "arbitrary"),
        name="wkv_rev" if reverse else "wkv_fwd",
    )(r, k, v, tw, aw, w2p, a2p, w0, a0, kk, ka, s0)


def _readout_kernel(yf_ref, yb_ref, r_ref, k_ref, v_ref, gs_ref, aw_ref, a2f_ref, a2b_ref, g2_ref,
                    a0_ref, ka_ref, rk_ref, lnw_ref, lnb_ref, o_ref):
    tn = o_ref.shape[-1]
    seg = _seg_ones(tn)
    aw = aw_ref[0]
    a_f = _sigmoid(a0_ref[0:1, :] + _dot(aw[:, :LANES], a2f_ref[...]))
    a_b = _sigmoid(a0_ref[1:2, :] + _dot(aw[:, LANES:], a2b_ref[...]))
    k = k_ref[0]
    ka = ka_ref[...]
    ksum = k * (1.0 + (a_f - 1.0) * ka) + k * (1.0 + (a_b - 1.0) * ka)
    y = yf_ref[0] + yb_ref[0]
    mu = _dot(y, seg, HI) * (1.0 / HEAD)
    yc = y - mu
    var = _dot(yc * yc, seg, HI) * (1.0 / HEAD)
    o = yc * lax.rsqrt(var + GN_EPS) * lnw_ref[...] + lnb_ref[...]
    bonus = _dot(r_ref[0] * ksum * rk_ref[...], seg, HI) * v_ref[0]
    g = _dot(gs_ref[0], g2_ref[...])
    o_ref[0] = ((o + bonus) * g).astype(o_ref.dtype)


def _readout(yf, yb, r, k, v, gs, aw, a2fp, a2bp, g2, a0, ka, rk, lnw, lnb):
    Bn, T, D = r.shape
    tm = _tile(T, 512, 16)
    tn = _tile(D, 512, LANES)
    G = gs.shape[-1]
    tok = pl.BlockSpec((1, tm, tn), lambda b, i, j: (b, i, j))
    vec = pl.BlockSpec((1, tn), lambda b, i, j: (0, j))
    return pl.pallas_call(
        _readout_kernel,
        grid=(Bn, T // tm, D // tn),
        in_specs=[tok, tok, tok, tok, tok,
                  pl.BlockSpec((1, tm, G), lambda b, i, j: (b, i, 0)),
                  pl.BlockSpec((1, tm, 2 * LANES), lambda b, i, j: (b, i, 0)),
                  pl.BlockSpec((LANES, tn), lambda b, i, j: (0, j)),
                  pl.BlockSpec((LANES, tn), lambda b, i, j: (0, j)),
                  pl.BlockSpec((G, tn), lambda b, i, j: (0, j)),
                  pl.BlockSpec((2, tn), lambda b, i, j: (0, j)),
                  vec, vec, vec, vec],
        out_specs=tok,
        out_shape=jax.ShapeDtypeStruct((Bn, T, D), BF16),
        compiler_params=_params("parallel", "parallel", "parallel"),
        name="rwkv_readout",
    )(yf, yb, r, k, v, gs, aw, a2fp, a2bp, g2, a0, ka, rk, lnw, lnb)


def _conv_kernel(gb_ref, z_ref, cw_ref, o_ref):
    z = z_ref[0]
    T = z.shape[0]
    t = lax.broadcasted_iota(jnp.int32, (T, 1), 0)
    zp = jnp.where(t == 0, 0.0, pltpu.roll(z, 1, 0))
    zn = jnp.where(t == T - 1, 0.0, pltpu.roll(z, T - 1, 0))
    conv = zp * cw_ref[0:1, :] + z * cw_ref[1:2, :] + zn * cw_ref[2:3, :]
    o_ref[0] = (gb_ref[0] * conv).astype(o_ref.dtype)


def _conv(gb, z, cw):
    Bn, T, D = z.shape
    tn = _tile(D, 256, LANES)
    tok = pl.BlockSpec((1, T, tn), lambda b, j: (b, 0, j))
    return pl.pallas_call(
        _conv_kernel,
        grid=(Bn, D // tn),
        in_specs=[tok, tok, pl.BlockSpec((3, tn), lambda b, j: (0, j))],
        out_specs=tok,
        out_shape=jax.ShapeDtypeStruct((Bn, T, D), BF16),
        compiler_params=_params("parallel", "parallel"),
        name="short_conv",
    )(gb, z, cw)


def _pad_rows(w, rows):
    return jnp.pad(w, ((0, rows - w.shape[0]), (0, 0)))


def _pad_cols(w, cols):
    return jnp.pad(w, ((0, 0), (0, cols - w.shape[1])))


def _split_mod(mod_rows, D):
    return [mod_rows[:, m * D:(m + 1) * D][:, None, :] for m in range(6)]


def _ffn(x, g, sh, sc, gt, w13, w2, tag):
    act = _norm_mm(x, g, sh, sc, w13, 2, _swiglu_combine, (BF16,), name="ffn_up_" + tag)[0]
    return _mm_res(act, w2, x, gt, name="ffn_down_" + tag)


def _rwkv_layer(x, ctx, mods_x, mods_c, g1, g2n, mix, wr, wk, wv, wo, w0, w1, w2, a0, a1, a2,
                lg1, lg2, k_k, k_a, r_k, ln_w, ln_b, w13, wdn):
    D = x.shape[-1]
    H = D // HEAD
    w1cat = jnp.concatenate([_pad_cols(w1[0], LANES), _pad_cols(w1[1], LANES)], axis=1).astype(BF16)
    a1cat = jnp.concatenate([_pad_cols(a1[0], LANES), _pad_cols(a1[1], LANES)], axis=1).astype(BF16)
    w2p = [_pad_rows(w2[d], LANES).astype(BF16) for d in range(2)]
    a2p = [_pad_rows(a2[d], LANES).astype(BF16) for d in range(2)]
    wr, wk, wv, wo = (t.astype(BF16) for t in (wr, wk, wv, wo))
    lg1, lg2 = lg1.astype(BF16), lg2.astype(BF16)
    rk = r_k.reshape(1, D)

    sets = {}
    for tag, tok, mods, prep in (("c", ctx, mods_c, _prep_ctx), ("x", x, mods_x, _prep_latent)):
        xr, xw, xk, xv, xa, xg = prep(tok, g1, mods[0], mods[1], mix)
        sets[tag] = dict(
            r=_mm(xr, wr, F32, name="proj_r_" + tag),
            k=_mm(xk, wk, F32, name="proj_k_" + tag),
            v=_mm(xv, wv, F32, name="proj_v_" + tag),
            tw=_mm(xw, w1cat, BF16, act="tanh", name="lora_w_" + tag),
            aw=_mm(xa, a1cat, BF16, name="lora_a_" + tag),
            gs=_mm(xg, lg1, BF16, act="sigmoid", name="lora_g_" + tag))

    ys = {"c": [], "x": []}
    zero_state = jnp.zeros((x.shape[0], H // 2, 2 * HEAD, 2 * HEAD), F32)
    for d in range(2):
        state = zero_state
        for tag in ("c", "x"):
            s = sets[tag]
            y, state = _wkv(s["r"], s["k"], s["v"], s["tw"], s["aw"], w2p[d], a2p[d],
                            w0[d][None], a0[d][None], k_k[None], k_a[None], state, d)
            ys[tag].append(y)

    outs = []
    for tag, tok, mods in (("c", ctx, mods_c), ("x", x, mods_x)):
        s = sets[tag]
        og = _readout(ys[tag][0], ys[tag][1], s["r"], s["k"], s["v"], s["gs"], s["aw"],
                      a2p[0], a2p[1], lg2, a0, k_a[None], rk, ln_w[None], ln_b[None])
        t1 = _mm_res(og, wo, tok, mods[2], name="proj_o_" + tag)
        outs.append(_ffn(t1, g2n, mods[3], mods[4], mods[5], w13, wdn, tag))
    return outs[1], outs[0]


def _conv_layer(x, mods, g1, g2n, w_in, conv_w, w_out, w13, wdn):
    gb, z = _norm_mm(x, g1, mods[0], mods[1], w_in.astype(BF16), 3, _conv_in_combine, (F32, F32),
                     name="conv_in")
    p = _conv(gb, z, conv_w)
    t1 = _mm_res(p, w_out.astype(BF16), x, mods[2], name="conv_out")
    return _ffn(t1, g2n, mods[3], mods[4], mods[5], w13, wdn, "x")


def kernel(x, c, ctx, c_ctx, norm1_g, norm2_g, ada_w, ada_b, rw_mix, rw_wr, rw_wk, rw_wv, rw_wo,
           rw_w0, rw_w1, rw_w2, rw_a0, rw_a1, rw_a2, rw_g1, rw_g2, rw_kk, rw_ka, rw_rk, rw_lnw,
           rw_lnb, sc_win, sc_conv, sc_wout, ffn_w13, ffn_w2, final_g):
    B, T, D = x.shape
    depth = norm1_g.shape[0]
    rows = -(-(B + 1) // 8) * 8
    cond = jnp.zeros((rows, D), F32).at[:B].set(c).at[B].set(c_ctx)
    for i in range(depth):
        last = i == depth - 1
        j = i // 2
        mod = _ada(cond, ada_w[i], ada_b[i])
        mods_x = _split_mod(mod[:B], D)
        mods_c = _split_mod(mod[B:B + 1], D)
        w13 = ffn_w13[i].astype(BF16)
        wdn = ffn_w2[i].astype(BF16)
        if i % 2 == 0:
            x, ctx_new = _rwkv_layer(
                x, ctx, mods_x, mods_c, norm1_g[i], norm2_g[i], rw_mix[j], rw_wr[j], rw_wk[j],
                rw_wv[j], rw_wo[j], rw_w0[j], rw_w1[j], rw_w2[j], rw_a0[j], rw_a1[j], rw_a2[j],
                rw_g1[j], rw_g2[j], rw_kk[j], rw_ka[j], rw_rk[j], rw_lnw[j], rw_lnb[j], w13, wdn)
            ctx = ctx_new
        else:
            x = _conv_layer(x, mods_x, norm1_g[i], norm2_g[i], sc_win[j], sc_conv[j], sc_wout[j],
                            w13, wdn)
            if not last:
                ctx = _conv_layer(ctx, mods_c, norm1_g[i], norm2_g[i], sc_win[j], sc_conv[j],
                                  sc_wout[j], w13, wdn)
    zeros = jnp.zeros((1, 1, D), F32)
    return _norm(x, final_g, zeros, zeros)
```

```python
import functools

import jax
import jax.numpy as jnp
from jax import lax
from jax.experimental import pallas as pl
from jax.experimental.pallas import tpu as pltpu

HEAD = 64
GRID_W = 64
CHUNK = 64
NORM_EPS = 1e-6
GN_EPS = 64e-5
LANES = 128
VMEM_LIMIT = 56 * 1024 * 1024

F32 = jnp.float32
BF16 = jnp.bfloat16
HI = lax.Precision.HIGHEST


def _params(*sem):
    return pltpu.CompilerParams(dimension_semantics=sem, vmem_limit_bytes=VMEM_LIMIT)


def _tile(n, pref, mult):
    t = min(pref, n)
    t -= t % mult
    while t >= mult:
        if n % t == 0:
            return t
        t -= mult
    return n


def _sigmoid(x):
    return 1.0 / (1.0 + jnp.exp(-x))


def _norm_mod(x, g, shift, scale):
    hn = x * lax.rsqrt(jnp.mean(x * x, axis=-1, keepdims=True) + NORM_EPS)
    return (hn * g) * (1.0 + scale) + shift


def _dot(a, b, precision=None):
    return jnp.dot(a, b, preferred_element_type=F32, precision=precision)


def _dot_t(a, b, ca, cb, precision=None):
    return lax.dot_general(a, b, (((ca,), (cb,)), ((), ())), preferred_element_type=F32,
                           precision=precision)


def _ada_kernel(c_ref, w_ref, b_ref, o_ref):
    c = c_ref[...]
    s = c * _sigmoid(c)
    o_ref[...] = _dot(s.astype(BF16), w_ref[...].astype(BF16)) + b_ref[...]


def _ada(cond, w, b):
    R, D = cond.shape
    N = w.shape[1]
    tn = _tile(N, 1024, LANES)
    return pl.pallas_call(
        _ada_kernel,
        grid=(N // tn,),
        in_specs=[pl.BlockSpec((R, D), lambda j: (0, 0)),
                  pl.BlockSpec((D, tn), lambda j: (0, j)),
                  pl.BlockSpec((1, tn), lambda j: (0, j))],
        out_specs=pl.BlockSpec((R, tn), lambda j: (0, j)),
        out_shape=jax.ShapeDtypeStruct((R, N), F32),
        compiler_params=_params("parallel"),
        name="ada_mod",
    )(cond, w, b[None])


def _norm_kernel(x_ref, g_ref, sh_ref, sc_ref, o_ref):
    o_ref[0] = _norm_mod(x_ref[0], g_ref[...], sh_ref[0], sc_ref[0])


def _bsel(arr):
    if arr.shape[0] == 1:
        return lambda b, *_: (0, 0, 0)
    return lambda b, *_: (b, 0, 0)


def _norm(x, g, shift, scale):
    Bn, T, D = x.shape
    tm = _tile(T, 512, 8)
    return pl.pallas_call(
        _norm_kernel,
        grid=(Bn, T // tm),
        in_specs=[pl.BlockSpec((1, tm, D), lambda b, i: (b, i, 0)),
                  pl.BlockSpec((1, D), lambda b, i: (0, 0)),
                  pl.BlockSpec((1, 1, D), _bsel(shift)),
                  pl.BlockSpec((1, 1, D), _bsel(scale))],
        out_specs=pl.BlockSpec((1, tm, D), lambda b, i: (b, i, 0)),
        out_shape=jax.ShapeDtypeStruct((Bn, T, D), F32),
        compiler_params=_params("parallel", "parallel"),
        name="norm",
    )(x, g[None], shift, scale)


def _write_mix(out_refs, mix_ref, h, shifted, c0, c1):
    xx = shifted - h
    for m, o_ref in enumerate(out_refs):
        o_ref[0, :, c0:c1] = (h + xx * mix_ref[m:m + 1, c0:c1]).astype(o_ref.dtype)


def _prep_latent_kernel(x_ref, xu_ref, xd_ref, g_ref, sh_ref, sc_ref, mix_ref, *out_refs):
    i = pl.program_id(1)
    n = pl.num_programs(1)
    g, sh, sc = g_ref[...], sh_ref[0], sc_ref[0]
    h = _norm_mod(x_ref[0], g, sh, sc)
    tm, D = h.shape
    q = D // 4
    hu = _norm_mod(xu_ref[0], g, sh, sc)[:, 2 * q:3 * q]
    hd = _norm_mod(xd_ref[0], g, sh, sc)[:, 3 * q:]
    hu = jnp.where(i > 0, hu, 0.0)
    hd = jnp.where(i < n - 1, hd, 0.0)
    col = lax.broadcasted_iota(jnp.int32, (tm, 1), 0) & (GRID_W - 1)
    h0, h1, h2, h3 = h[:, :q], h[:, q:2 * q], h[:, 2 * q:3 * q], h[:, 3 * q:]
    left = jnp.where(col == 0, 0.0, pltpu.roll(h0, 1, 0))
    right = jnp.where(col == GRID_W - 1, 0.0, pltpu.roll(h1, tm - 1, 0))
    if tm > GRID_W:
        up = jnp.concatenate([hu, h2[:tm - GRID_W]], axis=0)
        down = jnp.concatenate([h3[GRID_W:], hd], axis=0)
    else:
        up, down = hu, hd
    _write_mix(out_refs, mix_ref, h0, left, 0, q)
    _write_mix(out_refs, mix_ref, h1, right, q, 2 * q)
    _write_mix(out_refs, mix_ref, h2, up, 2 * q, 3 * q)
    _write_mix(out_refs, mix_ref, h3, down, 3 * q, D)


def _prep_latent(x, g, shift, scale, mix):
    Bn, T, D = x.shape
    rows_per_tile = _tile(T // GRID_W, 4, 1)
    tm = rows_per_tile * GRID_W
    nrow = T // GRID_W
    return pl.pallas_call(
        _prep_latent_kernel,
        grid=(Bn, T // tm),
        in_specs=[pl.BlockSpec((1, tm, D), lambda b, i: (b, i, 0)),
                  pl.BlockSpec((1, GRID_W, D),
                               lambda b, i: (b, jnp.maximum(i * rows_per_tile - 1, 0), 0)),
                  pl.BlockSpec((1, GRID_W, D),
                               lambda b, i: (b, jnp.minimum((i + 1) * rows_per_tile, nrow - 1), 0)),
                  pl.BlockSpec((1, D), lambda b, i: (0, 0)),
                  pl.BlockSpec((1, 1, D), _bsel(shift)),
                  pl.BlockSpec((1, 1, D), _bsel(scale)),
                  pl.BlockSpec((6, D), lambda b, i: (0, 0))],
        out_specs=[pl.BlockSpec((1, tm, D), lambda b, i: (b, i, 0))] * 6,
        out_shape=[jax.ShapeDtypeStruct((Bn, T, D), BF16)] * 6,
        compiler_params=_params("parallel", "parallel"),
        name="prep_latent",
    )(x, x, x, g[None], shift, scale, mix)


def _prep_ctx_kernel(x_ref, g_ref, sh_ref, sc_ref, mix_ref, *out_refs):
    h = _norm_mod(x_ref[0], g_ref[...], sh_ref[0], sc_ref[0])
    L, D = h.shape
    half = D // 2
    t = lax.broadcasted_iota(jnp.int32, (L, 1), 0)
    h0, h1 = h[:, :half], h[:, half:]
    prev = jnp.where(t == 0, 0.0, pltpu.roll(h0, 1, 0))
    nxt = jnp.where(t == L - 1, 0.0, pltpu.roll(h1, L - 1, 0))
    _write_mix(out_refs, mix_ref, h0, prev, 0, half)
    _write_mix(out_refs, mix_ref, h1, nxt, half, D)


def _prep_ctx(x, g, shift, scale, mix):
    Bn, L, D = x.shape
    return pl.pallas_call(
        _prep_ctx_kernel,
        grid=(Bn,),
        in_specs=[pl.BlockSpec((1, L, D), lambda b: (b, 0, 0)),
                  pl.BlockSpec((1, D), lambda b: (0, 0)),
                  pl.BlockSpec((1, 1, D), _bsel(shift)),
                  pl.BlockSpec((1, 1, D), _bsel(scale)),
                  pl.BlockSpec((6, D), lambda b: (0, 0))],
        out_specs=[pl.BlockSpec((1, L, D), lambda b: (b, 0, 0))] * 6,
        out_shape=[jax.ShapeDtypeStruct((Bn, L, D), BF16)] * 6,
        compiler_params=_params("parallel"),
        name="prep_ctx",
    )(x, g[None], shift, scale, mix)


def _mm_kernel(a_ref, w_ref, o_ref, *, act):
    acc = _dot(a_ref[0], w_ref[...])
    if act == "tanh":
        acc = jnp.tanh(acc)
    elif act == "sigmoid":
        acc = _sigmoid(acc)
    o_ref[0] = acc.astype(o_ref.dtype)


def _mm(a, w, out_dtype, act=None, name="mm"):
    Bn, T, K = a.shape
    N = w.shape[1]
    tm = _tile(T, 1024, 16)
    tn = _tile(N, 1024, LANES)
    return pl.pallas_call(
        functools.partial(_mm_kernel, act=act),
        grid=(Bn, T // tm, N // tn),
        in_specs=[pl.BlockSpec((1, tm, K), lambda b, i, j: (b, i, 0)),
                  pl.BlockSpec((K, tn), lambda b, i, j: (0, j))],
        out_specs=pl.BlockSpec((1, tm, tn), lambda b, i, j: (b, i, j)),
        out_shape=jax.ShapeDtypeStruct((Bn, T, N), out_dtype),
        compiler_params=_params("parallel", "parallel", "parallel"),
        name=name,
    )(a, w)


def _mm_res_kernel(a_ref, w_ref, res_ref, gate_ref, o_ref):
    o_ref[0] = res_ref[0] + gate_ref[0] * _dot(a_ref[0], w_ref[...])


def _mm_res(a, w, res, gate, name="mm_res"):
    Bn, T, K = a.shape
    N = w.shape[1]
    tm = _tile(T, 1024, 16)
    tn = _tile(N, 1024 if K <= 2048 else 512, LANES)
    gsel = _bsel(gate)
    return pl.pallas_call(
        _mm_res_kernel,
        grid=(Bn, T // tm, N // tn),
        in_specs=[pl.BlockSpec((1, tm, K), lambda b, i, j: (b, i, 0)),
                  pl.BlockSpec((K, tn), lambda b, i, j: (0, j)),
                  pl.BlockSpec((1, tm, tn), lambda b, i, j: (b, i, j)),
                  pl.BlockSpec((1, 1, tn), lambda b, i, j: gsel(b)[:2] + (j,))],
        out_specs=pl.BlockSpec((1, tm, tn), lambda b, i, j: (b, i, j)),
        out_shape=jax.ShapeDtypeStruct((Bn, T, N), F32),
        compiler_params=_params("parallel", "parallel", "parallel"),
        name=name,
    )(a, w, res, gate)


def _swiglu_combine(a, b):
    return (a * _sigmoid(a) * b,)


def _conv_in_combine(gb, gc, u):
    return gb, gc * u


def _norm_mm_kernel(x_ref, g_ref, sh_ref, sc_ref, *rest, nw, combine):
    w_refs, out_refs, h_scr = rest[:nw], rest[nw:-1], rest[-1]

    @pl.when(pl.program_id(2) == 0)
    def _():
        h_scr[...] = _norm_mod(x_ref[0], g_ref[...], sh_ref[0], sc_ref[0]).astype(BF16)

    h = h_scr[...]
    outs = combine(*[_dot(h, w_ref[...]) for w_ref in w_refs])
    for o_ref, o in zip(out_refs, outs):
        o_ref[0] = o.astype(o_ref.dtype)


def _norm_mm(x, g, shift, scale, w, nw, combine, out_dtypes, name):
    Bn, T, D = x.shape
    N = w.shape[1] // nw
    tm = _tile(T, 1024, 16)
    tn = _tile(N, 512, LANES)
    nj = N // tn
    w_specs = [pl.BlockSpec((D, tn), functools.partial(lambda b, i, j, m: (0, j + m * nj), m=m))
               for m in range(nw)]
    return pl.pallas_call(
        functools.partial(_norm_mm_kernel, nw=nw, combine=combine),
        grid=(Bn, T // tm, nj),
        in_specs=[pl.BlockSpec((1, tm, D), lambda b, i, j: (b, i, 0)),
                  pl.BlockSpec((1, D), lambda b, i, j: (0, 0)),
                  pl.BlockSpec((1, 1, D), _bsel(shift)),
                  pl.BlockSpec((1, 1, D), _bsel(scale))] + w_specs,
        out_specs=[pl.BlockSpec((1, tm, tn), lambda b, i, j: (b, i, j))] * len(out_dtypes),
        out_shape=[jax.ShapeDtypeStruct((Bn, T, N), dt) for dt in out_dtypes],
        scratch_shapes=[pltpu.VMEM((tm, D), BF16)],
        compiler_params=_params("parallel", "parallel", "arbitrary"),
        name=name,
    )(x, g[None], shift, scale, *([w] * nw))


def _seg_ones(width):
    shift = HEAD.bit_length() - 1
    r = lax.shift_right_logical(lax.broadcasted_iota(jnp.int32, (width, width), 0), shift)
    c = lax.shift_right_logical(lax.broadcasted_iota(jnp.int32, (width, width), 1), shift)
    return (r == c).astype(F32)


def _split3(x):
    hi = x.astype(BF16)
    r1 = x - hi.astype(F32)
    mid = r1.astype(BF16)
    lo = (r1 - mid.astype(F32)).astype(BF16)
    return hi, mid, lo


def _wkv_kernel(r_ref, k_ref, v_ref, tw_ref, aw_ref, w2_ref, a2_ref, w0_ref, a0_ref, kk_ref, ka_ref,
                s0_ref, y_ref, sout_ref, s_scr, x_scr, r2_scr, bv_scr, vb_scr, z_scr, wt_scr,
                *, reverse, npair):
    c = pl.program_id(2)
    C = CHUNK
    PW = 2 * HEAD

    @pl.when(c == 0)
    def _():
        s_scr[...] = s0_ref[0]
        x_scr[...] = jnp.zeros_like(x_scr)
        r2_scr[...] = jnp.zeros_like(r2_scr)
        bv_scr[...] = jnp.zeros_like(bv_scr)
        vb_scr[...] = jnp.zeros_like(vb_scr)
        z_scr[...] = jnp.zeros_like(z_scr)
        wt_scr[...] = jnp.ones_like(wt_scr)

    rr = lax.shift_right_logical(lax.broadcasted_iota(jnp.int32, (PW, PW), 0), HEAD.bit_length() - 1)
    cc = lax.shift_right_logical(lax.broadcasted_iota(jnp.int32, (PW, PW), 1), HEAD.bit_length() - 1)
    same = rr == cc
    same_bf = same.astype(BF16)

    def bd(x):
        xb = x.astype(BF16)
        return jnp.concatenate([xb, xb], axis=0) * same_bf

    t2 = lax.broadcasted_iota(jnp.int32, (C, PW), 0)
    s2 = lax.broadcasted_iota(jnp.int32, (C, PW), 1) & (HEAD - 1)
    before = (s2 > t2) if reverse else (s2 < t2)
    upto = before | (s2 == t2)
    pairs = range(npair)
    sls = [slice(p * PW, (p + 1) * PW) for p in pairs]

    r, k, v = r_ref[0], k_ref[0], v_ref[0]
    z = w0_ref[...] + _dot(tw_ref[0], w2_ref[...])
    a_pre = a0_ref[...] + _dot(aw_ref[0], a2_ref[...])

    S = [s_scr[p] for p in pairs]
    X = [x_scr[p] for p in pairs]
    G = [_dot_t(X[p], r2_scr[p], 1, 1) for p in pairs]
    XS = [_dot_t(X[p], S[p].astype(BF16), 1, 1) for p in pairs]
    u = [XS[p][:C] + _dot(jnp.where(before, G[p][:C, PW:], 0.0).astype(BF16), bv_scr[p]) for p in pairs]
    P = [jnp.where(before, G[p][:C, :PW], 0.0) for p in pairs]

    softplus = jnp.maximum(-z, 0.0) + jnp.log(1.0 + jnp.exp(-jnp.abs(z)))
    lw = -jnp.exp(-softplus - 0.5)
    kkv = k * kk_ref[...]
    kk2 = kkv * kkv
    t_i = lax.broadcasted_iota(jnp.int32, (C, C), 0)
    s_i = lax.broadcasted_iota(jnp.int32, (C, C), 1)
    tri = ((s_i >= t_i) if reverse else (s_i <= t_i)).astype(BF16)
    cum = _dot(jnp.concatenate([tri, tri, tri], axis=1),
               jnp.concatenate(_split3(lw), axis=0))
    same2 = jnp.concatenate([same_bf, same_bf], axis=0)
    ss = [_dot(jnp.concatenate(_split3(kk2[:, sl])[:2], axis=1), same2) for sl in sls]

    n_sq = C.bit_length() - 1
    for j in range(n_sq):
        Pb = [P[p].astype(BF16) for p in pairs]
        if j < n_sq - 1:
            PU = [_dot(Pb[p], jnp.concatenate([bd(P[p]), bd(u[p])], axis=1)) for p in pairs]
            P = [PU[p][:, :PW] for p in pairs]
            u = [u[p] + PU[p][:, PW:] for p in pairs]
        else:
            u = [u[p] + _dot(Pb[p], bd(u[p])) for p in pairs]
    for p in pairs:
        R = jnp.concatenate([jnp.where(upto, G[p][C:, :PW], 0.0),
                             jnp.where(upto, G[p][C:, PW:], 0.0)], axis=1).astype(BF16)
        y_ref[0, :, sls[p]] = XS[p][C:] + _dot(R, jnp.concatenate([bd(u[p]), bv_scr[p]], axis=0))
    for p in pairs:
        UV = jnp.concatenate([u[p].astype(BF16), vb_scr[p]], axis=0)
        dS = _dot_t(UV, z_scr[p], 0, 0)
        s_scr[p] = (S[p] + jnp.where(same, dS, 0.0)) * wt_scr[p, 0:1, :]

    a_sig = _sigmoid(a_pre)
    kd = k * (1.0 + (a_sig - 1.0) * ka_ref[...])
    e_pos = jnp.exp(cum)
    e_neg = jnp.exp(-cum)
    e_prev = jnp.exp(cum - lw)
    last = 0 if reverse else C - 1
    for p, sl in zip(pairs, sls):
        kkn = kkv[:, sl] * lax.rsqrt(jnp.maximum(ss[p], 1e-24))
        at = (-kkn) * e_prev[:, sl]
        bt = (kkn * a_sig[:, sl]) * e_neg[:, sl]
        rt = r[:, sl] * e_pos[:, sl]
        kt = kd[:, sl] * e_neg[:, sl]
        x_scr[p] = jnp.concatenate([at, rt], axis=0).astype(BF16)
        r2_scr[p] = jnp.concatenate([bd(bt), bd(kt)], axis=0)
        bv_scr[p] = bd(v[:, sl])
        vb_scr[p] = v[:, sl].astype(BF16)
        z_scr[p] = jnp.concatenate([bt, kt], axis=0).astype(BF16)
        wt_scr[p] = jnp.broadcast_to(e_pos[last:last + 1, sl], wt_scr.shape[1:])

    @pl.when(c == pl.num_programs(2) - 1)
    def _():
        sout_ref[0] = s_scr[...]


def _wkv(r, k, v, tw, aw, w2p, a2p, w0, a0, kk, ka, s0, d):
    Bn, T, D = r.shape
    PW = 2 * HEAD
    npairs = D // PW
    npair = _tile(npairs, 16, 1)
    hw = npair * PW
    nc = T // CHUNK
    reverse = d == 1
    pos = (lambda j: nc - 1 - j) if reverse else (lambda j: j)
    cin = lambda c: pos(jnp.minimum(c, nc - 1))
    cout = lambda c: pos(jnp.maximum(c - 1, 0))
    tok = pl.BlockSpec((1, CHUNK, hw), lambda b, g, c: (b, cin(c), g))
    lora = pl.BlockSpec((1, CHUNK, LANES), lambda b, g, c: (b, cin(c), d))
    lw2 = pl.BlockSpec((LANES, hw), lambda b, g, c: (0, g))
    vec = pl.BlockSpec((1, hw), lambda b, g, c: (0, g))
    st = pl.BlockSpec((1, npair, PW, PW), lambda b, g, c: (b, g, 0, 0))
    return pl.pallas_call(
        functools.partial(_wkv_kernel, reverse=reverse, npair=npair),
        grid=(Bn, npairs // npair, nc + 1),
        in_specs=[tok, tok, tok, lora, lora, lw2, lw2, vec, vec, vec, vec, st],
        out_specs=[pl.BlockSpec((1, CHUNK, hw), lambda b, g, c: (b, cout(c), g)), st],
        out_shape=[jax.ShapeDtypeStruct((Bn, T, D), F32),
                   jax.ShapeDtypeStruct((Bn, npairs, PW, PW), F32)],
        scratch_shapes=[pltpu.VMEM((npair, PW, PW), F32),
                        pltpu.VMEM((npair, 2 * CHUNK, PW), BF16),
                        pltpu.VMEM((npair, 2 * PW, PW), BF16),
                        pltpu.VMEM((npair, PW, PW), BF16),
                        pltpu.VMEM((npair, CHUNK, PW), BF16),
                        pltpu.VMEM((npair, 2 * CHUNK, PW), BF16),
                        pltpu.VMEM((npair, 8, PW), F32)],
        compiler_params=_params("parallel", "parallel", "arbitrary"),
        name="wkv_rev" if reverse else "wkv_fwd",
    )(r, k, v, tw, aw, w2p, a2p, w0, a0, kk, ka, s0)


def _readout_kernel(yf_ref, yb_ref, r_ref, k_ref, v_ref, gs_ref, aw_ref, a2f_ref, a2b_ref, g2_ref,
                    a0_ref, ka_ref, rk_ref, lnw_ref, lnb_ref, o_ref):
    tn = o_ref.shape[-1]
    PW = 2 * HEAD
    same = _seg_ones(PW).astype(BF16)
    same2 = jnp.concatenate([same, same], axis=0)

    def head_sum(x):
        return _dot(jnp.concatenate(_split3(x)[:2], axis=1), same2)

    aw = aw_ref[0]
    a_f = _sigmoid(a0_ref[0:1, :] + _dot(aw[:, :LANES], a2f_ref[...]))
    a_b = _sigmoid(a0_ref[1:2, :] + _dot(aw[:, LANES:], a2b_ref[...]))
    g = _dot(gs_ref[0], g2_ref[...])
    for p in range(tn // PW):
        sl = slice(p * PW, (p + 1) * PW)
        k = k_ref[0, :, sl]
        ka = ka_ref[:, sl]
        ksum = k * (1.0 + (a_f[:, sl] - 1.0) * ka) + k * (1.0 + (a_b[:, sl] - 1.0) * ka)
        y = yf_ref[0, :, sl] + yb_ref[0, :, sl]
        yc = y - head_sum(y) * (1.0 / HEAD)
        var = head_sum(yc * yc) * (1.0 / HEAD)
        o = yc * lax.rsqrt(var + GN_EPS) * lnw_ref[:, sl] + lnb_ref[:, sl]
        bonus = head_sum(r_ref[0, :, sl] * ksum * rk_ref[:, sl]) * v_ref[0, :, sl]
        o_ref[0, :, sl] = ((o + bonus) * g[:, sl]).astype(o_ref.dtype)


def _readout(yf, yb, r, k, v, gs, aw, a2fp, a2bp, g2, a0, ka, rk, lnw, lnb):
    Bn, T, D = r.shape
    tm = _tile(T, 512, 16)
    tn = _tile(D, 512, LANES)
    G = gs.shape[-1]
    tok = pl.BlockSpec((1, tm, tn), lambda b, i, j: (b, i, j))
    vec = pl.BlockSpec((1, tn), lambda b, i, j: (0, j))
    return pl.pallas_call(
        _readout_kernel,
        grid=(Bn, T // tm, D // tn),
        in_specs=[tok, tok, tok, tok, tok,
                  pl.BlockSpec((1, tm, G), lambda b, i, j: (b, i, 0)),
                  pl.BlockSpec((1, tm, 2 * LANES), lambda b, i, j: (b, i, 0)),
                  pl.BlockSpec((LANES, tn), lambda b, i, j: (0, j)),
                  pl.BlockSpec((LANES, tn), lambda b, i, j: (0, j)),
                  pl.BlockSpec((G, tn), lambda b, i, j: (0, j)),
                  pl.BlockSpec((2, tn), lambda b, i, j: (0, j)),
                  vec, vec, vec, vec],
        out_specs=tok,
        out_shape=jax.ShapeDtypeStruct((Bn, T, D), BF16),
        compiler_params=_params("parallel", "parallel", "parallel"),
        name="rwkv_readout",
    )(yf, yb, r, k, v, gs, aw, a2fp, a2bp, g2, a0, ka, rk, lnw, lnb)


def _conv_kernel(gb_ref, z_ref, cw_ref, o_ref):
    z = z_ref[0]
    T = z.shape[0]
    t = lax.broadcasted_iota(jnp.int32, (T, 1), 0)
    zp = jnp.where(t == 0, 0.0, pltpu.roll(z, 1, 0))
    zn = jnp.where(t == T - 1, 0.0, pltpu.roll(z, T - 1, 0))
    conv = zp * cw_ref[0:1, :] + z * cw_ref[1:2, :] + zn * cw_ref[2:3, :]
    o_ref[0] = (gb_ref[0] * conv).astype(o_ref.dtype)


def _conv(gb, z, cw):
    Bn, T, D = z.shape
    tn = _tile(D, 256, LANES)
    tok = pl.BlockSpec((1, T, tn), lambda b, j: (b, 0, j))
    return pl.pallas_call(
        _conv_kernel,
        grid=(Bn, D // tn),
        in_specs=[tok, tok, pl.BlockSpec((3, tn), lambda b, j: (0, j))],
        out_specs=tok,
        out_shape=jax.ShapeDtypeStruct((Bn, T, D), BF16),
        compiler_params=_params("parallel", "parallel"),
        name="short_conv",
    )(gb, z, cw)


def _pad_rows(w, rows):
    return jnp.pad(w, ((0, rows - w.shape[0]), (0, 0)))


def _pad_cols(w, cols):
    return jnp.pad(w, ((0, 0), (0, cols - w.shape[1])))


def _split_mod(mod_rows, D):
    return [mod_rows[:, m * D:(m + 1) * D][:, None, :] for m in range(6)]


def _ffn(x, g, sh, sc, gt, w13, w2, tag):
    act = _norm_mm(x, g, sh, sc, w13, 2, _swiglu_combine, (BF16,), name="ffn_up_" + tag)[0]
    return _mm_res(act, w2, x, gt, name="ffn_down_" + tag)


def _rwkv_layer(x, ctx, mods_x, mods_c, g1, g2n, mix, wr, wk, wv, wo, w0, w1, w2, a0, a1, a2,
                lg1, lg2, k_k, k_a, r_k, ln_w, ln_b, w13, wdn):
    D = x.shape[-1]
    H = D // HEAD
    w1cat = jnp.concatenate([_pad_cols(w1[0], LANES), _pad_cols(w1[1], LANES)], axis=1).astype(BF16)
    a1cat = jnp.concatenate([_pad_cols(a1[0], LANES), _pad_cols(a1[1], LANES)], axis=1).astype(BF16)
    w2p = [_pad_rows(w2[d], LANES).astype(BF16) for d in range(2)]
    a2p = [_pad_rows(a2[d], LANES).astype(BF16) for d in range(2)]
    wr, wk, wv, wo = (t.astype(BF16) for t in (wr, wk, wv, wo))
    lg1, lg2 = lg1.astype(BF16), lg2.astype(BF16)
    rk = r_k.reshape(1, D)

    sets = {}
    for tag, tok, mods, prep in (("c", ctx, mods_c, _prep_ctx), ("x", x, mods_x, _prep_latent)):
        xr, xw, xk, xv, xa, xg = prep(tok, g1, mods[0], mods[1], mix)
        sets[tag] = dict(
            r=_mm(xr, wr, F32, name="proj_r_" + tag),
            k=_mm(xk, wk, F32, name="proj_k_" + tag),
            v=_mm(xv, wv, F32, name="proj_v_" + tag),
            tw=_mm(xw, w1cat, BF16, act="tanh", name="lora_w_" + tag),
            aw=_mm(xa, a1cat, BF16, name="lora_a_" + tag),
            gs=_mm(xg, lg1, BF16, act="sigmoid", name="lora_g_" + tag))

    ys = {"c": [], "x": []}
    zero_state = jnp.zeros((x.shape[0], H // 2, 2 * HEAD, 2 * HEAD), F32)
    for d in range(2):
        state = zero_state
        for tag in ("c", "x"):
            s = sets[tag]
            y, state = _wkv(s["r"], s["k"], s["v"], s["tw"], s["aw"], w2p[d], a2p[d],
                            w0[d][None], a0[d][None], k_k[None], k_a[None], state, d)
            ys[tag].append(y)

    outs = []
    for tag, tok, mods in (("c", ctx, mods_c), ("x", x, mods_x)):
        s = sets[tag]
        og = _readout(ys[tag][0], ys[tag][1], s["r"], s["k"], s["v"], s["gs"], s["aw"],
                      a2p[0], a2p[1], lg2, a0, k_a[None], rk, ln_w[None], ln_b[None])
        t1 = _mm_res(og, wo, tok, mods[2], name="proj_o_" + tag)
        outs.append(_ffn(t1, g2n, mods[3], mods[4], mods[5], w13, wdn, tag))
    return outs[1], outs[0]


def _conv_layer(x, mods, g1, g2n, w_in, conv_w, w_out, w13, wdn):
    gb, z = _norm_mm(x, g1, mods[0], mods[1], w_in.astype(BF16), 3, _conv_in_combine, (F32, F32),
                     name="conv_in")
    p = _conv(gb, z, conv_w)
    t1 = _mm_res(p, w_out.astype(BF16), x, mods[2], name="conv_out")
    return _ffn(t1, g2n, mods[3], mods[4], mods[5], w13, wdn, "x")


def kernel(x, c, ctx, c_ctx, norm1_g, norm2_g, ada_w, ada_b, rw_mix, rw_wr, rw_wk, rw_wv, rw_wo,
           rw_w0, rw_w1, rw_w2, rw_a0, rw_a1, rw_a2, rw_g1, rw_g2, rw_kk, rw_ka, rw_rk, rw_lnw,
           rw_lnb, sc_win, sc_conv, sc_wout, ffn_w13, ffn_w2, final_g):
    B, T, D = x.shape
    depth = norm1_g.shape[0]
    rows = -(-(B + 1) // 8) * 8
    cond = jnp.zeros((rows, D), F32).at[:B].set(c).at[B].set(c_ctx)
    for i in range(depth):
        last = i == depth - 1
        j = i // 2
        mod = _ada(cond, ada_w[i], ada_b[i])
        mods_x = _split_mod(mod[:B], D)
        mods_c = _split_mod(mod[B:B + 1], D)
        w13 = ffn_w13[i].astype(BF16)
        wdn = ffn_w2[i].astype(BF16)
        if i % 2 == 0:
            x, ctx_new = _rwkv_layer(
                x, ctx, mods_x, mods_c, norm1_g[i], norm2_g[i], rw_mix[j], rw_wr[j], rw_wk[j],
                rw_wv[j], rw_wo[j], rw_w0[j], rw_w1[j], rw_w2[j], rw_a0[j], rw_a1[j], rw_a2[j],
                rw_g1[j], rw_g2[j], rw_kk[j], rw_ka[j], rw_rk[j], rw_lnw[j], rw_lnb[j], w13, wdn)
            ctx = ctx_new
        else:
            x = _conv_layer(x, mods_x, norm1_g[i], norm2_g[i], sc_win[j], sc_conv[j], sc_wout[j],
                            w13, wdn)
            if not last:
                ctx = _conv_layer(ctx, mods_c, norm1_g[i], norm2_g[i], sc_win[j], sc_conv[j],
                                  sc_wout[j], w13, wdn)
    zeros = jnp.zeros((1, 1, D), F32)
    return _norm(x, final_g, zeros, zeros)
```

```python
import functools

import jax
import jax.numpy as jnp
from jax import lax
from jax.experimental import pallas as pl
from jax.experimental.pallas import tpu as pltpu

HEAD = 64
GRID_W = 64
CHUNK = 64
NORM_EPS = 1e-6
GN_EPS = 64e-5
LANES = 128
VMEM_LIMIT = 56 * 1024 * 1024

F32 = jnp.float32
BF16 = jnp.bfloat16
HI = lax.Precision.HIGHEST


def _params(*sem):
    return pltpu.CompilerParams(dimension_semantics=sem, vmem_limit_bytes=VMEM_LIMIT)


def _tile(n, pref, mult):
    t = min(pref, n)
    t -= t % mult
    while t >= mult:
        if n % t == 0:
            return t
        t -= mult
    return n


def _sigmoid(x):
    return 1.0 / (1.0 + jnp.exp(-x))


def _norm_mod(x, g, shift, scale):
    hn = x * lax.rsqrt(jnp.mean(x * x, axis=-1, keepdims=True) + NORM_EPS)
    return (hn * g) * (1.0 + scale) + shift


def _dot(a, b, precision=None):
    return jnp.dot(a, b, preferred_element_type=F32, precision=precision)


def _dot_t(a, b, ca, cb, precision=None):
    return lax.dot_general(a, b, (((ca,), (cb,)), ((), ())), preferred_element_type=F32,
                           precision=precision)


def _ada_kernel(c_ref, w_ref, b_ref, o_ref):
    c = c_ref[...]
    s = c * _sigmoid(c)
    o_ref[...] = _dot(s.astype(BF16), w_ref[0].astype(BF16)) + b_ref[0]


def _ada(cond, w, b, layer):
    R, D = cond.shape
    N = w.shape[2]
    tn = _tile(N, 1024, LANES)
    return pl.pallas_call(
        _ada_kernel,
        grid=(N // tn,),
        in_specs=[pl.BlockSpec((R, D), lambda j: (0, 0)),
                  pl.BlockSpec((1, D, tn), lambda j: (layer, 0, j)),
                  pl.BlockSpec((1, 1, tn), lambda j: (layer, 0, j))],
        out_specs=pl.BlockSpec((R, tn), lambda j: (0, j)),
        out_shape=jax.ShapeDtypeStruct((R, N), F32),
        compiler_params=_params("parallel"),
        name="ada_mod",
    )(cond, w, b[:, None, :])


def _norm_kernel(x_ref, g_ref, sh_ref, sc_ref, o_ref):
    o_ref[0] = _norm_mod(x_ref[0], g_ref[...], sh_ref[0], sc_ref[0])


def _bsel(arr):
    if arr.shape[0] == 1:
        return lambda b, *_: (0, 0, 0)
    return lambda b, *_: (b, 0, 0)


def _norm(x, g, shift, scale):
    Bn, T, D = x.shape
    tm = _tile(T, 512, 8)
    return pl.pallas_call(
        _norm_kernel,
        grid=(Bn, T // tm),
        in_specs=[pl.BlockSpec((1, tm, D), lambda b, i: (b, i, 0)),
                  pl.BlockSpec((1, D), lambda b, i: (0, 0)),
                  pl.BlockSpec((1, 1, D), _bsel(shift)),
                  pl.BlockSpec((1, 1, D), _bsel(scale))],
        out_specs=pl.BlockSpec((1, tm, D), lambda b, i: (b, i, 0)),
        out_shape=jax.ShapeDtypeStruct((Bn, T, D), F32),
        compiler_params=_params("parallel", "parallel"),
        name="norm",
    )(x, g[None], shift, scale)


def _write_mix(out_refs, mix_ref, h, shifted, c0, c1):
    xx = shifted - h
    for m, o_ref in enumerate(out_refs):
        o_ref[0, :, c0:c1] = (h + xx * mix_ref[m:m + 1, c0:c1]).astype(o_ref.dtype)


def _prep_latent_kernel(x_ref, xu_ref, xd_ref, g_ref, sh_ref, sc_ref, mix_ref, *out_refs):
    i = pl.program_id(1)
    n = pl.num_programs(1)
    g, sh, sc = g_ref[...], sh_ref[0], sc_ref[0]
    h = _norm_mod(x_ref[0], g, sh, sc)
    tm, D = h.shape
    q = D // 4
    hu = _norm_mod(xu_ref[0], g, sh, sc)[:, 2 * q:3 * q]
    hd = _norm_mod(xd_ref[0], g, sh, sc)[:, 3 * q:]
    hu = jnp.where(i > 0, hu, 0.0)
    hd = jnp.where(i < n - 1, hd, 0.0)
    col = lax.broadcasted_iota(jnp.int32, (tm, 1), 0) & (GRID_W - 1)
    h0, h1, h2, h3 = h[:, :q], h[:, q:2 * q], h[:, 2 * q:3 * q], h[:, 3 * q:]
    left = jnp.where(col == 0, 0.0, pltpu.roll(h0, 1, 0))
    right = jnp.where(col == GRID_W - 1, 0.0, pltpu.roll(h1, tm - 1, 0))
    if tm > GRID_W:
        up = jnp.concatenate([hu, h2[:tm - GRID_W]], axis=0)
        down = jnp.concatenate([h3[GRID_W:], hd], axis=0)
    else:
        up, down = hu, hd
    _write_mix(out_refs, mix_ref, h0, left, 0, q)
    _write_mix(out_refs, mix_ref, h1, right, q, 2 * q)
    _write_mix(out_refs, mix_ref, h2, up, 2 * q, 3 * q)
    _write_mix(out_refs, mix_ref, h3, down, 3 * q, D)


def _prep_latent(x, g, shift, scale, mix):
    Bn, T, D = x.shape
    rows_per_tile = _tile(T // GRID_W, 4, 1)
    tm = rows_per_tile * GRID_W
    nrow = T // GRID_W
    return pl.pallas_call(
        _prep_latent_kernel,
        grid=(Bn, T // tm),
        in_specs=[pl.BlockSpec((1, tm, D), lambda b, i: (b, i, 0)),
                  pl.BlockSpec((1, GRID_W, D),
                               lambda b, i: (b, jnp.maximum(i * rows_per_tile - 1, 0), 0)),
                  pl.BlockSpec((1, GRID_W, D),
                               lambda b, i: (b, jnp.minimum((i + 1) * rows_per_tile, nrow - 1), 0)),
                  pl.BlockSpec((1, D), lambda b, i: (0, 0)),
                  pl.BlockSpec((1, 1, D), _bsel(shift)),
                  pl.BlockSpec((1, 1, D), _bsel(scale)),
                  pl.BlockSpec((6, D), lambda b, i: (0, 0))],
        out_specs=[pl.BlockSpec((1, tm, D), lambda b, i: (b, i, 0))] * 6,
        out_shape=[jax.ShapeDtypeStruct((Bn, T, D), BF16)] * 6,
        compiler_params=_params("parallel", "parallel"),
        name="prep_latent",
    )(x, x, x, g[None], shift, scale, mix)


def _prep_ctx_kernel(x_ref, g_ref, sh_ref, sc_ref, mix_ref, *out_refs):
    h = _norm_mod(x_ref[0], g_ref[...], sh_ref[0], sc_ref[0])
    L, D = h.shape
    half = D // 2
    t = lax.broadcasted_iota(jnp.int32, (L, 1), 0)
    h0, h1 = h[:, :half], h[:, half:]
    prev = jnp.where(t == 0, 0.0, pltpu.roll(h0, 1, 0))
    nxt = jnp.where(t == L - 1, 0.0, pltpu.roll(h1, L - 1, 0))
    _write_mix(out_refs, mix_ref, h0, prev, 0, half)
    _write_mix(out_refs, mix_ref, h1, nxt, half, D)


def _prep_ctx(x, g, shift, scale, mix):
    Bn, L, D = x.shape
    return pl.pallas_call(
        _prep_ctx_kernel,
        grid=(Bn,),
        in_specs=[pl.BlockSpec((1, L, D), lambda b: (b, 0, 0)),
                  pl.BlockSpec((1, D), lambda b: (0, 0)),
                  pl.BlockSpec((1, 1, D), _bsel(shift)),
                  pl.BlockSpec((1, 1, D), _bsel(scale)),
                  pl.BlockSpec((6, D), lambda b: (0, 0))],
        out_specs=[pl.BlockSpec((1, L, D), lambda b: (b, 0, 0))] * 6,
        out_shape=[jax.ShapeDtypeStruct((Bn, L, D), BF16)] * 6,
        compiler_params=_params("parallel"),
        name="prep_ctx",
    )(x, g[None], shift, scale, mix)


def _mm_kernel(a_ref, w_ref, o_ref, *, act):
    acc = _dot(a_ref[0], w_ref[...])
    if act == "tanh":
        acc = jnp.tanh(acc)
    elif act == "sigmoid":
        acc = _sigmoid(acc)
    o_ref[0] = acc.astype(o_ref.dtype)


def _mm(a, w, out_dtype, act=None, name="mm"):
    Bn, T, K = a.shape
    N = w.shape[1]
    tm = _tile(T, 1024, 16)
    tn = _tile(N, 1024, LANES)
    return pl.pallas_call(
        functools.partial(_mm_kernel, act=act),
        grid=(Bn, T // tm, N // tn),
        in_specs=[pl.BlockSpec((1, tm, K), lambda b, i, j: (b, i, 0)),
                  pl.BlockSpec((K, tn), lambda b, i, j: (0, j))],
        out_specs=pl.BlockSpec((1, tm, tn), lambda b, i, j: (b, i, j)),
        out_shape=jax.ShapeDtypeStruct((Bn, T, N), out_dtype),
        compiler_params=_params("parallel", "parallel", "parallel"),
        name=name,
    )(a, w)


def _mm_res_kernel(a_ref, w_ref, res_ref, gate_ref, o_ref):
    o_ref[0] = res_ref[0] + gate_ref[0] * _dot(a_ref[0], w_ref[...])


def _mm_res(a, w, res, gate, name="mm_res"):
    Bn, T, K = a.shape
    N = w.shape[1]
    tm = _tile(T, 1024, 16)
    tn = _tile(N, 1024 if K <= 2048 else 512, LANES)
    gsel = _bsel(gate)
    return pl.pallas_call(
        _mm_res_kernel,
        grid=(Bn, T // tm, N // tn),
        in_specs=[pl.BlockSpec((1, tm, K), lambda b, i, j: (b, i, 0)),
                  pl.BlockSpec((K, tn), lambda b, i, j: (0, j)),
                  pl.BlockSpec((1, tm, tn), lambda b, i, j: (b, i, j)),
                  pl.BlockSpec((1, 1, tn), lambda b, i, j: gsel(b)[:2] + (j,))],
        out_specs=pl.BlockSpec((1, tm, tn), lambda b, i, j: (b, i, j)),
        out_shape=jax.ShapeDtypeStruct((Bn, T, N), F32),
        compiler_params=_params("parallel", "parallel", "parallel"),
        name=name,
    )(a, w, res, gate)


def _mm_res_norm_kernel(a_ref, w_ref, res_ref, gate_ref, g_ref, sh_ref, sc_ref, o_ref, h_ref, *, rows):
    for r0 in range(0, a_ref.shape[1], rows):
        rs = slice(r0, r0 + rows)
        x1 = res_ref[0, rs, :] + gate_ref[0] * _dot(a_ref[0, rs, :], w_ref[...])
        o_ref[0, rs, :] = x1
        h_ref[0, rs, :] = _norm_mod(x1, g_ref[...], sh_ref[0], sc_ref[0]).astype(h_ref.dtype)


def _mm_res_norm(a, w, res, gate, g, shift, scale, name):
    Bn, T, K = a.shape
    N = w.shape[1]
    tm = _tile(T, 512, 16)
    rows = _tile(tm, 128, 16)
    row = pl.BlockSpec((1, tm, N), lambda b, i: (b, i, 0))
    return pl.pallas_call(
        functools.partial(_mm_res_norm_kernel, rows=rows),
        grid=(Bn, T // tm),
        in_specs=[pl.BlockSpec((1, tm, K), lambda b, i: (b, i, 0)),
                  pl.BlockSpec((K, N), lambda b, i: (0, 0)),
                  row,
                  pl.BlockSpec((1, 1, N), _bsel(gate)),
                  pl.BlockSpec((1, N), lambda b, i: (0, 0)),
                  pl.BlockSpec((1, 1, N), _bsel(shift)),
                  pl.BlockSpec((1, 1, N), _bsel(scale))],
        out_specs=[row, row],
        out_shape=[jax.ShapeDtypeStruct((Bn, T, N), F32), jax.ShapeDtypeStruct((Bn, T, N), BF16)],
        compiler_params=_params("parallel", "parallel"),
        name=name,
    )(a, w, res, gate, g[None], shift, scale)


def _ffn_up_kernel(h_ref, wa_ref, wb_ref, o_ref, w_scr):
    @pl.when((pl.program_id(1) == 0) & (pl.program_id(2) == 0))
    def _():
        w_scr[0] = wa_ref[0].astype(BF16)
        w_scr[1] = wb_ref[0].astype(BF16)

    h = h_ref[0]
    a = _dot(h, w_scr[0])
    o_ref[0] = (a * _sigmoid(a) * _dot(h, w_scr[1])).astype(o_ref.dtype)


def _ffn_up(h, w13, layer, name):
    Bn, T, D = h.shape
    F = w13.shape[2] // 2
    tm = _tile(T, 1024, 16)
    tn = _tile(F, 512, LANES)
    nj = F // tn
    return pl.pallas_call(
        _ffn_up_kernel,
        grid=(nj, Bn, T // tm),
        in_specs=[pl.BlockSpec((1, tm, D), lambda j, b, i: (b, i, 0)),
                  pl.BlockSpec((1, D, tn), lambda j, b, i: (layer, 0, j)),
                  pl.BlockSpec((1, D, tn), lambda j, b, i: (layer, 0, j + nj))],
        out_specs=pl.BlockSpec((1, tm, tn), lambda j, b, i: (b, i, j)),
        out_shape=jax.ShapeDtypeStruct((Bn, T, F), BF16),
        scratch_shapes=[pltpu.VMEM((2, D, tn), BF16)],
        compiler_params=_params("arbitrary", "arbitrary", "arbitrary"),
        name=name,
    )(h, w13, w13)


def _swiglu_combine(a, b):
    return (a * _sigmoid(a) * b,)


def _conv_in_combine(gb, gc, u):
    return gb, gc * u


def _norm_mm_kernel(x_ref, g_ref, sh_ref, sc_ref, *rest, nw, combine):
    w_refs, out_refs, h_scr = rest[:nw], rest[nw:-1], rest[-1]

    @pl.when(pl.program_id(2) == 0)
    def _():
        h_scr[...] = _norm_mod(x_ref[0], g_ref[...], sh_ref[0], sc_ref[0]).astype(BF16)

    h = h_scr[...]
    outs = combine(*[_dot(h, w_ref[...]) for w_ref in w_refs])
    for o_ref, o in zip(out_refs, outs):
        o_ref[0] = o.astype(o_ref.dtype)


def _norm_mm(x, g, shift, scale, w, nw, combine, out_dtypes, name):
    Bn, T, D = x.shape
    N = w.shape[1] // nw
    tm = _tile(T, 1024, 16)
    tn = _tile(N, 512, LANES)
    nj = N // tn
    w_specs = [pl.BlockSpec((D, tn), functools.partial(lambda b, i, j, m: (0, j + m * nj), m=m))
               for m in range(nw)]
    return pl.pallas_call(
        functools.partial(_norm_mm_kernel, nw=nw, combine=combine),
        grid=(Bn, T // tm, nj),
        in_specs=[pl.BlockSpec((1, tm, D), lambda b, i, j: (b, i, 0)),
                  pl.BlockSpec((1, D), lambda b, i, j: (0, 0)),
                  pl.BlockSpec((1, 1, D), _bsel(shift)),
                  pl.BlockSpec((1, 1, D), _bsel(scale))] + w_specs,
        out_specs=[pl.BlockSpec((1, tm, tn), lambda b, i, j: (b, i, j))] * len(out_dtypes),
        out_shape=[jax.ShapeDtypeStruct((Bn, T, N), dt) for dt in out_dtypes],
        scratch_shapes=[pltpu.VMEM((tm, D), BF16)],
        compiler_params=_params("parallel", "parallel", "arbitrary"),
        name=name,
    )(x, g[None], shift, scale, *([w] * nw))


def _seg_ones(width):
    shift = HEAD.bit_length() - 1
    r = lax.shift_right_logical(lax.broadcasted_iota(jnp.int32, (width, width), 0), shift)
    c = lax.shift_right_logical(lax.broadcasted_iota(jnp.int32, (width, width), 1), shift)
    return (r == c).astype(F32)


def _split3(x):
    hi = x.astype(BF16)
    r1 = x - hi.astype(F32)
    mid = r1.astype(BF16)
    lo = (r1 - mid.astype(F32)).astype(BF16)
    return hi, mid, lo


def _wkv_kernel(r_ref, k_ref, v_ref, tw_ref, aw_ref, w2_ref, a2_ref, w0_ref, a0_ref, kk_ref, ka_ref,
                s0_ref, y_ref, sout_ref, s_scr, x_scr, r2_scr, bv_scr, vb_scr, z_scr, wt_scr,
                *, reverse, npair):
    c = pl.program_id(2)
    C = CHUNK
    PW = 2 * HEAD

    @pl.when(c == 0)
    def _():
        s_scr[...] = s0_ref[0]
        x_scr[...] = jnp.zeros_like(x_scr)
        r2_scr[...] = jnp.zeros_like(r2_scr)
        bv_scr[...] = jnp.zeros_like(bv_scr)
        vb_scr[...] = jnp.zeros_like(vb_scr)
        z_scr[...] = jnp.zeros_like(z_scr)
        wt_scr[...] = jnp.ones_like(wt_scr)

    rr = lax.shift_right_logical(lax.broadcasted_iota(jnp.int32, (PW, PW), 0), HEAD.bit_length() - 1)
    cc = lax.shift_right_logical(lax.broadcasted_iota(jnp.int32, (PW, PW), 1), HEAD.bit_length() - 1)
    same = rr == cc
    same_bf = same.astype(BF16)

    def bd(x):
        xb = x.astype(BF16)
        return jnp.concatenate([xb, xb], axis=0) * same_bf

    t2 = lax.broadcasted_iota(jnp.int32, (C, PW), 0)
    s2 = lax.broadcasted_iota(jnp.int32, (C, PW), 1) & (HEAD - 1)
    before = (s2 > t2) if reverse else (s2 < t2)
    upto = before | (s2 == t2)
    pairs = range(npair)
    sls = [slice(p * PW, (p + 1) * PW) for p in pairs]

    r, k, v = r_ref[0], k_ref[0], v_ref[0]
    z = w0_ref[...] + _dot(tw_ref[0], w2_ref[...])
    a_pre = a0_ref[...] + _dot(aw_ref[0], a2_ref[...])

    S = [s_scr[p] for p in pairs]
    X = [x_scr[p] for p in pairs]
    G = [_dot_t(X[p], r2_scr[p], 1, 1) for p in pairs]
    XS = [_dot_t(X[p], S[p].astype(BF16), 1, 1) for p in pairs]
    u = [XS[p][:C] + _dot(jnp.where(before, G[p][:C, PW:], 0.0).astype(BF16), bv_scr[p]) for p in pairs]
    P = [jnp.where(before, G[p][:C, :PW], 0.0) for p in pairs]

    softplus = jnp.maximum(-z, 0.0) + jnp.log(1.0 + jnp.exp(-jnp.abs(z)))
    lw = -jnp.exp(-softplus - 0.5)
    kkv = k * kk_ref[...]
    kk2 = kkv * kkv
    t_i = lax.broadcasted_iota(jnp.int32, (C, C), 0)
    s_i = lax.broadcasted_iota(jnp.int32, (C, C), 1)
    tri = ((s_i >= t_i) if reverse else (s_i <= t_i)).astype(BF16)
    cum = _dot(jnp.concatenate([tri, tri, tri], axis=1),
               jnp.concatenate(_split3(lw), axis=0))
    same2 = jnp.concatenate([same_bf, same_bf], axis=0)
    ss = [_dot(jnp.concatenate(_split3(kk2[:, sl])[:2], axis=1), same2) for sl in sls]

    n_sq = C.bit_length() - 1
    for j in range(n_sq):
        Pb = [P[p].astype(BF16) for p in pairs]
        if j < n_sq - 1:
            PU = [_dot(Pb[p], jnp.concatenate([bd(P[p]), bd(u[p])], axis=1)) for p in pairs]
            P = [PU[p][:, :PW] for p in pairs]
            u = [u[p] + PU[p][:, PW:] for p in pairs]
        else:
            u = [u[p] + _dot(Pb[p], bd(u[p])) for p in pairs]
    for p in pairs:
        R = jnp.concatenate([jnp.where(upto, G[p][C:, :PW], 0.0),
                             jnp.where(upto, G[p][C:, PW:], 0.0)], axis=1).astype(BF16)
        y_ref[0, :, sls[p]] = XS[p][C:] + _dot(R, jnp.concatenate([bd(u[p]), bv_scr[p]], axis=0))
    for p in pairs:
        UV = jnp.concatenate([u[p].astype(BF16), vb_scr[p]], axis=0)
        dS = _dot_t(UV, z_scr[p], 0, 0)
        s_scr[p] = (S[p] + jnp.where(same, dS, 0.0)) * wt_scr[p, 0:1, :]

    a_sig = _sigmoid(a_pre)
    kd = k * (1.0 + (a_sig - 1.0) * ka_ref[...])
    e_pos = jnp.exp(cum)
    e_neg = jnp.exp(-cum)
    e_prev = jnp.exp(cum - lw)
    last = 0 if reverse else C - 1
    for p, sl in zip(pairs, sls):
        kkn = kkv[:, sl] * lax.rsqrt(jnp.maximum(ss[p], 1e-24))
        at = (-kkn) * e_prev[:, sl]
        bt = (kkn * a_sig[:, sl]) * e_neg[:, sl]
        rt = r[:, sl] * e_pos[:, sl]
        kt = kd[:, sl] * e_neg[:, sl]
        x_scr[p] = jnp.concatenate([at, rt], axis=0).astype(BF16)
        r2_scr[p] = jnp.concatenate([bd(bt), bd(kt)], axis=0)
        bv_scr[p] = bd(v[:, sl])
        vb_scr[p] = v[:, sl].astype(BF16)
        z_scr[p] = jnp.concatenate([bt, kt], axis=0).astype(BF16)
        wt_scr[p] = jnp.broadcast_to(e_pos[last:last + 1, sl], wt_scr.shape[1:])

    @pl.when(c == pl.num_programs(2) - 1)
    def _():
        sout_ref[0] = s_scr[...]


def _wkv(r, k, v, tw, aw, w2p, a2p, w0, a0, kk, ka, s0, d):
    Bn, T, D = r.shape
    PW = 2 * HEAD
    npairs = D // PW
    npair = _tile(npairs, 16, 1)
    hw = npair * PW
    nc = T // CHUNK
    reverse = d == 1
    pos = (lambda j: nc - 1 - j) if reverse else (lambda j: j)
    cin = lambda c: pos(jnp.minimum(c, nc - 1))
    cout = lambda c: pos(jnp.maximum(c - 1, 0))
    tok = pl.BlockSpec((1, CHUNK, hw), lambda b, g, c: (b, cin(c), g))
    lora = pl.BlockSpec((1, CHUNK, LANES), lambda b, g, c: (b, cin(c), d))
    lw2 = pl.BlockSpec((LANES, hw), lambda b, g, c: (0, g))
    vec = pl.BlockSpec((1, hw), lambda b, g, c: (0, g))
    st = pl.BlockSpec((1, npair, PW, PW), lambda b, g, c: (b, g, 0, 0))
    return pl.pallas_call(
        functools.partial(_wkv_kernel, reverse=reverse, npair=npair),
        grid=(Bn, npairs // npair, nc + 1),
        in_specs=[tok, tok, tok, lora, lora, lw2, lw2, vec, vec, vec, vec, st],
        out_specs=[pl.BlockSpec((1, CHUNK, hw), lambda b, g, c: (b, cout(c), g)), st],
        out_shape=[jax.ShapeDtypeStruct((Bn, T, D), F32),
                   jax.ShapeDtypeStruct((Bn, npairs, PW, PW), F32)],
        scratch_shapes=[pltpu.VMEM((npair, PW, PW), F32),
                        pltpu.VMEM((npair, 2 * CHUNK, PW), BF16),
                        pltpu.VMEM((npair, 2 * PW, PW), BF16),
                        pltpu.VMEM((npair, PW, PW), BF16),
                        pltpu.VMEM((npair, CHUNK, PW), BF16),
                        pltpu.VMEM((npair, 2 * CHUNK, PW), BF16),
                        pltpu.VMEM((npair, 8, PW), F32)],
        compiler_params=_params("parallel", "parallel", "arbitrary"),
        name="wkv_rev" if reverse else "wkv_fwd",
    )(r, k, v, tw, aw, w2p, a2p, w0, a0, kk, ka, s0)


def _readout_kernel(yf_ref, yb_ref, r_ref, k_ref, v_ref, gs_ref, aw_ref, a2f_ref, a2b_ref, g2_ref,
                    a0_ref, ka_ref, rk_ref, lnw_ref, lnb_ref, o_ref):
    tn = o_ref.shape[-1]
    PW = 2 * HEAD
    same = _seg_ones(PW).astype(BF16)
    same2 = jnp.concatenate([same, same], axis=0)

    def head_sum(x):
        return _dot(jnp.concatenate(_split3(x)[:2], axis=1), same2)

    aw = aw_ref[0]
    a_f = _sigmoid(a0_ref[0:1, :] + _dot(aw[:, :LANES], a2f_ref[...]))
    a_b = _sigmoid(a0_ref[1:2, :] + _dot(aw[:, LANES:], a2b_ref[...]))
    g = _dot(gs_ref[0], g2_ref[...])
    for p in range(tn // PW):
        sl = slice(p * PW, (p + 1) * PW)
        k = k_ref[0, :, sl]
        ka = ka_ref[:, sl]
        ksum = k * (1.0 + (a_f[:, sl] - 1.0) * ka) + k * (1.0 + (a_b[:, sl] - 1.0) * ka)
        y = yf_ref[0, :, sl] + yb_ref[0, :, sl]
        yc = y - head_sum(y) * (1.0 / HEAD)
        var = head_sum(yc * yc) * (1.0 / HEAD)
        o = yc * lax.rsqrt(var + GN_EPS) * lnw_ref[:, sl] + lnb_ref[:, sl]
        bonus = head_sum(r_ref[0, :, sl] * ksum * rk_ref[:, sl]) * v_ref[0, :, sl]
        o_ref[0, :, sl] = ((o + bonus) * g[:, sl]).astype(o_ref.dtype)


def _readout(yf, yb, r, k, v, gs, aw, a2fp, a2bp, g2, a0, ka, rk, lnw, lnb):
    Bn, T, D = r.shape
    tm = _tile(T, 512, 16)
    tn = _tile(D, 512, LANES)
    G = gs.shape[-1]
    tok = pl.BlockSpec((1, tm, tn), lambda b, i, j: (b, i, j))
    vec = pl.BlockSpec((1, tn), lambda b, i, j: (0, j))
    return pl.pallas_call(
        _readout_kernel,
        grid=(Bn, T // tm, D // tn),
        in_specs=[tok, tok, tok, tok, tok,
                  pl.BlockSpec((1, tm, G), lambda b, i, j: (b, i, 0)),
                  pl.BlockSpec((1, tm, 2 * LANES), lambda b, i, j: (b, i, 0)),
                  pl.BlockSpec((LANES, tn), lambda b, i, j: (0, j)),
                  pl.BlockSpec((LANES, tn), lambda b, i, j: (0, j)),
                  pl.BlockSpec((G, tn), lambda b, i, j: (0, j)),
                  pl.BlockSpec((2, tn), lambda b, i, j: (0, j)),
                  vec, vec, vec, vec],
        out_specs=tok,
        out_shape=jax.ShapeDtypeStruct((Bn, T, D), BF16),
        compiler_params=_params("parallel", "parallel", "parallel"),
        name="rwkv_readout",
    )(yf, yb, r, k, v, gs, aw, a2fp, a2bp, g2, a0, ka, rk, lnw, lnb)


def _conv_kernel(gb_ref, z_ref, cw_ref, o_ref):
    z = z_ref[0]
    T = z.shape[0]
    t = lax.broadcasted_iota(jnp.int32, (T, 1), 0)
    zp = jnp.where(t == 0, 0.0, pltpu.roll(z, 1, 0))
    zn = jnp.where(t == T - 1, 0.0, pltpu.roll(z, T - 1, 0))
    conv = zp * cw_ref[0:1, :] + z * cw_ref[1:2, :] + zn * cw_ref[2:3, :]
    o_ref[0] = (gb_ref[0] * conv).astype(o_ref.dtype)


def _conv(gb, z, cw):
    Bn, T, D = z.shape
    tn = _tile(D, 256, LANES)
    tok = pl.BlockSpec((1, T, tn), lambda b, j: (b, 0, j))
    return pl.pallas_call(
        _conv_kernel,
        grid=(Bn, D // tn),
        in_specs=[tok, tok, pl.BlockSpec((3, tn), lambda b, j: (0, j))],
        out_specs=tok,
        out_shape=jax.ShapeDtypeStruct((Bn, T, D), BF16),
        compiler_params=_params("parallel", "parallel"),
        name="short_conv",
    )(gb, z, cw)


def _pad_rows(w, rows):
    return jnp.pad(w, ((0, rows - w.shape[0]), (0, 0)))


def _pad_cols(w, cols):
    return jnp.pad(w, ((0, 0), (0, cols - w.shape[1])))


def _split_mod(mod_rows, D):
    return [mod_rows[:, m * D:(m + 1) * D][:, None, :] for m in range(6)]


def _mixer_out_ffn(a, w_o, tok, mods, g2n, ffn, tag):
    w13, layer, wdn = ffn
    t1, h2 = _mm_res_norm(a, w_o, tok, mods[2], g2n, mods[3], mods[4], name="mixer_out_" + tag)
    act = _ffn_up(h2, w13, layer, name="ffn_up_" + tag)
    return _mm_res(act, wdn, t1, mods[5], name="ffn_down_" + tag)


def _rwkv_layer(x, ctx, mods_x, mods_c, g1, g2n, mix, wr, wk, wv, wo, w0, w1, w2, a0, a1, a2,
                lg1, lg2, k_k, k_a, r_k, ln_w, ln_b, ffn):
    D = x.shape[-1]
    H = D // HEAD
    w1cat = jnp.concatenate([_pad_cols(w1[0], LANES), _pad_cols(w1[1], LANES)], axis=1).astype(BF16)
    a1cat = jnp.concatenate([_pad_cols(a1[0], LANES), _pad_cols(a1[1], LANES)], axis=1).astype(BF16)
    w2p = [_pad_rows(w2[d], LANES).astype(BF16) for d in range(2)]
    a2p = [_pad_rows(a2[d], LANES).astype(BF16) for d in range(2)]
    wr, wk, wv, wo = (t.astype(BF16) for t in (wr, wk, wv, wo))
    lg1, lg2 = lg1.astype(BF16), lg2.astype(BF16)
    rk = r_k.reshape(1, D)

    sets = {}
    for tag, tok, mods, prep in (("c", ctx, mods_c, _prep_ctx), ("x", x, mods_x, _prep_latent)):
        xr, xw, xk, xv, xa, xg = prep(tok, g1, mods[0], mods[1], mix)
        sets[tag] = dict(
            r=_mm(xr, wr, F32, name="proj_r_" + tag),
            k=_mm(xk, wk, F32, name="proj_k_" + tag),
            v=_mm(xv, wv, F32, name="proj_v_" + tag),
            tw=_mm(xw, w1cat, BF16, act="tanh", name="lora_w_" + tag),
            aw=_mm(xa, a1cat, BF16, name="lora_a_" + tag),
            gs=_mm(xg, lg1, BF16, act="sigmoid", name="lora_g_" + tag))

    ys = {"c": [], "x": []}
    zero_state = jnp.zeros((x.shape[0], H // 2, 2 * HEAD, 2 * HEAD), F32)
    for d in range(2):
        state = zero_state
        for tag in ("c", "x"):
            s = sets[tag]
            y, state = _wkv(s["r"], s["k"], s["v"], s["tw"], s["aw"], w2p[d], a2p[d],
                            w0[d][None], a0[d][None], k_k[None], k_a[None], state, d)
            ys[tag].append(y)

    outs = []
    for tag, tok, mods in (("c", ctx, mods_c), ("x", x, mods_x)):
        s = sets[tag]
        og = _readout(ys[tag][0], ys[tag][1], s["r"], s["k"], s["v"], s["gs"], s["aw"],
                      a2p[0], a2p[1], lg2, a0, k_a[None], rk, ln_w[None], ln_b[None])
        outs.append(_mixer_out_ffn(og, wo, tok, mods, g2n, ffn, tag))
    return outs[1], outs[0]


def _conv_layer(x, mods, g1, g2n, w_in, conv_w, w_out, ffn):
    gb, z = _norm_mm(x, g1, mods[0], mods[1], w_in.astype(BF16), 3, _conv_in_combine, (F32, F32),
                     name="conv_in")
    p = _conv(gb, z, conv_w)
    return _mixer_out_ffn(p, w_out.astype(BF16), x, mods, g2n, ffn, "x")


def kernel(x, c, ctx, c_ctx, norm1_g, norm2_g, ada_w, ada_b, rw_mix, rw_wr, rw_wk, rw_wv, rw_wo,
           rw_w0, rw_w1, rw_w2, rw_a0, rw_a1, rw_a2, rw_g1, rw_g2, rw_kk, rw_ka, rw_rk, rw_lnw,
           rw_lnb, sc_win, sc_conv, sc_wout, ffn_w13, ffn_w2, final_g):
    B, T, D = x.shape
    depth = norm1_g.shape[0]
    rows = -(-(B + 1) // 8) * 8
    cond = jnp.zeros((rows, D), F32).at[:B].set(c).at[B].set(c_ctx)
    for i in range(depth):
        last = i == depth - 1
        j = i // 2
        mod = _ada(cond, ada_w, ada_b, i)
        mods_x = _split_mod(mod[:B], D)
        mods_c = _split_mod(mod[B:B + 1], D)
        ffn = (ffn_w13, i, ffn_w2[i].astype(BF16))
        if i % 2 == 0:
            x, ctx_new = _rwkv_layer(
                x, ctx, mods_x, mods_c, norm1_g[i], norm2_g[i], rw_mix[j], rw_wr[j], rw_wk[j],
                rw_wv[j], rw_wo[j], rw_w0[j], rw_w1[j], rw_w2[j], rw_a0[j], rw_a1[j], rw_a2[j],
                rw_g1[j], rw_g2[j], rw_kk[j], rw_ka[j], rw_rk[j], rw_lnw[j], rw_lnb[j], ffn)
            ctx = ctx_new
        else:
            x = _conv_layer(x, mods_x, norm1_g[i], norm2_g[i], sc_win[j], sc_conv[j], sc_wout[j],
                            ffn)
            if not last:
                ctx = _conv_layer(ctx, mods_c, norm1_g[i], norm2_g[i], sc_win[j], sc_conv[j],
                                  sc_wout[j], ffn)
    zeros = jnp.zeros((1, 1, D), F32)
    return _norm(x, final_g, zeros, zeros)
```

```python
import functools
import math

import jax
import jax.numpy as jnp
from jax import lax
from jax.experimental import pallas as pl
from jax.experimental.pallas import tpu as pltpu

HEAD = 64
GRID_W = 64
CHUNK = 64
NORM_EPS = 1e-6
GN_EPS = 64e-5
LANES = 128
VMEM_LIMIT = 56 * 1024 * 1024

F32 = jnp.float32
BF16 = jnp.bfloat16
HI = lax.Precision.HIGHEST


def _params(*sem):
    return pltpu.CompilerParams(dimension_semantics=sem, vmem_limit_bytes=VMEM_LIMIT)


def _tile(n, pref, mult):
    t = min(pref, n)
    t -= t % mult
    while t >= mult:
        if n % t == 0:
            return t
        t -= mult
    return n


def _sigmoid(x):
    return 1.0 / (1.0 + jnp.exp(-x))


def _norm_mod(x, g, shift, scale):
    hn = x * lax.rsqrt(jnp.mean(x * x, axis=-1, keepdims=True) + NORM_EPS)
    return (hn * g) * (1.0 + scale) + shift


def _dot(a, b, precision=None):
    return jnp.dot(a, b, preferred_element_type=F32, precision=precision)


def _dot_t(a, b, ca, cb, precision=None):
    return lax.dot_general(a, b, (((ca,), (cb,)), ((), ())), preferred_element_type=F32,
                           precision=precision)


def _ada_kernel(c_ref, w_ref, b_ref, o_ref):
    c = c_ref[...]
    s = c * _sigmoid(c)
    o_ref[...] = _dot(s.astype(BF16), w_ref[0].astype(BF16)) + b_ref[0]


def _ada(cond, w, b, layer):
    R, D = cond.shape
    N = w.shape[2]
    tn = _tile(N, 1024, LANES)
    return pl.pallas_call(
        _ada_kernel,
        grid=(N // tn,),
        in_specs=[pl.BlockSpec((R, D), lambda j: (0, 0)),
                  pl.BlockSpec((1, D, tn), lambda j: (layer, 0, j)),
                  pl.BlockSpec((1, 1, tn), lambda j: (layer, 0, j))],
        out_specs=pl.BlockSpec((R, tn), lambda j: (0, j)),
        out_shape=jax.ShapeDtypeStruct((R, N), F32),
        compiler_params=_params("parallel"),
        name="ada_mod",
    )(cond, w, b[:, None, :])


def _norm_kernel(x_ref, g_ref, sh_ref, sc_ref, o_ref):
    o_ref[0] = _norm_mod(x_ref[0], g_ref[...], sh_ref[0], sc_ref[0])


def _bsel(arr):
    if arr.shape[0] == 1:
        return lambda b, *_: (0, 0, 0)
    return lambda b, *_: (b, 0, 0)


def _norm(x, g, shift, scale):
    Bn, T, D = x.shape
    tm = _tile(T, 512, 8)
    return pl.pallas_call(
        _norm_kernel,
        grid=(Bn, T // tm),
        in_specs=[pl.BlockSpec((1, tm, D), lambda b, i: (b, i, 0)),
                  pl.BlockSpec((1, D), lambda b, i: (0, 0)),
                  pl.BlockSpec((1, 1, D), _bsel(shift)),
                  pl.BlockSpec((1, 1, D), _bsel(scale))],
        out_specs=pl.BlockSpec((1, tm, D), lambda b, i: (b, i, 0)),
        out_shape=jax.ShapeDtypeStruct((Bn, T, D), F32),
        compiler_params=_params("parallel", "parallel"),
        name="norm",
    )(x, g[None], shift, scale)


def _write_mix(out_refs, mix_ref, h, shifted, c0, c1):
    xx = shifted - h
    for m, o_ref in enumerate(out_refs):
        o_ref[0, :, c0:c1] = (h + xx * mix_ref[m:m + 1, c0:c1]).astype(o_ref.dtype)


def _prep_latent_kernel(x_ref, xu_ref, xd_ref, g_ref, sh_ref, sc_ref, mix_ref, *out_refs):
    i = pl.program_id(1)
    n = pl.num_programs(1)
    g, sh, sc = g_ref[...], sh_ref[0], sc_ref[0]
    h = _norm_mod(x_ref[0], g, sh, sc)
    tm, D = h.shape
    q = D // 4
    hu = _norm_mod(xu_ref[0], g, sh, sc)[:, 2 * q:3 * q]
    hd = _norm_mod(xd_ref[0], g, sh, sc)[:, 3 * q:]
    hu = jnp.where(i > 0, hu, 0.0)
    hd = jnp.where(i < n - 1, hd, 0.0)
    col = lax.broadcasted_iota(jnp.int32, (tm, 1), 0) & (GRID_W - 1)
    h0, h1, h2, h3 = h[:, :q], h[:, q:2 * q], h[:, 2 * q:3 * q], h[:, 3 * q:]
    left = jnp.where(col == 0, 0.0, pltpu.roll(h0, 1, 0))
    right = jnp.where(col == GRID_W - 1, 0.0, pltpu.roll(h1, tm - 1, 0))
    if tm > GRID_W:
        up = jnp.concatenate([hu, h2[:tm - GRID_W]], axis=0)
        down = jnp.concatenate([h3[GRID_W:], hd], axis=0)
    else:
        up, down = hu, hd
    _write_mix(out_refs, mix_ref, h0, left, 0, q)
    _write_mix(out_refs, mix_ref, h1, right, q, 2 * q)
    _write_mix(out_refs, mix_ref, h2, up, 2 * q, 3 * q)
    _write_mix(out_refs, mix_ref, h3, down, 3 * q, D)


def _prep_latent(x, g, shift, scale, mix):
    Bn, T, D = x.shape
    rows_per_tile = _tile(T // GRID_W, 4, 1)
    tm = rows_per_tile * GRID_W
    nrow = T // GRID_W
    return pl.pallas_call(
        _prep_latent_kernel,
        grid=(Bn, T // tm),
        in_specs=[pl.BlockSpec((1, tm, D), lambda b, i: (b, i, 0)),
                  pl.BlockSpec((1, GRID_W, D),
                               lambda b, i: (b, jnp.maximum(i * rows_per_tile - 1, 0), 0)),
                  pl.BlockSpec((1, GRID_W, D),
                               lambda b, i: (b, jnp.minimum((i + 1) * rows_per_tile, nrow - 1), 0)),
                  pl.BlockSpec((1, D), lambda b, i: (0, 0)),
                  pl.BlockSpec((1, 1, D), _bsel(shift)),
                  pl.BlockSpec((1, 1, D), _bsel(scale)),
                  pl.BlockSpec((6, D), lambda b, i: (0, 0))],
        out_specs=[pl.BlockSpec((1, tm, D), lambda b, i: (b, i, 0))] * 6,
        out_shape=[jax.ShapeDtypeStruct((Bn, T, D), BF16)] * 6,
        compiler_params=_params("parallel", "parallel"),
        name="prep_latent",
    )(x, x, x, g[None], shift, scale, mix)


def _prep_ctx_kernel(x_ref, g_ref, sh_ref, sc_ref, mix_ref, *out_refs):
    h = _norm_mod(x_ref[0], g_ref[...], sh_ref[0], sc_ref[0])
    L, D = h.shape
    half = D // 2
    t = lax.broadcasted_iota(jnp.int32, (L, 1), 0)
    h0, h1 = h[:, :half], h[:, half:]
    prev = jnp.where(t == 0, 0.0, pltpu.roll(h0, 1, 0))
    nxt = jnp.where(t == L - 1, 0.0, pltpu.roll(h1, L - 1, 0))
    _write_mix(out_refs, mix_ref, h0, prev, 0, half)
    _write_mix(out_refs, mix_ref, h1, nxt, half, D)


def _prep_ctx(x, g, shift, scale, mix):
    Bn, L, D = x.shape
    return pl.pallas_call(
        _prep_ctx_kernel,
        grid=(Bn,),
        in_specs=[pl.BlockSpec((1, L, D), lambda b: (b, 0, 0)),
                  pl.BlockSpec((1, D), lambda b: (0, 0)),
                  pl.BlockSpec((1, 1, D), _bsel(shift)),
                  pl.BlockSpec((1, 1, D), _bsel(scale)),
                  pl.BlockSpec((6, D), lambda b: (0, 0))],
        out_specs=[pl.BlockSpec((1, L, D), lambda b: (b, 0, 0))] * 6,
        out_shape=[jax.ShapeDtypeStruct((Bn, L, D), BF16)] * 6,
        compiler_params=_params("parallel"),
        name="prep_ctx",
    )(x, g[None], shift, scale, mix)


def _mm_kernel(a_ref, w_ref, o_ref, *, act):
    acc = _dot(a_ref[0], w_ref[...])
    if act == "tanh":
        acc = jnp.tanh(acc)
    elif act == "sigmoid":
        acc = _sigmoid(acc)
    o_ref[0] = acc.astype(o_ref.dtype)


def _mm(a, w, out_dtype, act=None, name="mm"):
    Bn, T, K = a.shape
    N = w.shape[1]
    tm = _tile(T, 1024, 16)
    tn = _tile(N, 1024, LANES)
    return pl.pallas_call(
        functools.partial(_mm_kernel, act=act),
        grid=(Bn, T // tm, N // tn),
        in_specs=[pl.BlockSpec((1, tm, K), lambda b, i, j: (b, i, 0)),
                  pl.BlockSpec((K, tn), lambda b, i, j: (0, j))],
        out_specs=pl.BlockSpec((1, tm, tn), lambda b, i, j: (b, i, j)),
        out_shape=jax.ShapeDtypeStruct((Bn, T, N), out_dtype),
        compiler_params=_params("parallel", "parallel", "parallel"),
        name=name,
    )(a, w)


def _mm_res_kernel(a_ref, w_ref, res_ref, gate_ref, o_ref):
    o_ref[0] = res_ref[0] + gate_ref[0] * _dot(a_ref[0], w_ref[...])


def _mm_res(a, w, res, gate, name="mm_res"):
    Bn, T, K = a.shape
    N = w.shape[1]
    tm = _tile(T, 1024, 16)
    tn = _tile(N, 1024 if K <= 2048 else 512, LANES)
    gsel = _bsel(gate)
    return pl.pallas_call(
        _mm_res_kernel,
        grid=(Bn, T // tm, N // tn),
        in_specs=[pl.BlockSpec((1, tm, K), lambda b, i, j: (b, i, 0)),
                  pl.BlockSpec((K, tn), lambda b, i, j: (0, j)),
                  pl.BlockSpec((1, tm, tn), lambda b, i, j: (b, i, j)),
                  pl.BlockSpec((1, 1, tn), lambda b, i, j: gsel(b)[:2] + (j,))],
        out_specs=pl.BlockSpec((1, tm, tn), lambda b, i, j: (b, i, j)),
        out_shape=jax.ShapeDtypeStruct((Bn, T, N), F32),
        compiler_params=_params("parallel", "parallel", "parallel"),
        name=name,
    )(a, w, res, gate)


def _mm_res_norm_kernel(a_ref, w_ref, res_ref, gate_ref, g_ref, sh_ref, sc_ref, o_ref, h_ref, *, rows):
    for r0 in range(0, a_ref.shape[1], rows):
        rs = slice(r0, r0 + rows)
        x1 = res_ref[0, rs, :] + gate_ref[0] * _dot(a_ref[0, rs, :], w_ref[...])
        o_ref[0, rs, :] = x1
        h_ref[0, rs, :] = _norm_mod(x1, g_ref[...], sh_ref[0], sc_ref[0]).astype(h_ref.dtype)


def _mm_res_norm(a, w, res, gate, g, shift, scale, name):
    Bn, T, K = a.shape
    N = w.shape[1]
    tm = _tile(T, 512, 16)
    rows = _tile(tm, 256, 16)
    row = pl.BlockSpec((1, tm, N), lambda b, i: (b, i, 0))
    return pl.pallas_call(
        functools.partial(_mm_res_norm_kernel, rows=rows),
        grid=(Bn, T // tm),
        in_specs=[pl.BlockSpec((1, tm, K), lambda b, i: (b, i, 0)),
                  pl.BlockSpec((K, N), lambda b, i: (0, 0)),
                  row,
                  pl.BlockSpec((1, 1, N), _bsel(gate)),
                  pl.BlockSpec((1, N), lambda b, i: (0, 0)),
                  pl.BlockSpec((1, 1, N), _bsel(shift)),
                  pl.BlockSpec((1, 1, N), _bsel(scale))],
        out_specs=[row, row],
        out_shape=[jax.ShapeDtypeStruct((Bn, T, N), F32), jax.ShapeDtypeStruct((Bn, T, N), BF16)],
        compiler_params=_params("parallel", "parallel"),
        name=name,
    )(a, w, res, gate, g[None], shift, scale)


def _ffn_up_kernel(h_ref, wa_ref, wb_ref, o_ref, w_scr, *, rows):
    @pl.when((pl.program_id(1) == 0) & (pl.program_id(2) == 0))
    def _():
        w_scr[0] = wa_ref[0].astype(BF16)
        w_scr[1] = wb_ref[0].astype(BF16)

    for r0 in range(0, h_ref.shape[1], rows):
        rs = slice(r0, r0 + rows)
        h = h_ref[0, rs, :]
        a = _dot(h, w_scr[0])
        o_ref[0, rs, :] = (a * _sigmoid(a) * _dot(h, w_scr[1])).astype(o_ref.dtype)


def _ffn_up(h, w13, layer, name):
    Bn, T, D = h.shape
    F = w13.shape[2] // 2
    tm = _tile(T, 2048, 16)
    rows = _tile(tm, 512, 16)
    tn = _tile(F, 512, LANES)
    nj = F // tn
    return pl.pallas_call(
        functools.partial(_ffn_up_kernel, rows=rows),
        grid=(nj, Bn, T // tm),
        in_specs=[pl.BlockSpec((1, tm, D), lambda j, b, i: (b, i, 0)),
                  pl.BlockSpec((1, D, tn), lambda j, b, i: (layer, 0, j)),
                  pl.BlockSpec((1, D, tn), lambda j, b, i: (layer, 0, j + nj))],
        out_specs=pl.BlockSpec((1, tm, tn), lambda j, b, i: (b, i, j)),
        out_shape=jax.ShapeDtypeStruct((Bn, T, F), BF16),
        scratch_shapes=[pltpu.VMEM((2, D, tn), BF16)],
        compiler_params=_params("arbitrary", "arbitrary", "arbitrary"),
        name=name,
    )(h, w13, w13)


def _swiglu_combine(a, b):
    return (a * _sigmoid(a) * b,)


def _conv_in_combine(gb, gc, u):
    return gb, gc * u


def _norm_mm_kernel(x_ref, g_ref, sh_ref, sc_ref, *rest, nw, combine):
    w_refs, out_refs, h_scr = rest[:nw], rest[nw:-1], rest[-1]

    @pl.when(pl.program_id(2) == 0)
    def _():
        h_scr[...] = _norm_mod(x_ref[0], g_ref[...], sh_ref[0], sc_ref[0]).astype(BF16)

    h = h_scr[...]
    outs = combine(*[_dot(h, w_ref[...]) for w_ref in w_refs])
    for o_ref, o in zip(out_refs, outs):
        o_ref[0] = o.astype(o_ref.dtype)


def _norm_mm(x, g, shift, scale, w, nw, combine, out_dtypes, name):
    Bn, T, D = x.shape
    N = w.shape[1] // nw
    tm = _tile(T, 1024, 16)
    tn = _tile(N, 512, LANES)
    nj = N // tn
    w_specs = [pl.BlockSpec((D, tn), functools.partial(lambda b, i, j, m: (0, j + m * nj), m=m))
               for m in range(nw)]
    return pl.pallas_call(
        functools.partial(_norm_mm_kernel, nw=nw, combine=combine),
        grid=(Bn, T // tm, nj),
        in_specs=[pl.BlockSpec((1, tm, D), lambda b, i, j: (b, i, 0)),
                  pl.BlockSpec((1, D), lambda b, i, j: (0, 0)),
                  pl.BlockSpec((1, 1, D), _bsel(shift)),
                  pl.BlockSpec((1, 1, D), _bsel(scale))] + w_specs,
        out_specs=[pl.BlockSpec((1, tm, tn), lambda b, i, j: (b, i, j))] * len(out_dtypes),
        out_shape=[jax.ShapeDtypeStruct((Bn, T, N), dt) for dt in out_dtypes],
        scratch_shapes=[pltpu.VMEM((tm, D), BF16)],
        compiler_params=_params("parallel", "parallel", "arbitrary"),
        name=name,
    )(x, g[None], shift, scale, *([w] * nw))


def _seg_ones(width):
    shift = HEAD.bit_length() - 1
    r = lax.shift_right_logical(lax.broadcasted_iota(jnp.int32, (width, width), 0), shift)
    c = lax.shift_right_logical(lax.broadcasted_iota(jnp.int32, (width, width), 1), shift)
    return (r == c).astype(F32)


def _split3(x):
    hi = x.astype(BF16)
    r1 = x - hi.astype(F32)
    mid = r1.astype(BF16)
    lo = (r1 - mid.astype(F32)).astype(BF16)
    return hi, mid, lo


def _wkv_kernel(r_ref, k_ref, v_ref, tw_ref, aw_ref, w2_ref, a2_ref, w0_ref, a0_ref, kk_ref, ka_ref,
                s0_ref, y_ref, sout_ref, s_scr, x_scr, r2_scr, bv_scr, vb_scr, z_scr, wt_scr,
                *, reverse, npair):
    c = pl.program_id(2)
    C = CHUNK
    PW = 2 * HEAD

    @pl.when(c == 0)
    def _():
        s_scr[...] = s0_ref[0]
        x_scr[...] = jnp.zeros_like(x_scr)
        r2_scr[...] = jnp.zeros_like(r2_scr)
        bv_scr[...] = jnp.zeros_like(bv_scr)
        vb_scr[...] = jnp.zeros_like(vb_scr)
        z_scr[...] = jnp.zeros_like(z_scr)
        wt_scr[...] = jnp.ones_like(wt_scr)

    rr = lax.shift_right_logical(lax.broadcasted_iota(jnp.int32, (PW, PW), 0), HEAD.bit_length() - 1)
    cc = lax.shift_right_logical(lax.broadcasted_iota(jnp.int32, (PW, PW), 1), HEAD.bit_length() - 1)
    same = rr == cc
    same_bf = same.astype(BF16)

    def bd(x):
        xb = x.astype(BF16)
        return jnp.concatenate([xb, xb], axis=0) * same_bf

    t2 = lax.broadcasted_iota(jnp.int32, (C, PW), 0)
    s2 = lax.broadcasted_iota(jnp.int32, (C, PW), 1) & (HEAD - 1)
    before = (s2 > t2) if reverse else (s2 < t2)
    upto = before | (s2 == t2)
    pairs = range(npair)
    sls = [slice(p * PW, (p + 1) * PW) for p in pairs]

    r, k, v = r_ref[0], k_ref[0], v_ref[0]
    z = w0_ref[...] + _dot(tw_ref[0], w2_ref[...])
    a_pre = a0_ref[...] + _dot(aw_ref[0], a2_ref[...])

    S = [s_scr[p] for p in pairs]
    X = [x_scr[p] for p in pairs]
    G = [_dot_t(X[p], r2_scr[p], 1, 1) for p in pairs]
    XS = [_dot_t(X[p], S[p].astype(BF16), 1, 1) for p in pairs]
    u = [XS[p][:C] + _dot(jnp.where(before, G[p][:C, PW:], 0.0).astype(BF16), bv_scr[p]) for p in pairs]
    P = [jnp.where(before, G[p][:C, :PW], 0.0) for p in pairs]

    lw = -math.exp(-0.5) * _sigmoid(z)
    kkv = k * kk_ref[...]
    kk2 = kkv * kkv
    t_i = lax.broadcasted_iota(jnp.int32, (C, C), 0)
    s_i = lax.broadcasted_iota(jnp.int32, (C, C), 1)
    tri = ((s_i >= t_i) if reverse else (s_i <= t_i)).astype(BF16)
    cum = _dot(jnp.concatenate([tri, tri], axis=1),
               jnp.concatenate(_split3(lw)[:2], axis=0))
    same2 = jnp.concatenate([same_bf, same_bf], axis=0)
    ss = [_dot(jnp.concatenate(_split3(kk2[:, sl])[:2], axis=1), same2) for sl in sls]

    n_sq = C.bit_length() - 1
    for j in range(n_sq):
        Pb = [P[p].astype(BF16) for p in pairs]
        if j < n_sq - 1:
            PU = [_dot(Pb[p], jnp.concatenate([bd(P[p]), bd(u[p])], axis=1)) for p in pairs]
            P = [PU[p][:, :PW] for p in pairs]
            u = [u[p] + PU[p][:, PW:] for p in pairs]
        else:
            u = [u[p] + _dot(Pb[p], bd(u[p])) for p in pairs]
    for p in pairs:
        R = jnp.concatenate([jnp.where(upto, G[p][C:, :PW], 0.0),
                             jnp.where(upto, G[p][C:, PW:], 0.0)], axis=1).astype(BF16)
        y_ref[0, :, sls[p]] = XS[p][C:] + _dot(R, jnp.concatenate([bd(u[p]), bv_scr[p]], axis=0))
    for p in pairs:
        UV = jnp.concatenate([u[p].astype(BF16), vb_scr[p]], axis=0)
        dS = _dot_t(UV, z_scr[p], 0, 0)
        s_scr[p] = (S[p] + jnp.where(same, dS, 0.0)) * wt_scr[p, 0:1, :]

    a_sig = _sigmoid(a_pre)
    kd = k * (1.0 + (a_sig - 1.0) * ka_ref[...])
    e_pos = jnp.exp(cum)
    e_neg = jnp.exp(-cum)
    e_prev = jnp.exp(cum - lw)
    last = 0 if reverse else C - 1
    for p, sl in zip(pairs, sls):
        kkn = kkv[:, sl] * lax.rsqrt(jnp.maximum(ss[p], 1e-24))
        at = (-kkn) * e_prev[:, sl]
        bt = (kkn * a_sig[:, sl]) * e_neg[:, sl]
        rt = r[:, sl] * e_pos[:, sl]
        kt = kd[:, sl] * e_neg[:, sl]
        x_scr[p] = jnp.concatenate([at, rt], axis=0).astype(BF16)
        r2_scr[p] = jnp.concatenate([bd(bt), bd(kt)], axis=0)
        bv_scr[p] = bd(v[:, sl])
        vb_scr[p] = v[:, sl].astype(BF16)
        z_scr[p] = jnp.concatenate([bt, kt], axis=0).astype(BF16)
        wt_scr[p] = jnp.broadcast_to(e_pos[last:last + 1, sl], wt_scr.shape[1:])

    @pl.when(c == pl.num_programs(2) - 1)
    def _():
        sout_ref[0] = s_scr[...]


def _wkv(r, k, v, tw, aw, w2p, a2p, w0, a0, kk, ka, s0, d):
    Bn, T, D = r.shape
    PW = 2 * HEAD
    npairs = D // PW
    npair = _tile(npairs, 16, 1)
    hw = npair * PW
    nc = T // CHUNK
    reverse = d == 1
    pos = (lambda j: nc - 1 - j) if reverse else (lambda j: j)
    cin = lambda c: pos(jnp.minimum(c, nc - 1))
    cout = lambda c: pos(jnp.maximum(c - 1, 0))
    tok = pl.BlockSpec((1, CHUNK, hw), lambda b, g, c: (b, cin(c), g))
    lora = pl.BlockSpec((1, CHUNK, LANES), lambda b, g, c: (b, cin(c), d))
    lw2 = pl.BlockSpec((LANES, hw), lambda b, g, c: (0, g))
    vec = pl.BlockSpec((1, hw), lambda b, g, c: (0, g))
    st = pl.BlockSpec((1, npair, PW, PW), lambda b, g, c: (b, g, 0, 0))
    return pl.pallas_call(
        functools.partial(_wkv_kernel, reverse=reverse, npair=npair),
        grid=(Bn, npairs // npair, nc + 1),
        in_specs=[tok, tok, tok, lora, lora, lw2, lw2, vec, vec, vec, vec, st],
        out_specs=[pl.BlockSpec((1, CHUNK, hw), lambda b, g, c: (b, cout(c), g)), st],
        out_shape=[jax.ShapeDtypeStruct((Bn, T, D), F32),
                   jax.ShapeDtypeStruct((Bn, npairs, PW, PW), F32)],
        scratch_shapes=[pltpu.VMEM((npair, PW, PW), F32),
                        pltpu.VMEM((npair, 2 * CHUNK, PW), BF16),
                        pltpu.VMEM((npair, 2 * PW, PW), BF16),
                        pltpu.VMEM((npair, PW, PW), BF16),
                        pltpu.VMEM((npair, CHUNK, PW), BF16),
                        pltpu.VMEM((npair, 2 * CHUNK, PW), BF16),
                        pltpu.VMEM((npair, 8, PW), F32)],
        compiler_params=_params("parallel", "parallel", "arbitrary"),
        name="wkv_rev" if reverse else "wkv_fwd",
    )(r, k, v, tw, aw, w2p, a2p, w0, a0, kk, ka, s0)


def _readout_kernel(yf_ref, yb_ref, r_ref, k_ref, v_ref, gs_ref, aw_ref, a2f_ref, a2b_ref, g2_ref,
                    a0_ref, ka_ref, rk_ref, lnw_ref, lnb_ref, o_ref):
    tn = o_ref.shape[-1]
    PW = 2 * HEAD
    same = _seg_ones(PW).astype(BF16)
    same2 = jnp.concatenate([same, same], axis=0)

    def head_sum(x):
        return _dot(jnp.concatenate(_split3(x)[:2], axis=1), same2)

    aw = aw_ref[0]
    a_f = _sigmoid(a0_ref[0:1, :] + _dot(aw[:, :LANES], a2f_ref[...]))
    a_b = _sigmoid(a0_ref[1:2, :] + _dot(aw[:, LANES:], a2b_ref[...]))
    g = _dot(gs_ref[0], g2_ref[...])
    for p in range(tn // PW):
        sl = slice(p * PW, (p + 1) * PW)
        k = k_ref[0, :, sl]
        ka = ka_ref[:, sl]
        ksum = k * (1.0 + (a_f[:, sl] - 1.0) * ka) + k * (1.0 + (a_b[:, sl] - 1.0) * ka)
        y = yf_ref[0, :, sl] + yb_ref[0, :, sl]
        yc = y - head_sum(y) * (1.0 / HEAD)
        var = head_sum(yc * yc) * (1.0 / HEAD)
        o = yc * lax.rsqrt(var + GN_EPS) * lnw_ref[:, sl] + lnb_ref[:, sl]
        bonus = head_sum(r_ref[0, :, sl] * ksum * rk_ref[:, sl]) * v_ref[0, :, sl]
        o_ref[0, :, sl] = ((o + bonus) * g[:, sl]).astype(o_ref.dtype)


def _readout(yf, yb, r, k, v, gs, aw, a2fp, a2bp, g2, a0, ka, rk, lnw, lnb):
    Bn, T, D = r.shape
    tm = _tile(T, 512, 16)
    tn = _tile(D, 512, LANES)
    G = gs.shape[-1]
    tok = pl.BlockSpec((1, tm, tn), lambda b, i, j: (b, i, j))
    vec = pl.BlockSpec((1, tn), lambda b, i, j: (0, j))
    return pl.pallas_call(
        _readout_kernel,
        grid=(Bn, T // tm, D // tn),
        in_specs=[tok, tok, tok, tok, tok,
                  pl.BlockSpec((1, tm, G), lambda b, i, j: (b, i, 0)),
                  pl.BlockSpec((1, tm, 2 * LANES), lambda b, i, j: (b, i, 0)),
                  pl.BlockSpec((LANES, tn), lambda b, i, j: (0, j)),
                  pl.BlockSpec((LANES, tn), lambda b, i, j: (0, j)),
                  pl.BlockSpec((G, tn), lambda b, i, j: (0, j)),
                  pl.BlockSpec((2, tn), lambda b, i, j: (0, j)),
                  vec, vec, vec, vec],
        out_specs=tok,
        out_shape=jax.ShapeDtypeStruct((Bn, T, D), BF16),
        compiler_params=_params("parallel", "parallel", "parallel"),
        name="rwkv_readout",
    )(yf, yb, r, k, v, gs, aw, a2fp, a2bp, g2, a0, ka, rk, lnw, lnb)


def _conv_kernel(gb_ref, z_ref, cw_ref, o_ref):
    z = z_ref[0]
    T = z.shape[0]
    t = lax.broadcasted_iota(jnp.int32, (T, 1), 0)
    zp = jnp.where(t == 0, 0.0, pltpu.roll(z, 1, 0))
    zn = jnp.where(t == T - 1, 0.0, pltpu.roll(z, T - 1, 0))
    conv = zp * cw_ref[0:1, :] + z * cw_ref[1:2, :] + zn * cw_ref[2:3, :]
    o_ref[0] = (gb_ref[0] * conv).astype(o_ref.dtype)


def _conv(gb, z, cw):
    Bn, T, D = z.shape
    tn = _tile(D, 256, LANES)
    tok = pl.BlockSpec((1, T, tn), lambda b, j: (b, 0, j))
    return pl.pallas_call(
        _conv_kernel,
        grid=(Bn, D // tn),
        in_specs=[tok, tok, pl.BlockSpec((3, tn), lambda b, j: (0, j))],
        out_specs=tok,
        out_shape=jax.ShapeDtypeStruct((Bn, T, D), BF16),
        compiler_params=_params("parallel", "parallel"),
        name="short_conv",
    )(gb, z, cw)


def _pad_rows(w, rows):
    return jnp.pad(w, ((0, rows - w.shape[0]), (0, 0)))


def _pad_cols(w, cols):
    return jnp.pad(w, ((0, 0), (0, cols - w.shape[1])))


def _split_mod(mod_rows, D):
    return [mod_rows[:, m * D:(m + 1) * D][:, None, :] for m in range(6)]


def _mixer_out_ffn(a, w_o, tok, mods, g2n, ffn, tag):
    w13, layer, wdn = ffn
    t1, h2 = _mm_res_norm(a, w_o, tok, mods[2], g2n, mods[3], mods[4], name="mixer_out_" + tag)
    act = _ffn_up(h2, w13, layer, name="ffn_up_" + tag)
    return _mm_res(act, wdn, t1, mods[5], name="ffn_down_" + tag)


def _rwkv_layer(x, ctx, mods_x, mods_c, g1, g2n, mix, wr, wk, wv, wo, w0, w1, w2, a0, a1, a2,
                lg1, lg2, k_k, k_a, r_k, ln_w, ln_b, ffn):
    D = x.shape[-1]
    H = D // HEAD
    w1cat = jnp.concatenate([_pad_cols(w1[0], LANES), _pad_cols(w1[1], LANES)], axis=1).astype(BF16)
    a1cat = jnp.concatenate([_pad_cols(a1[0], LANES), _pad_cols(a1[1], LANES)], axis=1).astype(BF16)
    w2p = [_pad_rows(w2[d], LANES).astype(BF16) for d in range(2)]
    a2p = [_pad_rows(a2[d], LANES).astype(BF16) for d in range(2)]
    wr, wk, wv, wo = (t.astype(BF16) for t in (wr, wk, wv, wo))
    lg1, lg2 = lg1.astype(BF16), lg2.astype(BF16)
    rk = r_k.reshape(1, D)

    sets = {}
    for tag, tok, mods, prep in (("c", ctx, mods_c, _prep_ctx), ("x", x, mods_x, _prep_latent)):
        xr, xw, xk, xv, xa, xg = prep(tok, g1, mods[0], mods[1], mix)
        sets[tag] = dict(
            r=_mm(xr, wr, F32, name="proj_r_" + tag),
            k=_mm(xk, wk, F32, name="proj_k_" + tag),
            v=_mm(xv, wv, F32, name="proj_v_" + tag),
            tw=_mm(xw, w1cat, BF16, act="tanh", name="lora_w_" + tag),
            aw=_mm(xa, a1cat, BF16, name="lora_a_" + tag),
            gs=_mm(xg, lg1, BF16, act="sigmoid", name="lora_g_" + tag))

    ys = {"c": [], "x": []}
    zero_state = jnp.zeros((x.shape[0], H // 2, 2 * HEAD, 2 * HEAD), F32)
    for d in range(2):
        state = zero_state
        for tag in ("c", "x"):
            s = sets[tag]
            y, state = _wkv(s["r"], s["k"], s["v"], s["tw"], s["aw"], w2p[d], a2p[d],
                            w0[d][None], a0[d][None], k_k[None], k_a[None], state, d)
            ys[tag].append(y)

    outs = []
    for tag, tok, mods in (("c", ctx, mods_c), ("x", x, mods_x)):
        s = sets[tag]
        og = _readout(ys[tag][0], ys[tag][1], s["r"], s["k"], s["v"], s["gs"], s["aw"],
                      a2p[0], a2p[1], lg2, a0, k_a[None], rk, ln_w[None], ln_b[None])
        outs.append(_mixer_out_ffn(og, wo, tok, mods, g2n, ffn, tag))
    return outs[1], outs[0]


def _conv_layer(x, mods, g1, g2n, w_in, conv_w, w_out, ffn):
    gb, z = _norm_mm(x, g1, mods[0], mods[1], w_in.astype(BF16), 3, _conv_in_combine, (F32, F32),
                     name="conv_in")
    p = _conv(gb, z, conv_w)
    return _mixer_out_ffn(p, w_out.astype(BF16), x, mods, g2n, ffn, "x")


def kernel(x, c, ctx, c_ctx, norm1_g, norm2_g, ada_w, ada_b, rw_mix, rw_wr, rw_wk, rw_wv, rw_wo,
           rw_w0, rw_w1, rw_w2, rw_a0, rw_a1, rw_a2, rw_g1, rw_g2, rw_kk, rw_ka, rw_rk, rw_lnw,
           rw_lnb, sc_win, sc_conv, sc_wout, ffn_w13, ffn_w2, final_g):
    B, T, D = x.shape
    depth = norm1_g.shape[0]
    rows = -(-(B + 1) // 8) * 8
    cond = jnp.zeros((rows, D), F32).at[:B].set(c).at[B].set(c_ctx)
    for i in range(depth):
        last = i == depth - 1
        j = i // 2
        mod = _ada(cond, ada_w, ada_b, i)
        mods_x = _split_mod(mod[:B], D)
        mods_c = _split_mod(mod[B:B + 1], D)
        ffn = (ffn_w13, i, ffn_w2[i].astype(BF16))
        if i % 2 == 0:
            x, ctx_new = _rwkv_layer(
                x, ctx, mods_x, mods_c, norm1_g[i], norm2_g[i], rw_mix[j], rw_wr[j], rw_wk[j],
                rw_wv[j], rw_wo[j], rw_w0[j], rw_w1[j], rw_w2[j], rw_a0[j], rw_a1[j], rw_a2[j],
                rw_g1[j], rw_g2[j], rw_kk[j], rw_ka[j], rw_rk[j], rw_lnw[j], rw_lnb[j], ffn)
            ctx = ctx_new
        else:
            x = _conv_layer(x, mods_x, norm1_g[i], norm2_g[i], sc_win[j], sc_conv[j], sc_wout[j],
                            ffn)
            if not last:
                ctx = _conv_layer(ctx, mods_c, norm1_g[i], norm2_g[i], sc_win[j], sc_conv[j],
                                  sc_wout[j], ffn)
    zeros = jnp.zeros((1, 1, D), F32)
    return _norm(x, final_g, zeros, zeros)
```

```python
import functools
import math

import jax
import jax.numpy as jnp
from jax import lax
from jax.experimental import pallas as pl
from jax.experimental.pallas import tpu as pltpu

HEAD = 64
GRID_W = 64
CHUNK = 64
NORM_EPS = 1e-6
GN_EPS = 64e-5
LANES = 128
VMEM_LIMIT = 56 * 1024 * 1024

F32 = jnp.float32
BF16 = jnp.bfloat16
HI = lax.Precision.HIGHEST


def _params(*sem):
    return pltpu.CompilerParams(dimension_semantics=sem, vmem_limit_bytes=VMEM_LIMIT)


def _tile(n, pref, mult):
    t = min(pref, n)
    t -= t % mult
    while t >= mult:
        if n % t == 0:
            return t
        t -= mult
    return n


def _sigmoid(x):
    return 1.0 / (1.0 + jnp.exp(-x))


def _norm_mod(x, g, shift, scale, cols=None):
    rs = lax.rsqrt(jnp.mean(x * x, axis=-1, keepdims=True) + NORM_EPS)
    if cols is not None:
        x, g, shift, scale = (t[:, cols[0]:cols[1]] for t in (x, g, shift, scale))
    return (x * rs) * (g * (1.0 + scale)) + shift


def _dot(a, b, precision=None):
    return jnp.dot(a, b, preferred_element_type=F32, precision=precision)


def _dot_t(a, b, ca, cb, precision=None):
    return lax.dot_general(a, b, (((ca,), (cb,)), ((), ())), preferred_element_type=F32,
                           precision=precision)


def _ada_kernel(c_ref, w_ref, b_ref, o_ref):
    c = c_ref[...]
    s = c * _sigmoid(c)
    o_ref[...] = _dot(s.astype(BF16), w_ref[0].astype(BF16)) + b_ref[0]


def _ada(cond, w, b, layer):
    R, D = cond.shape
    N = w.shape[2]
    tn = _tile(N, 1024, LANES)
    return pl.pallas_call(
        _ada_kernel,
        grid=(N // tn,),
        in_specs=[pl.BlockSpec((R, D), lambda j: (0, 0)),
                  pl.BlockSpec((1, D, tn), lambda j: (layer, 0, j)),
                  pl.BlockSpec((1, 1, tn), lambda j: (layer, 0, j))],
        out_specs=pl.BlockSpec((R, tn), lambda j: (0, j)),
        out_shape=jax.ShapeDtypeStruct((R, N), F32),
        compiler_params=_params("parallel"),
        name="ada_mod",
    )(cond, w, b[:, None, :])


def _norm_kernel(x_ref, g_ref, sh_ref, sc_ref, o_ref):
    o_ref[0] = _norm_mod(x_ref[0], g_ref[...], sh_ref[0], sc_ref[0])


def _bsel(arr):
    if arr.shape[0] == 1:
        return lambda b, *_: (0, 0, 0)
    return lambda b, *_: (b, 0, 0)


def _norm(x, g, shift, scale):
    Bn, T, D = x.shape
    tm = _tile(T, 512, 8)
    return pl.pallas_call(
        _norm_kernel,
        grid=(Bn, T // tm),
        in_specs=[pl.BlockSpec((1, tm, D), lambda b, i: (b, i, 0)),
                  pl.BlockSpec((1, D), lambda b, i: (0, 0)),
                  pl.BlockSpec((1, 1, D), _bsel(shift)),
                  pl.BlockSpec((1, 1, D), _bsel(scale))],
        out_specs=pl.BlockSpec((1, tm, D), lambda b, i: (b, i, 0)),
        out_shape=jax.ShapeDtypeStruct((Bn, T, D), F32),
        compiler_params=_params("parallel", "parallel"),
        name="norm",
    )(x, g[None], shift, scale)


def _write_mix(out_refs, mix_ref, h, shifted, c0, c1):
    xx = shifted - h
    for m, o_ref in enumerate(out_refs):
        o_ref[0, :, c0:c1] = (h + xx * mix_ref[m:m + 1, c0:c1]).astype(o_ref.dtype)


def _prep_latent_kernel(x_ref, xu_ref, xd_ref, g_ref, sh_ref, sc_ref, mix_ref, *rest):
    out_refs, h_scr = rest[:-1], rest[-1]
    i = pl.program_id(1)
    n = pl.num_programs(1)
    g, sh, sc = g_ref[...], sh_ref[0], sc_ref[0]
    tm, D = x_ref.shape[1:]
    q = D // 4
    W = GRID_W
    h_scr[0:W, 2 * q:3 * q] = jnp.where(i > 0, _norm_mod(xu_ref[0], g, sh, sc, (2 * q, 3 * q)), 0.0)
    h_scr[W - 8:W, 0:q] = jnp.zeros((8, q), F32)
    for r0 in range(0, tm, W):
        h_scr[W + r0:2 * W + r0, :] = _norm_mod(x_ref[0, r0:r0 + W, :], g, sh, sc)
    h_scr[W + tm:, 3 * q:] = jnp.where(i < n - 1, _norm_mod(xd_ref[0], g, sh, sc, (3 * q, D)), 0.0)
    h_scr[W + tm:W + tm + 8, q:2 * q] = jnp.zeros((8, q), F32)
    R = 32
    row = lax.broadcasted_iota(jnp.int32, (R, 1), 0)
    for k, off in enumerate((-1, 1, -W, W)):
        c0, c1 = k * q, (k + 1) * q
        mixk = [mix_ref[m:m + 1, c0:c1] for m in range(len(out_refs))]
        for r0 in range(0, tm, R):
            h = h_scr[W + r0:W + r0 + R, c0:c1]
            s = h_scr[W + r0 + off:W + r0 + off + R, c0:c1]
            if off == -1 and r0 % W == 0:
                s = jnp.where(row == 0, 0.0, s)
            if off == 1 and (r0 + R) % W == 0:
                s = jnp.where(row == R - 1, 0.0, s)
            xx = s - h
            for m, o_ref in enumerate(out_refs):
                o_ref[0, r0:r0 + R, c0:c1] = (h + xx * mixk[m]).astype(o_ref.dtype)


def _prep_latent(x, g, shift, scale, mix):
    Bn, T, D = x.shape
    rows_per_tile = _tile(T // GRID_W, 4, 1)
    tm = rows_per_tile * GRID_W
    nrow = T // GRID_W
    return pl.pallas_call(
        _prep_latent_kernel,
        grid=(Bn, T // tm),
        in_specs=[pl.BlockSpec((1, tm, D), lambda b, i: (b, i, 0)),
                  pl.BlockSpec((1, GRID_W, D),
                               lambda b, i: (b, jnp.maximum(i * rows_per_tile - 1, 0), 0)),
                  pl.BlockSpec((1, GRID_W, D),
                               lambda b, i: (b, jnp.minimum((i + 1) * rows_per_tile, nrow - 1), 0)),
                  pl.BlockSpec((1, D), lambda b, i: (0, 0)),
                  pl.BlockSpec((1, 1, D), _bsel(shift)),
                  pl.BlockSpec((1, 1, D), _bsel(scale)),
                  pl.BlockSpec((6, D), lambda b, i: (0, 0))],
        out_specs=[pl.BlockSpec((1, tm, D), lambda b, i: (b, i, 0))] * 6,
        out_shape=[jax.ShapeDtypeStruct((Bn, T, D), BF16)] * 6,
        scratch_shapes=[pltpu.VMEM((tm + 2 * GRID_W, D), F32)],
        compiler_params=_params("parallel", "parallel"),
        name="prep_latent",
    )(x, x, x, g[None], shift, scale, mix)


def _prep_ctx_kernel(x_ref, g_ref, sh_ref, sc_ref, mix_ref, *out_refs):
    h = _norm_mod(x_ref[0], g_ref[...], sh_ref[0], sc_ref[0])
    L, D = h.shape
    half = D // 2
    t = lax.broadcasted_iota(jnp.int32, (L, 1), 0)
    h0, h1 = h[:, :half], h[:, half:]
    prev = jnp.where(t == 0, 0.0, pltpu.roll(h0, 1, 0))
    nxt = jnp.where(t == L - 1, 0.0, pltpu.roll(h1, L - 1, 0))
    _write_mix(out_refs, mix_ref, h0, prev, 0, half)
    _write_mix(out_refs, mix_ref, h1, nxt, half, D)


def _prep_ctx(x, g, shift, scale, mix):
    Bn, L, D = x.shape
    return pl.pallas_call(
        _prep_ctx_kernel,
        grid=(Bn,),
        in_specs=[pl.BlockSpec((1, L, D), lambda b: (b, 0, 0)),
                  pl.BlockSpec((1, D), lambda b: (0, 0)),
                  pl.BlockSpec((1, 1, D), _bsel(shift)),
                  pl.BlockSpec((1, 1, D), _bsel(scale)),
                  pl.BlockSpec((6, D), lambda b: (0, 0))],
        out_specs=[pl.BlockSpec((1, L, D), lambda b: (b, 0, 0))] * 6,
        out_shape=[jax.ShapeDtypeStruct((Bn, L, D), BF16)] * 6,
        compiler_params=_params("parallel"),
        name="prep_ctx",
    )(x, g[None], shift, scale, mix)


def _mm_kernel(a_ref, w_ref, o_ref, *, act):
    acc = _dot(a_ref[0], w_ref[...])
    if act == "tanh":
        acc = jnp.tanh(acc)
    elif act == "sigmoid":
        acc = _sigmoid(acc)
    o_ref[0] = acc.astype(o_ref.dtype)


def _mm(a, w, out_dtype, act=None, name="mm"):
    Bn, T, K = a.shape
    N = w.shape[1]
    tm = _tile(T, 1024, 16)
    tn = _tile(N, 1024, LANES)
    return pl.pallas_call(
        functools.partial(_mm_kernel, act=act),
        grid=(Bn, T // tm, N // tn),
        in_specs=[pl.BlockSpec((1, tm, K), lambda b, i, j: (b, i, 0)),
                  pl.BlockSpec((K, tn), lambda b, i, j: (0, j))],
        out_specs=pl.BlockSpec((1, tm, tn), lambda b, i, j: (b, i, j)),
        out_shape=jax.ShapeDtypeStruct((Bn, T, N), out_dtype),
        compiler_params=_params("parallel", "parallel", "parallel"),
        name=name,
    )(a, w)


def _mm_res_kernel(a_ref, w_ref, res_ref, gate_ref, o_ref):
    o_ref[0] = res_ref[0] + gate_ref[0] * _dot(a_ref[0], w_ref[...])


def _mm_res(a, w, res, gate, name="mm_res"):
    Bn, T, K = a.shape
    N = w.shape[1]
    tm = _tile(T, 1024, 16)
    tn = _tile(N, 1024 if K <= 2048 else 512, LANES)
    gsel = _bsel(gate)
    return pl.pallas_call(
        _mm_res_kernel,
        grid=(Bn, T // tm, N // tn),
        in_specs=[pl.BlockSpec((1, tm, K), lambda b, i, j: (b, i, 0)),
                  pl.BlockSpec((K, tn), lambda b, i, j: (0, j)),
                  pl.BlockSpec((1, tm, tn), lambda b, i, j: (b, i, j)),
                  pl.BlockSpec((1, 1, tn), lambda b, i, j: gsel(b)[:2] + (j,))],
        out_specs=pl.BlockSpec((1, tm, tn), lambda b, i, j: (b, i, j)),
        out_shape=jax.ShapeDtypeStruct((Bn, T, N), F32),
        compiler_params=_params("parallel", "parallel", "parallel"),
        name=name,
    )(a, w, res, gate)


def _mm_res_norm_kernel(a_ref, w_ref, res_ref, gate_ref, g_ref, sh_ref, sc_ref, o_ref, h_ref, *, rows):
    for r0 in range(0, a_ref.shape[1], rows):
        rs = slice(r0, r0 + rows)
        x1 = res_ref[0, rs, :] + gate_ref[0] * _dot(a_ref[0, rs, :], w_ref[...])
        o_ref[0, rs, :] = x1
        h_ref[0, rs, :] = _norm_mod(x1, g_ref[...], sh_ref[0], sc_ref[0]).astype(h_ref.dtype)


def _mm_res_norm(a, w, res, gate, g, shift, scale, name):
    Bn, T, K = a.shape
    N = w.shape[1]
    tm = _tile(T, 512, 16)
    rows = _tile(tm, 256, 16)
    row = pl.BlockSpec((1, tm, N), lambda b, i: (b, i, 0))
    return pl.pallas_call(
        functools.partial(_mm_res_norm_kernel, rows=rows),
        grid=(Bn, T // tm),
        in_specs=[pl.BlockSpec((1, tm, K), lambda b, i: (b, i, 0)),
                  pl.BlockSpec((K, N), lambda b, i: (0, 0)),
                  row,
                  pl.BlockSpec((1, 1, N), _bsel(gate)),
                  pl.BlockSpec((1, N), lambda b, i: (0, 0)),
                  pl.BlockSpec((1, 1, N), _bsel(shift)),
                  pl.BlockSpec((1, 1, N), _bsel(scale))],
        out_specs=[row, row],
        out_shape=[jax.ShapeDtypeStruct((Bn, T, N), F32), jax.ShapeDtypeStruct((Bn, T, N), BF16)],
        compiler_params=_params("parallel", "parallel"),
        name=name,
    )(a, w, res, gate, g[None], shift, scale)


def _ffn_up_kernel(h_ref, wa_ref, wb_ref, o_ref, w_scr, *, rows):
    @pl.when((pl.program_id(1) == 0) & (pl.program_id(2) == 0))
    def _():
        w_scr[0] = wa_ref[0].astype(BF16)
        w_scr[1] = wb_ref[0].astype(BF16)

    for r0 in range(0, h_ref.shape[1], rows):
        rs = slice(r0, r0 + rows)
        h = h_ref[0, rs, :]
        a = _dot(h, w_scr[0])
        o_ref[0, rs, :] = (a * _sigmoid(a) * _dot(h, w_scr[1])).astype(o_ref.dtype)


def _ffn_up(h, w13, layer, name):
    Bn, T, D = h.shape
    F = w13.shape[2] // 2
    tm = _tile(T, 2048, 16)
    rows = _tile(tm, 512, 16)
    tn = _tile(F, 512, LANES)
    nj = F // tn
    return pl.pallas_call(
        functools.partial(_ffn_up_kernel, rows=rows),
        grid=(nj, Bn, T // tm),
        in_specs=[pl.BlockSpec((1, tm, D), lambda j, b, i: (b, i, 0)),
                  pl.BlockSpec((1, D, tn), lambda j, b, i: (layer, 0, j)),
                  pl.BlockSpec((1, D, tn), lambda j, b, i: (layer, 0, j + nj))],
        out_specs=pl.BlockSpec((1, tm, tn), lambda j, b, i: (b, i, j)),
        out_shape=jax.ShapeDtypeStruct((Bn, T, F), BF16),
        scratch_shapes=[pltpu.VMEM((2, D, tn), BF16)],
        compiler_params=_params("arbitrary", "arbitrary", "arbitrary"),
        name=name,
    )(h, w13, w13)


def _swiglu_combine(a, b):
    return (a * _sigmoid(a) * b,)


def _conv_in_combine(gb, gc, u):
    return gb, gc * u


def _norm_mm_kernel(x_ref, g_ref, sh_ref, sc_ref, *rest, nw, combine):
    w_refs, out_refs, h_scr = rest[:nw], rest[nw:-1], rest[-1]

    @pl.when(pl.program_id(2) == 0)
    def _():
        h_scr[...] = _norm_mod(x_ref[0], g_ref[...], sh_ref[0], sc_ref[0]).astype(BF16)

    h = h_scr[...]
    outs = combine(*[_dot(h, w_ref[...]) for w_ref in w_refs])
    for o_ref, o in zip(out_refs, outs):
        o_ref[0] = o.astype(o_ref.dtype)


def _norm_mm(x, g, shift, scale, w, nw, combine, out_dtypes, name):
    Bn, T, D = x.shape
    N = w.shape[1] // nw
    tm = _tile(T, 1024, 16)
    tn = _tile(N, 512, LANES)
    nj = N // tn
    w_specs = [pl.BlockSpec((D, tn), functools.partial(lambda b, i, j, m: (0, j + m * nj), m=m))
               for m in range(nw)]
    return pl.pallas_call(
        functools.partial(_norm_mm_kernel, nw=nw, combine=combine),
        grid=(Bn, T // tm, nj),
        in_specs=[pl.BlockSpec((1, tm, D), lambda b, i, j: (b, i, 0)),
                  pl.BlockSpec((1, D), lambda b, i, j: (0, 0)),
                  pl.BlockSpec((1, 1, D), _bsel(shift)),
                  pl.BlockSpec((1, 1, D), _bsel(scale))] + w_specs,
        out_specs=[pl.BlockSpec((1, tm, tn), lambda b, i, j: (b, i, j))] * len(out_dtypes),
        out_shape=[jax.ShapeDtypeStruct((Bn, T, N), dt) for dt in out_dtypes],
        scratch_shapes=[pltpu.VMEM((tm, D), BF16)],
        compiler_params=_params("parallel", "parallel", "arbitrary"),
        name=name,
    )(x, g[None], shift, scale, *([w] * nw))


def _seg_ones(width):
    shift = HEAD.bit_length() - 1
    r = lax.shift_right_logical(lax.broadcasted_iota(jnp.int32, (width, width), 0), shift)
    c = lax.shift_right_logical(lax.broadcasted_iota(jnp.int32, (width, width), 1), shift)
    return (r == c).astype(F32)


def _split3(x):
    hi = x.astype(BF16)
    r1 = x - hi.astype(F32)
    mid = r1.astype(BF16)
    lo = (r1 - mid.astype(F32)).astype(BF16)
    return hi, mid, lo


def _wkv_kernel(r_ref, k_ref, v_ref, tw_ref, aw_ref, w2_ref, a2_ref, w0_ref, a0_ref, kk_ref, ka_ref,
                s0_ref, y_ref, sout_ref, s_scr, x_scr, r2_scr, bv_scr, vb_scr, z_scr, wt_scr,
                *, reverse, npair):
    c = pl.program_id(2)
    C = CHUNK
    PW = 2 * HEAD

    @pl.when(c == 0)
    def _():
        s_scr[...] = s0_ref[0]
        x_scr[...] = jnp.zeros_like(x_scr)
        r2_scr[...] = jnp.zeros_like(r2_scr)
        bv_scr[...] = jnp.zeros_like(bv_scr)
        vb_scr[...] = jnp.zeros_like(vb_scr)
        z_scr[...] = jnp.zeros_like(z_scr)
        wt_scr[...] = jnp.ones_like(wt_scr)

    rr = lax.shift_right_logical(lax.broadcasted_iota(jnp.int32, (PW, PW), 0), HEAD.bit_length() - 1)
    cc = lax.shift_right_logical(lax.broadcasted_iota(jnp.int32, (PW, PW), 1), HEAD.bit_length() - 1)
    same = rr == cc
    same_bf = same.astype(BF16)

    def bd(x):
        xb = x.astype(BF16)
        return jnp.concatenate([xb, xb], axis=0) * same_bf

    t2 = lax.broadcasted_iota(jnp.int32, (C, PW), 0)
    s2 = lax.broadcasted_iota(jnp.int32, (C, PW), 1) & (HEAD - 1)
    before = (s2 > t2) if reverse else (s2 < t2)
    upto = before | (s2 == t2)
    pairs = range(npair)
    sls = [slice(p * PW, (p + 1) * PW) for p in pairs]

    r, k, v = r_ref[0], k_ref[0], v_ref[0]
    z = w0_ref[...] + _dot(tw_ref[0], w2_ref[...])
    a_pre = a0_ref[...] + _dot(aw_ref[0], a2_ref[...])

    S = [s_scr[p] for p in pairs]
    X = [x_scr[p] for p in pairs]
    G = [_dot_t(X[p], r2_scr[p], 1, 1) for p in pairs]
    XS = [_dot_t(X[p], S[p].astype(BF16), 1, 1) for p in pairs]
    u = [XS[p][:C] + _dot(jnp.where(before, G[p][:C, PW:], 0.0).astype(BF16), bv_scr[p]) for p in pairs]
    P = [jnp.where(before, G[p][:C, :PW], 0.0) for p in pairs]

    lw = -math.exp(-0.5) * _sigmoid(z)
    kkv = k * kk_ref[...]
    kk2 = kkv * kkv
    t_i = lax.broadcasted_iota(jnp.int32, (C, C), 0)
    s_i = lax.broadcasted_iota(jnp.int32, (C, C), 1)
    tri = ((s_i >= t_i) if reverse else (s_i <= t_i)).astype(BF16)
    cum = _dot(jnp.concatenate([tri, tri], axis=1),
               jnp.concatenate(_split3(lw)[:2], axis=0))
    same2 = jnp.concatenate([same_bf, same_bf], axis=0)
    ss = [_dot(jnp.concatenate(_split3(kk2[:, sl])[:2], axis=1), same2) for sl in sls]

    n_sq = C.bit_length() - 1
    for j in range(n_sq):
        Pb = [P[p].astype(BF16) for p in pairs]
        if j < n_sq - 1:
            PU = [_dot(Pb[p], jnp.concatenate([bd(P[p]), bd(u[p])], axis=1)) for p in pairs]
            P = [PU[p][:, :PW] for p in pairs]
            u = [u[p] + PU[p][:, PW:] for p in pairs]
        else:
            u = [u[p] + _dot(Pb[p], bd(u[p])) for p in pairs]
    for p in pairs:
        R = jnp.concatenate([jnp.where(upto, G[p][C:, :PW], 0.0),
                             jnp.where(upto, G[p][C:, PW:], 0.0)], axis=1).astype(BF16)
        y_ref[0, :, sls[p]] = XS[p][C:] + _dot(R, jnp.concatenate([bd(u[p]), bv_scr[p]], axis=0))
    for p in pairs:
        UV = jnp.concatenate([u[p].astype(BF16), vb_scr[p]], axis=0)
        dS = _dot_t(UV, z_scr[p], 0, 0)
        s_scr[p] = (S[p] + jnp.where(same, dS, 0.0)) * wt_scr[p, 0:1, :]

    a_sig = _sigmoid(a_pre)
    kd = k * (1.0 + (a_sig - 1.0) * ka_ref[...])
    e_pos = jnp.exp(cum)
    e_neg = jnp.exp(-cum)
    e_prev = jnp.exp(cum - lw)
    last = 0 if reverse else C - 1
    for p, sl in zip(pairs, sls):
        kkn = kkv[:, sl] * lax.rsqrt(jnp.maximum(ss[p], 1e-24))
        at = (-kkn) * e_prev[:, sl]
        bt = (kkn * a_sig[:, sl]) * e_neg[:, sl]
        rt = r[:, sl] * e_pos[:, sl]
        kt = kd[:, sl] * e_neg[:, sl]
        x_scr[p] = jnp.concatenate([at, rt], axis=0).astype(BF16)
        r2_scr[p] = jnp.concatenate([bd(bt), bd(kt)], axis=0)
        bv_scr[p] = bd(v[:, sl])
        vb_scr[p] = v[:, sl].astype(BF16)
        z_scr[p] = jnp.concatenate([bt, kt], axis=0).astype(BF16)
        wt_scr[p] = jnp.broadcast_to(e_pos[last:last + 1, sl], wt_scr.shape[1:])

    @pl.when(c == pl.num_programs(2) - 1)
    def _():
        sout_ref[0] = s_scr[...]


def _wkv(r, k, v, tw, aw, w2p, a2p, w0, a0, kk, ka, s0, d):
    Bn, T, D = r.shape
    PW = 2 * HEAD
    npairs = D // PW
    npair = _tile(npairs, 16, 1)
    hw = npair * PW
    nc = T // CHUNK
    reverse = d == 1
    pos = (lambda j: nc - 1 - j) if reverse else (lambda j: j)
    cin = lambda c: pos(jnp.minimum(c, nc - 1))
    cout = lambda c: pos(jnp.maximum(c - 1, 0))
    tok = pl.BlockSpec((1, CHUNK, hw), lambda b, g, c: (b, cin(c), g))
    lora = pl.BlockSpec((1, CHUNK, LANES), lambda b, g, c: (b, cin(c), d))
    lw2 = pl.BlockSpec((LANES, hw), lambda b, g, c: (0, g))
    vec = pl.BlockSpec((1, hw), lambda b, g, c: (0, g))
    st = pl.BlockSpec((1, npair, PW, PW), lambda b, g, c: (b, g, 0, 0))
    return pl.pallas_call(
        functools.partial(_wkv_kernel, reverse=reverse, npair=npair),
        grid=(Bn, npairs // npair, nc + 1),
        in_specs=[tok, tok, tok, lora, lora, lw2, lw2, vec, vec, vec, vec, st],
        out_specs=[pl.BlockSpec((1, CHUNK, hw), lambda b, g, c: (b, cout(c), g)), st],
        out_shape=[jax.ShapeDtypeStruct((Bn, T, D), F32),
                   jax.ShapeDtypeStruct((Bn, npairs, PW, PW), F32)],
        scratch_shapes=[pltpu.VMEM((npair, PW, PW), F32),
                        pltpu.VMEM((npair, 2 * CHUNK, PW), BF16),
                        pltpu.VMEM((npair, 2 * PW, PW), BF16),
                        pltpu.VMEM((npair, PW, PW), BF16),
                        pltpu.VMEM((npair, CHUNK, PW), BF16),
                        pltpu.VMEM((npair, 2 * CHUNK, PW), BF16),
                        pltpu.VMEM((npair, 8, PW), F32)],
        compiler_params=_params("parallel", "parallel", "arbitrary"),
        name="wkv_rev" if reverse else "wkv_fwd",
    )(r, k, v, tw, aw, w2p, a2p, w0, a0, kk, ka, s0)


def _readout_kernel(yf_ref, yb_ref, r_ref, k_ref, v_ref, gs_ref, aw_ref, a2f_ref, a2b_ref, g2_ref,
                    a0_ref, ka_ref, rk_ref, lnw_ref, lnb_ref, o_ref):
    tn = o_ref.shape[-1]
    PW = 2 * HEAD
    same = _seg_ones(PW).astype(BF16)
    same2 = jnp.concatenate([same, same], axis=0)

    def head_sum(x, pieces):
        if pieces == 1:
            return _dot(x.astype(BF16), same)
        return _dot(jnp.concatenate(_split3(x)[:2], axis=1), same2)

    aw = aw_ref[0]
    a_f = _sigmoid(a0_ref[0:1, :] + _dot(aw[:, :LANES], a2f_ref[...]))
    a_b = _sigmoid(a0_ref[1:2, :] + _dot(aw[:, LANES:], a2b_ref[...]))
    g = _dot(gs_ref[0], g2_ref[...])
    for p in range(tn // PW):
        sl = slice(p * PW, (p + 1) * PW)
        ksum = k_ref[0, :, sl] * (2.0 + (a_f[:, sl] + a_b[:, sl] - 2.0) * ka_ref[:, sl])
        y = yf_ref[0, :, sl] + yb_ref[0, :, sl]
        yc = y - head_sum(y, 2) * (1.0 / HEAD)
        var = head_sum(yc * yc, 1) * (1.0 / HEAD)
        o = yc * lax.rsqrt(var + GN_EPS) * lnw_ref[:, sl] + lnb_ref[:, sl]
        bonus = head_sum(r_ref[0, :, sl] * ksum * rk_ref[:, sl], 1) * v_ref[0, :, sl]
        o_ref[0, :, sl] = ((o + bonus) * g[:, sl]).astype(o_ref.dtype)


def _readout(yf, yb, r, k, v, gs, aw, a2fp, a2bp, g2, a0, ka, rk, lnw, lnb):
    Bn, T, D = r.shape
    tm = _tile(T, 512, 16)
    tn = _tile(D, 512, LANES)
    G = gs.shape[-1]
    tok = pl.BlockSpec((1, tm, tn), lambda b, i, j: (b, i, j))
    vec = pl.BlockSpec((1, tn), lambda b, i, j: (0, j))
    return pl.pallas_call(
        _readout_kernel,
        grid=(Bn, T // tm, D // tn),
        in_specs=[tok, tok, tok, tok, tok,
                  pl.BlockSpec((1, tm, G), lambda b, i, j: (b, i, 0)),
                  pl.BlockSpec((1, tm, 2 * LANES), lambda b, i, j: (b, i, 0)),
                  pl.BlockSpec((LANES, tn), lambda b, i, j: (0, j)),
                  pl.BlockSpec((LANES, tn), lambda b, i, j: (0, j)),
                  pl.BlockSpec((G, tn), lambda b, i, j: (0, j)),
                  pl.BlockSpec((2, tn), lambda b, i, j: (0, j)),
                  vec, vec, vec, vec],
        out_specs=tok,
        out_shape=jax.ShapeDtypeStruct((Bn, T, D), BF16),
        compiler_params=_params("parallel", "parallel", "parallel"),
        name="rwkv_readout",
    )(yf, yb, r, k, v, gs, aw, a2fp, a2bp, g2, a0, ka, rk, lnw, lnb)


def _conv_kernel(gb_ref, z_ref, cw_ref, o_ref):
    z = z_ref[0].astype(F32)
    T = z.shape[0]
    t = lax.broadcasted_iota(jnp.int32, (T, 1), 0)
    zp = jnp.where(t == 0, 0.0, pltpu.roll(z, 1, 0))
    zn = jnp.where(t == T - 1, 0.0, pltpu.roll(z, T - 1, 0))
    conv = zp * cw_ref[0:1, :] + z * cw_ref[1:2, :] + zn * cw_ref[2:3, :]
    o_ref[0] = (gb_ref[0] * conv).astype(o_ref.dtype)


def _conv(gb, z, cw):
    Bn, T, D = z.shape
    tn = _tile(D, 256, LANES)
    tok = pl.BlockSpec((1, T, tn), lambda b, j: (b, 0, j))
    return pl.pallas_call(
        _conv_kernel,
        grid=(Bn, D // tn),
        in_specs=[tok, tok, pl.BlockSpec((3, tn), lambda b, j: (0, j))],
        out_specs=tok,
        out_shape=jax.ShapeDtypeStruct((Bn, T, D), BF16),
        compiler_params=_params("parallel", "parallel"),
        name="short_conv",
    )(gb, z, cw)


def _pad_rows(w, rows):
    return jnp.pad(w, ((0, rows - w.shape[0]), (0, 0)))


def _pad_cols(w, cols):
    return jnp.pad(w, ((0, 0), (0, cols - w.shape[1])))


def _split_mod(mod_rows, D):
    return [mod_rows[:, m * D:(m + 1) * D][:, None, :] for m in range(6)]


def _mixer_out_ffn(a, w_o, tok, mods, g2n, ffn, tag):
    w13, layer, wdn = ffn
    t1, h2 = _mm_res_norm(a, w_o, tok, mods[2], g2n, mods[3], mods[4], name="mixer_out_" + tag)
    act = _ffn_up(h2, w13, layer, name="ffn_up_" + tag)
    return _mm_res(act, wdn, t1, mods[5], name="ffn_down_" + tag)


def _rwkv_layer(x, ctx, mods_x, mods_c, g1, g2n, mix, wr, wk, wv, wo, w0, w1, w2, a0, a1, a2,
                lg1, lg2, k_k, k_a, r_k, ln_w, ln_b, ffn):
    D = x.shape[-1]
    H = D // HEAD
    w1cat = jnp.concatenate([_pad_cols(w1[0], LANES), _pad_cols(w1[1], LANES)], axis=1).astype(BF16)
    a1cat = jnp.concatenate([_pad_cols(a1[0], LANES), _pad_cols(a1[1], LANES)], axis=1).astype(BF16)
    w2p = [_pad_rows(w2[d], LANES).astype(BF16) for d in range(2)]
    a2p = [_pad_rows(a2[d], LANES).astype(BF16) for d in range(2)]
    wr, wk, wv, wo = (t.astype(BF16) for t in (wr, wk, wv, wo))
    lg1, lg2 = lg1.astype(BF16), lg2.astype(BF16)
    rk = r_k.reshape(1, D)

    sets = {}
    for tag, tok, mods, prep in (("c", ctx, mods_c, _prep_ctx), ("x", x, mods_x, _prep_latent)):
        xr, xw, xk, xv, xa, xg = prep(tok, g1, mods[0], mods[1], mix)
        sets[tag] = dict(
            r=_mm(xr, wr, F32, name="proj_r_" + tag),
            k=_mm(xk, wk, F32, name="proj_k_" + tag),
            v=_mm(xv, wv, F32, name="proj_v_" + tag),
            tw=_mm(xw, w1cat, BF16, act="tanh", name="lora_w_" + tag),
            aw=_mm(xa, a1cat, BF16, name="lora_a_" + tag),
            gs=_mm(xg, lg1, BF16, act="sigmoid", name="lora_g_" + tag))

    ys = {"c": [], "x": []}
    zero_state = jnp.zeros((x.shape[0], H // 2, 2 * HEAD, 2 * HEAD), F32)
    for d in range(2):
        state = zero_state
        for tag in ("c", "x"):
            s = sets[tag]
            y, state = _wkv(s["r"], s["k"], s["v"], s["tw"], s["aw"], w2p[d], a2p[d],
                            w0[d][None], a0[d][None], k_k[None], k_a[None], state, d)
            ys[tag].append(y)

    outs = []
    for tag, tok, mods in (("c", ctx, mods_c), ("x", x, mods_x)):
        s = sets[tag]
        og = _readout(ys[tag][0], ys[tag][1], s["r"], s["k"], s["v"], s["gs"], s["aw"],
                      a2p[0], a2p[1], lg2, a0, k_a[None], rk, ln_w[None], ln_b[None])
        outs.append(_mixer_out_ffn(og, wo, tok, mods, g2n, ffn, tag))
    return outs[1], outs[0]


def _conv_layer(x, mods, g1, g2n, w_in, conv_w, w_out, ffn):
    gb, z = _norm_mm(x, g1, mods[0], mods[1], w_in.astype(BF16), 3, _conv_in_combine, (BF16, BF16),
                     name="conv_in")
    p = _conv(gb, z, conv_w)
    return _mixer_out_ffn(p, w_out.astype(BF16), x, mods, g2n, ffn, "x")


def kernel(x, c, ctx, c_ctx, norm1_g, norm2_g, ada_w, ada_b, rw_mix, rw_wr, rw_wk, rw_wv, rw_wo,
           rw_w0, rw_w1, rw_w2, rw_a0, rw_a1, rw_a2, rw_g1, rw_g2, rw_kk, rw_ka, rw_rk, rw_lnw,
           rw_lnb, sc_win, sc_conv, sc_wout, ffn_w13, ffn_w2, final_g):
    B, T, D = x.shape
    depth = norm1_g.shape[0]
    rows = -(-(B + 1) // 8) * 8
    cond = jnp.zeros((rows, D), F32).at[:B].set(c).at[B].set(c_ctx)
    for i in range(depth):
        last = i == depth - 1
        j = i // 2
        mod = _ada(cond, ada_w, ada_b, i)
        mods_x = _split_mod(mod[:B], D)
        mods_c = _split_mod(mod[B:B + 1], D)
        ffn = (ffn_w13, i, ffn_w2[i].astype(BF16))
        if i % 2 == 0:
            x, ctx_new = _rwkv_layer(
                x, ctx, mods_x, mods_c, norm1_g[i], norm2_g[i], rw_mix[j], rw_wr[j], rw_wk[j],
                rw_wv[j], rw_wo[j], rw_w0[j], rw_w1[j], rw_w2[j], rw_a0[j], rw_a1[j], rw_a2[j],
                rw_g1[j], rw_g2[j], rw_kk[j], rw_ka[j], rw_rk[j], rw_lnw[j], rw_lnb[j], ffn)
            ctx = ctx_new
        else:
            x = _conv_layer(x, mods_x, norm1_g[i], norm2_g[i], sc_win[j], sc_conv[j], sc_wout[j],
                            ffn)
            if not last:
                ctx = _conv_layer(ctx, mods_c, norm1_g[i], norm2_g[i], sc_win[j], sc_conv[j],
                                  sc_wout[j], ffn)
    zeros = jnp.zeros((1, 1, D), F32)
    return _norm(x, final_g, zeros, zeros)
```

```python
import functools
import math

import jax
import jax.numpy as jnp
from jax import lax
from jax.experimental import pallas as pl
from jax.experimental.pallas import tpu as pltpu

HEAD = 64
GRID_W = 64
CHUNK = 64
NORM_EPS = 1e-6
GN_EPS = 64e-5
LANES = 128
VMEM_LIMIT = 56 * 1024 * 1024

F32 = jnp.float32
BF16 = jnp.bfloat16
HI = lax.Precision.HIGHEST


def _params(*sem):
    return pltpu.CompilerParams(dimension_semantics=sem, vmem_limit_bytes=VMEM_LIMIT)


def _tile(n, pref, mult):
    t = min(pref, n)
    t -= t % mult
    while t >= mult:
        if n % t == 0:
            return t
        t -= mult
    return n


def _sigmoid(x):
    return 1.0 / (1.0 + jnp.exp(-x))


def _norm_mod(x, g, shift, scale, cols=None):
    rs = lax.rsqrt(jnp.mean(x * x, axis=-1, keepdims=True) + NORM_EPS)
    if cols is not None:
        x, g, shift, scale = (t[:, cols[0]:cols[1]] for t in (x, g, shift, scale))
    return (x * rs) * (g * (1.0 + scale)) + shift


def _dot(a, b, precision=None):
    return jnp.dot(a, b, preferred_element_type=F32, precision=precision)


def _dot_t(a, b, ca, cb, precision=None):
    return lax.dot_general(a, b, (((ca,), (cb,)), ((), ())), preferred_element_type=F32,
                           precision=precision)


def _ada_kernel(c_ref, w_ref, b_ref, o_ref):
    c = c_ref[...]
    s = c * _sigmoid(c)
    o_ref[...] = _dot(s.astype(BF16), w_ref[0].astype(BF16)) + b_ref[0]


def _ada(cond, w, b, layer):
    R, D = cond.shape
    N = w.shape[2]
    tn = _tile(N, 1024, LANES)
    return pl.pallas_call(
        _ada_kernel,
        grid=(N // tn,),
        in_specs=[pl.BlockSpec((R, D), lambda j: (0, 0)),
                  pl.BlockSpec((1, D, tn), lambda j: (layer, 0, j)),
                  pl.BlockSpec((1, 1, tn), lambda j: (layer, 0, j))],
        out_specs=pl.BlockSpec((R, tn), lambda j: (0, j)),
        out_shape=jax.ShapeDtypeStruct((R, N), F32),
        compiler_params=_params("parallel"),
        name="ada_mod",
    )(cond, w, b[:, None, :])


def _norm_kernel(x_ref, g_ref, sh_ref, sc_ref, o_ref):
    o_ref[0] = _norm_mod(x_ref[0], g_ref[...], sh_ref[0], sc_ref[0])


def _bsel(arr):
    if arr.shape[0] == 1:
        return lambda b, *_: (0, 0, 0)
    return lambda b, *_: (b, 0, 0)


def _norm(x, g, shift, scale):
    Bn, T, D = x.shape
    tm = _tile(T, 512, 8)
    return pl.pallas_call(
        _norm_kernel,
        grid=(Bn, T // tm),
        in_specs=[pl.BlockSpec((1, tm, D), lambda b, i: (b, i, 0)),
                  pl.BlockSpec((1, D), lambda b, i: (0, 0)),
                  pl.BlockSpec((1, 1, D), _bsel(shift)),
                  pl.BlockSpec((1, 1, D), _bsel(scale))],
        out_specs=pl.BlockSpec((1, tm, D), lambda b, i: (b, i, 0)),
        out_shape=jax.ShapeDtypeStruct((Bn, T, D), F32),
        compiler_params=_params("parallel", "parallel"),
        name="norm",
    )(x, g[None], shift, scale)


def _write_mix(out_refs, mix_ref, h, shifted, c0, c1):
    xx = shifted - h
    for m, o_ref in enumerate(out_refs):
        o_ref[0, :, c0:c1] = (h + xx * mix_ref[m:m + 1, c0:c1]).astype(o_ref.dtype)


def _prep_latent_kernel(x_ref, xu_ref, xd_ref, g_ref, sh_ref, sc_ref, mix_ref, *rest):
    out_refs, h_scr = rest[:-1], rest[-1]
    i = pl.program_id(1)
    n = pl.num_programs(1)
    g, sh, sc = g_ref[...], sh_ref[0], sc_ref[0]
    tm, D = x_ref.shape[1:]
    q = D // 4
    W = GRID_W
    h_scr[0:W, 2 * q:3 * q] = jnp.where(i > 0, _norm_mod(xu_ref[0], g, sh, sc, (2 * q, 3 * q)), 0.0)
    h_scr[W - 8:W, 0:q] = jnp.zeros((8, q), F32)
    for r0 in range(0, tm, W):
        h_scr[W + r0:2 * W + r0, :] = _norm_mod(x_ref[0, r0:r0 + W, :], g, sh, sc)
    h_scr[W + tm:, 3 * q:] = jnp.where(i < n - 1, _norm_mod(xd_ref[0], g, sh, sc, (3 * q, D)), 0.0)
    h_scr[W + tm:W + tm + 8, q:2 * q] = jnp.zeros((8, q), F32)
    R = 32
    row = lax.broadcasted_iota(jnp.int32, (R, 1), 0)
    for k, off in enumerate((-1, 1, -W, W)):
        c0, c1 = k * q, (k + 1) * q
        mixk = [mix_ref[m:m + 1, c0:c1] for m in range(len(out_refs))]
        for r0 in range(0, tm, R):
            h = h_scr[W + r0:W + r0 + R, c0:c1]
            s = h_scr[W + r0 + off:W + r0 + off + R, c0:c1]
            if off == -1 and r0 % W == 0:
                s = jnp.where(row == 0, 0.0, s)
            if off == 1 and (r0 + R) % W == 0:
                s = jnp.where(row == R - 1, 0.0, s)
            xx = s - h
            for m, o_ref in enumerate(out_refs):
                o_ref[0, r0:r0 + R, c0:c1] = (h + xx * mixk[m]).astype(o_ref.dtype)


def _prep_latent(x, g, shift, scale, mix):
    Bn, T, D = x.shape
    rows_per_tile = _tile(T // GRID_W, 4, 1)
    tm = rows_per_tile * GRID_W
    nrow = T // GRID_W
    return pl.pallas_call(
        _prep_latent_kernel,
        grid=(Bn, T // tm),
        in_specs=[pl.BlockSpec((1, tm, D), lambda b, i: (b, i, 0)),
                  pl.BlockSpec((1, GRID_W, D),
                               lambda b, i: (b, jnp.maximum(i * rows_per_tile - 1, 0), 0)),
                  pl.BlockSpec((1, GRID_W, D),
                               lambda b, i: (b, jnp.minimum((i + 1) * rows_per_tile, nrow - 1), 0)),
                  pl.BlockSpec((1, D), lambda b, i: (0, 0)),
                  pl.BlockSpec((1, 1, D), _bsel(shift)),
                  pl.BlockSpec((1, 1, D), _bsel(scale)),
                  pl.BlockSpec((6, D), lambda b, i: (0, 0))],
        out_specs=[pl.BlockSpec((1, tm, D), lambda b, i: (b, i, 0))] * 6,
        out_shape=[jax.ShapeDtypeStruct((Bn, T, D), BF16)] * 6,
        scratch_shapes=[pltpu.VMEM((tm + 2 * GRID_W, D), F32)],
        compiler_params=_params("parallel", "parallel"),
        name="prep_latent",
    )(x, x, x, g[None], shift, scale, mix)


def _prep_ctx_kernel(x_ref, g_ref, sh_ref, sc_ref, mix_ref, *out_refs):
    h = _norm_mod(x_ref[0], g_ref[...], sh_ref[0], sc_ref[0])
    L, D = h.shape
    half = D // 2
    t = lax.broadcasted_iota(jnp.int32, (L, 1), 0)
    h0, h1 = h[:, :half], h[:, half:]
    prev = jnp.where(t == 0, 0.0, pltpu.roll(h0, 1, 0))
    nxt = jnp.where(t == L - 1, 0.0, pltpu.roll(h1, L - 1, 0))
    _write_mix(out_refs, mix_ref, h0, prev, 0, half)
    _write_mix(out_refs, mix_ref, h1, nxt, half, D)


def _prep_ctx(x, g, shift, scale, mix):
    Bn, L, D = x.shape
    return pl.pallas_call(
        _prep_ctx_kernel,
        grid=(Bn,),
        in_specs=[pl.BlockSpec((1, L, D), lambda b: (b, 0, 0)),
                  pl.BlockSpec((1, D), lambda b: (0, 0)),
                  pl.BlockSpec((1, 1, D), _bsel(shift)),
                  pl.BlockSpec((1, 1, D), _bsel(scale)),
                  pl.BlockSpec((6, D), lambda b: (0, 0))],
        out_specs=[pl.BlockSpec((1, L, D), lambda b: (b, 0, 0))] * 6,
        out_shape=[jax.ShapeDtypeStruct((Bn, L, D), BF16)] * 6,
        compiler_params=_params("parallel"),
        name="prep_ctx",
    )(x, g[None], shift, scale, mix)


def _mm_kernel(a_ref, w_ref, o_ref, *, act):
    acc = _dot(a_ref[0], w_ref[...])
    if act == "tanh":
        acc = jnp.tanh(acc)
    elif act == "sigmoid":
        acc = _sigmoid(acc)
    o_ref[0] = acc.astype(o_ref.dtype)


def _mm(a, w, out_dtype, act=None, name="mm"):
    Bn, T, K = a.shape
    N = w.shape[1]
    tm = _tile(T, 1024, 16)
    tn = _tile(N, 1024, LANES)
    return pl.pallas_call(
        functools.partial(_mm_kernel, act=act),
        grid=(Bn, T // tm, N // tn),
        in_specs=[pl.BlockSpec((1, tm, K), lambda b, i, j: (b, i, 0)),
                  pl.BlockSpec((K, tn), lambda b, i, j: (0, j))],
        out_specs=pl.BlockSpec((1, tm, tn), lambda b, i, j: (b, i, j)),
        out_shape=jax.ShapeDtypeStruct((Bn, T, N), out_dtype),
        compiler_params=_params("parallel", "parallel", "parallel"),
        name=name,
    )(a, w)


def _mm_res_kernel(a_ref, w_ref, res_ref, gate_ref, o_ref):
    o_ref[0] = res_ref[0] + gate_ref[0] * _dot(a_ref[0], w_ref[...])


def _mm_res(a, w, res, gate, name="mm_res"):
    Bn, T, K = a.shape
    N = w.shape[1]
    tm = _tile(T, 1024, 16)
    tn = _tile(N, 1024 if K <= 2048 else 512, LANES)
    gsel = _bsel(gate)
    return pl.pallas_call(
        _mm_res_kernel,
        grid=(Bn, T // tm, N // tn),
        in_specs=[pl.BlockSpec((1, tm, K), lambda b, i, j: (b, i, 0)),
                  pl.BlockSpec((K, tn), lambda b, i, j: (0, j)),
                  pl.BlockSpec((1, tm, tn), lambda b, i, j: (b, i, j)),
                  pl.BlockSpec((1, 1, tn), lambda b, i, j: gsel(b)[:2] + (j,))],
        out_specs=pl.BlockSpec((1, tm, tn), lambda b, i, j: (b, i, j)),
        out_shape=jax.ShapeDtypeStruct((Bn, T, N), F32),
        compiler_params=_params("parallel", "parallel", "parallel"),
        name=name,
    )(a, w, res, gate)


def _mm_res_norm_kernel(a_ref, w_ref, res_ref, gate_ref, g_ref, sh_ref, sc_ref, o_ref, h_ref, *, rows):
    for r0 in range(0, a_ref.shape[1], rows):
        rs = slice(r0, r0 + rows)
        x1 = res_ref[0, rs, :] + gate_ref[0] * _dot(a_ref[0, rs, :], w_ref[...])
        o_ref[0, rs, :] = x1
        h_ref[0, rs, :] = _norm_mod(x1, g_ref[...], sh_ref[0], sc_ref[0]).astype(h_ref.dtype)


def _mm_res_norm(a, w, res, gate, g, shift, scale, name):
    Bn, T, K = a.shape
    N = w.shape[1]
    tm = _tile(T, 512, 16)
    rows = _tile(tm, 256, 16)
    row = pl.BlockSpec((1, tm, N), lambda b, i: (b, i, 0))
    return pl.pallas_call(
        functools.partial(_mm_res_norm_kernel, rows=rows),
        grid=(Bn, T // tm),
        in_specs=[pl.BlockSpec((1, tm, K), lambda b, i: (b, i, 0)),
                  pl.BlockSpec((K, N), lambda b, i: (0, 0)),
                  row,
                  pl.BlockSpec((1, 1, N), _bsel(gate)),
                  pl.BlockSpec((1, N), lambda b, i: (0, 0)),
                  pl.BlockSpec((1, 1, N), _bsel(shift)),
                  pl.BlockSpec((1, 1, N), _bsel(scale))],
        out_specs=[row, row],
        out_shape=[jax.ShapeDtypeStruct((Bn, T, N), F32), jax.ShapeDtypeStruct((Bn, T, N), BF16)],
        compiler_params=_params("parallel", "parallel"),
        name=name,
    )(a, w, res, gate, g[None], shift, scale)


def _ffn_up_kernel(h_ref, wa_ref, wb_ref, o_ref, w_scr, *, rows):
    @pl.when((pl.program_id(1) == 0) & (pl.program_id(2) == 0))
    def _():
        w_scr[0] = wa_ref[0].astype(BF16)
        w_scr[1] = wb_ref[0].astype(BF16)

    for r0 in range(0, h_ref.shape[1], rows):
        rs = slice(r0, r0 + rows)
        h = h_ref[0, rs, :]
        a = _dot(h, w_scr[0])
        o_ref[0, rs, :] = (a * _sigmoid(a) * _dot(h, w_scr[1])).astype(o_ref.dtype)


def _ffn_up(h, w13, layer, name):
    Bn, T, D = h.shape
    F = w13.shape[2] // 2
    tm = _tile(T, 2048, 16)
    rows = _tile(tm, 512, 16)
    tn = _tile(F, 512, LANES)
    nj = F // tn
    return pl.pallas_call(
        functools.partial(_ffn_up_kernel, rows=rows),
        grid=(nj, Bn, T // tm),
        in_specs=[pl.BlockSpec((1, tm, D), lambda j, b, i: (b, i, 0)),
                  pl.BlockSpec((1, D, tn), lambda j, b, i: (layer, 0, j)),
                  pl.BlockSpec((1, D, tn), lambda j, b, i: (layer, 0, j + nj))],
        out_specs=pl.BlockSpec((1, tm, tn), lambda j, b, i: (b, i, j)),
        out_shape=jax.ShapeDtypeStruct((Bn, T, F), BF16),
        scratch_shapes=[pltpu.VMEM((2, D, tn), BF16)],
        compiler_params=_params("arbitrary", "arbitrary", "arbitrary"),
        name=name,
    )(h, w13, w13)


def _swiglu_combine(a, b):
    return (a * _sigmoid(a) * b,)


def _conv_in_combine(gb, gc, u):
    return gb, gc * u


def _norm_mm_kernel(x_ref, g_ref, sh_ref, sc_ref, *rest, nw, combine):
    w_refs, out_refs, h_scr = rest[:nw], rest[nw:-1], rest[-1]

    @pl.when(pl.program_id(2) == 0)
    def _():
        h_scr[...] = _norm_mod(x_ref[0], g_ref[...], sh_ref[0], sc_ref[0]).astype(BF16)

    h = h_scr[...]
    outs = combine(*[_dot(h, w_ref[...]) for w_ref in w_refs])
    for o_ref, o in zip(out_refs, outs):
        o_ref[0] = o.astype(o_ref.dtype)


def _norm_mm(x, g, shift, scale, w, nw, combine, out_dtypes, name):
    Bn, T, D = x.shape
    N = w.shape[1] // nw
    tm = _tile(T, 1024, 16)
    tn = _tile(N, 512, LANES)
    nj = N // tn
    w_specs = [pl.BlockSpec((D, tn), functools.partial(lambda b, i, j, m: (0, j + m * nj), m=m))
               for m in range(nw)]
    return pl.pallas_call(
        functools.partial(_norm_mm_kernel, nw=nw, combine=combine),
        grid=(Bn, T // tm, nj),
        in_specs=[pl.BlockSpec((1, tm, D), lambda b, i, j: (b, i, 0)),
                  pl.BlockSpec((1, D), lambda b, i, j: (0, 0)),
                  pl.BlockSpec((1, 1, D), _bsel(shift)),
                  pl.BlockSpec((1, 1, D), _bsel(scale))] + w_specs,
        out_specs=[pl.BlockSpec((1, tm, tn), lambda b, i, j: (b, i, j))] * len(out_dtypes),
        out_shape=[jax.ShapeDtypeStruct((Bn, T, N), dt) for dt in out_dtypes],
        scratch_shapes=[pltpu.VMEM((tm, D), BF16)],
        compiler_params=_params("parallel", "parallel", "arbitrary"),
        name=name,
    )(x, g[None], shift, scale, *([w] * nw))


def _seg_ones(width):
    shift = HEAD.bit_length() - 1
    r = lax.shift_right_logical(lax.broadcasted_iota(jnp.int32, (width, width), 0), shift)
    c = lax.shift_right_logical(lax.broadcasted_iota(jnp.int32, (width, width), 1), shift)
    return (r == c).astype(F32)


def _split3(x):
    hi = x.astype(BF16)
    r1 = x - hi.astype(F32)
    mid = r1.astype(BF16)
    lo = (r1 - mid.astype(F32)).astype(BF16)
    return hi, mid, lo


def _wkv_kernel(r_ref, k_ref, v_ref, tw_ref, aw_ref, w2_ref, a2_ref, w0_ref, a0_ref, kk_ref, ka_ref,
                s0_ref, y_ref, sout_ref, s_scr, x_scr, r2_scr, bv_scr, vb_scr, z_scr, wt_scr,
                *, reverse, npair):
    c = pl.program_id(2)
    C = CHUNK
    PW = 2 * HEAD

    @pl.when(c == 0)
    def _():
        s_scr[...] = s0_ref[0]
        x_scr[...] = jnp.zeros_like(x_scr)
        r2_scr[...] = jnp.zeros_like(r2_scr)
        bv_scr[...] = jnp.zeros_like(bv_scr)
        vb_scr[...] = jnp.zeros_like(vb_scr)
        z_scr[...] = jnp.zeros_like(z_scr)
        wt_scr[...] = jnp.ones_like(wt_scr)

    rr = lax.shift_right_logical(lax.broadcasted_iota(jnp.int32, (PW, PW), 0), HEAD.bit_length() - 1)
    cc = lax.shift_right_logical(lax.broadcasted_iota(jnp.int32, (PW, PW), 1), HEAD.bit_length() - 1)
    same = rr == cc
    same_bf = same.astype(BF16)

    def bd(x):
        xb = x.astype(BF16)
        return jnp.concatenate([xb, xb], axis=0) * same_bf

    t2 = lax.broadcasted_iota(jnp.int32, (C, PW), 0)
    s2 = lax.broadcasted_iota(jnp.int32, (C, PW), 1) & (HEAD - 1)
    before = (s2 > t2) if reverse else (s2 < t2)
    upto = before | (s2 == t2)
    pairs = range(npair)
    sls = [slice(p * PW, (p + 1) * PW) for p in pairs]

    r, k, v = (t[0].astype(F32) for t in (r_ref, k_ref, v_ref))
    z = w0_ref[...] + _dot(tw_ref[0], w2_ref[...])
    a_pre = a0_ref[...] + _dot(aw_ref[0], a2_ref[...])

    S = [s_scr[p] for p in pairs]
    X = [x_scr[p] for p in pairs]
    G = [_dot_t(X[p], r2_scr[p], 1, 1) for p in pairs]
    XS = [_dot_t(X[p], S[p].astype(BF16), 1, 1) for p in pairs]
    u = [XS[p][:C] + _dot(jnp.where(before, G[p][:C, PW:], 0.0).astype(BF16), bv_scr[p]) for p in pairs]
    P = [jnp.where(before, G[p][:C, :PW], 0.0) for p in pairs]

    lw = -math.exp(-0.5) * _sigmoid(z)
    kkv = k * kk_ref[...]
    kk2 = kkv * kkv
    t_i = lax.broadcasted_iota(jnp.int32, (C, C), 0)
    s_i = lax.broadcasted_iota(jnp.int32, (C, C), 1)
    tri = ((s_i >= t_i) if reverse else (s_i <= t_i)).astype(BF16)
    cum = _dot(jnp.concatenate([tri, tri], axis=1),
               jnp.concatenate(_split3(lw)[:2], axis=0))
    same2 = jnp.concatenate([same_bf, same_bf], axis=0)
    ss = [_dot(jnp.concatenate(_split3(kk2[:, sl])[:2], axis=1), same2) for sl in sls]

    n_sq = C.bit_length() - 1
    for j in range(n_sq):
        Pb = [P[p].astype(BF16) for p in pairs]
        if j < n_sq - 1:
            PU = [_dot(Pb[p], jnp.concatenate([bd(P[p]), bd(u[p])], axis=1)) for p in pairs]
            P = [PU[p][:, :PW] for p in pairs]
            u = [u[p] + PU[p][:, PW:] for p in pairs]
        else:
            u = [u[p] + _dot(Pb[p], bd(u[p])) for p in pairs]
    for p in pairs:
        R = jnp.concatenate([jnp.where(upto, G[p][C:, :PW], 0.0),
                             jnp.where(upto, G[p][C:, PW:], 0.0)], axis=1).astype(BF16)
        y_ref[0, :, sls[p]] = XS[p][C:] + _dot(R, jnp.concatenate([bd(u[p]), bv_scr[p]], axis=0))
    for p in pairs:
        UV = jnp.concatenate([u[p].astype(BF16), vb_scr[p]], axis=0)
        dS = _dot_t(UV, z_scr[p], 0, 0)
        s_scr[p] = (S[p] + jnp.where(same, dS, 0.0)) * wt_scr[p, 0:1, :]

    a_sig = _sigmoid(a_pre)
    kd = k * (1.0 + (a_sig - 1.0) * ka_ref[...])
    e_pos = jnp.exp(cum)
    e_neg = jnp.exp(-cum)
    e_prev = jnp.exp(cum - lw)
    last = 0 if reverse else C - 1
    for p, sl in zip(pairs, sls):
        kkn = kkv[:, sl] * lax.rsqrt(jnp.maximum(ss[p], 1e-24))
        at = (-kkn) * e_prev[:, sl]
        bt = (kkn * a_sig[:, sl]) * e_neg[:, sl]
        rt = r[:, sl] * e_pos[:, sl]
        kt = kd[:, sl] * e_neg[:, sl]
        x_scr[p] = jnp.concatenate([at, rt], axis=0).astype(BF16)
        r2_scr[p] = jnp.concatenate([bd(bt), bd(kt)], axis=0)
        bv_scr[p] = bd(v[:, sl])
        vb_scr[p] = v[:, sl].astype(BF16)
        z_scr[p] = jnp.concatenate([bt, kt], axis=0).astype(BF16)
        wt_scr[p] = jnp.broadcast_to(e_pos[last:last + 1, sl], wt_scr.shape[1:])

    @pl.when(c == pl.num_programs(2) - 1)
    def _():
        sout_ref[0] = s_scr[...]


def _wkv(r, k, v, tw, aw, w2p, a2p, w0, a0, kk, ka, s0, d):
    Bn, T, D = r.shape
    PW = 2 * HEAD
    npairs = D // PW
    npair = _tile(npairs, 16, 1)
    hw = npair * PW
    nc = T // CHUNK
    reverse = d == 1
    pos = (lambda j: nc - 1 - j) if reverse else (lambda j: j)
    cin = lambda c: pos(jnp.minimum(c, nc - 1))
    cout = lambda c: pos(jnp.maximum(c - 1, 0))
    tok = pl.BlockSpec((1, CHUNK, hw), lambda b, g, c: (b, cin(c), g))
    lora = pl.BlockSpec((1, CHUNK, LANES), lambda b, g, c: (b, cin(c), d))
    lw2 = pl.BlockSpec((LANES, hw), lambda b, g, c: (0, g))
    vec = pl.BlockSpec((1, hw), lambda b, g, c: (0, g))
    st = pl.BlockSpec((1, npair, PW, PW), lambda b, g, c: (b, g, 0, 0))
    return pl.pallas_call(
        functools.partial(_wkv_kernel, reverse=reverse, npair=npair),
        grid=(Bn, npairs // npair, nc + 1),
        in_specs=[tok, tok, tok, lora, lora, lw2, lw2, vec, vec, vec, vec, st],
        out_specs=[pl.BlockSpec((1, CHUNK, hw), lambda b, g, c: (b, cout(c), g)), st],
        out_shape=[jax.ShapeDtypeStruct((Bn, T, D), F32),
                   jax.ShapeDtypeStruct((Bn, npairs, PW, PW), F32)],
        scratch_shapes=[pltpu.VMEM((npair, PW, PW), F32),
                        pltpu.VMEM((npair, 2 * CHUNK, PW), BF16),
                        pltpu.VMEM((npair, 2 * PW, PW), BF16),
                        pltpu.VMEM((npair, PW, PW), BF16),
                        pltpu.VMEM((npair, CHUNK, PW), BF16),
                        pltpu.VMEM((npair, 2 * CHUNK, PW), BF16),
                        pltpu.VMEM((npair, 8, PW), F32)],
        compiler_params=_params("parallel", "parallel", "arbitrary"),
        name="wkv_rev" if reverse else "wkv_fwd",
    )(r, k, v, tw, aw, w2p, a2p, w0, a0, kk, ka, s0)


def _rwkv_out_kernel(yf_ref, yb_ref, r_ref, k_ref, v_ref, gs_ref, aw_ref, res_ref, a2f_ref, a2b_ref,
                     g2_ref, wo_ref, a0_ref, ka_ref, rk_ref, lnw_ref, lnb_ref, gate_ref, g_ref, sh_ref,
                     sc_ref, o_ref, h_ref, og_scr, *, rows):
    tm, D = o_ref.shape[1:]
    PW = 2 * HEAD
    same = _seg_ones(PW).astype(BF16)
    same2 = jnp.concatenate([same, same], axis=0)

    def head_sum(x, pieces):
        if pieces == 1:
            return _dot(x.astype(BF16), same)
        return _dot(jnp.concatenate(_split3(x)[:2], axis=1), same2)

    for r0 in range(0, tm, rows):
        rs = slice(r0, r0 + rows)
        aw = aw_ref[0, rs, :]
        a_f = _sigmoid(a0_ref[0:1, :] + _dot(aw[:, :LANES], a2f_ref[...]))
        a_b = _sigmoid(a0_ref[1:2, :] + _dot(aw[:, LANES:], a2b_ref[...]))
        g = _dot(gs_ref[0, rs, :], g2_ref[...])
        for p in range(D // PW):
            sl = slice(p * PW, (p + 1) * PW)
            ksum = k_ref[0, rs, sl].astype(F32) * (2.0 + (a_f[:, sl] + a_b[:, sl] - 2.0) * ka_ref[:, sl])
            y = yf_ref[0, rs, sl] + yb_ref[0, rs, sl]
            yc = y - head_sum(y, 2) * (1.0 / HEAD)
            var = head_sum(yc * yc, 1) * (1.0 / HEAD)
            o = yc * lax.rsqrt(var + GN_EPS) * lnw_ref[:, sl] + lnb_ref[:, sl]
            bonus = (head_sum(r_ref[0, rs, sl].astype(F32) * ksum * rk_ref[:, sl], 1)
                     * v_ref[0, rs, sl].astype(F32))
            og_scr[rs, sl] = ((o + bonus) * g[:, sl]).astype(og_scr.dtype)
        x1 = res_ref[0, rs, :] + gate_ref[0] * _dot(og_scr[rs, :], wo_ref[...])
        o_ref[0, rs, :] = x1
        h_ref[0, rs, :] = _norm_mod(x1, g_ref[...], sh_ref[0], sc_ref[0]).astype(h_ref.dtype)


def _rwkv_out(yf, yb, r, k, v, gs, aw, res, a2fp, a2bp, g2, wo, a0, ka, rk, lnw, lnb, gate, g, shift,
              scale, name):
    Bn, T, D = res.shape
    tm = _tile(T, 256, 16)
    rows = _tile(tm, 128, 16)
    G = gs.shape[-1]
    row = pl.BlockSpec((1, tm, D), lambda b, i: (b, i, 0))
    vec = pl.BlockSpec((1, D), lambda b, i: (0, 0))
    full = lambda arr: pl.BlockSpec(arr.shape, lambda b, i: (0,) * arr.ndim)
    return pl.pallas_call(
        functools.partial(_rwkv_out_kernel, rows=rows),
        grid=(Bn, T // tm),
        in_specs=[row, row, row, row, row,
                  pl.BlockSpec((1, tm, G), lambda b, i: (b, i, 0)),
                  pl.BlockSpec((1, tm, 2 * LANES), lambda b, i: (b, i, 0)),
                  row, full(a2fp), full(a2bp), full(g2), full(wo), full(a0),
                  vec, vec, vec, vec,
                  pl.BlockSpec((1, 1, D), _bsel(gate)), vec,
                  pl.BlockSpec((1, 1, D), _bsel(shift)),
                  pl.BlockSpec((1, 1, D), _bsel(scale))],
        out_specs=[row, row],
        out_shape=[jax.ShapeDtypeStruct((Bn, T, D), F32), jax.ShapeDtypeStruct((Bn, T, D), BF16)],
        scratch_shapes=[pltpu.VMEM((tm, D), BF16)],
        compiler_params=_params("parallel", "parallel"),
        name=name,
    )(yf, yb, r, k, v, gs, aw, res, a2fp, a2bp, g2, wo, a0, ka, rk, lnw, lnb, gate, g[None], shift, scale)


def _conv_kernel(gb_ref, z_ref, cw_ref, o_ref):
    z = z_ref[0].astype(F32)
    T = z.shape[0]
    t = lax.broadcasted_iota(jnp.int32, (T, 1), 0)
    zp = jnp.where(t == 0, 0.0, pltpu.roll(z, 1, 0))
    zn = jnp.where(t == T - 1, 0.0, pltpu.roll(z, T - 1, 0))
    conv = zp * cw_ref[0:1, :] + z * cw_ref[1:2, :] + zn * cw_ref[2:3, :]
    o_ref[0] = (gb_ref[0] * conv).astype(o_ref.dtype)


def _conv(gb, z, cw):
    Bn, T, D = z.shape
    tn = _tile(D, 256, LANES)
    tok = pl.BlockSpec((1, T, tn), lambda b, j: (b, 0, j))
    return pl.pallas_call(
        _conv_kernel,
        grid=(Bn, D // tn),
        in_specs=[tok, tok, pl.BlockSpec((3, tn), lambda b, j: (0, j))],
        out_specs=tok,
        out_shape=jax.ShapeDtypeStruct((Bn, T, D), BF16),
        compiler_params=_params("parallel", "parallel"),
        name="short_conv",
    )(gb, z, cw)


def _pad_rows(w, rows):
    return jnp.pad(w, ((0, rows - w.shape[0]), (0, 0)))


def _pad_cols(w, cols):
    return jnp.pad(w, ((0, 0), (0, cols - w.shape[1])))


def _split_mod(mod_rows, D):
    return [mod_rows[:, m * D:(m + 1) * D][:, None, :] for m in range(6)]


def _ffn_branch(t1, h2, mods, ffn, tag):
    w13, layer, wdn = ffn
    act = _ffn_up(h2, w13, layer, name="ffn_up_" + tag)
    return _mm_res(act, wdn, t1, mods[5], name="ffn_down_" + tag)


def _rwkv_layer(x, ctx, mods_x, mods_c, g1, g2n, mix, wr, wk, wv, wo, w0, w1, w2, a0, a1, a2,
                lg1, lg2, k_k, k_a, r_k, ln_w, ln_b, ffn):
    D = x.shape[-1]
    H = D // HEAD
    w1cat = jnp.concatenate([_pad_cols(w1[0], LANES), _pad_cols(w1[1], LANES)], axis=1).astype(BF16)
    a1cat = jnp.concatenate([_pad_cols(a1[0], LANES), _pad_cols(a1[1], LANES)], axis=1).astype(BF16)
    w2p = [_pad_rows(w2[d], LANES).astype(BF16) for d in range(2)]
    a2p = [_pad_rows(a2[d], LANES).astype(BF16) for d in range(2)]
    wr, wk, wv, wo = (t.astype(BF16) for t in (wr, wk, wv, wo))
    lg1, lg2 = lg1.astype(BF16), lg2.astype(BF16)
    rk = r_k.reshape(1, D)

    sets = {}
    for tag, tok, mods, prep in (("c", ctx, mods_c, _prep_ctx), ("x", x, mods_x, _prep_latent)):
        xr, xw, xk, xv, xa, xg = prep(tok, g1, mods[0], mods[1], mix)
        sets[tag] = dict(
            r=_mm(xr, wr, BF16, name="proj_r_" + tag),
            k=_mm(xk, wk, BF16, name="proj_k_" + tag),
            v=_mm(xv, wv, BF16, name="proj_v_" + tag),
            tw=_mm(xw, w1cat, BF16, act="tanh", name="lora_w_" + tag),
            aw=_mm(xa, a1cat, BF16, name="lora_a_" + tag),
            gs=_mm(xg, lg1, BF16, act="sigmoid", name="lora_g_" + tag))

    ys = {"c": [], "x": []}
    zero_state = jnp.zeros((x.shape[0], H // 2, 2 * HEAD, 2 * HEAD), F32)
    for d in range(2):
        state = zero_state
        for tag in ("c", "x"):
            s = sets[tag]
            y, state = _wkv(s["r"], s["k"], s["v"], s["tw"], s["aw"], w2p[d], a2p[d],
                            w0[d][None], a0[d][None], k_k[None], k_a[None], state, d)
            ys[tag].append(y)

    outs = []
    for tag, tok, mods in (("c", ctx, mods_c), ("x", x, mods_x)):
        s = sets[tag]
        t1, h2 = _rwkv_out(ys[tag][0], ys[tag][1], s["r"], s["k"], s["v"], s["gs"], s["aw"], tok,
                           a2p[0], a2p[1], lg2, wo, a0, k_a[None], rk, ln_w[None], ln_b[None],
                           mods[2], g2n, mods[3], mods[4], name="rwkv_out_" + tag)
        outs.append(_ffn_branch(t1, h2, mods, ffn, tag))
    return outs[1], outs[0]


def _conv_layer(x, mods, g1, g2n, w_in, conv_w, w_out, ffn):
    gb, z = _norm_mm(x, g1, mods[0], mods[1], w_in.astype(BF16), 3, _conv_in_combine, (BF16, BF16),
                     name="conv_in")
    p = _conv(gb, z, conv_w)
    t1, h2 = _mm_res_norm(p, w_out.astype(BF16), x, mods[2], g2n, mods[3], mods[4], name="conv_out")
    return _ffn_branch(t1, h2, mods, ffn, "x")


def kernel(x, c, ctx, c_ctx, norm1_g, norm2_g, ada_w, ada_b, rw_mix, rw_wr, rw_wk, rw_wv, rw_wo,
           rw_w0, rw_w1, rw_w2, rw_a0, rw_a1, rw_a2, rw_g1, rw_g2, rw_kk, rw_ka, rw_rk, rw_lnw,
           rw_lnb, sc_win, sc_conv, sc_wout, ffn_w13, ffn_w2, final_g):
    B, T, D = x.shape
    depth = norm1_g.shape[0]
    rows = -(-(B + 1) // 8) * 8
    cond = jnp.zeros((rows, D), F32).at[:B].set(c).at[B].set(c_ctx)
    for i in range(depth):
        last = i == depth - 1
        j = i // 2
        mod = _ada(cond, ada_w, ada_b, i)
        mods_x = _split_mod(mod[:B], D)
        mods_c = _split_mod(mod[B:B + 1], D)
        ffn = (ffn_w13, i, ffn_w2[i].astype(BF16))
        if i % 2 == 0:
            x, ctx_new = _rwkv_layer(
                x, ctx, mods_x, mods_c, norm1_g[i], norm2_g[i], rw_mix[j], rw_wr[j], rw_wk[j],
                rw_wv[j], rw_wo[j], rw_w0[j], rw_w1[j], rw_w2[j], rw_a0[j], rw_a1[j], rw_a2[j],
                rw_g1[j], rw_g2[j], rw_kk[j], rw_ka[j], rw_rk[j], rw_lnw[j], rw_lnb[j], ffn)
            ctx = ctx_new
        else:
            x = _conv_layer(x, mods_x, norm1_g[i], norm2_g[i], sc_win[j], sc_conv[j], sc_wout[j],
                            ffn)
            if not last:
                ctx = _conv_layer(ctx, mods_c, norm1_g[i], norm2_g[i], sc_win[j], sc_conv[j],
                                  sc_wout[j], ffn)
    zeros = jnp.zeros((1, 1, D), F32)
    return _norm(x, final_g, zeros, zeros)
```

```python
import functools
import math

import jax
import jax.numpy as jnp
from jax import lax
from jax.experimental import pallas as pl
from jax.experimental.pallas import tpu as pltpu

HEAD = 64
GRID_W = 64
CHUNK = 64
NORM_EPS = 1e-6
GN_EPS = 64e-5
LANES = 128
VMEM_LIMIT = 56 * 1024 * 1024

F32 = jnp.float32
BF16 = jnp.bfloat16
HI = lax.Precision.HIGHEST


def _params(*sem):
    return pltpu.CompilerParams(dimension_semantics=sem, vmem_limit_bytes=VMEM_LIMIT)


def _tile(n, pref, mult):
    t = min(pref, n)
    t -= t % mult
    while t >= mult:
        if n % t == 0:
            return t
        t -= mult
    return n


def _sigmoid(x):
    return 1.0 / (1.0 + jnp.exp(-x))


def _norm_mod(x, g, shift, scale, cols=None):
    rs = lax.rsqrt(jnp.mean(x * x, axis=-1, keepdims=True) + NORM_EPS)
    if cols is not None:
        x, g, shift, scale = (t[:, cols[0]:cols[1]] for t in (x, g, shift, scale))
    return (x * rs) * (g * (1.0 + scale)) + shift


def _dot(a, b, precision=None):
    return jnp.dot(a, b, preferred_element_type=F32, precision=precision)


def _dot_t(a, b, ca, cb, precision=None):
    return lax.dot_general(a, b, (((ca,), (cb,)), ((), ())), preferred_element_type=F32,
                           precision=precision)


def _ada_kernel(c_ref, w_ref, b_ref, o_ref):
    c = c_ref[...]
    s = c * _sigmoid(c)
    o_ref[...] = _dot(s.astype(BF16), w_ref[0].astype(BF16)) + b_ref[0]


def _ada(cond, w, b, layer):
    R, D = cond.shape
    N = w.shape[2]
    tn = _tile(N, 1024, LANES)
    return pl.pallas_call(
        _ada_kernel,
        grid=(N // tn,),
        in_specs=[pl.BlockSpec((R, D), lambda j: (0, 0)),
                  pl.BlockSpec((1, D, tn), lambda j: (layer, 0, j)),
                  pl.BlockSpec((1, 1, tn), lambda j: (layer, 0, j))],
        out_specs=pl.BlockSpec((R, tn), lambda j: (0, j)),
        out_shape=jax.ShapeDtypeStruct((R, N), F32),
        compiler_params=_params("parallel"),
        name="ada_mod",
    )(cond, w, b[:, None, :])


def _norm_kernel(x_ref, g_ref, sh_ref, sc_ref, o_ref):
    o_ref[0] = _norm_mod(x_ref[0], g_ref[...], sh_ref[0], sc_ref[0])


def _bsel(arr):
    if arr.shape[0] == 1:
        return lambda b, *_: (0, 0, 0)
    return lambda b, *_: (b, 0, 0)


def _norm(x, g, shift, scale):
    Bn, T, D = x.shape
    tm = _tile(T, 512, 8)
    return pl.pallas_call(
        _norm_kernel,
        grid=(Bn, T // tm),
        in_specs=[pl.BlockSpec((1, tm, D), lambda b, i: (b, i, 0)),
                  pl.BlockSpec((1, D), lambda b, i: (0, 0)),
                  pl.BlockSpec((1, 1, D), _bsel(shift)),
                  pl.BlockSpec((1, 1, D), _bsel(scale))],
        out_specs=pl.BlockSpec((1, tm, D), lambda b, i: (b, i, 0)),
        out_shape=jax.ShapeDtypeStruct((Bn, T, D), F32),
        compiler_params=_params("parallel", "parallel"),
        name="norm",
    )(x, g[None], shift, scale)


def _write_mix(out_refs, mix_ref, h, shifted, c0, c1):
    xx = shifted - h
    for m, o_ref in enumerate(out_refs):
        o_ref[0, :, c0:c1] = (h + xx * mix_ref[m:m + 1, c0:c1]).astype(o_ref.dtype)


def _prep_latent_kernel(x_ref, xu_ref, xd_ref, g_ref, sh_ref, sc_ref, mix_ref, w1_ref, a1_ref, g1_ref,
                        xr_ref, xk_ref, xv_ref, tw_ref, aw_ref, gs_ref, h_scr, xm_scr):
    i = pl.program_id(1)
    n = pl.num_programs(1)
    g, sh, sc = g_ref[...], sh_ref[0], sc_ref[0]
    tm, D = x_ref.shape[1:]
    q = D // 4
    W = GRID_W
    h_scr[0:W, 2 * q:3 * q] = jnp.where(i > 0, _norm_mod(xu_ref[0], g, sh, sc, (2 * q, 3 * q)), 0.0)
    h_scr[W - 8:W, 0:q] = jnp.zeros((8, q), F32)
    for r0 in range(0, tm, W):
        h_scr[W + r0:2 * W + r0, :] = _norm_mod(x_ref[0, r0:r0 + W, :], g, sh, sc)
    h_scr[W + tm:, 3 * q:] = jnp.where(i < n - 1, _norm_mod(xd_ref[0], g, sh, sc, (3 * q, D)), 0.0)
    h_scr[W + tm:W + tm + 8, q:2 * q] = jnp.zeros((8, q), F32)
    wide = {0: xr_ref, 2: xk_ref, 3: xv_ref}
    lora = {1: 0, 4: 1, 5: 2}
    lora_w = (w1_ref, a1_ref, g1_ref)
    acc = [None] * 3
    R = 32
    row = lax.broadcasted_iota(jnp.int32, (R, 1), 0)
    for k, off in enumerate((-1, 1, -W, W)):
        c0, c1 = k * q, (k + 1) * q
        mixk = [mix_ref[m:m + 1, c0:c1] for m in range(6)]
        for r0 in range(0, tm, R):
            h = h_scr[W + r0:W + r0 + R, c0:c1]
            s = h_scr[W + r0 + off:W + r0 + off + R, c0:c1]
            if off == -1 and r0 % W == 0:
                s = jnp.where(row == 0, 0.0, s)
            if off == 1 and (r0 + R) % W == 0:
                s = jnp.where(row == R - 1, 0.0, s)
            xx = s - h
            for m in range(6):
                xm = (h + xx * mixk[m]).astype(BF16)
                if m in wide:
                    wide[m][0, r0:r0 + R, c0:c1] = xm
                else:
                    xm_scr[lora[m], r0:r0 + R, c0:c1] = xm
        for l in range(3):
            part = _dot(xm_scr[l, :, c0:c1], lora_w[l][c0:c1, :])
            acc[l] = part if acc[l] is None else acc[l] + part
    tw_ref[0] = jnp.tanh(acc[0]).astype(tw_ref.dtype)
    aw_ref[0] = acc[1].astype(aw_ref.dtype)
    gs_ref[0] = _sigmoid(acc[2]).astype(gs_ref.dtype)


def _prep_latent(x, g, shift, scale, mix, w1cat, a1cat, lg1):
    Bn, T, D = x.shape
    rows_per_tile = _tile(T // GRID_W, 4, 1)
    tm = rows_per_tile * GRID_W
    nrow = T // GRID_W
    tok = pl.BlockSpec((1, tm, D), lambda b, i: (b, i, 0))
    full = lambda arr: pl.BlockSpec(arr.shape, lambda b, i: (0,) * arr.ndim)
    small = lambda arr: pl.BlockSpec((1, tm, arr.shape[1]), lambda b, i: (b, i, 0))
    return pl.pallas_call(
        _prep_latent_kernel,
        grid=(Bn, T // tm),
        in_specs=[tok,
                  pl.BlockSpec((1, GRID_W, D),
                               lambda b, i: (b, jnp.maximum(i * rows_per_tile - 1, 0), 0)),
                  pl.BlockSpec((1, GRID_W, D),
                               lambda b, i: (b, jnp.minimum((i + 1) * rows_per_tile, nrow - 1), 0)),
                  pl.BlockSpec((1, D), lambda b, i: (0, 0)),
                  pl.BlockSpec((1, 1, D), _bsel(shift)),
                  pl.BlockSpec((1, 1, D), _bsel(scale)),
                  pl.BlockSpec((6, D), lambda b, i: (0, 0)),
                  full(w1cat), full(a1cat), full(lg1)],
        out_specs=[tok, tok, tok, small(w1cat), small(a1cat), small(lg1)],
        out_shape=[jax.ShapeDtypeStruct((Bn, T, D), BF16)] * 3
        + [jax.ShapeDtypeStruct((Bn, T, w.shape[1]), BF16) for w in (w1cat, a1cat, lg1)],
        scratch_shapes=[pltpu.VMEM((tm + 2 * GRID_W, D), F32), pltpu.VMEM((3, tm, D), BF16)],
        compiler_params=_params("parallel", "parallel"),
        name="prep_latent",
    )(x, x, x, g[None], shift, scale, mix, w1cat, a1cat, lg1)


def _prep_ctx_kernel(x_ref, g_ref, sh_ref, sc_ref, mix_ref, *out_refs):
    h = _norm_mod(x_ref[0], g_ref[...], sh_ref[0], sc_ref[0])
    L, D = h.shape
    half = D // 2
    t = lax.broadcasted_iota(jnp.int32, (L, 1), 0)
    h0, h1 = h[:, :half], h[:, half:]
    prev = jnp.where(t == 0, 0.0, pltpu.roll(h0, 1, 0))
    nxt = jnp.where(t == L - 1, 0.0, pltpu.roll(h1, L - 1, 0))
    _write_mix(out_refs, mix_ref, h0, prev, 0, half)
    _write_mix(out_refs, mix_ref, h1, nxt, half, D)


def _prep_ctx(x, g, shift, scale, mix):
    Bn, L, D = x.shape
    return pl.pallas_call(
        _prep_ctx_kernel,
        grid=(Bn,),
        in_specs=[pl.BlockSpec((1, L, D), lambda b: (b, 0, 0)),
                  pl.BlockSpec((1, D), lambda b: (0, 0)),
                  pl.BlockSpec((1, 1, D), _bsel(shift)),
                  pl.BlockSpec((1, 1, D), _bsel(scale)),
                  pl.BlockSpec((6, D), lambda b: (0, 0))],
        out_specs=[pl.BlockSpec((1, L, D), lambda b: (b, 0, 0))] * 6,
        out_shape=[jax.ShapeDtypeStruct((Bn, L, D), BF16)] * 6,
        compiler_params=_params("parallel"),
        name="prep_ctx",
    )(x, g[None], shift, scale, mix)


def _mm_kernel(a_ref, w_ref, o_ref, *, act):
    acc = _dot(a_ref[0], w_ref[...])
    if act == "tanh":
        acc = jnp.tanh(acc)
    elif act == "sigmoid":
        acc = _sigmoid(acc)
    o_ref[0] = acc.astype(o_ref.dtype)


def _mm(a, w, out_dtype, act=None, name="mm"):
    Bn, T, K = a.shape
    N = w.shape[1]
    tm = _tile(T, 1024, 16)
    tn = _tile(N, 1024, LANES)
    return pl.pallas_call(
        functools.partial(_mm_kernel, act=act),
        grid=(Bn, T // tm, N // tn),
        in_specs=[pl.BlockSpec((1, tm, K), lambda b, i, j: (b, i, 0)),
                  pl.BlockSpec((K, tn), lambda b, i, j: (0, j))],
        out_specs=pl.BlockSpec((1, tm, tn), lambda b, i, j: (b, i, j)),
        out_shape=jax.ShapeDtypeStruct((Bn, T, N), out_dtype),
        compiler_params=_params("parallel", "parallel", "parallel"),
        name=name,
    )(a, w)


def _mm_res_kernel(a_ref, w_ref, res_ref, gate_ref, o_ref):
    o_ref[0] = res_ref[0] + gate_ref[0] * _dot(a_ref[0], w_ref[...])


def _mm_res(a, w, res, gate, name="mm_res"):
    Bn, T, K = a.shape
    N = w.shape[1]
    tm = _tile(T, 1024, 16)
    tn = _tile(N, 1024 if K <= 2048 else 512, LANES)
    gsel = _bsel(gate)
    return pl.pallas_call(
        _mm_res_kernel,
        grid=(Bn, T // tm, N // tn),
        in_specs=[pl.BlockSpec((1, tm, K), lambda b, i, j: (b, i, 0)),
                  pl.BlockSpec((K, tn), lambda b, i, j: (0, j)),
                  pl.BlockSpec((1, tm, tn), lambda b, i, j: (b, i, j)),
                  pl.BlockSpec((1, 1, tn), lambda b, i, j: gsel(b)[:2] + (j,))],
        out_specs=pl.BlockSpec((1, tm, tn), lambda b, i, j: (b, i, j)),
        out_shape=jax.ShapeDtypeStruct((Bn, T, N), F32),
        compiler_params=_params("parallel", "parallel", "parallel"),
        name=name,
    )(a, w, res, gate)


def _mm_res_norm_kernel(a_ref, w_ref, res_ref, gate_ref, g_ref, sh_ref, sc_ref, o_ref, h_ref, *, rows):
    for r0 in range(0, a_ref.shape[1], rows):
        rs = slice(r0, r0 + rows)
        x1 = res_ref[0, rs, :] + gate_ref[0] * _dot(a_ref[0, rs, :], w_ref[...])
        o_ref[0, rs, :] = x1
        h_ref[0, rs, :] = _norm_mod(x1, g_ref[...], sh_ref[0], sc_ref[0]).astype(h_ref.dtype)


def _mm_res_norm(a, w, res, gate, g, shift, scale, name):
    Bn, T, K = a.shape
    N = w.shape[1]
    tm = _tile(T, 512, 16)
    rows = _tile(tm, 256, 16)
    row = pl.BlockSpec((1, tm, N), lambda b, i: (b, i, 0))
    return pl.pallas_call(
        functools.partial(_mm_res_norm_kernel, rows=rows),
        grid=(Bn, T // tm),
        in_specs=[pl.BlockSpec((1, tm, K), lambda b, i: (b, i, 0)),
                  pl.BlockSpec((K, N), lambda b, i: (0, 0)),
                  row,
                  pl.BlockSpec((1, 1, N), _bsel(gate)),
                  pl.BlockSpec((1, N), lambda b, i: (0, 0)),
                  pl.BlockSpec((1, 1, N), _bsel(shift)),
                  pl.BlockSpec((1, 1, N), _bsel(scale))],
        out_specs=[row, row],
        out_shape=[jax.ShapeDtypeStruct((Bn, T, N), F32), jax.ShapeDtypeStruct((Bn, T, N), BF16)],
        compiler_params=_params("parallel", "parallel"),
        name=name,
    )(a, w, res, gate, g[None], shift, scale)


def _ffn_up_kernel(h_ref, wa_ref, wb_ref, o_ref, w_scr, *, rows):
    @pl.when((pl.program_id(1) == 0) & (pl.program_id(2) == 0))
    def _():
        w_scr[0] = wa_ref[0].astype(BF16)
        w_scr[1] = wb_ref[0].astype(BF16)

    for r0 in range(0, h_ref.shape[1], rows):
        rs = slice(r0, r0 + rows)
        h = h_ref[0, rs, :]
        a = _dot(h, w_scr[0])
        o_ref[0, rs, :] = (a * _sigmoid(a) * _dot(h, w_scr[1])).astype(o_ref.dtype)


def _ffn_up(h, w13, layer, name):
    Bn, T, D = h.shape
    F = w13.shape[2] // 2
    tm = _tile(T, 2048, 16)
    rows = _tile(tm, 512, 16)
    tn = _tile(F, 512, LANES)
    nj = F // tn
    return pl.pallas_call(
        functools.partial(_ffn_up_kernel, rows=rows),
        grid=(nj, Bn, T // tm),
        in_specs=[pl.BlockSpec((1, tm, D), lambda j, b, i: (b, i, 0)),
                  pl.BlockSpec((1, D, tn), lambda j, b, i: (layer, 0, j)),
                  pl.BlockSpec((1, D, tn), lambda j, b, i: (layer, 0, j + nj))],
        out_specs=pl.BlockSpec((1, tm, tn), lambda j, b, i: (b, i, j)),
        out_shape=jax.ShapeDtypeStruct((Bn, T, F), BF16),
        scratch_shapes=[pltpu.VMEM((2, D, tn), BF16)],
        compiler_params=_params("arbitrary", "arbitrary", "arbitrary"),
        name=name,
    )(h, w13, w13)


def _swiglu_combine(a, b):
    return (a * _sigmoid(a) * b,)


def _conv_in_combine(gb, gc, u):
    return gb, gc * u


def _norm_mm_kernel(x_ref, g_ref, sh_ref, sc_ref, *rest, nw, combine):
    w_refs, out_refs, h_scr = rest[:nw], rest[nw:-1], rest[-1]

    @pl.when(pl.program_id(2) == 0)
    def _():
        h_scr[...] = _norm_mod(x_ref[0], g_ref[...], sh_ref[0], sc_ref[0]).astype(BF16)

    h = h_scr[...]
    outs = combine(*[_dot(h, w_ref[...]) for w_ref in w_refs])
    for o_ref, o in zip(out_refs, outs):
        o_ref[0] = o.astype(o_ref.dtype)


def _norm_mm(x, g, shift, scale, w, nw, combine, out_dtypes, name):
    Bn, T, D = x.shape
    N = w.shape[1] // nw
    tm = _tile(T, 1024, 16)
    tn = _tile(N, 512, LANES)
    nj = N // tn
    w_specs = [pl.BlockSpec((D, tn), functools.partial(lambda b, i, j, m: (0, j + m * nj), m=m))
               for m in range(nw)]
    return pl.pallas_call(
        functools.partial(_norm_mm_kernel, nw=nw, combine=combine),
        grid=(Bn, T // tm, nj),
        in_specs=[pl.BlockSpec((1, tm, D), lambda b, i, j: (b, i, 0)),
                  pl.BlockSpec((1, D), lambda b, i, j: (0, 0)),
                  pl.BlockSpec((1, 1, D), _bsel(shift)),
                  pl.BlockSpec((1, 1, D), _bsel(scale))] + w_specs,
        out_specs=[pl.BlockSpec((1, tm, tn), lambda b, i, j: (b, i, j))] * len(out_dtypes),
        out_shape=[jax.ShapeDtypeStruct((Bn, T, N), dt) for dt in out_dtypes],
        scratch_shapes=[pltpu.VMEM((tm, D), BF16)],
        compiler_params=_params("parallel", "parallel", "arbitrary"),
        name=name,
    )(x, g[None], shift, scale, *([w] * nw))


def _seg_ones(width):
    shift = HEAD.bit_length() - 1
    r = lax.shift_right_logical(lax.broadcasted_iota(jnp.int32, (width, width), 0), shift)
    c = lax.shift_right_logical(lax.broadcasted_iota(jnp.int32, (width, width), 1), shift)
    return (r == c).astype(F32)


def _split3(x):
    hi = x.astype(BF16)
    r1 = x - hi.astype(F32)
    mid = r1.astype(BF16)
    lo = (r1 - mid.astype(F32)).astype(BF16)
    return hi, mid, lo


def _wkv_kernel(r_ref, k_ref, v_ref, tw_ref, aw_ref, w2_ref, a2_ref, w0_ref, a0_ref, kk_ref, ka_ref,
                s0_ref, y_ref, sout_ref, s_scr, x_scr, r2_scr, bv_scr, vb_scr, z_scr, wt_scr,
                *, reverse, npair):
    c = pl.program_id(2)
    C = CHUNK
    PW = 2 * HEAD

    @pl.when(c == 0)
    def _():
        s_scr[...] = s0_ref[0]
        x_scr[...] = jnp.zeros_like(x_scr)
        r2_scr[...] = jnp.zeros_like(r2_scr)
        bv_scr[...] = jnp.zeros_like(bv_scr)
        vb_scr[...] = jnp.zeros_like(vb_scr)
        z_scr[...] = jnp.zeros_like(z_scr)
        wt_scr[...] = jnp.ones_like(wt_scr)

    rr = lax.shift_right_logical(lax.broadcasted_iota(jnp.int32, (PW, PW), 0), HEAD.bit_length() - 1)
    cc = lax.shift_right_logical(lax.broadcasted_iota(jnp.int32, (PW, PW), 1), HEAD.bit_length() - 1)
    same = rr == cc
    same_bf = same.astype(BF16)

    def bd(x):
        xb = x.astype(BF16)
        return jnp.concatenate([xb, xb], axis=0) * same_bf

    t2 = lax.broadcasted_iota(jnp.int32, (C, PW), 0)
    s2 = lax.broadcasted_iota(jnp.int32, (C, PW), 1) & (HEAD - 1)
    before = (s2 > t2) if reverse else (s2 < t2)
    upto = before | (s2 == t2)
    pairs = range(npair)
    sls = [slice(p * PW, (p + 1) * PW) for p in pairs]

    r, k, v = (t[0].astype(F32) for t in (r_ref, k_ref, v_ref))
    z = w0_ref[...] + _dot(tw_ref[0], w2_ref[...])
    a_pre = a0_ref[...] + _dot(aw_ref[0], a2_ref[...])

    S = [s_scr[p] for p in pairs]
    X = [x_scr[p] for p in pairs]
    G = [_dot_t(X[p], r2_scr[p], 1, 1) for p in pairs]
    XS = [_dot_t(X[p], S[p].astype(BF16), 1, 1) for p in pairs]
    u = [XS[p][:C] + _dot(jnp.where(before, G[p][:C, PW:], 0.0).astype(BF16), bv_scr[p]) for p in pairs]
    P = [jnp.where(before, G[p][:C, :PW], 0.0) for p in pairs]

    lw = -math.exp(-0.5) * _sigmoid(z)
    kkv = k * kk_ref[...]
    kk2 = kkv * kkv
    t_i = lax.broadcasted_iota(jnp.int32, (C, C), 0)
    s_i = lax.broadcasted_iota(jnp.int32, (C, C), 1)
    tri = ((s_i >= t_i) if reverse else (s_i <= t_i)).astype(BF16)
    cum = _dot(jnp.concatenate([tri, tri], axis=1),
               jnp.concatenate(_split3(lw)[:2], axis=0))
    same2 = jnp.concatenate([same_bf, same_bf], axis=0)
    ss = [_dot(jnp.concatenate(_split3(kk2[:, sl])[:2], axis=1), same2) for sl in sls]

    n_sq = C.bit_length() - 1
    for j in range(n_sq):
        Pb = [P[p].astype(BF16) for p in pairs]
        if j < n_sq - 1:
            PU = [_dot(Pb[p], jnp.concatenate([bd(P[p]), bd(u[p])], axis=1)) for p in pairs]
            P = [PU[p][:, :PW] for p in pairs]
            u = [u[p] + PU[p][:, PW:] for p in pairs]
        else:
            u = [u[p] + _dot(Pb[p], bd(u[p])) for p in pairs]
    for p in pairs:
        R = jnp.concatenate([jnp.where(upto, G[p][C:, :PW], 0.0),
                             jnp.where(upto, G[p][C:, PW:], 0.0)], axis=1).astype(BF16)
        y_ref[0, :, sls[p]] = XS[p][C:] + _dot(R, jnp.concatenate([bd(u[p]), bv_scr[p]], axis=0))
    for p in pairs:
        UV = jnp.concatenate([u[p].astype(BF16), vb_scr[p]], axis=0)
        dS = _dot_t(UV, z_scr[p], 0, 0)
        s_scr[p] = (S[p] + jnp.where(same, dS, 0.0)) * wt_scr[p, 0:1, :]

    a_sig = _sigmoid(a_pre)
    kd = k * (1.0 + (a_sig - 1.0) * ka_ref[...])
    e_pos = jnp.exp(cum)
    e_neg = jnp.exp(-cum)
    e_prev = jnp.exp(cum - lw)
    last = 0 if reverse else C - 1
    for p, sl in zip(pairs, sls):
        kkn = kkv[:, sl] * lax.rsqrt(jnp.maximum(ss[p], 1e-24))
        at = (-kkn) * e_prev[:, sl]
        bt = (kkn * a_sig[:, sl]) * e_neg[:, sl]
        rt = r[:, sl] * e_pos[:, sl]
        kt = kd[:, sl] * e_neg[:, sl]
        x_scr[p] = jnp.concatenate([at, rt], axis=0).astype(BF16)
        r2_scr[p] = jnp.concatenate([bd(bt), bd(kt)], axis=0)
        bv_scr[p] = bd(v[:, sl])
        vb_scr[p] = v[:, sl].astype(BF16)
        z_scr[p] = jnp.concatenate([bt, kt], axis=0).astype(BF16)
        wt_scr[p] = jnp.broadcast_to(e_pos[last:last + 1, sl], wt_scr.shape[1:])

    @pl.when(c == pl.num_programs(2) - 1)
    def _():
        sout_ref[0] = s_scr[...]


def _wkv(r, k, v, tw, aw, w2p, a2p, w0, a0, kk, ka, s0, d):
    Bn, T, D = r.shape
    PW = 2 * HEAD
    npairs = D // PW
    npair = _tile(npairs, 16, 1)
    hw = npair * PW
    nc = T // CHUNK
    reverse = d == 1
    pos = (lambda j: nc - 1 - j) if reverse else (lambda j: j)
    cin = lambda c: pos(jnp.minimum(c, nc - 1))
    cout = lambda c: pos(jnp.maximum(c - 1, 0))
    tok = pl.BlockSpec((1, CHUNK, hw), lambda b, g, c: (b, cin(c), g))
    lora = pl.BlockSpec((1, CHUNK, LANES), lambda b, g, c: (b, cin(c), d))
    lw2 = pl.BlockSpec((LANES, hw), lambda b, g, c: (0, g))
    vec = pl.BlockSpec((1, hw), lambda b, g, c: (0, g))
    st = pl.BlockSpec((1, npair, PW, PW), lambda b, g, c: (b, g, 0, 0))
    return pl.pallas_call(
        functools.partial(_wkv_kernel, reverse=reverse, npair=npair),
        grid=(Bn, npairs // npair, nc + 1),
        in_specs=[tok, tok, tok, lora, lora, lw2, lw2, vec, vec, vec, vec, st],
        out_specs=[pl.BlockSpec((1, CHUNK, hw), lambda b, g, c: (b, cout(c), g)), st],
        out_shape=[jax.ShapeDtypeStruct((Bn, T, D), F32),
                   jax.ShapeDtypeStruct((Bn, npairs, PW, PW), F32)],
        scratch_shapes=[pltpu.VMEM((npair, PW, PW), F32),
                        pltpu.VMEM((npair, 2 * CHUNK, PW), BF16),
                        pltpu.VMEM((npair, 2 * PW, PW), BF16),
                        pltpu.VMEM((npair, PW, PW), BF16),
                        pltpu.VMEM((npair, CHUNK, PW), BF16),
                        pltpu.VMEM((npair, 2 * CHUNK, PW), BF16),
                        pltpu.VMEM((npair, 8, PW), F32)],
        compiler_params=_params("parallel", "parallel", "arbitrary"),
        name="wkv_rev" if reverse else "wkv_fwd",
    )(r, k, v, tw, aw, w2p, a2p, w0, a0, kk, ka, s0)


def _rwkv_out_kernel(yf_ref, yb_ref, r_ref, k_ref, v_ref, gs_ref, aw_ref, res_ref, a2f_ref, a2b_ref,
                     g2_ref, wo_ref, a0_ref, ka_ref, rk_ref, lnw_ref, lnb_ref, gate_ref, g_ref, sh_ref,
                     sc_ref, o_ref, h_ref, og_scr, *, rows):
    tm, D = o_ref.shape[1:]
    PW = 2 * HEAD
    same = _seg_ones(PW).astype(BF16)
    same2 = jnp.concatenate([same, same], axis=0)

    def head_sum(x, pieces):
        if pieces == 1:
            return _dot(x.astype(BF16), same)
        return _dot(jnp.concatenate(_split3(x)[:2], axis=1), same2)

    for r0 in range(0, tm, rows):
        rs = slice(r0, r0 + rows)
        aw = aw_ref[0, rs, :]
        a_f = _sigmoid(a0_ref[0:1, :] + _dot(aw[:, :LANES], a2f_ref[...]))
        a_b = _sigmoid(a0_ref[1:2, :] + _dot(aw[:, LANES:], a2b_ref[...]))
        g = _dot(gs_ref[0, rs, :], g2_ref[...])
        for p in range(D // PW):
            sl = slice(p * PW, (p + 1) * PW)
            ksum = k_ref[0, rs, sl].astype(F32) * (2.0 + (a_f[:, sl] + a_b[:, sl] - 2.0) * ka_ref[:, sl])
            y = yf_ref[0, rs, sl] + yb_ref[0, rs, sl]
            yc = y - head_sum(y, 2) * (1.0 / HEAD)
            var = head_sum(yc * yc, 1) * (1.0 / HEAD)
            o = yc * lax.rsqrt(var + GN_EPS) * lnw_ref[:, sl] + lnb_ref[:, sl]
            bonus = (head_sum(r_ref[0, rs, sl].astype(F32) * ksum * rk_ref[:, sl], 1)
                     * v_ref[0, rs, sl].astype(F32))
            og_scr[rs, sl] = ((o + bonus) * g[:, sl]).astype(og_scr.dtype)
        x1 = res_ref[0, rs, :] + gate_ref[0] * _dot(og_scr[rs, :], wo_ref[...])
        o_ref[0, rs, :] = x1
        h_ref[0, rs, :] = _norm_mod(x1, g_ref[...], sh_ref[0], sc_ref[0]).astype(h_ref.dtype)


def _rwkv_out(yf, yb, r, k, v, gs, aw, res, a2fp, a2bp, g2, wo, a0, ka, rk, lnw, lnb, gate, g, shift,
              scale, name):
    Bn, T, D = res.shape
    tm = _tile(T, 256, 16)
    rows = _tile(tm, 128, 16)
    G = gs.shape[-1]
    row = pl.BlockSpec((1, tm, D), lambda b, i: (b, i, 0))
    vec = pl.BlockSpec((1, D), lambda b, i: (0, 0))
    full = lambda arr: pl.BlockSpec(arr.shape, lambda b, i: (0,) * arr.ndim)
    return pl.pallas_call(
        functools.partial(_rwkv_out_kernel, rows=rows),
        grid=(Bn, T // tm),
        in_specs=[row, row, row, row, row,
                  pl.BlockSpec((1, tm, G), lambda b, i: (b, i, 0)),
                  pl.BlockSpec((1, tm, 2 * LANES), lambda b, i: (b, i, 0)),
                  row, full(a2fp), full(a2bp), full(g2), full(wo), full(a0),
                  vec, vec, vec, vec,
                  pl.BlockSpec((1, 1, D), _bsel(gate)), vec,
                  pl.BlockSpec((1, 1, D), _bsel(shift)),
                  pl.BlockSpec((1, 1, D), _bsel(scale))],
        out_specs=[row, row],
        out_shape=[jax.ShapeDtypeStruct((Bn, T, D), F32), jax.ShapeDtypeStruct((Bn, T, D), BF16)],
        scratch_shapes=[pltpu.VMEM((tm, D), BF16)],
        compiler_params=_params("parallel", "parallel"),
        name=name,
    )(yf, yb, r, k, v, gs, aw, res, a2fp, a2bp, g2, wo, a0, ka, rk, lnw, lnb, gate, g[None], shift, scale)


def _conv_kernel(gb_ref, z_ref, cw_ref, o_ref):
    z = z_ref[0].astype(F32)
    T = z.shape[0]
    t = lax.broadcasted_iota(jnp.int32, (T, 1), 0)
    zp = jnp.where(t == 0, 0.0, pltpu.roll(z, 1, 0))
    zn = jnp.where(t == T - 1, 0.0, pltpu.roll(z, T - 1, 0))
    conv = zp * cw_ref[0:1, :] + z * cw_ref[1:2, :] + zn * cw_ref[2:3, :]
    o_ref[0] = (gb_ref[0] * conv).astype(o_ref.dtype)


def _conv(gb, z, cw):
    Bn, T, D = z.shape
    tn = _tile(D, 256, LANES)
    tok = pl.BlockSpec((1, T, tn), lambda b, j: (b, 0, j))
    return pl.pallas_call(
        _conv_kernel,
        grid=(Bn, D // tn),
        in_specs=[tok, tok, pl.BlockSpec((3, tn), lambda b, j: (0, j))],
        out_specs=tok,
        out_shape=jax.ShapeDtypeStruct((Bn, T, D), BF16),
        compiler_params=_params("parallel", "parallel"),
        name="short_conv",
    )(gb, z, cw)


def _pad_rows(w, rows):
    return jnp.pad(w, ((0, rows - w.shape[0]), (0, 0)))


def _pad_cols(w, cols):
    return jnp.pad(w, ((0, 0), (0, cols - w.shape[1])))


def _split_mod(mod_rows, D):
    return [mod_rows[:, m * D:(m + 1) * D][:, None, :] for m in range(6)]


def _ffn_branch(t1, h2, mods, ffn, tag):
    w13, layer, wdn = ffn
    act = _ffn_up(h2, w13, layer, name="ffn_up_" + tag)
    return _mm_res(act, wdn, t1, mods[5], name="ffn_down_" + tag)


def _rwkv_layer(x, ctx, mods_x, mods_c, g1, g2n, mix, wr, wk, wv, wo, w0, w1, w2, a0, a1, a2,
                lg1, lg2, k_k, k_a, r_k, ln_w, ln_b, ffn):
    D = x.shape[-1]
    H = D // HEAD
    w1cat = jnp.concatenate([_pad_cols(w1[0], LANES), _pad_cols(w1[1], LANES)], axis=1).astype(BF16)
    a1cat = jnp.concatenate([_pad_cols(a1[0], LANES), _pad_cols(a1[1], LANES)], axis=1).astype(BF16)
    w2p = [_pad_rows(w2[d], LANES).astype(BF16) for d in range(2)]
    a2p = [_pad_rows(a2[d], LANES).astype(BF16) for d in range(2)]
    wr, wk, wv, wo = (t.astype(BF16) for t in (wr, wk, wv, wo))
    lg1, lg2 = lg1.astype(BF16), lg2.astype(BF16)
    rk = r_k.reshape(1, D)

    xr, xw, xk, xv, xa, xg = _prep_ctx(ctx, g1, mods_c[0], mods_c[1], mix)
    lora_c = dict(tw=_mm(xw, w1cat, BF16, act="tanh", name="lora_w_c"),
                  aw=_mm(xa, a1cat, BF16, name="lora_a_c"),
                  gs=_mm(xg, lg1, BF16, act="sigmoid", name="lora_g_c"))
    ins = {"c": (xr, xk, xv, lora_c["tw"], lora_c["aw"], lora_c["gs"]),
           "x": _prep_latent(x, g1, mods_x[0], mods_x[1], mix, w1cat, a1cat, lg1)}
    sets = {}
    for tag, (xr, xk, xv, tw, aw, gs) in ins.items():
        sets[tag] = dict(r=_mm(xr, wr, BF16, name="proj_r_" + tag),
                         k=_mm(xk, wk, BF16, name="proj_k_" + tag),
                         v=_mm(xv, wv, BF16, name="proj_v_" + tag), tw=tw, aw=aw, gs=gs)

    ys = {"c": [], "x": []}
    zero_state = jnp.zeros((x.shape[0], H // 2, 2 * HEAD, 2 * HEAD), F32)
    for d in range(2):
        state = zero_state
        for tag in ("c", "x"):
            s = sets[tag]
            y, state = _wkv(s["r"], s["k"], s["v"], s["tw"], s["aw"], w2p[d], a2p[d],
                            w0[d][None], a0[d][None], k_k[None], k_a[None], state, d)
            ys[tag].append(y)

    outs = []
    for tag, tok, mods in (("c", ctx, mods_c), ("x", x, mods_x)):
        s = sets[tag]
        t1, h2 = _rwkv_out(ys[tag][0], ys[tag][1], s["r"], s["k"], s["v"], s["gs"], s["aw"], tok,
                           a2p[0], a2p[1], lg2, wo, a0, k_a[None], rk, ln_w[None], ln_b[None],
                           mods[2], g2n, mods[3], mods[4], name="rwkv_out_" + tag)
        outs.append(_ffn_branch(t1, h2, mods, ffn, tag))
    return outs[1], outs[0]


def _conv_layer(x, mods, g1, g2n, w_in, conv_w, w_out, ffn):
    gb, z = _norm_mm(x, g1, mods[0], mods[1], w_in.astype(BF16), 3, _conv_in_combine, (BF16, BF16),
                     name="conv_in")
    p = _conv(gb, z, conv_w)
    t1, h2 = _mm_res_norm(p, w_out.astype(BF16), x, mods[2], g2n, mods[3], mods[4], name="conv_out")
    return _ffn_branch(t1, h2, mods, ffn, "x")


def kernel(x, c, ctx, c_ctx, norm1_g, norm2_g, ada_w, ada_b, rw_mix, rw_wr, rw_wk, rw_wv, rw_wo,
           rw_w0, rw_w1, rw_w2, rw_a0, rw_a1, rw_a2, rw_g1, rw_g2, rw_kk, rw_ka, rw_rk, rw_lnw,
           rw_lnb, sc_win, sc_conv, sc_wout, ffn_w13, ffn_w2, final_g):
    B, T, D = x.shape
    depth = norm1_g.shape[0]
    rows = -(-(B + 1) // 8) * 8
    cond = jnp.zeros((rows, D), F32).at[:B].set(c).at[B].set(c_ctx)
    for i in range(depth):
        last = i == depth - 1
        j = i // 2
        mod = _ada(cond, ada_w, ada_b, i)
        mods_x = _split_mod(mod[:B], D)
        mods_c = _split_mod(mod[B:B + 1], D)
        ffn = (ffn_w13, i, ffn_w2[i].astype(BF16))
        if i % 2 == 0:
            x, ctx_new = _rwkv_layer(
                x, ctx, mods_x, mods_c, norm1_g[i], norm2_g[i], rw_mix[j], rw_wr[j], rw_wk[j],
                rw_wv[j], rw_wo[j], rw_w0[j], rw_w1[j], rw_w2[j], rw_a0[j], rw_a1[j], rw_a2[j],
                rw_g1[j], rw_g2[j], rw_kk[j], rw_ka[j], rw_rk[j], rw_lnw[j], rw_lnb[j], ffn)
            ctx = ctx_new
        else:
            x = _conv_layer(x, mods_x, norm1_g[i], norm2_g[i], sc_win[j], sc_conv[j], sc_wout[j],
                            ffn)
            if not last:
                ctx = _conv_layer(ctx, mods_c, norm1_g[i], norm2_g[i], sc_win[j], sc_conv[j],
                                  sc_wout[j], ffn)
    zeros = jnp.zeros((1, 1, D), F32)
    return _norm(x, final_g, zeros, zeros)
```

```python
import functools
import math

import jax
import jax.numpy as jnp
from jax import lax
from jax.experimental import pallas as pl
from jax.experimental.pallas import tpu as pltpu

HEAD = 64
GRID_W = 64
CHUNK = 64
NORM_EPS = 1e-6
GN_EPS = 64e-5
LANES = 128
VMEM_LIMIT = 56 * 1024 * 1024

F32 = jnp.float32
BF16 = jnp.bfloat16
HI = lax.Precision.HIGHEST


def _params(*sem):
    return pltpu.CompilerParams(dimension_semantics=sem, vmem_limit_bytes=VMEM_LIMIT)


def _tile(n, pref, mult):
    t = min(pref, n)
    t -= t % mult
    while t >= mult:
        if n % t == 0:
            return t
        t -= mult
    return n


def _sigmoid(x):
    return 1.0 / (1.0 + jnp.exp(-x))


def _norm_mod(x, g, shift, scale, cols=None):
    rs = lax.rsqrt(jnp.mean(x * x, axis=-1, keepdims=True) + NORM_EPS)
    if cols is not None:
        x, g, shift, scale = (t[:, cols[0]:cols[1]] for t in (x, g, shift, scale))
    return (x * rs) * (g * (1.0 + scale)) + shift


def _dot(a, b, precision=None):
    return jnp.dot(a, b, preferred_element_type=F32, precision=precision)


def _dot_t(a, b, ca, cb, precision=None):
    return lax.dot_general(a, b, (((ca,), (cb,)), ((), ())), preferred_element_type=F32,
                           precision=precision)


def _ada_kernel(c_ref, w_ref, b_ref, o_ref):
    c = c_ref[...]
    s = c * _sigmoid(c)
    o_ref[...] = _dot(s.astype(BF16), w_ref[0].astype(BF16)) + b_ref[0]


def _ada(cond, w, b, layer):
    R, D = cond.shape
    N = w.shape[2]
    tn = _tile(N, 1024, LANES)
    return pl.pallas_call(
        _ada_kernel,
        grid=(N // tn,),
        in_specs=[pl.BlockSpec((R, D), lambda j: (0, 0)),
                  pl.BlockSpec((1, D, tn), lambda j: (layer, 0, j)),
                  pl.BlockSpec((1, 1, tn), lambda j: (layer, 0, j))],
        out_specs=pl.BlockSpec((R, tn), lambda j: (0, j)),
        out_shape=jax.ShapeDtypeStruct((R, N), F32),
        compiler_params=_params("parallel"),
        name="ada_mod",
    )(cond, w, b[:, None, :])


def _norm_kernel(x_ref, g_ref, sh_ref, sc_ref, o_ref):
    o_ref[0] = _norm_mod(x_ref[0], g_ref[...], sh_ref[0], sc_ref[0])


def _bsel(arr):
    if arr.shape[0] == 1:
        return lambda b, *_: (0, 0, 0)
    return lambda b, *_: (b, 0, 0)


def _norm(x, g, shift, scale):
    Bn, T, D = x.shape
    tm = _tile(T, 512, 8)
    return pl.pallas_call(
        _norm_kernel,
        grid=(Bn, T // tm),
        in_specs=[pl.BlockSpec((1, tm, D), lambda b, i: (b, i, 0)),
                  pl.BlockSpec((1, D), lambda b, i: (0, 0)),
                  pl.BlockSpec((1, 1, D), _bsel(shift)),
                  pl.BlockSpec((1, 1, D), _bsel(scale))],
        out_specs=pl.BlockSpec((1, tm, D), lambda b, i: (b, i, 0)),
        out_shape=jax.ShapeDtypeStruct((Bn, T, D), F32),
        compiler_params=_params("parallel", "parallel"),
        name="norm",
    )(x, g[None], shift, scale)


def _write_mix(out_refs, mix_ref, h, shifted, c0, c1):
    xx = shifted - h
    for m, o_ref in enumerate(out_refs):
        o_ref[0, :, c0:c1] = (h + xx * mix_ref[m:m + 1, c0:c1]).astype(o_ref.dtype)


def _prep_latent_kernel(x_ref, xu_ref, xd_ref, g_ref, sh_ref, sc_ref, mix_ref, w1_ref, a1_ref, g1_ref,
                        xr_ref, xk_ref, xv_ref, tw_ref, aw_ref, gs_ref, h_scr, xm_scr):
    i = pl.program_id(1)
    n = pl.num_programs(1)
    g, sh, sc = g_ref[...], sh_ref[0], sc_ref[0]
    tm, D = x_ref.shape[1:]
    q = D // 4
    W = GRID_W
    h_scr[0:W, 2 * q:3 * q] = jnp.where(i > 0, _norm_mod(xu_ref[0], g, sh, sc, (2 * q, 3 * q)), 0.0)
    h_scr[W - 8:W, 0:q] = jnp.zeros((8, q), F32)
    for r0 in range(0, tm, W):
        h_scr[W + r0:2 * W + r0, :] = _norm_mod(x_ref[0, r0:r0 + W, :], g, sh, sc)
    h_scr[W + tm:, 3 * q:] = jnp.where(i < n - 1, _norm_mod(xd_ref[0], g, sh, sc, (3 * q, D)), 0.0)
    h_scr[W + tm:W + tm + 8, q:2 * q] = jnp.zeros((8, q), F32)
    wide = {0: xr_ref, 2: xk_ref, 3: xv_ref}
    lora = {1: 0, 4: 1, 5: 2}
    lora_w = (w1_ref, a1_ref, g1_ref)
    acc = [None] * 3
    R = 32
    row = lax.broadcasted_iota(jnp.int32, (R, 1), 0)
    for k, off in enumerate((-1, 1, -W, W)):
        c0, c1 = k * q, (k + 1) * q
        mixk = [mix_ref[m:m + 1, c0:c1] for m in range(6)]
        for r0 in range(0, tm, R):
            h = h_scr[W + r0:W + r0 + R, c0:c1]
            s = h_scr[W + r0 + off:W + r0 + off + R, c0:c1]
            if off == -1 and r0 % W == 0:
                s = jnp.where(row == 0, 0.0, s)
            if off == 1 and (r0 + R) % W == 0:
                s = jnp.where(row == R - 1, 0.0, s)
            xx = s - h
            for m in range(6):
                xm = (h + xx * mixk[m]).astype(BF16)
                if m in wide:
                    wide[m][0, r0:r0 + R, c0:c1] = xm
                else:
                    xm_scr[lora[m], r0:r0 + R, c0:c1] = xm
        for l in range(3):
            part = _dot(xm_scr[l, :, c0:c1], lora_w[l][c0:c1, :])
            acc[l] = part if acc[l] is None else acc[l] + part
    tw_ref[0] = jnp.tanh(acc[0]).astype(tw_ref.dtype)
    aw_ref[0] = acc[1].astype(aw_ref.dtype)
    gs_ref[0] = _sigmoid(acc[2]).astype(gs_ref.dtype)


def _prep_latent(x, g, shift, scale, mix, w1cat, a1cat, lg1):
    Bn, T, D = x.shape
    rows_per_tile = _tile(T // GRID_W, 4, 1)
    tm = rows_per_tile * GRID_W
    nrow = T // GRID_W
    tok = pl.BlockSpec((1, tm, D), lambda b, i: (b, i, 0))
    full = lambda arr: pl.BlockSpec(arr.shape, lambda b, i: (0,) * arr.ndim)
    small = lambda arr: pl.BlockSpec((1, tm, arr.shape[1]), lambda b, i: (b, i, 0))
    return pl.pallas_call(
        _prep_latent_kernel,
        grid=(Bn, T // tm),
        in_specs=[tok,
                  pl.BlockSpec((1, GRID_W, D),
                               lambda b, i: (b, jnp.maximum(i * rows_per_tile - 1, 0), 0)),
                  pl.BlockSpec((1, GRID_W, D),
                               lambda b, i: (b, jnp.minimum((i + 1) * rows_per_tile, nrow - 1), 0)),
                  pl.BlockSpec((1, D), lambda b, i: (0, 0)),
                  pl.BlockSpec((1, 1, D), _bsel(shift)),
                  pl.BlockSpec((1, 1, D), _bsel(scale)),
                  pl.BlockSpec((6, D), lambda b, i: (0, 0)),
                  full(w1cat), full(a1cat), full(lg1)],
        out_specs=[tok, tok, tok, small(w1cat), small(a1cat), small(lg1)],
        out_shape=[jax.ShapeDtypeStruct((Bn, T, D), BF16)] * 3
        + [jax.ShapeDtypeStruct((Bn, T, w.shape[1]), BF16) for w in (w1cat, a1cat, lg1)],
        scratch_shapes=[pltpu.VMEM((tm + 2 * GRID_W, D), F32), pltpu.VMEM((3, tm, D), BF16)],
        compiler_params=_params("parallel", "parallel"),
        name="prep_latent",
    )(x, x, x, g[None], shift, scale, mix, w1cat, a1cat, lg1)


def _prep_ctx_kernel(x_ref, g_ref, sh_ref, sc_ref, mix_ref, *out_refs):
    h = _norm_mod(x_ref[0], g_ref[...], sh_ref[0], sc_ref[0])
    L, D = h.shape
    half = D // 2
    t = lax.broadcasted_iota(jnp.int32, (L, 1), 0)
    h0, h1 = h[:, :half], h[:, half:]
    prev = jnp.where(t == 0, 0.0, pltpu.roll(h0, 1, 0))
    nxt = jnp.where(t == L - 1, 0.0, pltpu.roll(h1, L - 1, 0))
    _write_mix(out_refs, mix_ref, h0, prev, 0, half)
    _write_mix(out_refs, mix_ref, h1, nxt, half, D)


def _prep_ctx(x, g, shift, scale, mix):
    Bn, L, D = x.shape
    return pl.pallas_call(
        _prep_ctx_kernel,
        grid=(Bn,),
        in_specs=[pl.BlockSpec((1, L, D), lambda b: (b, 0, 0)),
                  pl.BlockSpec((1, D), lambda b: (0, 0)),
                  pl.BlockSpec((1, 1, D), _bsel(shift)),
                  pl.BlockSpec((1, 1, D), _bsel(scale)),
                  pl.BlockSpec((6, D), lambda b: (0, 0))],
        out_specs=[pl.BlockSpec((1, L, D), lambda b: (b, 0, 0))] * 6,
        out_shape=[jax.ShapeDtypeStruct((Bn, L, D), BF16)] * 6,
        compiler_params=_params("parallel"),
        name="prep_ctx",
    )(x, g[None], shift, scale, mix)


def _mm_kernel(a_ref, w_ref, o_ref, *, act):
    acc = _dot(a_ref[0], w_ref[...].astype(BF16))
    if act == "tanh":
        acc = jnp.tanh(acc)
    elif act == "sigmoid":
        acc = _sigmoid(acc)
    o_ref[0] = acc.astype(o_ref.dtype)


def _mm(a, w, out_dtype, act=None, name="mm"):
    Bn, T, K = a.shape
    N = w.shape[1]
    tm = _tile(T, 1024, 16)
    tn = _tile(N, 1024, LANES)
    return pl.pallas_call(
        functools.partial(_mm_kernel, act=act),
        grid=(Bn, T // tm, N // tn),
        in_specs=[pl.BlockSpec((1, tm, K), lambda b, i, j: (b, i, 0)),
                  pl.BlockSpec((K, tn), lambda b, i, j: (0, j))],
        out_specs=pl.BlockSpec((1, tm, tn), lambda b, i, j: (b, i, j)),
        out_shape=jax.ShapeDtypeStruct((Bn, T, N), out_dtype),
        compiler_params=_params("parallel", "parallel", "parallel"),
        name=name,
    )(a, w)


def _mm_res_kernel(a_ref, w_ref, res_ref, gate_ref, o_ref):
    o_ref[0] = res_ref[0] + gate_ref[0] * _dot(a_ref[0], w_ref[...])


def _mm_res(a, w, res, gate, name="mm_res"):
    Bn, T, K = a.shape
    N = w.shape[1]
    tm = _tile(T, 1024, 16)
    tn = _tile(N, 1024 if K <= 2048 else 512, LANES)
    gsel = _bsel(gate)
    return pl.pallas_call(
        _mm_res_kernel,
        grid=(Bn, T // tm, N // tn),
        in_specs=[pl.BlockSpec((1, tm, K), lambda b, i, j: (b, i, 0)),
                  pl.BlockSpec((K, tn), lambda b, i, j: (0, j)),
                  pl.BlockSpec((1, tm, tn), lambda b, i, j: (b, i, j)),
                  pl.BlockSpec((1, 1, tn), lambda b, i, j: gsel(b)[:2] + (j,))],
        out_specs=pl.BlockSpec((1, tm, tn), lambda b, i, j: (b, i, j)),
        out_shape=jax.ShapeDtypeStruct((Bn, T, N), F32),
        compiler_params=_params("parallel", "parallel", "parallel"),
        name=name,
    )(a, w, res, gate)


def _mm_res_norm_kernel(a_ref, w_ref, res_ref, gate_ref, g_ref, sh_ref, sc_ref, o_ref, h_ref, *, rows):
    for r0 in range(0, a_ref.shape[1], rows):
        rs = slice(r0, r0 + rows)
        x1 = res_ref[0, rs, :] + gate_ref[0] * _dot(a_ref[0, rs, :], w_ref[...])
        o_ref[0, rs, :] = x1
        h_ref[0, rs, :] = _norm_mod(x1, g_ref[...], sh_ref[0], sc_ref[0]).astype(h_ref.dtype)


def _mm_res_norm(a, w, res, gate, g, shift, scale, name):
    Bn, T, K = a.shape
    N = w.shape[1]
    tm = _tile(T, 512, 16)
    rows = _tile(tm, 256, 16)
    row = pl.BlockSpec((1, tm, N), lambda b, i: (b, i, 0))
    return pl.pallas_call(
        functools.partial(_mm_res_norm_kernel, rows=rows),
        grid=(Bn, T // tm),
        in_specs=[pl.BlockSpec((1, tm, K), lambda b, i: (b, i, 0)),
                  pl.BlockSpec((K, N), lambda b, i: (0, 0)),
                  row,
                  pl.BlockSpec((1, 1, N), _bsel(gate)),
                  pl.BlockSpec((1, N), lambda b, i: (0, 0)),
                  pl.BlockSpec((1, 1, N), _bsel(shift)),
                  pl.BlockSpec((1, 1, N), _bsel(scale))],
        out_specs=[row, row],
        out_shape=[jax.ShapeDtypeStruct((Bn, T, N), F32), jax.ShapeDtypeStruct((Bn, T, N), BF16)],
        compiler_params=_params("parallel", "parallel"),
        name=name,
    )(a, w, res, gate, g[None], shift, scale)


def _ffn_up_kernel(h_ref, wa_ref, wb_ref, o_ref, w_scr, *, rows):
    @pl.when((pl.program_id(1) == 0) & (pl.program_id(2) == 0))
    def _():
        w_scr[0] = wa_ref[0].astype(BF16)
        w_scr[1] = wb_ref[0].astype(BF16)

    for r0 in range(0, h_ref.shape[1], rows):
        rs = slice(r0, r0 + rows)
        h = h_ref[0, rs, :]
        a = _dot(h, w_scr[0])
        o_ref[0, rs, :] = (a * _sigmoid(a) * _dot(h, w_scr[1])).astype(o_ref.dtype)


def _ffn_up(h, w13, layer, name):
    Bn, T, D = h.shape
    F = w13.shape[2] // 2
    tm = _tile(T, 2048, 16)
    rows = _tile(tm, 512, 16)
    tn = _tile(F, 512, LANES)
    nj = F // tn
    return pl.pallas_call(
        functools.partial(_ffn_up_kernel, rows=rows),
        grid=(nj, Bn, T // tm),
        in_specs=[pl.BlockSpec((1, tm, D), lambda j, b, i: (b, i, 0)),
                  pl.BlockSpec((1, D, tn), lambda j, b, i: (layer, 0, j)),
                  pl.BlockSpec((1, D, tn), lambda j, b, i: (layer, 0, j + nj))],
        out_specs=pl.BlockSpec((1, tm, tn), lambda j, b, i: (b, i, j)),
        out_shape=jax.ShapeDtypeStruct((Bn, T, F), BF16),
        scratch_shapes=[pltpu.VMEM((2, D, tn), BF16)],
        compiler_params=_params("arbitrary", "arbitrary", "arbitrary"),
        name=name,
    )(h, w13, w13)


def _swiglu_combine(a, b):
    return (a * _sigmoid(a) * b,)


def _conv_in_combine(gb, gc, u):
    return gb, gc * u


def _norm_mm_kernel(x_ref, g_ref, sh_ref, sc_ref, *rest, nw, combine):
    w_refs, out_refs, h_scr = rest[:nw], rest[nw:-1], rest[-1]

    @pl.when(pl.program_id(2) == 0)
    def _():
        h_scr[...] = _norm_mod(x_ref[0], g_ref[...], sh_ref[0], sc_ref[0]).astype(BF16)

    h = h_scr[...]
    outs = combine(*[_dot(h, w_ref[...].astype(BF16)) for w_ref in w_refs])
    for o_ref, o in zip(out_refs, outs):
        o_ref[0] = o.astype(o_ref.dtype)


def _norm_mm(x, g, shift, scale, w, nw, combine, out_dtypes, name):
    Bn, T, D = x.shape
    N = w.shape[1] // nw
    tm = _tile(T, 1024, 16)
    tn = _tile(N, 512, LANES)
    nj = N // tn
    w_specs = [pl.BlockSpec((D, tn), functools.partial(lambda b, i, j, m: (0, j + m * nj), m=m))
               for m in range(nw)]
    return pl.pallas_call(
        functools.partial(_norm_mm_kernel, nw=nw, combine=combine),
        grid=(Bn, T // tm, nj),
        in_specs=[pl.BlockSpec((1, tm, D), lambda b, i, j: (b, i, 0)),
                  pl.BlockSpec((1, D), lambda b, i, j: (0, 0)),
                  pl.BlockSpec((1, 1, D), _bsel(shift)),
                  pl.BlockSpec((1, 1, D), _bsel(scale))] + w_specs,
        out_specs=[pl.BlockSpec((1, tm, tn), lambda b, i, j: (b, i, j))] * len(out_dtypes),
        out_shape=[jax.ShapeDtypeStruct((Bn, T, N), dt) for dt in out_dtypes],
        scratch_shapes=[pltpu.VMEM((tm, D), BF16)],
        compiler_params=_params("parallel", "parallel", "arbitrary"),
        name=name,
    )(x, g[None], shift, scale, *([w] * nw))


def _seg_ones(width):
    shift = HEAD.bit_length() - 1
    r = lax.shift_right_logical(lax.broadcasted_iota(jnp.int32, (width, width), 0), shift)
    c = lax.shift_right_logical(lax.broadcasted_iota(jnp.int32, (width, width), 1), shift)
    return (r == c).astype(F32)


def _split3(x):
    hi = x.astype(BF16)
    r1 = x - hi.astype(F32)
    mid = r1.astype(BF16)
    lo = (r1 - mid.astype(F32)).astype(BF16)
    return hi, mid, lo


def _wkv_kernel(r_ref, k_ref, v_ref, tw_ref, aw_ref, w2_ref, a2_ref, w0_ref, a0_ref, kk_ref, ka_ref,
                s0_ref, y_ref, sout_ref, s_scr, x_scr, r2_scr, bv_scr, vb_scr, z_scr, wt_scr,
                *, reverse, npair):
    c = pl.program_id(2)
    C = CHUNK
    PW = 2 * HEAD

    @pl.when(c == 0)
    def _():
        s_scr[...] = s0_ref[0]
        x_scr[...] = jnp.zeros_like(x_scr)
        r2_scr[...] = jnp.zeros_like(r2_scr)
        bv_scr[...] = jnp.zeros_like(bv_scr)
        vb_scr[...] = jnp.zeros_like(vb_scr)
        z_scr[...] = jnp.zeros_like(z_scr)
        wt_scr[...] = jnp.ones_like(wt_scr)

    rr = lax.shift_right_logical(lax.broadcasted_iota(jnp.int32, (PW, PW), 0), HEAD.bit_length() - 1)
    cc = lax.shift_right_logical(lax.broadcasted_iota(jnp.int32, (PW, PW), 1), HEAD.bit_length() - 1)
    same = rr == cc
    same_bf = same.astype(BF16)

    def bd(x):
        xb = x.astype(BF16)
        return jnp.concatenate([xb, xb], axis=0) * same_bf

    t2 = lax.broadcasted_iota(jnp.int32, (C, PW), 0)
    s2 = lax.broadcasted_iota(jnp.int32, (C, PW), 1) & (HEAD - 1)
    before = (s2 > t2) if reverse else (s2 < t2)
    upto = before | (s2 == t2)
    pairs = range(npair)
    sls = [slice(p * PW, (p + 1) * PW) for p in pairs]

    r, k, v = (t[0].astype(F32) for t in (r_ref, k_ref, v_ref))
    z = w0_ref[...] + _dot(tw_ref[0], w2_ref[...])
    a_pre = a0_ref[...] + _dot(aw_ref[0], a2_ref[...])

    S = [s_scr[p] for p in pairs]
    X = [x_scr[p] for p in pairs]
    G = [_dot_t(X[p], r2_scr[p], 1, 1) for p in pairs]
    XS = [_dot_t(X[p], S[p].astype(BF16), 1, 1) for p in pairs]
    u = [XS[p][:C] + _dot(jnp.where(before, G[p][:C, PW:], 0.0).astype(BF16), bv_scr[p]) for p in pairs]
    P = [jnp.where(before, G[p][:C, :PW], 0.0) for p in pairs]

    lw = -math.exp(-0.5) * _sigmoid(z)
    kkv = k * kk_ref[...]
    kk2 = kkv * kkv
    t_i = lax.broadcasted_iota(jnp.int32, (C, C), 0)
    s_i = lax.broadcasted_iota(jnp.int32, (C, C), 1)
    tri = ((s_i >= t_i) if reverse else (s_i <= t_i)).astype(BF16)
    cum = _dot(jnp.concatenate([tri, tri], axis=1),
               jnp.concatenate(_split3(lw)[:2], axis=0))
    same2 = jnp.concatenate([same_bf, same_bf], axis=0)
    ss = [_dot(jnp.concatenate(_split3(kk2[:, sl])[:2], axis=1), same2) for sl in sls]

    n_sq = C.bit_length() - 1
    for j in range(n_sq):
        Pb = [P[p].astype(BF16) for p in pairs]
        if j < n_sq - 1:
            PU = [_dot(Pb[p], jnp.concatenate([bd(P[p]), bd(u[p])], axis=1)) for p in pairs]
            P = [PU[p][:, :PW] for p in pairs]
            u = [u[p] + PU[p][:, PW:] for p in pairs]
        else:
            u = [u[p] + _dot(Pb[p], bd(u[p])) for p in pairs]
    for p in pairs:
        R = jnp.concatenate([jnp.where(upto, G[p][C:, :PW], 0.0),
                             jnp.where(upto, G[p][C:, PW:], 0.0)], axis=1).astype(BF16)
        y_ref[0, :, sls[p]] = XS[p][C:] + _dot(R, jnp.concatenate([bd(u[p]), bv_scr[p]], axis=0))
    for p in pairs:
        UV = jnp.concatenate([u[p].astype(BF16), vb_scr[p]], axis=0)
        dS = _dot_t(UV, z_scr[p], 0, 0)
        s_scr[p] = (S[p] + jnp.where(same, dS, 0.0)) * wt_scr[p, 0:1, :]

    a_sig = _sigmoid(a_pre)
    kd = k * (1.0 + (a_sig - 1.0) * ka_ref[...])
    e_pos = jnp.exp(cum)
    e_neg = jnp.exp(-cum)
    e_prev = jnp.exp(cum - lw)
    last = 0 if reverse else C - 1
    for p, sl in zip(pairs, sls):
        kkn = kkv[:, sl] * lax.rsqrt(jnp.maximum(ss[p], 1e-24))
        at = (-kkn) * e_prev[:, sl]
        bt = (kkn * a_sig[:, sl]) * e_neg[:, sl]
        rt = r[:, sl] * e_pos[:, sl]
        kt = kd[:, sl] * e_neg[:, sl]
        x_scr[p] = jnp.concatenate([at, rt], axis=0).astype(BF16)
        r2_scr[p] = jnp.concatenate([bd(bt), bd(kt)], axis=0)
        bv_scr[p] = bd(v[:, sl])
        vb_scr[p] = v[:, sl].astype(BF16)
        z_scr[p] = jnp.concatenate([bt, kt], axis=0).astype(BF16)
        wt_scr[p] = jnp.broadcast_to(e_pos[last:last + 1, sl], wt_scr.shape[1:])

    @pl.when(c == pl.num_programs(2) - 1)
    def _():
        sout_ref[0] = s_scr[...]


def _wkv(r, k, v, tw, aw, w2p, a2p, w0, a0, kk, ka, s0, d):
    Bn, T, D = r.shape
    PW = 2 * HEAD
    npairs = D // PW
    npair = _tile(npairs, 16, 1)
    hw = npair * PW
    nc = T // CHUNK
    reverse = d == 1
    pos = (lambda j: nc - 1 - j) if reverse else (lambda j: j)
    cin = lambda c: pos(jnp.minimum(c, nc - 1))
    cout = lambda c: pos(jnp.maximum(c - 1, 0))
    tok = pl.BlockSpec((1, CHUNK, hw), lambda b, g, c: (b, cin(c), g))
    lora = pl.BlockSpec((1, CHUNK, LANES), lambda b, g, c: (b, cin(c), d))
    lw2 = pl.BlockSpec((LANES, hw), lambda b, g, c: (0, g))
    vec = pl.BlockSpec((1, hw), lambda b, g, c: (0, g))
    st = pl.BlockSpec((1, npair, PW, PW), lambda b, g, c: (b, g, 0, 0))
    return pl.pallas_call(
        functools.partial(_wkv_kernel, reverse=reverse, npair=npair),
        grid=(Bn, npairs // npair, nc + 1),
        in_specs=[tok, tok, tok, lora, lora, lw2, lw2, vec, vec, vec, vec, st],
        out_specs=[pl.BlockSpec((1, CHUNK, hw), lambda b, g, c: (b, cout(c), g)), st],
        out_shape=[jax.ShapeDtypeStruct((Bn, T, D), F32),
                   jax.ShapeDtypeStruct((Bn, npairs, PW, PW), F32)],
        scratch_shapes=[pltpu.VMEM((npair, PW, PW), F32),
                        pltpu.VMEM((npair, 2 * CHUNK, PW), BF16),
                        pltpu.VMEM((npair, 2 * PW, PW), BF16),
                        pltpu.VMEM((npair, PW, PW), BF16),
                        pltpu.VMEM((npair, CHUNK, PW), BF16),
                        pltpu.VMEM((npair, 2 * CHUNK, PW), BF16),
                        pltpu.VMEM((npair, 8, PW), F32)],
        compiler_params=_params("parallel", "parallel", "arbitrary"),
        name="wkv_rev" if reverse else "wkv_fwd",
    )(r, k, v, tw, aw, w2p, a2p, w0, a0, kk, ka, s0)


def _rwkv_out_kernel(yf_ref, yb_ref, r_ref, k_ref, v_ref, gs_ref, aw_ref, res_ref, a2f_ref, a2b_ref,
                     g2_ref, wo_ref, a0_ref, ka_ref, rk_ref, lnw_ref, lnb_ref, gate_ref, g_ref, sh_ref,
                     sc_ref, o_ref, h_ref, og_scr, *, rows):
    tm, D = o_ref.shape[1:]
    PW = 2 * HEAD
    same = _seg_ones(PW).astype(BF16)
    same2 = jnp.concatenate([same, same], axis=0)

    def head_sum(x, pieces):
        if pieces == 1:
            return _dot(x.astype(BF16), same)
        return _dot(jnp.concatenate(_split3(x)[:2], axis=1), same2)

    for r0 in range(0, tm, rows):
        rs = slice(r0, r0 + rows)
        aw = aw_ref[0, rs, :]
        a_f = _sigmoid(a0_ref[0:1, :] + _dot(aw[:, :LANES], a2f_ref[...]))
        a_b = _sigmoid(a0_ref[1:2, :] + _dot(aw[:, LANES:], a2b_ref[...]))
        g = _dot(gs_ref[0, rs, :], g2_ref[...])
        for p in range(D // PW):
            sl = slice(p * PW, (p + 1) * PW)
            ksum = k_ref[0, rs, sl].astype(F32) * (2.0 + (a_f[:, sl] + a_b[:, sl] - 2.0) * ka_ref[:, sl])
            y = yf_ref[0, rs, sl] + yb_ref[0, rs, sl]
            yc = y - head_sum(y, 2) * (1.0 / HEAD)
            var = head_sum(yc * yc, 1) * (1.0 / HEAD)
            o = yc * lax.rsqrt(var + GN_EPS) * lnw_ref[:, sl] + lnb_ref[:, sl]
            bonus = (head_sum(r_ref[0, rs, sl].astype(F32) * ksum * rk_ref[:, sl], 1)
                     * v_ref[0, rs, sl].astype(F32))
            og_scr[rs, sl] = ((o + bonus) * g[:, sl]).astype(og_scr.dtype)
        x1 = res_ref[0, rs, :] + gate_ref[0] * _dot(og_scr[rs, :], wo_ref[...])
        o_ref[0, rs, :] = x1
        h_ref[0, rs, :] = _norm_mod(x1, g_ref[...], sh_ref[0], sc_ref[0]).astype(h_ref.dtype)


def _rwkv_out(yf, yb, r, k, v, gs, aw, res, a2fp, a2bp, g2, wo, a0, ka, rk, lnw, lnb, gate, g, shift,
              scale, name):
    Bn, T, D = res.shape
    tm = _tile(T, 256, 16)
    rows = _tile(tm, 128, 16)
    G = gs.shape[-1]
    row = pl.BlockSpec((1, tm, D), lambda b, i: (b, i, 0))
    vec = pl.BlockSpec((1, D), lambda b, i: (0, 0))
    full = lambda arr: pl.BlockSpec(arr.shape, lambda b, i: (0,) * arr.ndim)
    return pl.pallas_call(
        functools.partial(_rwkv_out_kernel, rows=rows),
        grid=(Bn, T // tm),
        in_specs=[row, row, row, row, row,
                  pl.BlockSpec((1, tm, G), lambda b, i: (b, i, 0)),
                  pl.BlockSpec((1, tm, 2 * LANES), lambda b, i: (b, i, 0)),
                  row, full(a2fp), full(a2bp), full(g2), full(wo), full(a0),
                  vec, vec, vec, vec,
                  pl.BlockSpec((1, 1, D), _bsel(gate)), vec,
                  pl.BlockSpec((1, 1, D), _bsel(shift)),
                  pl.BlockSpec((1, 1, D), _bsel(scale))],
        out_specs=[row, row],
        out_shape=[jax.ShapeDtypeStruct((Bn, T, D), F32), jax.ShapeDtypeStruct((Bn, T, D), BF16)],
        scratch_shapes=[pltpu.VMEM((tm, D), BF16)],
        compiler_params=_params("parallel", "parallel"),
        name=name,
    )(yf, yb, r, k, v, gs, aw, res, a2fp, a2bp, g2, wo, a0, ka, rk, lnw, lnb, gate, g[None], shift, scale)


def _conv_kernel(gb_ref, z_ref, cw_ref, o_ref):
    z = z_ref[0].astype(F32)
    T = z.shape[0]
    t = lax.broadcasted_iota(jnp.int32, (T, 1), 0)
    zp = jnp.where(t == 0, 0.0, pltpu.roll(z, 1, 0))
    zn = jnp.where(t == T - 1, 0.0, pltpu.roll(z, T - 1, 0))
    conv = zp * cw_ref[0:1, :] + z * cw_ref[1:2, :] + zn * cw_ref[2:3, :]
    o_ref[0] = (gb_ref[0] * conv).astype(o_ref.dtype)


def _conv(gb, z, cw):
    Bn, T, D = z.shape
    tn = _tile(D, 256, LANES)
    tok = pl.BlockSpec((1, T, tn), lambda b, j: (b, 0, j))
    return pl.pallas_call(
        _conv_kernel,
        grid=(Bn, D // tn),
        in_specs=[tok, tok, pl.BlockSpec((3, tn), lambda b, j: (0, j))],
        out_specs=tok,
        out_shape=jax.ShapeDtypeStruct((Bn, T, D), BF16),
        compiler_params=_params("parallel", "parallel"),
        name="short_conv",
    )(gb, z, cw)


def _pad_rows(w, rows):
    return jnp.pad(w, ((0, rows - w.shape[0]), (0, 0)))


def _pad_cols(w, cols):
    return jnp.pad(w, ((0, 0), (0, cols - w.shape[1])))


def _split_mod(mod_rows, D):
    return [mod_rows[:, m * D:(m + 1) * D][:, None, :] for m in range(6)]


def _ffn_branch(t1, h2, mods, ffn, tag):
    w13, layer, wdn = ffn
    act = _ffn_up(h2, w13, layer, name="ffn_up_" + tag)
    return _mm_res(act, wdn, t1, mods[5], name="ffn_down_" + tag)


def _rwkv_layer(x, ctx, mods_x, mods_c, g1, g2n, mix, wr, wk, wv, wo, w0, w1, w2, a0, a1, a2,
                lg1, lg2, k_k, k_a, r_k, ln_w, ln_b, ffn):
    D = x.shape[-1]
    H = D // HEAD
    w1cat = jnp.concatenate([_pad_cols(w1[0], LANES), _pad_cols(w1[1], LANES)], axis=1).astype(BF16)
    a1cat = jnp.concatenate([_pad_cols(a1[0], LANES), _pad_cols(a1[1], LANES)], axis=1).astype(BF16)
    w2p = [_pad_rows(w2[d], LANES).astype(BF16) for d in range(2)]
    a2p = [_pad_rows(a2[d], LANES).astype(BF16) for d in range(2)]
    wo = wo.astype(BF16)
    lg1, lg2 = lg1.astype(BF16), lg2.astype(BF16)
    rk = r_k.reshape(1, D)

    xr, xw, xk, xv, xa, xg = _prep_ctx(ctx, g1, mods_c[0], mods_c[1], mix)
    lora_c = dict(tw=_mm(xw, w1cat, BF16, act="tanh", name="lora_w_c"),
                  aw=_mm(xa, a1cat, BF16, name="lora_a_c"),
                  gs=_mm(xg, lg1, BF16, act="sigmoid", name="lora_g_c"))
    ins = {"c": (xr, xk, xv, lora_c["tw"], lora_c["aw"], lora_c["gs"]),
           "x": _prep_latent(x, g1, mods_x[0], mods_x[1], mix, w1cat, a1cat, lg1)}
    sets = {}
    for tag, (xr, xk, xv, tw, aw, gs) in ins.items():
        sets[tag] = dict(r=_mm(xr, wr, BF16, name="proj_r_" + tag),
                         k=_mm(xk, wk, BF16, name="proj_k_" + tag),
                         v=_mm(xv, wv, BF16, name="proj_v_" + tag), tw=tw, aw=aw, gs=gs)

    ys = {"c": [], "x": []}
    zero_state = jnp.zeros((x.shape[0], H // 2, 2 * HEAD, 2 * HEAD), F32)
    for d in range(2):
        state = zero_state
        for tag in ("c", "x"):
            s = sets[tag]
            y, state = _wkv(s["r"], s["k"], s["v"], s["tw"], s["aw"], w2p[d], a2p[d],
                            w0[d][None], a0[d][None], k_k[None], k_a[None], state, d)
            ys[tag].append(y)

    outs = []
    for tag, tok, mods in (("c", ctx, mods_c), ("x", x, mods_x)):
        s = sets[tag]
        t1, h2 = _rwkv_out(ys[tag][0], ys[tag][1], s["r"], s["k"], s["v"], s["gs"], s["aw"], tok,
                           a2p[0], a2p[1], lg2, wo, a0, k_a[None], rk, ln_w[None], ln_b[None],
                           mods[2], g2n, mods[3], mods[4], name="rwkv_out_" + tag)
        outs.append(_ffn_branch(t1, h2, mods, ffn, tag))
    return outs[1], outs[0]


def _conv_layer(x, mods, g1, g2n, w_in, conv_w, w_out, ffn):
    gb, z = _norm_mm(x, g1, mods[0], mods[1], w_in, 3, _conv_in_combine, (BF16, BF16),
                     name="conv_in")
    p = _conv(gb, z, conv_w)
    t1, h2 = _mm_res_norm(p, w_out.astype(BF16), x, mods[2], g2n, mods[3], mods[4], name="conv_out")
    return _ffn_branch(t1, h2, mods, ffn, "x")


def kernel(x, c, ctx, c_ctx, norm1_g, norm2_g, ada_w, ada_b, rw_mix, rw_wr, rw_wk, rw_wv, rw_wo,
           rw_w0, rw_w1, rw_w2, rw_a0, rw_a1, rw_a2, rw_g1, rw_g2, rw_kk, rw_ka, rw_rk, rw_lnw,
           rw_lnb, sc_win, sc_conv, sc_wout, ffn_w13, ffn_w2, final_g):
    B, T, D = x.shape
    depth = norm1_g.shape[0]
    rows = -(-(B + 1) // 8) * 8
    cond = jnp.zeros((rows, D), F32).at[:B].set(c).at[B].set(c_ctx)
    for i in range(depth):
        last = i == depth - 1
        j = i // 2
        mod = _ada(cond, ada_w, ada_b, i)
        mods_x = _split_mod(mod[:B], D)
        mods_c = _split_mod(mod[B:B + 1], D)
        ffn = (ffn_w13, i, ffn_w2[i].astype(BF16))
        if i % 2 == 0:
            x, ctx_new = _rwkv_layer(
                x, ctx, mods_x, mods_c, norm1_g[i], norm2_g[i], rw_mix[j], rw_wr[j], rw_wk[j],
                rw_wv[j], rw_wo[j], rw_w0[j], rw_w1[j], rw_w2[j], rw_a0[j], rw_a1[j], rw_a2[j],
                rw_g1[j], rw_g2[j], rw_kk[j], rw_ka[j], rw_rk[j], rw_lnw[j], rw_lnb[j], ffn)
            ctx = ctx_new
        else:
            x = _conv_layer(x, mods_x, norm1_g[i], norm2_g[i], sc_win[j], sc_conv[j], sc_wout[j],
                            ffn)
            if not last:
                ctx = _conv_layer(ctx, mods_c, norm1_g[i], norm2_g[i], sc_win[j], sc_conv[j],
                                  sc_wout[j], ffn)
    zeros = jnp.zeros((1, 1, D), F32)
    return _norm(x, final_g, zeros, zeros)
```

```python
import functools
import math

import jax
import jax.numpy as jnp
from jax import lax
from jax.experimental import pallas as pl
from jax.experimental.pallas import tpu as pltpu

HEAD = 64
GRID_W = 64
CHUNK = 64
NORM_EPS = 1e-6
GN_EPS = 64e-5
LANES = 128
VMEM_LIMIT = 56 * 1024 * 1024

F32 = jnp.float32
BF16 = jnp.bfloat16
HI = lax.Precision.HIGHEST


def _params(*sem):
    return pltpu.CompilerParams(dimension_semantics=sem, vmem_limit_bytes=VMEM_LIMIT)


def _tile(n, pref, mult):
    t = min(pref, n)
    t -= t % mult
    while t >= mult:
        if n % t == 0:
            return t
        t -= mult
    return n


def _sigmoid(x):
    return 1.0 / (1.0 + jnp.exp(-x))


def _norm_mod(x, g, shift, scale, cols=None):
    rs = lax.rsqrt(jnp.mean(x * x, axis=-1, keepdims=True) + NORM_EPS)
    if cols is not None:
        x, g, shift, scale = (t[:, cols[0]:cols[1]] for t in (x, g, shift, scale))
    return (x * rs) * (g * (1.0 + scale)) + shift


def _dot(a, b, precision=None):
    return jnp.dot(a, b, preferred_element_type=F32, precision=precision)


def _dot_t(a, b, ca, cb, precision=None):
    return lax.dot_general(a, b, (((ca,), (cb,)), ((), ())), preferred_element_type=F32,
                           precision=precision)


def _ada_kernel(c_ref, w_ref, b_ref, o_ref):
    c = c_ref[...]
    s = c * _sigmoid(c)
    o_ref[...] = _dot(s.astype(BF16), w_ref[0].astype(BF16)) + b_ref[0]


def _ada(cond, w, b, layer):
    R, D = cond.shape
    N = w.shape[2]
    tn = _tile(N, 1024, LANES)
    return pl.pallas_call(
        _ada_kernel,
        grid=(N // tn,),
        in_specs=[pl.BlockSpec((R, D), lambda j: (0, 0)),
                  pl.BlockSpec((1, D, tn), lambda j: (layer, 0, j)),
                  pl.BlockSpec((1, 1, tn), lambda j: (layer, 0, j))],
        out_specs=pl.BlockSpec((R, tn), lambda j: (0, j)),
        out_shape=jax.ShapeDtypeStruct((R, N), F32),
        compiler_params=_params("parallel"),
        name="ada_mod",
    )(cond, w, b[:, None, :])


def _norm_kernel(x_ref, g_ref, sh_ref, sc_ref, o_ref):
    o_ref[0] = _norm_mod(x_ref[0], g_ref[...], sh_ref[0], sc_ref[0])


def _bsel(arr):
    if arr.shape[0] == 1:
        return lambda b, *_: (0, 0, 0)
    return lambda b, *_: (b, 0, 0)


def _norm(x, g, shift, scale):
    Bn, T, D = x.shape
    tm = _tile(T, 512, 8)
    return pl.pallas_call(
        _norm_kernel,
        grid=(Bn, T // tm),
        in_specs=[pl.BlockSpec((1, tm, D), lambda b, i: (b, i, 0)),
                  pl.BlockSpec((1, D), lambda b, i: (0, 0)),
                  pl.BlockSpec((1, 1, D), _bsel(shift)),
                  pl.BlockSpec((1, 1, D), _bsel(scale))],
        out_specs=pl.BlockSpec((1, tm, D), lambda b, i: (b, i, 0)),
        out_shape=jax.ShapeDtypeStruct((Bn, T, D), F32),
        compiler_params=_params("parallel", "parallel"),
        name="norm",
    )(x, g[None], shift, scale)


def _write_mix(out_refs, mix_ref, h, shifted, c0, c1):
    xx = shifted - h
    for m, o_ref in enumerate(out_refs):
        o_ref[0, :, c0:c1] = (h + xx * mix_ref[m:m + 1, c0:c1]).astype(o_ref.dtype)


def _prep_latent_kernel(x_ref, xu_ref, xd_ref, g_ref, sh_ref, sc_ref, mix_ref, w1_ref, a1_ref, g1_ref,
                        xr_ref, xk_ref, xv_ref, tw_ref, aw_ref, gs_ref, h_scr, xm_scr):
    i = pl.program_id(1)
    n = pl.num_programs(1)
    g, sh, sc = g_ref[...], sh_ref[0], sc_ref[0]
    tm, D = x_ref.shape[1:]
    q = D // 4
    W = GRID_W
    h_scr[0:W, 2 * q:3 * q] = jnp.where(i > 0, _norm_mod(xu_ref[0], g, sh, sc, (2 * q, 3 * q)), 0.0)
    h_scr[W - 8:W, 0:q] = jnp.zeros((8, q), F32)
    for r0 in range(0, tm, W):
        h_scr[W + r0:2 * W + r0, :] = _norm_mod(x_ref[0, r0:r0 + W, :], g, sh, sc)
    h_scr[W + tm:, 3 * q:] = jnp.where(i < n - 1, _norm_mod(xd_ref[0], g, sh, sc, (3 * q, D)), 0.0)
    h_scr[W + tm:W + tm + 8, q:2 * q] = jnp.zeros((8, q), F32)
    wide = {0: xr_ref, 2: xk_ref, 3: xv_ref}
    lora = {1: 0, 4: 1, 5: 2}
    lora_w = (w1_ref, a1_ref, g1_ref)
    acc = [None] * 3
    R = 32
    row = lax.broadcasted_iota(jnp.int32, (R, 1), 0)
    for k, off in enumerate((-1, 1, -W, W)):
        c0, c1 = k * q, (k + 1) * q
        mixk = [mix_ref[m:m + 1, c0:c1] for m in range(6)]
        for r0 in range(0, tm, R):
            h = h_scr[W + r0:W + r0 + R, c0:c1]
            s = h_scr[W + r0 + off:W + r0 + off + R, c0:c1]
            if off == -1 and r0 % W == 0:
                s = jnp.where(row == 0, 0.0, s)
            if off == 1 and (r0 + R) % W == 0:
                s = jnp.where(row == R - 1, 0.0, s)
            xx = s - h
            for m in range(6):
                xm = (h + xx * mixk[m]).astype(BF16)
                if m in wide:
                    wide[m][0, r0:r0 + R, c0:c1] = xm
                else:
                    xm_scr[lora[m], r0:r0 + R, c0:c1] = xm
        for l in range(3):
            part = _dot(xm_scr[l, :, c0:c1], lora_w[l][c0:c1, :])
            acc[l] = part if acc[l] is None else acc[l] + part
    tw_ref[0] = jnp.tanh(acc[0]).astype(tw_ref.dtype)
    aw_ref[0] = acc[1].astype(aw_ref.dtype)
    gs_ref[0] = _sigmoid(acc[2]).astype(gs_ref.dtype)


def _prep_latent(x, g, shift, scale, mix, w1cat, a1cat, lg1):
    Bn, T, D = x.shape
    rows_per_tile = _tile(T // GRID_W, 4, 1)
    tm = rows_per_tile * GRID_W
    nrow = T // GRID_W
    tok = pl.BlockSpec((1, tm, D), lambda b, i: (b, i, 0))
    full = lambda arr: pl.BlockSpec(arr.shape, lambda b, i: (0,) * arr.ndim)
    small = lambda arr: pl.BlockSpec((1, tm, arr.shape[1]), lambda b, i: (b, i, 0))
    return pl.pallas_call(
        _prep_latent_kernel,
        grid=(Bn, T // tm),
        in_specs=[tok,
                  pl.BlockSpec((1, GRID_W, D),
                               lambda b, i: (b, jnp.maximum(i * rows_per_tile - 1, 0), 0)),
                  pl.BlockSpec((1, GRID_W, D),
                               lambda b, i: (b, jnp.minimum((i + 1) * rows_per_tile, nrow - 1), 0)),
                  pl.BlockSpec((1, D), lambda b, i: (0, 0)),
                  pl.BlockSpec((1, 1, D), _bsel(shift)),
                  pl.BlockSpec((1, 1, D), _bsel(scale)),
                  pl.BlockSpec((6, D), lambda b, i: (0, 0)),
                  full(w1cat), full(a1cat), full(lg1)],
        out_specs=[tok, tok, tok, small(w1cat), small(a1cat), small(lg1)],
        out_shape=[jax.ShapeDtypeStruct((Bn, T, D), BF16)] * 3
        + [jax.ShapeDtypeStruct((Bn, T, w.shape[1]), BF16) for w in (w1cat, a1cat, lg1)],
        scratch_shapes=[pltpu.VMEM((tm + 2 * GRID_W, D), F32), pltpu.VMEM((3, tm, D), BF16)],
        compiler_params=_params("parallel", "parallel"),
        name="prep_latent",
    )(x, x, x, g[None], shift, scale, mix, w1cat, a1cat, lg1)


def _prep_ctx_kernel(x_ref, g_ref, sh_ref, sc_ref, mix_ref, *out_refs):
    h = _norm_mod(x_ref[0], g_ref[...], sh_ref[0], sc_ref[0])
    L, D = h.shape
    half = D // 2
    t = lax.broadcasted_iota(jnp.int32, (L, 1), 0)
    h0, h1 = h[:, :half], h[:, half:]
    prev = jnp.where(t == 0, 0.0, pltpu.roll(h0, 1, 0))
    nxt = jnp.where(t == L - 1, 0.0, pltpu.roll(h1, L - 1, 0))
    _write_mix(out_refs, mix_ref, h0, prev, 0, half)
    _write_mix(out_refs, mix_ref, h1, nxt, half, D)


def _prep_ctx(x, g, shift, scale, mix):
    Bn, L, D = x.shape
    return pl.pallas_call(
        _prep_ctx_kernel,
        grid=(Bn,),
        in_specs=[pl.BlockSpec((1, L, D), lambda b: (b, 0, 0)),
                  pl.BlockSpec((1, D), lambda b: (0, 0)),
                  pl.BlockSpec((1, 1, D), _bsel(shift)),
                  pl.BlockSpec((1, 1, D), _bsel(scale)),
                  pl.BlockSpec((6, D), lambda b: (0, 0))],
        out_specs=[pl.BlockSpec((1, L, D), lambda b: (b, 0, 0))] * 6,
        out_shape=[jax.ShapeDtypeStruct((Bn, L, D), BF16)] * 6,
        compiler_params=_params("parallel"),
        name="prep_ctx",
    )(x, g[None], shift, scale, mix)


def _mm_kernel(a_ref, w_ref, o_ref, *, act):
    acc = _dot(a_ref[0], w_ref[...])
    if act == "tanh":
        acc = jnp.tanh(acc)
    elif act == "sigmoid":
        acc = _sigmoid(acc)
    o_ref[0] = acc.astype(o_ref.dtype)


def _mm(a, w, out_dtype, act=None, name="mm"):
    Bn, T, K = a.shape
    N = w.shape[1]
    M = Bn * T
    tm = _tile(M, 1024, 16)
    tn = _tile(N, 1024, LANES)
    out = pl.pallas_call(
        functools.partial(_mm_kernel, act=act),
        grid=(1, M // tm, N // tn),
        in_specs=[pl.BlockSpec((1, tm, K), lambda b, i, j: (b, i, 0)),
                  pl.BlockSpec((K, tn), lambda b, i, j: (0, j))],
        out_specs=pl.BlockSpec((1, tm, tn), lambda b, i, j: (b, i, j)),
        out_shape=jax.ShapeDtypeStruct((1, M, N), out_dtype),
        compiler_params=_params("parallel", "parallel", "parallel"),
        name=name,
    )(a.reshape(1, M, K), w)
    return out.reshape(Bn, T, N)


def _mm_res_kernel(a_ref, w_ref, res_ref, gate_ref, o_ref):
    o_ref[0] = res_ref[0] + gate_ref[0] * _dot(a_ref[0], w_ref[...])


def _mm_res(a, w, res, gate, name="mm_res"):
    Bn, T, K = a.shape
    N = w.shape[1]
    tm = _tile(T, 1024, 16)
    tn = _tile(N, 1024 if K <= 2048 else 512, LANES)
    gsel = _bsel(gate)
    return pl.pallas_call(
        _mm_res_kernel,
        grid=(Bn, T // tm, N // tn),
        in_specs=[pl.BlockSpec((1, tm, K), lambda b, i, j: (b, i, 0)),
                  pl.BlockSpec((K, tn), lambda b, i, j: (0, j)),
                  pl.BlockSpec((1, tm, tn), lambda b, i, j: (b, i, j)),
                  pl.BlockSpec((1, 1, tn), lambda b, i, j: gsel(b)[:2] + (j,))],
        out_specs=pl.BlockSpec((1, tm, tn), lambda b, i, j: (b, i, j)),
        out_shape=jax.ShapeDtypeStruct((Bn, T, N), F32),
        compiler_params=_params("parallel", "parallel", "parallel"),
        name=name,
    )(a, w, res, gate)


def _mm_res_norm_kernel(a_ref, w_ref, res_ref, gate_ref, g_ref, sh_ref, sc_ref, o_ref, h_ref, *, rows):
    for r0 in range(0, a_ref.shape[1], rows):
        rs = slice(r0, r0 + rows)
        x1 = res_ref[0, rs, :] + gate_ref[0] * _dot(a_ref[0, rs, :], w_ref[...])
        o_ref[0, rs, :] = x1
        h_ref[0, rs, :] = _norm_mod(x1, g_ref[...], sh_ref[0], sc_ref[0]).astype(h_ref.dtype)


def _mm_res_norm(a, w, res, gate, g, shift, scale, name):
    Bn, T, K = a.shape
    N = w.shape[1]
    tm = _tile(T, 512, 16)
    rows = _tile(tm, 256, 16)
    row = pl.BlockSpec((1, tm, N), lambda b, i: (b, i, 0))
    return pl.pallas_call(
        functools.partial(_mm_res_norm_kernel, rows=rows),
        grid=(Bn, T // tm),
        in_specs=[pl.BlockSpec((1, tm, K), lambda b, i: (b, i, 0)),
                  pl.BlockSpec((K, N), lambda b, i: (0, 0)),
                  row,
                  pl.BlockSpec((1, 1, N), _bsel(gate)),
                  pl.BlockSpec((1, N), lambda b, i: (0, 0)),
                  pl.BlockSpec((1, 1, N), _bsel(shift)),
                  pl.BlockSpec((1, 1, N), _bsel(scale))],
        out_specs=[row, row],
        out_shape=[jax.ShapeDtypeStruct((Bn, T, N), F32), jax.ShapeDtypeStruct((Bn, T, N), BF16)],
        compiler_params=_params("parallel", "parallel"),
        name=name,
    )(a, w, res, gate, g[None], shift, scale)


def _ffn_up_kernel(h_ref, wa_ref, wb_ref, o_ref, w_scr, *, rows):
    @pl.when((pl.program_id(1) == 0) & (pl.program_id(2) == 0))
    def _():
        w_scr[0] = wa_ref[0].astype(BF16)
        w_scr[1] = wb_ref[0].astype(BF16)

    for r0 in range(0, h_ref.shape[1], rows):
        rs = slice(r0, r0 + rows)
        h = h_ref[0, rs, :]
        a = _dot(h, w_scr[0])
        o_ref[0, rs, :] = (a * _sigmoid(a) * _dot(h, w_scr[1])).astype(o_ref.dtype)


def _ffn_up(h, w13, layer, name):
    Bn, T, D = h.shape
    F = w13.shape[2] // 2
    tm = _tile(T, 2048, 16)
    rows = _tile(tm, 512, 16)
    tn = _tile(F, 512, LANES)
    nj = F // tn
    return pl.pallas_call(
        functools.partial(_ffn_up_kernel, rows=rows),
        grid=(nj, Bn, T // tm),
        in_specs=[pl.BlockSpec((1, tm, D), lambda j, b, i: (b, i, 0)),
                  pl.BlockSpec((1, D, tn), lambda j, b, i: (layer, 0, j)),
                  pl.BlockSpec((1, D, tn), lambda j, b, i: (layer, 0, j + nj))],
        out_specs=pl.BlockSpec((1, tm, tn), lambda j, b, i: (b, i, j)),
        out_shape=jax.ShapeDtypeStruct((Bn, T, F), BF16),
        scratch_shapes=[pltpu.VMEM((2, D, tn), BF16)],
        compiler_params=_params("arbitrary", "arbitrary", "arbitrary"),
        name=name,
    )(h, w13, w13)


def _swiglu_combine(a, b):
    return (a * _sigmoid(a) * b,)


def _conv_in_combine(gb, gc, u):
    return gb, gc * u


def _norm_mm_kernel(x_ref, g_ref, sh_ref, sc_ref, *rest, nw, combine):
    w_refs, out_refs, h_scr = rest[:nw], rest[nw:-1], rest[-1]

    @pl.when(pl.program_id(2) == 0)
    def _():
        h_scr[...] = _norm_mod(x_ref[0], g_ref[...], sh_ref[0], sc_ref[0]).astype(BF16)

    h = h_scr[...]
    outs = combine(*[_dot(h, w_ref[...]) for w_ref in w_refs])
    for o_ref, o in zip(out_refs, outs):
        o_ref[0] = o.astype(o_ref.dtype)


def _norm_mm(x, g, shift, scale, w, nw, combine, out_dtypes, name):
    Bn, T, D = x.shape
    N = w.shape[1] // nw
    tm = _tile(T, 1024, 16)
    tn = _tile(N, 512, LANES)
    nj = N // tn
    w_specs = [pl.BlockSpec((D, tn), functools.partial(lambda b, i, j, m: (0, j + m * nj), m=m))
               for m in range(nw)]
    return pl.pallas_call(
        functools.partial(_norm_mm_kernel, nw=nw, combine=combine),
        grid=(Bn, T // tm, nj),
        in_specs=[pl.BlockSpec((1, tm, D), lambda b, i, j: (b, i, 0)),
                  pl.BlockSpec((1, D), lambda b, i, j: (0, 0)),
                  pl.BlockSpec((1, 1, D), _bsel(shift)),
                  pl.BlockSpec((1, 1, D), _bsel(scale))] + w_specs,
        out_specs=[pl.BlockSpec((1, tm, tn), lambda b, i, j: (b, i, j))] * len(out_dtypes),
        out_shape=[jax.ShapeDtypeStruct((Bn, T, N), dt) for dt in out_dtypes],
        scratch_shapes=[pltpu.VMEM((tm, D), BF16)],
        compiler_params=_params("parallel", "parallel", "arbitrary"),
        name=name,
    )(x, g[None], shift, scale, *([w] * nw))


def _seg_ones(width):
    shift = HEAD.bit_length() - 1
    r = lax.shift_right_logical(lax.broadcasted_iota(jnp.int32, (width, width), 0), shift)
    c = lax.shift_right_logical(lax.broadcasted_iota(jnp.int32, (width, width), 1), shift)
    return (r == c).astype(F32)


def _split3(x):
    hi = x.astype(BF16)
    r1 = x - hi.astype(F32)
    mid = r1.astype(BF16)
    lo = (r1 - mid.astype(F32)).astype(BF16)
    return hi, mid, lo


def _wkv_kernel(r_ref, k_ref, v_ref, tw_ref, aw_ref, w2_ref, a2_ref, w0_ref, a0_ref, kk_ref, ka_ref,
                s0_ref, y_ref, sout_ref, s_scr, x_scr, r2_scr, bv_scr, vb_scr, z_scr, wt_scr,
                *, reverse, npair):
    c = pl.program_id(2)
    C = CHUNK
    PW = 2 * HEAD

    @pl.when(c == 0)
    def _():
        s_scr[...] = s0_ref[0]
        x_scr[...] = jnp.zeros_like(x_scr)
        r2_scr[...] = jnp.zeros_like(r2_scr)
        bv_scr[...] = jnp.zeros_like(bv_scr)
        vb_scr[...] = jnp.zeros_like(vb_scr)
        z_scr[...] = jnp.zeros_like(z_scr)
        wt_scr[...] = jnp.ones_like(wt_scr)

    rr = lax.shift_right_logical(lax.broadcasted_iota(jnp.int32, (PW, PW), 0), HEAD.bit_length() - 1)
    cc = lax.shift_right_logical(lax.broadcasted_iota(jnp.int32, (PW, PW), 1), HEAD.bit_length() - 1)
    same = rr == cc
    same_bf = same.astype(BF16)

    def bd(x):
        xb = x.astype(BF16)
        return jnp.concatenate([xb, xb], axis=0) * same_bf

    t2 = lax.broadcasted_iota(jnp.int32, (C, PW), 0)
    s2 = lax.broadcasted_iota(jnp.int32, (C, PW), 1) & (HEAD - 1)
    before = (s2 > t2) if reverse else (s2 < t2)
    upto = before | (s2 == t2)
    pairs = range(npair)
    sls = [slice(p * PW, (p + 1) * PW) for p in pairs]

    r, k, v = (t[0].astype(F32) for t in (r_ref, k_ref, v_ref))
    z = w0_ref[...] + _dot(tw_ref[0], w2_ref[...])
    a_pre = a0_ref[...] + _dot(aw_ref[0], a2_ref[...])

    S = [s_scr[p] for p in pairs]
    X = [x_scr[p] for p in pairs]
    G = [_dot_t(X[p], r2_scr[p], 1, 1) for p in pairs]
    XS = [_dot_t(X[p], S[p].astype(BF16), 1, 1) for p in pairs]
    u = [XS[p][:C] + _dot(jnp.where(before, G[p][:C, PW:], 0.0).astype(BF16), bv_scr[p]) for p in pairs]
    P = [jnp.where(before, G[p][:C, :PW], 0.0) for p in pairs]

    lw = -math.exp(-0.5) * _sigmoid(z)
    kkv = k * kk_ref[...]
    kk2 = kkv * kkv
    t_i = lax.broadcasted_iota(jnp.int32, (C, C), 0)
    s_i = lax.broadcasted_iota(jnp.int32, (C, C), 1)
    tri = ((s_i >= t_i) if reverse else (s_i <= t_i)).astype(BF16)
    cum = _dot(jnp.concatenate([tri, tri], axis=1),
               jnp.concatenate(_split3(lw)[:2], axis=0))
    same2 = jnp.concatenate([same_bf, same_bf], axis=0)
    ss = [_dot(jnp.concatenate(_split3(kk2[:, sl])[:2], axis=1), same2) for sl in sls]

    n_sq = C.bit_length() - 1
    for j in range(n_sq):
        Pb = [P[p].astype(BF16) for p in pairs]
        if j < n_sq - 1:
            PU = [_dot(Pb[p], jnp.concatenate([bd(P[p]), bd(u[p])], axis=1)) for p in pairs]
            P = [PU[p][:, :PW] for p in pairs]
            u = [u[p] + PU[p][:, PW:] for p in pairs]
        else:
            u = [u[p] + _dot(Pb[p], bd(u[p])) for p in pairs]
    for p in pairs:
        R = jnp.concatenate([jnp.where(upto, G[p][C:, :PW], 0.0),
                             jnp.where(upto, G[p][C:, PW:], 0.0)], axis=1).astype(BF16)
        y_ref[0, :, sls[p]] = XS[p][C:] + _dot(R, jnp.concatenate([bd(u[p]), bv_scr[p]], axis=0))
    for p in pairs:
        UV = jnp.concatenate([u[p].astype(BF16), vb_scr[p]], axis=0)
        dS = _dot_t(UV, z_scr[p], 0, 0)
        s_scr[p] = (S[p] + jnp.where(same, dS, 0.0)) * wt_scr[p, 0:1, :]

    a_sig = _sigmoid(a_pre)
    kd = k * (1.0 + (a_sig - 1.0) * ka_ref[...])
    e_pos = jnp.exp(cum)
    e_neg = jnp.exp(-cum)
    e_prev = jnp.exp(cum - lw)
    last = 0 if reverse else C - 1
    for p, sl in zip(pairs, sls):
        kkn = kkv[:, sl] * lax.rsqrt(jnp.maximum(ss[p], 1e-24))
        at = (-kkn) * e_prev[:, sl]
        bt = (kkn * a_sig[:, sl]) * e_neg[:, sl]
        rt = r[:, sl] * e_pos[:, sl]
        kt = kd[:, sl] * e_neg[:, sl]
        x_scr[p] = jnp.concatenate([at, rt], axis=0).astype(BF16)
        r2_scr[p] = jnp.concatenate([bd(bt), bd(kt)], axis=0)
        bv_scr[p] = bd(v[:, sl])
        vb_scr[p] = v[:, sl].astype(BF16)
        z_scr[p] = jnp.concatenate([bt, kt], axis=0).astype(BF16)
        wt_scr[p] = jnp.broadcast_to(e_pos[last:last + 1, sl], wt_scr.shape[1:])

    @pl.when(c == pl.num_programs(2) - 1)
    def _():
        sout_ref[0] = s_scr[...]


def _wkv(r, k, v, tw, aw, w2p, a2p, w0, a0, kk, ka, s0, d):
    Bn, T, D = r.shape
    PW = 2 * HEAD
    npairs = D // PW
    npair = _tile(npairs, 16, 1)
    hw = npair * PW
    nc = T // CHUNK
    reverse = d == 1
    pos = (lambda j: nc - 1 - j) if reverse else (lambda j: j)
    cin = lambda c: pos(jnp.minimum(c, nc - 1))
    cout = lambda c: pos(jnp.maximum(c - 1, 0))
    tok = pl.BlockSpec((1, CHUNK, hw), lambda b, g, c: (b, cin(c), g))
    lora = pl.BlockSpec((1, CHUNK, LANES), lambda b, g, c: (b, cin(c), d))
    lw2 = pl.BlockSpec((LANES, hw), lambda b, g, c: (0, g))
    vec = pl.BlockSpec((1, hw), lambda b, g, c: (0, g))
    st = pl.BlockSpec((1, npair, PW, PW), lambda b, g, c: (b, g, 0, 0))
    return pl.pallas_call(
        functools.partial(_wkv_kernel, reverse=reverse, npair=npair),
        grid=(Bn, npairs // npair, nc + 1),
        in_specs=[tok, tok, tok, lora, lora, lw2, lw2, vec, vec, vec, vec, st],
        out_specs=[pl.BlockSpec((1, CHUNK, hw), lambda b, g, c: (b, cout(c), g)), st],
        out_shape=[jax.ShapeDtypeStruct((Bn, T, D), F32),
                   jax.ShapeDtypeStruct((Bn, npairs, PW, PW), F32)],
        scratch_shapes=[pltpu.VMEM((npair, PW, PW), F32),
                        pltpu.VMEM((npair, 2 * CHUNK, PW), BF16),
                        pltpu.VMEM((npair, 2 * PW, PW), BF16),
                        pltpu.VMEM((npair, PW, PW), BF16),
                        pltpu.VMEM((npair, CHUNK, PW), BF16),
                        pltpu.VMEM((npair, 2 * CHUNK, PW), BF16),
                        pltpu.VMEM((npair, 8, PW), F32)],
        compiler_params=_params("parallel", "parallel", "arbitrary"),
        name="wkv_rev" if reverse else "wkv_fwd",
    )(r, k, v, tw, aw, w2p, a2p, w0, a0, kk, ka, s0)


def _rwkv_out_kernel(yf_ref, yb_ref, r_ref, k_ref, v_ref, gs_ref, aw_ref, res_ref, a2f_ref, a2b_ref,
                     g2_ref, wo_ref, a0_ref, ka_ref, rk_ref, lnw_ref, lnb_ref, gate_ref, g_ref, sh_ref,
                     sc_ref, o_ref, h_ref, og_scr, *, rows):
    tm, D = o_ref.shape[1:]
    PW = 2 * HEAD
    same = _seg_ones(PW).astype(BF16)
    same2 = jnp.concatenate([same, same], axis=0)

    def head_sum(x, pieces):
        if pieces == 1:
            return _dot(x.astype(BF16), same)
        return _dot(jnp.concatenate(_split3(x)[:2], axis=1), same2)

    for r0 in range(0, tm, rows):
        rs = slice(r0, r0 + rows)
        aw = aw_ref[0, rs, :]
        a_f = _sigmoid(a0_ref[0:1, :] + _dot(aw[:, :LANES], a2f_ref[...]))
        a_b = _sigmoid(a0_ref[1:2, :] + _dot(aw[:, LANES:], a2b_ref[...]))
        g = _dot(gs_ref[0, rs, :], g2_ref[...])
        for p in range(D // PW):
            sl = slice(p * PW, (p + 1) * PW)
            ksum = k_ref[0, rs, sl].astype(F32) * (2.0 + (a_f[:, sl] + a_b[:, sl] - 2.0) * ka_ref[:, sl])
            y = yf_ref[0, rs, sl] + yb_ref[0, rs, sl]
            yc = y - head_sum(y, 2) * (1.0 / HEAD)
            var = head_sum(yc * yc, 1) * (1.0 / HEAD)
            o = yc * lax.rsqrt(var + GN_EPS) * lnw_ref[:, sl] + lnb_ref[:, sl]
            bonus = (head_sum(r_ref[0, rs, sl].astype(F32) * ksum * rk_ref[:, sl], 1)
                     * v_ref[0, rs, sl].astype(F32))
            og_scr[rs, sl] = ((o + bonus) * g[:, sl]).astype(og_scr.dtype)
        x1 = res_ref[0, rs, :] + gate_ref[0] * _dot(og_scr[rs, :], wo_ref[...])
        o_ref[0, rs, :] = x1
        h_ref[0, rs, :] = _norm_mod(x1, g_ref[...], sh_ref[0], sc_ref[0]).astype(h_ref.dtype)


def _rwkv_out(yf, yb, r, k, v, gs, aw, res, a2fp, a2bp, g2, wo, a0, ka, rk, lnw, lnb, gate, g, shift,
              scale, name):
    Bn, T, D = res.shape
    tm = _tile(T, 256, 16)
    rows = _tile(tm, 128, 16)
    G = gs.shape[-1]
    row = pl.BlockSpec((1, tm, D), lambda b, i: (b, i, 0))
    vec = pl.BlockSpec((1, D), lambda b, i: (0, 0))
    full = lambda arr: pl.BlockSpec(arr.shape, lambda b, i: (0,) * arr.ndim)
    return pl.pallas_call(
        functools.partial(_rwkv_out_kernel, rows=rows),
        grid=(Bn, T // tm),
        in_specs=[row, row, row, row, row,
                  pl.BlockSpec((1, tm, G), lambda b, i: (b, i, 0)),
                  pl.BlockSpec((1, tm, 2 * LANES), lambda b, i: (b, i, 0)),
                  row, full(a2fp), full(a2bp), full(g2), full(wo), full(a0),
                  vec, vec, vec, vec,
                  pl.BlockSpec((1, 1, D), _bsel(gate)), vec,
                  pl.BlockSpec((1, 1, D), _bsel(shift)),
                  pl.BlockSpec((1, 1, D), _bsel(scale))],
        out_specs=[row, row],
        out_shape=[jax.ShapeDtypeStruct((Bn, T, D), F32), jax.ShapeDtypeStruct((Bn, T, D), BF16)],
        scratch_shapes=[pltpu.VMEM((tm, D), BF16)],
        compiler_params=_params("parallel", "parallel"),
        name=name,
    )(yf, yb, r, k, v, gs, aw, res, a2fp, a2bp, g2, wo, a0, ka, rk, lnw, lnb, gate, g[None], shift, scale)


def _conv_kernel(gb_ref, z_ref, cw_ref, o_ref):
    z = z_ref[0].astype(F32)
    T = z.shape[0]
    t = lax.broadcasted_iota(jnp.int32, (T, 1), 0)
    zp = jnp.where(t == 0, 0.0, pltpu.roll(z, 1, 0))
    zn = jnp.where(t == T - 1, 0.0, pltpu.roll(z, T - 1, 0))
    conv = zp * cw_ref[0:1, :] + z * cw_ref[1:2, :] + zn * cw_ref[2:3, :]
    o_ref[0] = (gb_ref[0] * conv).astype(o_ref.dtype)


def _conv(gb, z, cw):
    Bn, T, D = z.shape
    tn = _tile(D, 512, LANES)
    tok = pl.BlockSpec((1, T, tn), lambda b, j: (b, 0, j))
    return pl.pallas_call(
        _conv_kernel,
        grid=(Bn, D // tn),
        in_specs=[tok, tok, pl.BlockSpec((3, tn), lambda b, j: (0, j))],
        out_specs=tok,
        out_shape=jax.ShapeDtypeStruct((Bn, T, D), BF16),
        compiler_params=_params("parallel", "parallel"),
        name="short_conv",
    )(gb, z, cw)


def _pad_rows(w, rows):
    return jnp.pad(w, ((0, rows - w.shape[0]), (0, 0)))


def _pad_cols(w, cols):
    return jnp.pad(w, ((0, 0), (0, cols - w.shape[1])))


def _split_mod(mod_rows, D):
    return [mod_rows[:, m * D:(m + 1) * D][:, None, :] for m in range(6)]


def _ffn_branch(t1, h2, mods, ffn, tag):
    w13, layer, wdn = ffn
    act = _ffn_up(h2, w13, layer, name="ffn_up_" + tag)
    return _mm_res(act, wdn, t1, mods[5], name="ffn_down_" + tag)


def _rwkv_layer(x, ctx, mods_x, mods_c, g1, g2n, mix, wr, wk, wv, wo, w0, w1, w2, a0, a1, a2,
                lg1, lg2, k_k, k_a, r_k, ln_w, ln_b, ffn):
    D = x.shape[-1]
    H = D // HEAD
    w1cat = jnp.concatenate([_pad_cols(w1[0], LANES), _pad_cols(w1[1], LANES)], axis=1).astype(BF16)
    a1cat = jnp.concatenate([_pad_cols(a1[0], LANES), _pad_cols(a1[1], LANES)], axis=1).astype(BF16)
    w2p = [_pad_rows(w2[d], LANES).astype(BF16) for d in range(2)]
    a2p = [_pad_rows(a2[d], LANES).astype(BF16) for d in range(2)]
    wr, wk, wv, wo = (t.astype(BF16) for t in (wr, wk, wv, wo))
    lg1, lg2 = lg1.astype(BF16), lg2.astype(BF16)
    rk = r_k.reshape(1, D)

    xr, xw, xk, xv, xa, xg = _prep_ctx(ctx, g1, mods_c[0], mods_c[1], mix)
    lora_c = dict(tw=_mm(xw, w1cat, BF16, act="tanh", name="lora_w_c"),
                  aw=_mm(xa, a1cat, BF16, name="lora_a_c"),
                  gs=_mm(xg, lg1, BF16, act="sigmoid", name="lora_g_c"))
    ins = {"c": (xr, xk, xv, lora_c["tw"], lora_c["aw"], lora_c["gs"]),
           "x": _prep_latent(x, g1, mods_x[0], mods_x[1], mix, w1cat, a1cat, lg1)}
    sets = {}
    for tag, (xr, xk, xv, tw, aw, gs) in ins.items():
        sets[tag] = dict(r=_mm(xr, wr, BF16, name="proj_r_" + tag),
                         k=_mm(xk, wk, BF16, name="proj_k_" + tag),
                         v=_mm(xv, wv, BF16, name="proj_v_" + tag), tw=tw, aw=aw, gs=gs)

    ys = {"c": [], "x": []}
    zero_state = jnp.zeros((x.shape[0], H // 2, 2 * HEAD, 2 * HEAD), F32)
    for d in range(2):
        state = zero_state
        for tag in ("c", "x"):
            s = sets[tag]
            y, state = _wkv(s["r"], s["k"], s["v"], s["tw"], s["aw"], w2p[d], a2p[d],
                            w0[d][None], a0[d][None], k_k[None], k_a[None], state, d)
            ys[tag].append(y)

    outs = []
    for tag, tok, mods in (("c", ctx, mods_c), ("x", x, mods_x)):
        s = sets[tag]
        t1, h2 = _rwkv_out(ys[tag][0], ys[tag][1], s["r"], s["k"], s["v"], s["gs"], s["aw"], tok,
                           a2p[0], a2p[1], lg2, wo, a0, k_a[None], rk, ln_w[None], ln_b[None],
                           mods[2], g2n, mods[3], mods[4], name="rwkv_out_" + tag)
        outs.append(_ffn_branch(t1, h2, mods, ffn, tag))
    return outs[1], outs[0]


def _conv_layer(x, mods, g1, g2n, w_in, conv_w, w_out, ffn):
    gb, z = _norm_mm(x, g1, mods[0], mods[1], w_in.astype(BF16), 3, _conv_in_combine, (BF16, BF16),
                     name="conv_in")
    p = _conv(gb, z, conv_w)
    t1, h2 = _mm_res_norm(p, w_out.astype(BF16), x, mods[2], g2n, mods[3], mods[4], name="conv_out")
    return _ffn_branch(t1, h2, mods, ffn, "x")


def kernel(x, c, ctx, c_ctx, norm1_g, norm2_g, ada_w, ada_b, rw_mix, rw_wr, rw_wk, rw_wv, rw_wo,
           rw_w0, rw_w1, rw_w2, rw_a0, rw_a1, rw_a2, rw_g1, rw_g2, rw_kk, rw_ka, rw_rk, rw_lnw,
           rw_lnb, sc_win, sc_conv, sc_wout, ffn_w13, ffn_w2, final_g):
    B, T, D = x.shape
    depth = norm1_g.shape[0]
    rows = -(-(B + 1) // 8) * 8
    cond = jnp.zeros((rows, D), F32).at[:B].set(c).at[B].set(c_ctx)
    for i in range(depth):
        last = i == depth - 1
        j = i // 2
        mod = _ada(cond, ada_w, ada_b, i)
        mods_x = _split_mod(mod[:B], D)
        mods_c = _split_mod(mod[B:B + 1], D)
        ffn = (ffn_w13, i, ffn_w2[i].astype(BF16))
        if i % 2 == 0:
            x, ctx_new = _rwkv_layer(
                x, ctx, mods_x, mods_c, norm1_g[i], norm2_g[i], rw_mix[j], rw_wr[j], rw_wk[j],
                rw_wv[j], rw_wo[j], rw_w0[j], rw_w1[j], rw_w2[j], rw_a0[j], rw_a1[j], rw_a2[j],
                rw_g1[j], rw_g2[j], rw_kk[j], rw_ka[j], rw_rk[j], rw_lnw[j], rw_lnb[j], ffn)
            ctx = ctx_new
        else:
            x = _conv_layer(x, mods_x, norm1_g[i], norm2_g[i], sc_win[j], sc_conv[j], sc_wout[j],
                            ffn)
            if not last:
                ctx = _conv_layer(ctx, mods_c, norm1_g[i], norm2_g[i], sc_win[j], sc_conv[j],
                                  sc_wout[j], ffn)
    zeros = jnp.zeros((1, 1, D), F32)
    return _norm(x, final_g, zeros, zeros)
```

```python
import functools
import math

import jax
import jax.numpy as jnp
from jax import lax
from jax.experimental import pallas as pl
from jax.experimental.pallas import tpu as pltpu

HEAD = 64
GRID_W = 64
CHUNK = 64
NORM_EPS = 1e-6
GN_EPS = 64e-5
LANES = 128
VMEM_LIMIT = 56 * 1024 * 1024

F32 = jnp.float32
BF16 = jnp.bfloat16


def _params(*sem):
    return pltpu.CompilerParams(dimension_semantics=sem, vmem_limit_bytes=VMEM_LIMIT)


def _tile(n, pref, mult):
    t = min(pref, n)
    t -= t % mult
    while t >= mult:
        if n % t == 0:
            return t
        t -= mult
    return n


def _sigmoid(x):
    return 1.0 / (1.0 + jnp.exp(-x))


def _norm_mod(x, g, shift, scale, cols=None):
    rs = lax.rsqrt(jnp.mean(x * x, axis=-1, keepdims=True) + NORM_EPS)
    if cols is not None:
        x, g, shift, scale = (t[:, cols[0]:cols[1]] for t in (x, g, shift, scale))
    return (x * rs) * (g * (1.0 + scale)) + shift


def _dot(a, b):
    return jnp.dot(a, b, preferred_element_type=F32)


def _dot_t(a, b, ca, cb):
    return lax.dot_general(a, b, (((ca,), (cb,)), ((), ())), preferred_element_type=F32)


def _ada_kernel(c_ref, w_ref, b_ref, o_ref):
    c = c_ref[...]
    s = c * _sigmoid(c)
    o_ref[...] = _dot(s.astype(BF16), w_ref[0].astype(BF16)) + b_ref[0]


def _ada(cond, w, b, layer):
    R, D = cond.shape
    N = w.shape[2]
    tn = _tile(N, 1024, LANES)
    return pl.pallas_call(
        _ada_kernel,
        grid=(N // tn,),
        in_specs=[pl.BlockSpec((R, D), lambda j: (0, 0)),
                  pl.BlockSpec((1, D, tn), lambda j: (layer, 0, j)),
                  pl.BlockSpec((1, 1, tn), lambda j: (layer, 0, j))],
        out_specs=pl.BlockSpec((R, tn), lambda j: (0, j)),
        out_shape=jax.ShapeDtypeStruct((R, N), F32),
        compiler_params=_params("parallel"),
        name="ada_mod",
    )(cond, w, b[:, None, :])


def _norm_kernel(x_ref, g_ref, sh_ref, sc_ref, o_ref):
    o_ref[0] = _norm_mod(x_ref[0], g_ref[...], sh_ref[0], sc_ref[0])


def _bsel(arr):
    if arr.shape[0] == 1:
        return lambda b, *_: (0, 0, 0)
    return lambda b, *_: (b, 0, 0)


def _norm(x, g, shift, scale):
    Bn, T, D = x.shape
    tm = _tile(T, 512, 8)
    return pl.pallas_call(
        _norm_kernel,
        grid=(Bn, T // tm),
        in_specs=[pl.BlockSpec((1, tm, D), lambda b, i: (b, i, 0)),
                  pl.BlockSpec((1, D), lambda b, i: (0, 0)),
                  pl.BlockSpec((1, 1, D), _bsel(shift)),
                  pl.BlockSpec((1, 1, D), _bsel(scale))],
        out_specs=pl.BlockSpec((1, tm, D), lambda b, i: (b, i, 0)),
        out_shape=jax.ShapeDtypeStruct((Bn, T, D), F32),
        compiler_params=_params("parallel", "parallel"),
        name="norm",
    )(x, g[None], shift, scale)


def _write_mix(out_refs, mix_ref, h, shifted, c0, c1):
    xx = shifted - h
    for m, o_ref in enumerate(out_refs):
        o_ref[0, :, c0:c1] = (h + xx * mix_ref[m:m + 1, c0:c1]).astype(o_ref.dtype)


def _prep_latent_kernel(x_ref, xu_ref, xd_ref, g_ref, sh_ref, sc_ref, mix_ref, w1_ref, a1_ref, g1_ref,
                        xr_ref, xk_ref, xv_ref, tw_ref, aw_ref, gs_ref, h_scr, xm_scr):
    i = pl.program_id(1)
    n = pl.num_programs(1)
    g, sh, sc = g_ref[...], sh_ref[0], sc_ref[0]
    tm, D = x_ref.shape[1:]
    q = D // 4
    W = GRID_W
    h_scr[0:W, 2 * q:3 * q] = jnp.where(i > 0, _norm_mod(xu_ref[0], g, sh, sc, (2 * q, 3 * q)), 0.0)
    h_scr[W - 8:W, 0:q] = jnp.zeros((8, q), F32)
    for r0 in range(0, tm, W):
        h_scr[W + r0:2 * W + r0, :] = _norm_mod(x_ref[0, r0:r0 + W, :], g, sh, sc)
    h_scr[W + tm:, 3 * q:] = jnp.where(i < n - 1, _norm_mod(xd_ref[0], g, sh, sc, (3 * q, D)), 0.0)
    h_scr[W + tm:W + tm + 8, q:2 * q] = jnp.zeros((8, q), F32)
    wide = {0: xr_ref, 2: xk_ref, 3: xv_ref}
    lora = {1: 0, 4: 1, 5: 2}
    lora_w = (w1_ref, a1_ref, g1_ref)
    acc = [None] * 3
    R = 32
    row = lax.broadcasted_iota(jnp.int32, (R, 1), 0)
    for k, off in enumerate((-1, 1, -W, W)):
        c0, c1 = k * q, (k + 1) * q
        mixk = [mix_ref[m:m + 1, c0:c1] for m in range(6)]
        for r0 in range(0, tm, R):
            h = h_scr[W + r0:W + r0 + R, c0:c1]
            s = h_scr[W + r0 + off:W + r0 + off + R, c0:c1]
            if off == -1 and r0 % W == 0:
                s = jnp.where(row == 0, 0.0, s)
            if off == 1 and (r0 + R) % W == 0:
                s = jnp.where(row == R - 1, 0.0, s)
            xx = s - h
            for m in range(6):
                xm = (h + xx * mixk[m]).astype(BF16)
                if m in wide:
                    wide[m][0, r0:r0 + R, c0:c1] = xm
                else:
                    xm_scr[lora[m], r0:r0 + R, c0:c1] = xm
        for l in range(3):
            part = _dot(xm_scr[l, :, c0:c1], lora_w[l][c0:c1, :])
            acc[l] = part if acc[l] is None else acc[l] + part
    tw_ref[0] = jnp.tanh(acc[0]).astype(tw_ref.dtype)
    aw_ref[0] = acc[1].astype(aw_ref.dtype)
    gs_ref[0] = _sigmoid(acc[2]).astype(gs_ref.dtype)


def _prep_latent(x, g, shift, scale, mix, w1cat, a1cat, lg1):
    Bn, T, D = x.shape
    rows_per_tile = _tile(T // GRID_W, 4, 1)
    tm = rows_per_tile * GRID_W
    nrow = T // GRID_W
    tok = pl.BlockSpec((1, tm, D), lambda b, i: (b, i, 0))
    full = lambda arr: pl.BlockSpec(arr.shape, lambda b, i: (0,) * arr.ndim)
    small = lambda arr: pl.BlockSpec((1, tm, arr.shape[1]), lambda b, i: (b, i, 0))
    return pl.pallas_call(
        _prep_latent_kernel,
        grid=(Bn, T // tm),
        in_specs=[tok,
                  pl.BlockSpec((1, GRID_W, D),
                               lambda b, i: (b, jnp.maximum(i * rows_per_tile - 1, 0), 0)),
                  pl.BlockSpec((1, GRID_W, D),
                               lambda b, i: (b, jnp.minimum((i + 1) * rows_per_tile, nrow - 1), 0)),
                  pl.BlockSpec((1, D), lambda b, i: (0, 0)),
                  pl.BlockSpec((1, 1, D), _bsel(shift)),
                  pl.BlockSpec((1, 1, D), _bsel(scale)),
                  pl.BlockSpec((6, D), lambda b, i: (0, 0)),
                  full(w1cat), full(a1cat), full(lg1)],
        out_specs=[tok, tok, tok, small(w1cat), small(a1cat), small(lg1)],
        out_shape=[jax.ShapeDtypeStruct((Bn, T, D), BF16)] * 3
        + [jax.ShapeDtypeStruct((Bn, T, w.shape[1]), BF16) for w in (w1cat, a1cat, lg1)],
        scratch_shapes=[pltpu.VMEM((tm + 2 * GRID_W, D), F32), pltpu.VMEM((3, tm, D), BF16)],
        compiler_params=_params("parallel", "parallel"),
        name="prep_latent",
    )(x, x, x, g[None], shift, scale, mix, w1cat, a1cat, lg1)


def _prep_ctx_kernel(x_ref, g_ref, sh_ref, sc_ref, mix_ref, *out_refs):
    h = _norm_mod(x_ref[0], g_ref[...], sh_ref[0], sc_ref[0])
    L, D = h.shape
    half = D // 2
    t = lax.broadcasted_iota(jnp.int32, (L, 1), 0)
    h0, h1 = h[:, :half], h[:, half:]
    prev = jnp.where(t == 0, 0.0, pltpu.roll(h0, 1, 0))
    nxt = jnp.where(t == L - 1, 0.0, pltpu.roll(h1, L - 1, 0))
    _write_mix(out_refs, mix_ref, h0, prev, 0, half)
    _write_mix(out_refs, mix_ref, h1, nxt, half, D)


def _prep_ctx(x, g, shift, scale, mix):
    Bn, L, D = x.shape
    return pl.pallas_call(
        _prep_ctx_kernel,
        grid=(Bn,),
        in_specs=[pl.BlockSpec((1, L, D), lambda b: (b, 0, 0)),
                  pl.BlockSpec((1, D), lambda b: (0, 0)),
                  pl.BlockSpec((1, 1, D), _bsel(shift)),
                  pl.BlockSpec((1, 1, D), _bsel(scale)),
                  pl.BlockSpec((6, D), lambda b: (0, 0))],
        out_specs=[pl.BlockSpec((1, L, D), lambda b: (b, 0, 0))] * 6,
        out_shape=[jax.ShapeDtypeStruct((Bn, L, D), BF16)] * 6,
        compiler_params=_params("parallel"),
        name="prep_ctx",
    )(x, g[None], shift, scale, mix)


def _mm_kernel(a_ref, w_ref, o_ref, *, act):
    acc = _dot(a_ref[0], w_ref[...])
    if act == "tanh":
        acc = jnp.tanh(acc)
    elif act == "sigmoid":
        acc = _sigmoid(acc)
    o_ref[0] = acc.astype(o_ref.dtype)


def _mm(a, w, out_dtype, act=None, name="mm"):
    Bn, T, K = a.shape
    N = w.shape[1]
    M = Bn * T
    tm = _tile(M, 1024, 16)
    tn = _tile(N, 2048, LANES)
    out = pl.pallas_call(
        functools.partial(_mm_kernel, act=act),
        grid=(1, M // tm, N // tn),
        in_specs=[pl.BlockSpec((1, tm, K), lambda b, i, j: (b, i, 0)),
                  pl.BlockSpec((K, tn), lambda b, i, j: (0, j))],
        out_specs=pl.BlockSpec((1, tm, tn), lambda b, i, j: (b, i, j)),
        out_shape=jax.ShapeDtypeStruct((1, M, N), out_dtype),
        compiler_params=_params("parallel", "parallel", "parallel"),
        name=name,
    )(a.reshape(1, M, K), w)
    return out.reshape(Bn, T, N)


def _mm_res_kernel(a_ref, w_ref, res_ref, gate_ref, o_ref):
    o_ref[0] = res_ref[0] + gate_ref[0] * _dot(a_ref[0], w_ref[...])


def _mm_res(a, w, res, gate, name="mm_res"):
    Bn, T, K = a.shape
    N = w.shape[1]
    tm = _tile(T, 1024, 16)
    tn = _tile(N, 1024 if K <= 2048 else 512, LANES)
    gsel = _bsel(gate)
    return pl.pallas_call(
        _mm_res_kernel,
        grid=(Bn, T // tm, N // tn),
        in_specs=[pl.BlockSpec((1, tm, K), lambda b, i, j: (b, i, 0)),
                  pl.BlockSpec((K, tn), lambda b, i, j: (0, j)),
                  pl.BlockSpec((1, tm, tn), lambda b, i, j: (b, i, j)),
                  pl.BlockSpec((1, 1, tn), lambda b, i, j: gsel(b)[:2] + (j,))],
        out_specs=pl.BlockSpec((1, tm, tn), lambda b, i, j: (b, i, j)),
        out_shape=jax.ShapeDtypeStruct((Bn, T, N), F32),
        compiler_params=_params("parallel", "parallel", "parallel"),
        name=name,
    )(a, w, res, gate)


def _mm_res_norm_kernel(a_ref, w_ref, res_ref, gate_ref, g_ref, sh_ref, sc_ref, o_ref, h_ref, *, rows):
    for r0 in range(0, a_ref.shape[1], rows):
        rs = slice(r0, r0 + rows)
        x1 = res_ref[0, rs, :] + gate_ref[0] * _dot(a_ref[0, rs, :], w_ref[...])
        o_ref[0, rs, :] = x1
        h_ref[0, rs, :] = _norm_mod(x1, g_ref[...], sh_ref[0], sc_ref[0]).astype(h_ref.dtype)


def _mm_res_norm(a, w, res, gate, g, shift, scale, name):
    Bn, T, K = a.shape
    N = w.shape[1]
    tm = _tile(T, 512, 16)
    rows = _tile(tm, 256, 16)
    row = pl.BlockSpec((1, tm, N), lambda b, i: (b, i, 0))
    return pl.pallas_call(
        functools.partial(_mm_res_norm_kernel, rows=rows),
        grid=(Bn, T // tm),
        in_specs=[pl.BlockSpec((1, tm, K), lambda b, i: (b, i, 0)),
                  pl.BlockSpec((K, N), lambda b, i: (0, 0)),
                  row,
                  pl.BlockSpec((1, 1, N), _bsel(gate)),
                  pl.BlockSpec((1, N), lambda b, i: (0, 0)),
                  pl.BlockSpec((1, 1, N), _bsel(shift)),
                  pl.BlockSpec((1, 1, N), _bsel(scale))],
        out_specs=[row, row],
        out_shape=[jax.ShapeDtypeStruct((Bn, T, N), F32), jax.ShapeDtypeStruct((Bn, T, N), BF16)],
        compiler_params=_params("parallel", "parallel"),
        name=name,
    )(a, w, res, gate, g[None], shift, scale)


def _ffn_up_kernel(h_ref, wa_ref, wb_ref, o_ref, w_scr, *, rows):
    @pl.when((pl.program_id(1) == 0) & (pl.program_id(2) == 0))
    def _():
        w_scr[0] = wa_ref[0].astype(BF16)
        w_scr[1] = wb_ref[0].astype(BF16)

    for r0 in range(0, h_ref.shape[1], rows):
        rs = slice(r0, r0 + rows)
        h = h_ref[0, rs, :]
        a = _dot(h, w_scr[0])
        o_ref[0, rs, :] = (a * _sigmoid(a) * _dot(h, w_scr[1])).astype(o_ref.dtype)


def _ffn_up(h, w13, layer, name):
    Bn, T, D = h.shape
    F = w13.shape[2] // 2
    tm = _tile(T, 2048, 16)
    rows = _tile(tm, 512, 16)
    tn = _tile(F, 512, LANES)
    nj = F // tn
    return pl.pallas_call(
        functools.partial(_ffn_up_kernel, rows=rows),
        grid=(nj, Bn, T // tm),
        in_specs=[pl.BlockSpec((1, tm, D), lambda j, b, i: (b, i, 0)),
                  pl.BlockSpec((1, D, tn), lambda j, b, i: (layer, 0, j)),
                  pl.BlockSpec((1, D, tn), lambda j, b, i: (layer, 0, j + nj))],
        out_specs=pl.BlockSpec((1, tm, tn), lambda j, b, i: (b, i, j)),
        out_shape=jax.ShapeDtypeStruct((Bn, T, F), BF16),
        scratch_shapes=[pltpu.VMEM((2, D, tn), BF16)],
        compiler_params=_params("arbitrary", "arbitrary", "arbitrary"),
        name=name,
    )(h, w13, w13)


def _conv_in_combine(gb, gc, u):
    return gb, gc * u


def _norm_mm_kernel(x_ref, g_ref, sh_ref, sc_ref, *rest, nw, combine, rows):
    w_refs, out_refs, h_scr = rest[:nw], rest[nw:-1], rest[-1]

    def emit(rs, h):
        outs = combine(*[_dot(h, w_ref[...]) for w_ref in w_refs])
        for o_ref, o in zip(out_refs, outs):
            o_ref[0, rs, :] = o.astype(o_ref.dtype)

    @pl.when(pl.program_id(2) == 0)
    def _():
        for r0 in range(0, x_ref.shape[1], rows):
            rs = slice(r0, r0 + rows)
            h = _norm_mod(x_ref[0, rs, :], g_ref[...], sh_ref[0], sc_ref[0]).astype(BF16)
            h_scr[rs, :] = h
            emit(rs, h)

    @pl.when(pl.program_id(2) > 0)
    def _():
        emit(slice(None), h_scr[...])


def _norm_mm(x, g, shift, scale, w, nw, combine, out_dtypes, name):
    Bn, T, D = x.shape
    N = w.shape[1] // nw
    tm = _tile(T, 1024, 16)
    rows = _tile(tm, 256, 16)
    tn = _tile(N, 512, LANES)
    nj = N // tn
    w_specs = [pl.BlockSpec((D, tn), functools.partial(lambda b, i, j, m: (0, j + m * nj), m=m))
               for m in range(nw)]
    return pl.pallas_call(
        functools.partial(_norm_mm_kernel, nw=nw, combine=combine, rows=rows),
        grid=(Bn, T // tm, nj),
        in_specs=[pl.BlockSpec((1, tm, D), lambda b, i, j: (b, i, 0)),
                  pl.BlockSpec((1, D), lambda b, i, j: (0, 0)),
                  pl.BlockSpec((1, 1, D), _bsel(shift)),
                  pl.BlockSpec((1, 1, D), _bsel(scale))] + w_specs,
        out_specs=[pl.BlockSpec((1, tm, tn), lambda b, i, j: (b, i, j))] * len(out_dtypes),
        out_shape=[jax.ShapeDtypeStruct((Bn, T, N), dt) for dt in out_dtypes],
        scratch_shapes=[pltpu.VMEM((tm, D), BF16)],
        compiler_params=_params("parallel", "parallel", "arbitrary"),
        name=name,
    )(x, g[None], shift, scale, *([w] * nw))


def _seg_ones(width):
    shift = HEAD.bit_length() - 1
    r = lax.shift_right_logical(lax.broadcasted_iota(jnp.int32, (width, width), 0), shift)
    c = lax.shift_right_logical(lax.broadcasted_iota(jnp.int32, (width, width), 1), shift)
    return (r == c).astype(F32)


def _split3(x):
    hi = x.astype(BF16)
    r1 = x - hi.astype(F32)
    mid = r1.astype(BF16)
    lo = (r1 - mid.astype(F32)).astype(BF16)
    return hi, mid, lo


def _wkv_kernel(r_ref, k_ref, v_ref, tw_ref, aw_ref, w2_ref, a2_ref, w0_ref, a0_ref, kk_ref, ka_ref,
                s0_ref, y_ref, sout_ref, s_scr, x_scr, r2_scr, bv_scr, vb_scr, z_scr, wt_scr,
                *, reverse, npair):
    c = pl.program_id(2)
    C = CHUNK
    PW = 2 * HEAD

    @pl.when(c == 0)
    def _():
        s_scr[...] = s0_ref[0]
        x_scr[...] = jnp.zeros_like(x_scr)
        r2_scr[...] = jnp.zeros_like(r2_scr)
        bv_scr[...] = jnp.zeros_like(bv_scr)
        vb_scr[...] = jnp.zeros_like(vb_scr)
        z_scr[...] = jnp.zeros_like(z_scr)
        wt_scr[...] = jnp.ones_like(wt_scr)

    rr = lax.shift_right_logical(lax.broadcasted_iota(jnp.int32, (PW, PW), 0), HEAD.bit_length() - 1)
    cc = lax.shift_right_logical(lax.broadcasted_iota(jnp.int32, (PW, PW), 1), HEAD.bit_length() - 1)
    same = rr == cc
    same_bf = same.astype(BF16)

    def bd(x):
        xb = x.astype(BF16)
        return jnp.concatenate([xb, xb], axis=0) * same_bf

    t2 = lax.broadcasted_iota(jnp.int32, (C, PW), 0)
    s2 = lax.broadcasted_iota(jnp.int32, (C, PW), 1) & (HEAD - 1)
    before = (s2 > t2) if reverse else (s2 < t2)
    upto = before | (s2 == t2)
    pairs = range(npair)
    sls = [slice(p * PW, (p + 1) * PW) for p in pairs]

    r, k, v = (t[0].astype(F32) for t in (r_ref, k_ref, v_ref))
    z = w0_ref[...] + _dot(tw_ref[0], w2_ref[...])
    a_pre = a0_ref[...] + _dot(aw_ref[0], a2_ref[...])

    S = [s_scr[p] for p in pairs]
    X = [x_scr[p] for p in pairs]
    G = [_dot_t(X[p], r2_scr[p], 1, 1) for p in pairs]
    XS = [_dot_t(X[p], S[p].astype(BF16), 1, 1) for p in pairs]
    u = [XS[p][:C] + _dot(jnp.where(before, G[p][:C, PW:], 0.0).astype(BF16), bv_scr[p]) for p in pairs]
    P = [jnp.where(before, G[p][:C, :PW], 0.0) for p in pairs]

    lw = -math.exp(-0.5) * _sigmoid(z)
    kkv = k * kk_ref[...]
    kk2 = kkv * kkv
    t_i = lax.broadcasted_iota(jnp.int32, (C, C), 0)
    s_i = lax.broadcasted_iota(jnp.int32, (C, C), 1)
    tri = ((s_i >= t_i) if reverse else (s_i <= t_i)).astype(BF16)
    cum = _dot(jnp.concatenate([tri, tri], axis=1),
               jnp.concatenate(_split3(lw)[:2], axis=0))
    same2 = jnp.concatenate([same_bf, same_bf], axis=0)
    ss = [_dot(jnp.concatenate(_split3(kk2[:, sl])[:2], axis=1), same2) for sl in sls]

    n_sq = C.bit_length() - 1
    for j in range(n_sq):
        Pb = [P[p].astype(BF16) for p in pairs]
        if j < n_sq - 1:
            PU = [_dot(Pb[p], jnp.concatenate([bd(P[p]), bd(u[p])], axis=1)) for p in pairs]
            P = [PU[p][:, :PW] for p in pairs]
            u = [u[p] + PU[p][:, PW:] for p in pairs]
        else:
            u = [u[p] + _dot(Pb[p], bd(u[p])) for p in pairs]
    for p in pairs:
        R = jnp.concatenate([jnp.where(upto, G[p][C:, :PW], 0.0),
                             jnp.where(upto, G[p][C:, PW:], 0.0)], axis=1).astype(BF16)
        y_ref[0, :, sls[p]] = XS[p][C:] + _dot(R, jnp.concatenate([bd(u[p]), bv_scr[p]], axis=0))
    for p in pairs:
        UV = jnp.concatenate([u[p].astype(BF16), vb_scr[p]], axis=0)
        dS = _dot_t(UV, z_scr[p], 0, 0)
        s_scr[p] = (S[p] + jnp.where(same, dS, 0.0)) * wt_scr[p, 0:1, :]

    a_sig = _sigmoid(a_pre)
    kd = k * (1.0 + (a_sig - 1.0) * ka_ref[...])
    e_pos = jnp.exp(cum)
    e_neg = jnp.exp(-cum)
    e_prev = jnp.exp(cum - lw)
    last = 0 if reverse else C - 1
    for p, sl in zip(pairs, sls):
        kkn = kkv[:, sl] * lax.rsqrt(jnp.maximum(ss[p], 1e-24))
        at = (-kkn) * e_prev[:, sl]
        bt = (kkn * a_sig[:, sl]) * e_neg[:, sl]
        rt = r[:, sl] * e_pos[:, sl]
        kt = kd[:, sl] * e_neg[:, sl]
        x_scr[p] = jnp.concatenate([at, rt], axis=0).astype(BF16)
        r2_scr[p] = jnp.concatenate([bd(bt), bd(kt)], axis=0)
        bv_scr[p] = bd(v[:, sl])
        vb_scr[p] = v[:, sl].astype(BF16)
        z_scr[p] = jnp.concatenate([bt, kt], axis=0).astype(BF16)
        wt_scr[p] = jnp.broadcast_to(e_pos[last:last + 1, sl], wt_scr.shape[1:])

    @pl.when(c == pl.num_programs(2) - 1)
    def _():
        sout_ref[0] = s_scr[...]


def _wkv(r, k, v, tw, aw, w2p, a2p, w0, a0, kk, ka, s0, d):
    Bn, T, D = r.shape
    PW = 2 * HEAD
    npairs = D // PW
    npair = _tile(npairs, 16, 1)
    hw = npair * PW
    nc = T // CHUNK
    reverse = d == 1
    pos = (lambda j: nc - 1 - j) if reverse else (lambda j: j)
    cin = lambda c: pos(jnp.minimum(c, nc - 1))
    cout = lambda c: pos(jnp.maximum(c - 1, 0))
    tok = pl.BlockSpec((1, CHUNK, hw), lambda b, g, c: (b, cin(c), g))
    lora = pl.BlockSpec((1, CHUNK, LANES), lambda b, g, c: (b, cin(c), d))
    lw2 = pl.BlockSpec((LANES, hw), lambda b, g, c: (0, g))
    vec = pl.BlockSpec((1, hw), lambda b, g, c: (0, g))
    st = pl.BlockSpec((1, npair, PW, PW), lambda b, g, c: (b, g, 0, 0))
    return pl.pallas_call(
        functools.partial(_wkv_kernel, reverse=reverse, npair=npair),
        grid=(Bn, npairs // npair, nc + 1),
        in_specs=[tok, tok, tok, lora, lora, lw2, lw2, vec, vec, vec, vec, st],
        out_specs=[pl.BlockSpec((1, CHUNK, hw), lambda b, g, c: (b, cout(c), g)), st],
        out_shape=[jax.ShapeDtypeStruct((Bn, T, D), F32),
                   jax.ShapeDtypeStruct((Bn, npairs, PW, PW), F32)],
        scratch_shapes=[pltpu.VMEM((npair, PW, PW), F32),
                        pltpu.VMEM((npair, 2 * CHUNK, PW), BF16),
                        pltpu.VMEM((npair, 2 * PW, PW), BF16),
                        pltpu.VMEM((npair, PW, PW), BF16),
                        pltpu.VMEM((npair, CHUNK, PW), BF16),
                        pltpu.VMEM((npair, 2 * CHUNK, PW), BF16),
                        pltpu.VMEM((npair, 8, PW), F32)],
        compiler_params=_params("parallel", "parallel", "arbitrary"),
        name="wkv_rev" if reverse else "wkv_fwd",
    )(r, k, v, tw, aw, w2p, a2p, w0, a0, kk, ka, s0)


def _rwkv_out_kernel(yf_ref, yb_ref, r_ref, k_ref, v_ref, gs_ref, aw_ref, res_ref, a2f_ref, a2b_ref,
                     g2_ref, wo_ref, a0_ref, ka_ref, rk_ref, lnw_ref, lnb_ref, gate_ref, g_ref, sh_ref,
                     sc_ref, o_ref, h_ref, og_scr, *, rows):
    tm, D = o_ref.shape[1:]
    PW = 2 * HEAD
    same = _seg_ones(PW).astype(BF16)
    same2 = jnp.concatenate([same, same], axis=0)

    def head_sum(x, pieces):
        if pieces == 1:
            return _dot(x.astype(BF16), same)
        return _dot(jnp.concatenate(_split3(x)[:2], axis=1), same2)

    for r0 in range(0, tm, rows):
        rs = slice(r0, r0 + rows)
        aw = aw_ref[0, rs, :]
        a_f = _sigmoid(a0_ref[0:1, :] + _dot(aw[:, :LANES], a2f_ref[...]))
        a_b = _sigmoid(a0_ref[1:2, :] + _dot(aw[:, LANES:], a2b_ref[...]))
        g = _dot(gs_ref[0, rs, :], g2_ref[...])
        for p in range(D // PW):
            sl = slice(p * PW, (p + 1) * PW)
            ksum = k_ref[0, rs, sl].astype(F32) * (2.0 + (a_f[:, sl] + a_b[:, sl] - 2.0) * ka_ref[:, sl])
            y = yf_ref[0, rs, sl] + yb_ref[0, rs, sl]
            yc = y - head_sum(y, 2) * (1.0 / HEAD)
            var = head_sum(yc * yc, 1) * (1.0 / HEAD)
            o = yc * lax.rsqrt(var + GN_EPS) * lnw_ref[:, sl] + lnb_ref[:, sl]
            bonus = (head_sum(r_ref[0, rs, sl].astype(F32) * ksum * rk_ref[:, sl], 1)
                     * v_ref[0, rs, sl].astype(F32))
            og_scr[rs, sl] = ((o + bonus) * g[:, sl]).astype(og_scr.dtype)
        x1 = res_ref[0, rs, :] + gate_ref[0] * _dot(og_scr[rs, :], wo_ref[...])
        o_ref[0, rs, :] = x1
        h_ref[0, rs, :] = _norm_mod(x1, g_ref[...], sh_ref[0], sc_ref[0]).astype(h_ref.dtype)


def _rwkv_out(yf, yb, r, k, v, gs, aw, res, a2fp, a2bp, g2, wo, a0, ka, rk, lnw, lnb, gate, g, shift,
              scale, name):
    Bn, T, D = res.shape
    tm = _tile(T, 256, 16)
    rows = _tile(tm, 128, 16)
    G = gs.shape[-1]
    row = pl.BlockSpec((1, tm, D), lambda b, i: (b, i, 0))
    vec = pl.BlockSpec((1, D), lambda b, i: (0, 0))
    full = lambda arr: pl.BlockSpec(arr.shape, lambda b, i: (0,) * arr.ndim)
    return pl.pallas_call(
        functools.partial(_rwkv_out_kernel, rows=rows),
        grid=(Bn, T // tm),
        in_specs=[row, row, row, row, row,
                  pl.BlockSpec((1, tm, G), lambda b, i: (b, i, 0)),
                  pl.BlockSpec((1, tm, 2 * LANES), lambda b, i: (b, i, 0)),
                  row, full(a2fp), full(a2bp), full(g2), full(wo), full(a0),
                  vec, vec, vec, vec,
                  pl.BlockSpec((1, 1, D), _bsel(gate)), vec,
                  pl.BlockSpec((1, 1, D), _bsel(shift)),
                  pl.BlockSpec((1, 1, D), _bsel(scale))],
        out_specs=[row, row],
        out_shape=[jax.ShapeDtypeStruct((Bn, T, D), F32), jax.ShapeDtypeStruct((Bn, T, D), BF16)],
        scratch_shapes=[pltpu.VMEM((tm, D), BF16)],
        compiler_params=_params("parallel", "parallel"),
        name=name,
    )(yf, yb, r, k, v, gs, aw, res, a2fp, a2bp, g2, wo, a0, ka, rk, lnw, lnb, gate, g[None], shift, scale)


def _conv_kernel(gb_ref, z_ref, cw_ref, o_ref):
    z = z_ref[0].astype(F32)
    T = z.shape[0]
    t = lax.broadcasted_iota(jnp.int32, (T, 1), 0)
    zp = jnp.where(t == 0, 0.0, pltpu.roll(z, 1, 0))
    zn = jnp.where(t == T - 1, 0.0, pltpu.roll(z, T - 1, 0))
    conv = zp * cw_ref[0:1, :] + z * cw_ref[1:2, :] + zn * cw_ref[2:3, :]
    o_ref[0] = (gb_ref[0] * conv).astype(o_ref.dtype)


def _conv(gb, z, cw):
    Bn, T, D = z.shape
    tn = _tile(D, 512, LANES)
    tok = pl.BlockSpec((1, T, tn), lambda b, j: (b, 0, j))
    return pl.pallas_call(
        _conv_kernel,
        grid=(Bn, D // tn),
        in_specs=[tok, tok, pl.BlockSpec((3, tn), lambda b, j: (0, j))],
        out_specs=tok,
        out_shape=jax.ShapeDtypeStruct((Bn, T, D), BF16),
        compiler_params=_params("parallel", "parallel"),
        name="short_conv",
    )(gb, z, cw)


def _pad_rows(w, rows):
    return jnp.pad(w, ((0, rows - w.shape[0]), (0, 0)))


def _pad_cols(w, cols):
    return jnp.pad(w, ((0, 0), (0, cols - w.shape[1])))


def _split_mod(mod_rows, D):
    return [mod_rows[:, m * D:(m + 1) * D][:, None, :] for m in range(6)]


def _ffn_branch(t1, h2, mods, ffn, tag):
    w13, layer, wdn = ffn
    act = _ffn_up(h2, w13, layer, name="ffn_up_" + tag)
    return _mm_res(act, wdn, t1, mods[5], name="ffn_down_" + tag)


def _rwkv_layer(x, ctx, mods_x, mods_c, g1, g2n, mix, wr, wk, wv, wo, w0, w1, w2, a0, a1, a2,
                lg1, lg2, k_k, k_a, r_k, ln_w, ln_b, ffn):
    D = x.shape[-1]
    H = D // HEAD
    w1cat = jnp.concatenate([_pad_cols(w1[0], LANES), _pad_cols(w1[1], LANES)], axis=1).astype(BF16)
    a1cat = jnp.concatenate([_pad_cols(a1[0], LANES), _pad_cols(a1[1], LANES)], axis=1).astype(BF16)
    w2p = [_pad_rows(w2[d], LANES).astype(BF16) for d in range(2)]
    a2p = [_pad_rows(a2[d], LANES).astype(BF16) for d in range(2)]
    wr, wk, wv, wo = (t.astype(BF16) for t in (wr, wk, wv, wo))
    lg1, lg2 = lg1.astype(BF16), lg2.astype(BF16)
    rk = r_k.reshape(1, D)

    xr, xw, xk, xv, xa, xg = _prep_ctx(ctx, g1, mods_c[0], mods_c[1], mix)
    lora_c = dict(tw=_mm(xw, w1cat, BF16, act="tanh", name="lora_w_c"),
                  aw=_mm(xa, a1cat, BF16, name="lora_a_c"),
                  gs=_mm(xg, lg1, BF16, act="sigmoid", name="lora_g_c"))
    ins = {"c": (xr, xk, xv, lora_c["tw"], lora_c["aw"], lora_c["gs"]),
           "x": _prep_latent(x, g1, mods_x[0], mods_x[1], mix, w1cat, a1cat, lg1)}
    sets = {}
    for tag, (xr, xk, xv, tw, aw, gs) in ins.items():
        sets[tag] = dict(r=_mm(xr, wr, BF16, name="proj_r_" + tag),
                         k=_mm(xk, wk, BF16, name="proj_k_" + tag),
                         v=_mm(xv, wv, BF16, name="proj_v_" + tag), tw=tw, aw=aw, gs=gs)

    ys = {"c": [], "x": []}
    zero_state = jnp.zeros((x.shape[0], H // 2, 2 * HEAD, 2 * HEAD), F32)
    for d in range(2):
        state = zero_state
        for tag in ("c", "x"):
            s = sets[tag]
            y, state = _wkv(s["r"], s["k"], s["v"], s["tw"], s["aw"], w2p[d], a2p[d],
                            w0[d][None], a0[d][None], k_k[None], k_a[None], state, d)
            ys[tag].append(y)

    outs = []
    for tag, tok, mods in (("c", ctx, mods_c), ("x", x, mods_x)):
        s = sets[tag]
        t1, h2 = _rwkv_out(ys[tag][0], ys[tag][1], s["r"], s["k"], s["v"], s["gs"], s["aw"], tok,
                           a2p[0], a2p[1], lg2, wo, a0, k_a[None], rk, ln_w[None], ln_b[None],
                           mods[2], g2n, mods[3], mods[4], name="rwkv_out_" + tag)
        outs.append(_ffn_branch(t1, h2, mods, ffn, tag))
    return outs[1], outs[0]


def _conv_layer(x, mods, g1, g2n, w_in, conv_w, w_out, ffn):
    gb, z = _norm_mm(x, g1, mods[0], mods[1], w_in.astype(BF16), 3, _conv_in_combine, (BF16, BF16),
                     name="conv_in")
    p = _conv(gb, z, conv_w)
    t1, h2 = _mm_res_norm(p, w_out.astype(BF16), x, mods[2], g2n, mods[3], mods[4], name="conv_out")
    return _ffn_branch(t1, h2, mods, ffn, "x")


def kernel(x, c, ctx, c_ctx, norm1_g, norm2_g, ada_w, ada_b, rw_mix, rw_wr, rw_wk, rw_wv, rw_wo,
           rw_w0, rw_w1, rw_w2, rw_a0, rw_a1, rw_a2, rw_g1, rw_g2, rw_kk, rw_ka, rw_rk, rw_lnw,
           rw_lnb, sc_win, sc_conv, sc_wout, ffn_w13, ffn_w2, final_g):
    B, T, D = x.shape
    depth = norm1_g.shape[0]
    rows = -(-(B + 1) // 8) * 8
    cond = jnp.zeros((rows, D), F32).at[:B].set(c).at[B].set(c_ctx)
    for i in range(depth):
        last = i == depth - 1
        j = i // 2
        mod = _ada(cond, ada_w, ada_b, i)
        mods_x = _split_mod(mod[:B], D)
        mods_c = _split_mod(mod[B:B + 1], D)
        ffn = (ffn_w13, i, ffn_w2[i].astype(BF16))
        if i % 2 == 0:
            x, ctx_new = _rwkv_layer(
                x, ctx, mods_x, mods_c, norm1_g[i], norm2_g[i], rw_mix[j], rw_wr[j], rw_wk[j],
                rw_wv[j], rw_wo[j], rw_w0[j], rw_w1[j], rw_w2[j], rw_a0[j], rw_a1[j], rw_a2[j],
                rw_g1[j], rw_g2[j], rw_kk[j], rw_ka[j], rw_rk[j], rw_lnw[j], rw_lnb[j], ffn)
            ctx = ctx_new
        else:
            x = _conv_layer(x, mods_x, norm1_g[i], norm2_g[i], sc_win[j], sc_conv[j], sc_wout[j],
                            ffn)
            if not last:
                ctx = _conv_layer(ctx, mods_c, norm1_g[i], norm2_g[i], sc_win[j], sc_conv[j],
                                  sc_wout[j], ffn)
    zeros = jnp.zeros((1, 1, D), F32)
    return _norm(x, final_g, zeros, zeros)
```

```python
import functools
import math

import jax
import jax.numpy as jnp
from jax import lax
from jax.experimental import pallas as pl
from jax.experimental.pallas import tpu as pltpu

HEAD = 64
GRID_W = 64
CHUNK = 64
NORM_EPS = 1e-6
GN_EPS = 64e-5
LANES = 128
VMEM_LIMIT = 56 * 1024 * 1024

F32 = jnp.float32
BF16 = jnp.bfloat16


def _params(*sem):
    return pltpu.CompilerParams(dimension_semantics=sem, vmem_limit_bytes=VMEM_LIMIT)


def _tile(n, pref, mult):
    t = min(pref, n)
    t -= t % mult
    while t >= mult:
        if n % t == 0:
            return t
        t -= mult
    return n


def _sigmoid(x):
    return 1.0 / (1.0 + jnp.exp(-x))


def _norm_mod(x, g, shift, scale, cols=None):
    rs = lax.rsqrt(jnp.mean(x * x, axis=-1, keepdims=True) + NORM_EPS)
    if cols is not None:
        x, g, shift, scale = (t[:, cols[0]:cols[1]] for t in (x, g, shift, scale))
    return (x * rs) * (g * (1.0 + scale)) + shift


def _dot(a, b):
    return jnp.dot(a, b, preferred_element_type=F32)


def _dot_t(a, b, ca, cb):
    return lax.dot_general(a, b, (((ca,), (cb,)), ((), ())), preferred_element_type=F32)


def _ada_kernel(c_ref, w_ref, b_ref, o_ref):
    c = c_ref[...]
    s = c * _sigmoid(c)
    o_ref[...] = _dot(s.astype(BF16), w_ref[0].astype(BF16)) + b_ref[0]


def _ada(cond, w, b, layer):
    R, D = cond.shape
    N = w.shape[2]
    tn = _tile(N, 1024, LANES)
    return pl.pallas_call(
        _ada_kernel,
        grid=(N // tn,),
        in_specs=[pl.BlockSpec((R, D), lambda j: (0, 0)),
                  pl.BlockSpec((1, D, tn), lambda j: (layer, 0, j)),
                  pl.BlockSpec((1, 1, tn), lambda j: (layer, 0, j))],
        out_specs=pl.BlockSpec((R, tn), lambda j: (0, j)),
        out_shape=jax.ShapeDtypeStruct((R, N), F32),
        compiler_params=_params("parallel"),
        name="ada_mod",
    )(cond, w, b[:, None, :])


def _norm_kernel(x_ref, g_ref, sh_ref, sc_ref, o_ref):
    o_ref[0] = _norm_mod(x_ref[0], g_ref[...], sh_ref[0], sc_ref[0])


def _bsel(arr):
    if arr.shape[0] == 1:
        return lambda b, *_: (0, 0, 0)
    return lambda b, *_: (b, 0, 0)


def _norm(x, g, shift, scale):
    Bn, T, D = x.shape
    tm = _tile(T, 512, 8)
    return pl.pallas_call(
        _norm_kernel,
        grid=(Bn, T // tm),
        in_specs=[pl.BlockSpec((1, tm, D), lambda b, i: (b, i, 0)),
                  pl.BlockSpec((1, D), lambda b, i: (0, 0)),
                  pl.BlockSpec((1, 1, D), _bsel(shift)),
                  pl.BlockSpec((1, 1, D), _bsel(scale))],
        out_specs=pl.BlockSpec((1, tm, D), lambda b, i: (b, i, 0)),
        out_shape=jax.ShapeDtypeStruct((Bn, T, D), F32),
        compiler_params=_params("parallel", "parallel"),
        name="norm",
    )(x, g[None], shift, scale)


def _write_mix(out_refs, mix_ref, h, shifted, c0, c1):
    xx = shifted - h
    for m, o_ref in enumerate(out_refs):
        o_ref[0, :, c0:c1] = (h + xx * mix_ref[m:m + 1, c0:c1]).astype(o_ref.dtype)


def _prep_latent_kernel(x_ref, xu_ref, xd_ref, g_ref, sh_ref, sc_ref, mix_ref, w1_ref, a1_ref, g1_ref,
                        xr_ref, xk_ref, xv_ref, tw_ref, aw_ref, gs_ref, h_scr, xm_scr):
    i = pl.program_id(1)
    n = pl.num_programs(1)
    g, sh, sc = g_ref[...], sh_ref[0], sc_ref[0]
    tm, D = x_ref.shape[1:]
    q = D // 4
    W = GRID_W
    h_scr[0:W, 2 * q:3 * q] = jnp.where(i > 0, _norm_mod(xu_ref[0], g, sh, sc, (2 * q, 3 * q)), 0.0)
    h_scr[W - 8:W, 0:q] = jnp.zeros((8, q), F32)
    for r0 in range(0, tm, W):
        h_scr[W + r0:2 * W + r0, :] = _norm_mod(x_ref[0, r0:r0 + W, :], g, sh, sc)
    h_scr[W + tm:, 3 * q:] = jnp.where(i < n - 1, _norm_mod(xd_ref[0], g, sh, sc, (3 * q, D)), 0.0)
    h_scr[W + tm:W + tm + 8, q:2 * q] = jnp.zeros((8, q), F32)
    wide = {0: xr_ref, 2: xk_ref, 3: xv_ref}
    lora = {1: 0, 4: 1, 5: 2}
    lora_w = (w1_ref, a1_ref, g1_ref)
    acc = [None] * 3
    R = 32
    row = lax.broadcasted_iota(jnp.int32, (R, 1), 0)
    for k, off in enumerate((-1, 1, -W, W)):
        c0, c1 = k * q, (k + 1) * q
        mixk = [mix_ref[m:m + 1, c0:c1] for m in range(6)]
        for r0 in range(0, tm, R):
            h = h_scr[W + r0:W + r0 + R, c0:c1]
            s = h_scr[W + r0 + off:W + r0 + off + R, c0:c1]
            if off == -1 and r0 % W == 0:
                s = jnp.where(row == 0, 0.0, s)
            if off == 1 and (r0 + R) % W == 0:
                s = jnp.where(row == R - 1, 0.0, s)
            xx = s - h
            for m in range(6):
                xm = (h + xx * mixk[m]).astype(BF16)
                if m in wide:
                    wide[m][0, r0:r0 + R, c0:c1] = xm
                else:
                    xm_scr[lora[m], r0:r0 + R, c0:c1] = xm
        for l in range(3):
            part = _dot(xm_scr[l, :, c0:c1], lora_w[l][c0:c1, :])
            acc[l] = part if acc[l] is None else acc[l] + part
    tw_ref[0] = jnp.tanh(acc[0]).astype(tw_ref.dtype)
    aw_ref[0] = acc[1].astype(aw_ref.dtype)
    gs_ref[0] = _sigmoid(acc[2]).astype(gs_ref.dtype)


def _prep_latent(x, g, shift, scale, mix, w1cat, a1cat, lg1):
    Bn, T, D = x.shape
    rows_per_tile = _tile(T // GRID_W, 4, 1)
    tm = rows_per_tile * GRID_W
    nrow = T // GRID_W
    tok = pl.BlockSpec((1, tm, D), lambda b, i: (b, i, 0))
    full = lambda arr: pl.BlockSpec(arr.shape, lambda b, i: (0,) * arr.ndim)
    small = lambda arr: pl.BlockSpec((1, tm, arr.shape[1]), lambda b, i: (b, i, 0))
    return pl.pallas_call(
        _prep_latent_kernel,
        grid=(Bn, T // tm),
        in_specs=[tok,
                  pl.BlockSpec((1, GRID_W, D),
                               lambda b, i: (b, jnp.maximum(i * rows_per_tile - 1, 0), 0)),
                  pl.BlockSpec((1, GRID_W, D),
                               lambda b, i: (b, jnp.minimum((i + 1) * rows_per_tile, nrow - 1), 0)),
                  pl.BlockSpec((1, D), lambda b, i: (0, 0)),
                  pl.BlockSpec((1, 1, D), _bsel(shift)),
                  pl.BlockSpec((1, 1, D), _bsel(scale)),
                  pl.BlockSpec((6, D), lambda b, i: (0, 0)),
                  full(w1cat), full(a1cat), full(lg1)],
        out_specs=[tok, tok, tok, small(w1cat), small(a1cat), small(lg1)],
        out_shape=[jax.ShapeDtypeStruct((Bn, T, D), BF16)] * 3
        + [jax.ShapeDtypeStruct((Bn, T, w.shape[1]), BF16) for w in (w1cat, a1cat, lg1)],
        scratch_shapes=[pltpu.VMEM((tm + 2 * GRID_W, D), F32), pltpu.VMEM((3, tm, D), BF16)],
        compiler_params=_params("parallel", "parallel"),
        name="prep_latent",
    )(x, x, x, g[None], shift, scale, mix, w1cat, a1cat, lg1)


def _prep_ctx_kernel(x_ref, g_ref, sh_ref, sc_ref, mix_ref, *out_refs):
    h = _norm_mod(x_ref[0], g_ref[...], sh_ref[0], sc_ref[0])
    L, D = h.shape
    half = D // 2
    t = lax.broadcasted_iota(jnp.int32, (L, 1), 0)
    h0, h1 = h[:, :half], h[:, half:]
    prev = jnp.where(t == 0, 0.0, pltpu.roll(h0, 1, 0))
    nxt = jnp.where(t == L - 1, 0.0, pltpu.roll(h1, L - 1, 0))
    _write_mix(out_refs, mix_ref, h0, prev, 0, half)
    _write_mix(out_refs, mix_ref, h1, nxt, half, D)


def _prep_ctx(x, g, shift, scale, mix):
    Bn, L, D = x.shape
    return pl.pallas_call(
        _prep_ctx_kernel,
        grid=(Bn,),
        in_specs=[pl.BlockSpec((1, L, D), lambda b: (b, 0, 0)),
                  pl.BlockSpec((1, D), lambda b: (0, 0)),
                  pl.BlockSpec((1, 1, D), _bsel(shift)),
                  pl.BlockSpec((1, 1, D), _bsel(scale)),
                  pl.BlockSpec((6, D), lambda b: (0, 0))],
        out_specs=[pl.BlockSpec((1, L, D), lambda b: (b, 0, 0))] * 6,
        out_shape=[jax.ShapeDtypeStruct((Bn, L, D), BF16)] * 6,
        compiler_params=_params("parallel"),
        name="prep_ctx",
    )(x, g[None], shift, scale, mix)


def _mm_kernel(a_ref, w_ref, o_ref, *, act):
    acc = _dot(a_ref[0], w_ref[...])
    if act == "tanh":
        acc = jnp.tanh(acc)
    elif act == "sigmoid":
        acc = _sigmoid(acc)
    o_ref[0] = acc.astype(o_ref.dtype)


def _mm(a, w, out_dtype, act=None, name="mm"):
    Bn, T, K = a.shape
    N = w.shape[1]
    M = Bn * T
    tm = _tile(M, 1024, 16)
    tn = _tile(N, 2048, LANES)
    out = pl.pallas_call(
        functools.partial(_mm_kernel, act=act),
        grid=(1, M // tm, N // tn),
        in_specs=[pl.BlockSpec((1, tm, K), lambda b, i, j: (b, i, 0)),
                  pl.BlockSpec((K, tn), lambda b, i, j: (0, j))],
        out_specs=pl.BlockSpec((1, tm, tn), lambda b, i, j: (b, i, j)),
        out_shape=jax.ShapeDtypeStruct((1, M, N), out_dtype),
        compiler_params=_params("parallel", "parallel", "parallel"),
        name=name,
    )(a.reshape(1, M, K), w)
    return out.reshape(Bn, T, N)


def _mm_res_kernel(a_ref, w_ref, res_ref, gate_ref, *rest):
    ncast = (len(rest) - 1) // 2
    cast_in, o_ref, cast_out = rest[:ncast], rest[ncast], rest[ncast + 1:]
    o_ref[0] = res_ref[0] + gate_ref[0] * _dot(a_ref[0], w_ref[...])
    for src, dst in zip(cast_in, cast_out):
        dst[...] = src[...].astype(BF16)


def _mm_res(a, w, res, gate, name="mm_res", casts=()):
    Bn, T, K = a.shape
    N = w.shape[1]
    tm = _tile(T, 1024, 16)
    tn = _tile(N, 1024 if K <= 2048 else 512, LANES)
    ni, nj = T // tm, N // tn
    steps = Bn * ni * nj
    gsel = _bsel(gate)
    step = lambda b, i, j: (b * ni + i) * nj + j
    cast_specs = []
    for cw in casts:
        rows = cw.shape[0] // steps
        assert rows * steps == cw.shape[0] and rows % 16 == 0, (cw.shape, steps)
        cast_specs.append(pl.BlockSpec((rows, cw.shape[1]), lambda b, i, j: (step(b, i, j), 0)))
    tile = pl.BlockSpec((1, tm, tn), lambda b, i, j: (b, i, j))
    outs = pl.pallas_call(
        _mm_res_kernel,
        grid=(Bn, ni, nj),
        in_specs=[pl.BlockSpec((1, tm, K), lambda b, i, j: (b, i, 0)),
                  pl.BlockSpec((K, tn), lambda b, i, j: (0, j)),
                  tile,
                  pl.BlockSpec((1, 1, tn), lambda b, i, j: gsel(b)[:2] + (j,))] + cast_specs,
        out_specs=[tile] + cast_specs,
        out_shape=[jax.ShapeDtypeStruct((Bn, T, N), F32)]
        + [jax.ShapeDtypeStruct(cw.shape, BF16) for cw in casts],
        compiler_params=_params("arbitrary", "arbitrary", "arbitrary"),
        name=name,
    )(a, w, res, gate, *casts)
    return outs[0], tuple(outs[1:])


def _mm_res_norm_kernel(a_ref, w_ref, res_ref, gate_ref, g_ref, sh_ref, sc_ref, o_ref, h_ref, *, rows):
    for r0 in range(0, a_ref.shape[1], rows):
        rs = slice(r0, r0 + rows)
        x1 = res_ref[0, rs, :] + gate_ref[0] * _dot(a_ref[0, rs, :], w_ref[...])
        o_ref[0, rs, :] = x1
        h_ref[0, rs, :] = _norm_mod(x1, g_ref[...], sh_ref[0], sc_ref[0]).astype(h_ref.dtype)


def _mm_res_norm(a, w, res, gate, g, shift, scale, name):
    Bn, T, K = a.shape
    N = w.shape[1]
    tm = _tile(T, 512, 16)
    rows = _tile(tm, 256, 16)
    row = pl.BlockSpec((1, tm, N), lambda b, i: (b, i, 0))
    return pl.pallas_call(
        functools.partial(_mm_res_norm_kernel, rows=rows),
        grid=(Bn, T // tm),
        in_specs=[pl.BlockSpec((1, tm, K), lambda b, i: (b, i, 0)),
                  pl.BlockSpec((K, N), lambda b, i: (0, 0)),
                  row,
                  pl.BlockSpec((1, 1, N), _bsel(gate)),
                  pl.BlockSpec((1, N), lambda b, i: (0, 0)),
                  pl.BlockSpec((1, 1, N), _bsel(shift)),
                  pl.BlockSpec((1, 1, N), _bsel(scale))],
        out_specs=[row, row],
        out_shape=[jax.ShapeDtypeStruct((Bn, T, N), F32), jax.ShapeDtypeStruct((Bn, T, N), BF16)],
        compiler_params=_params("parallel", "parallel"),
        name=name,
    )(a, w, res, gate, g[None], shift, scale)


def _ffn_up_kernel(h_ref, wa_ref, wb_ref, w2_ref, o_ref, w2o_ref, w_scr, *, rows):
    @pl.when((pl.program_id(1) == 0) & (pl.program_id(2) == 0))
    def _():
        w_scr[0] = wa_ref[0].astype(BF16)
        w_scr[1] = wb_ref[0].astype(BF16)

    for r0 in range(0, h_ref.shape[1], rows):
        rs = slice(r0, r0 + rows)
        h = h_ref[0, rs, :]
        a = _dot(h, w_scr[0])
        o_ref[0, rs, :] = (a * _sigmoid(a) * _dot(h, w_scr[1])).astype(o_ref.dtype)
    w2o_ref[...] = w2_ref[0].astype(BF16)


def _ffn_up(h, w13, w2, layer, name):
    Bn, T, D = h.shape
    F = w13.shape[2] // 2
    tm = _tile(T, 2048, 16)
    rows = _tile(tm, 512, 16)
    tn = _tile(F, 512, LANES)
    nj, ni = F // tn, T // tm
    steps = nj * Bn * ni
    F2, D2 = w2.shape[1:]
    rows2 = F2 // steps
    assert rows2 * steps == F2 and rows2 % 16 == 0, (F2, steps)
    step = lambda j, b, i: (j * Bn + b) * ni + i
    return pl.pallas_call(
        functools.partial(_ffn_up_kernel, rows=rows),
        grid=(nj, Bn, ni),
        in_specs=[pl.BlockSpec((1, tm, D), lambda j, b, i: (b, i, 0)),
                  pl.BlockSpec((1, D, tn), lambda j, b, i: (layer, 0, j)),
                  pl.BlockSpec((1, D, tn), lambda j, b, i: (layer, 0, j + nj)),
                  pl.BlockSpec((1, rows2, D2), lambda j, b, i: (layer, step(j, b, i), 0))],
        out_specs=[pl.BlockSpec((1, tm, tn), lambda j, b, i: (b, i, j)),
                   pl.BlockSpec((rows2, D2), lambda j, b, i: (step(j, b, i), 0))],
        out_shape=[jax.ShapeDtypeStruct((Bn, T, F), BF16), jax.ShapeDtypeStruct((F2, D2), BF16)],
        scratch_shapes=[pltpu.VMEM((2, D, tn), BF16)],
        compiler_params=_params("arbitrary", "arbitrary", "arbitrary"),
        name=name,
    )(h, w13, w13, w2)


def _conv_in_combine(gb, gc, u):
    return gb, gc * u


def _norm_mm_kernel(x_ref, g_ref, sh_ref, sc_ref, *rest, nw, combine, rows):
    w_refs, out_refs, h_scr = rest[:nw], rest[nw:-1], rest[-1]

    def emit(rs, h):
        outs = combine(*[_dot(h, w_ref[...]) for w_ref in w_refs])
        for o_ref, o in zip(out_refs, outs):
            o_ref[0, rs, :] = o.astype(o_ref.dtype)

    @pl.when(pl.program_id(2) == 0)
    def _():
        for r0 in range(0, x_ref.shape[1], rows):
            rs = slice(r0, r0 + rows)
            h = _norm_mod(x_ref[0, rs, :], g_ref[...], sh_ref[0], sc_ref[0]).astype(BF16)
            h_scr[rs, :] = h
            emit(rs, h)

    @pl.when(pl.program_id(2) > 0)
    def _():
        emit(slice(None), h_scr[...])


def _norm_mm(x, g, shift, scale, w, nw, combine, out_dtypes, name):
    Bn, T, D = x.shape
    N = w.shape[1] // nw
    tm = _tile(T, 1024, 16)
    rows = _tile(tm, 256, 16)
    tn = _tile(N, 512, LANES)
    nj = N // tn
    w_specs = [pl.BlockSpec((D, tn), functools.partial(lambda b, i, j, m: (0, j + m * nj), m=m))
               for m in range(nw)]
    return pl.pallas_call(
        functools.partial(_norm_mm_kernel, nw=nw, combine=combine, rows=rows),
        grid=(Bn, T // tm, nj),
        in_specs=[pl.BlockSpec((1, tm, D), lambda b, i, j: (b, i, 0)),
                  pl.BlockSpec((1, D), lambda b, i, j: (0, 0)),
                  pl.BlockSpec((1, 1, D), _bsel(shift)),
                  pl.BlockSpec((1, 1, D), _bsel(scale))] + w_specs,
        out_specs=[pl.BlockSpec((1, tm, tn), lambda b, i, j: (b, i, j))] * len(out_dtypes),
        out_shape=[jax.ShapeDtypeStruct((Bn, T, N), dt) for dt in out_dtypes],
        scratch_shapes=[pltpu.VMEM((tm, D), BF16)],
        compiler_params=_params("parallel", "parallel", "arbitrary"),
        name=name,
    )(x, g[None], shift, scale, *([w] * nw))


def _seg_ones(width):
    shift = HEAD.bit_length() - 1
    r = lax.shift_right_logical(lax.broadcasted_iota(jnp.int32, (width, width), 0), shift)
    c = lax.shift_right_logical(lax.broadcasted_iota(jnp.int32, (width, width), 1), shift)
    return (r == c).astype(F32)


def _split3(x):
    hi = x.astype(BF16)
    r1 = x - hi.astype(F32)
    mid = r1.astype(BF16)
    lo = (r1 - mid.astype(F32)).astype(BF16)
    return hi, mid, lo


def _wkv_kernel(r_ref, k_ref, v_ref, tw_ref, aw_ref, w2_ref, a2_ref, w0_ref, a0_ref, kk_ref, ka_ref,
                s0_ref, y_ref, sout_ref, s_scr, x_scr, r2_scr, bv_scr, vb_scr, z_scr, wt_scr,
                *, reverse, npair):
    c = pl.program_id(2)
    C = CHUNK
    PW = 2 * HEAD

    @pl.when(c == 0)
    def _():
        s_scr[...] = s0_ref[0]
        x_scr[...] = jnp.zeros_like(x_scr)
        r2_scr[...] = jnp.zeros_like(r2_scr)
        bv_scr[...] = jnp.zeros_like(bv_scr)
        vb_scr[...] = jnp.zeros_like(vb_scr)
        z_scr[...] = jnp.zeros_like(z_scr)
        wt_scr[...] = jnp.ones_like(wt_scr)

    rr = lax.shift_right_logical(lax.broadcasted_iota(jnp.int32, (PW, PW), 0), HEAD.bit_length() - 1)
    cc = lax.shift_right_logical(lax.broadcasted_iota(jnp.int32, (PW, PW), 1), HEAD.bit_length() - 1)
    same = rr == cc
    same_bf = same.astype(BF16)

    def bd(x):
        xb = x.astype(BF16)
        return jnp.concatenate([xb, xb], axis=0) * same_bf

    t2 = lax.broadcasted_iota(jnp.int32, (C, PW), 0)
    s2 = lax.broadcasted_iota(jnp.int32, (C, PW), 1) & (HEAD - 1)
    before = (s2 > t2) if reverse else (s2 < t2)
    upto = before | (s2 == t2)
    pairs = range(npair)
    sls = [slice(p * PW, (p + 1) * PW) for p in pairs]

    r, k, v = (t[0].astype(F32) for t in (r_ref, k_ref, v_ref))
    z = w0_ref[...] + _dot(tw_ref[0], w2_ref[...])
    a_pre = a0_ref[...] + _dot(aw_ref[0], a2_ref[...])

    S = [s_scr[p] for p in pairs]
    X = [x_scr[p] for p in pairs]
    G = [_dot_t(X[p], r2_scr[p], 1, 1) for p in pairs]
    XS = [_dot_t(X[p], S[p].astype(BF16), 1, 1) for p in pairs]
    u = [XS[p][:C] + _dot(jnp.where(before, G[p][:C, PW:], 0.0).astype(BF16), bv_scr[p]) for p in pairs]
    P = [jnp.where(before, G[p][:C, :PW], 0.0) for p in pairs]

    lw = -math.exp(-0.5) * _sigmoid(z)
    kkv = k * kk_ref[...]
    kk2 = kkv * kkv
    t_i = lax.broadcasted_iota(jnp.int32, (C, C), 0)
    s_i = lax.broadcasted_iota(jnp.int32, (C, C), 1)
    tri = ((s_i >= t_i) if reverse else (s_i <= t_i)).astype(BF16)
    cum = _dot(jnp.concatenate([tri, tri], axis=1),
               jnp.concatenate(_split3(lw)[:2], axis=0))
    same2 = jnp.concatenate([same_bf, same_bf], axis=0)
    ss = [_dot(jnp.concatenate(_split3(kk2[:, sl])[:2], axis=1), same2) for sl in sls]

    n_sq = C.bit_length() - 1
    for j in range(n_sq):
        Pb = [P[p].astype(BF16) for p in pairs]
        if j < n_sq - 1:
            PU = [_dot(Pb[p], jnp.concatenate([bd(P[p]), bd(u[p])], axis=1)) for p in pairs]
            P = [PU[p][:, :PW] for p in pairs]
            u = [u[p] + PU[p][:, PW:] for p in pairs]
        else:
            u = [u[p] + _dot(Pb[p], bd(u[p])) for p in pairs]
    for p in pairs:
        R = jnp.concatenate([jnp.where(upto, G[p][C:, :PW], 0.0),
                             jnp.where(upto, G[p][C:, PW:], 0.0)], axis=1).astype(BF16)
        y_ref[0, :, sls[p]] = XS[p][C:] + _dot(R, jnp.concatenate([bd(u[p]), bv_scr[p]], axis=0))
    for p in pairs:
        UV = jnp.concatenate([u[p].astype(BF16), vb_scr[p]], axis=0)
        dS = _dot_t(UV, z_scr[p], 0, 0)
        s_scr[p] = (S[p] + jnp.where(same, dS, 0.0)) * wt_scr[p, 0:1, :]

    a_sig = _sigmoid(a_pre)
    kd = k * (1.0 + (a_sig - 1.0) * ka_ref[...])
    e_pos = jnp.exp(cum)
    e_neg = jnp.exp(-cum)
    e_prev = jnp.exp(cum - lw)
    last = 0 if reverse else C - 1
    for p, sl in zip(pairs, sls):
        kkn = kkv[:, sl] * lax.rsqrt(jnp.maximum(ss[p], 1e-24))
        at = (-kkn) * e_prev[:, sl]
        bt = (kkn * a_sig[:, sl]) * e_neg[:, sl]
        rt = r[:, sl] * e_pos[:, sl]
        kt = kd[:, sl] * e_neg[:, sl]
        x_scr[p] = jnp.concatenate([at, rt], axis=0).astype(BF16)
        r2_scr[p] = jnp.concatenate([bd(bt), bd(kt)], axis=0)
        bv_scr[p] = bd(v[:, sl])
        vb_scr[p] = v[:, sl].astype(BF16)
        z_scr[p] = jnp.concatenate([bt, kt], axis=0).astype(BF16)
        wt_scr[p] = jnp.broadcast_to(e_pos[last:last + 1, sl], wt_scr.shape[1:])

    @pl.when(c == pl.num_programs(2) - 1)
    def _():
        sout_ref[0] = s_scr[...]


def _wkv(r, k, v, tw, aw, w2p, a2p, w0, a0, kk, ka, s0, d):
    Bn, T, D = r.shape
    PW = 2 * HEAD
    npairs = D // PW
    npair = _tile(npairs, 16, 1)
    hw = npair * PW
    nc = T // CHUNK
    reverse = d == 1
    pos = (lambda j: nc - 1 - j) if reverse else (lambda j: j)
    cin = lambda c: pos(jnp.minimum(c, nc - 1))
    cout = lambda c: pos(jnp.maximum(c - 1, 0))
    tok = pl.BlockSpec((1, CHUNK, hw), lambda b, g, c: (b, cin(c), g))
    lora = pl.BlockSpec((1, CHUNK, LANES), lambda b, g, c: (b, cin(c), d))
    lw2 = pl.BlockSpec((LANES, hw), lambda b, g, c: (0, g))
    vec = pl.BlockSpec((1, hw), lambda b, g, c: (0, g))
    st = pl.BlockSpec((1, npair, PW, PW), lambda b, g, c: (b, g, 0, 0))
    return pl.pallas_call(
        functools.partial(_wkv_kernel, reverse=reverse, npair=npair),
        grid=(Bn, npairs // npair, nc + 1),
        in_specs=[tok, tok, tok, lora, lora, lw2, lw2, vec, vec, vec, vec, st],
        out_specs=[pl.BlockSpec((1, CHUNK, hw), lambda b, g, c: (b, cout(c), g)), st],
        out_shape=[jax.ShapeDtypeStruct((Bn, T, D), F32),
                   jax.ShapeDtypeStruct((Bn, npairs, PW, PW), F32)],
        scratch_shapes=[pltpu.VMEM((npair, PW, PW), F32),
                        pltpu.VMEM((npair, 2 * CHUNK, PW), BF16),
                        pltpu.VMEM((npair, 2 * PW, PW), BF16),
                        pltpu.VMEM((npair, PW, PW), BF16),
                        pltpu.VMEM((npair, CHUNK, PW), BF16),
                        pltpu.VMEM((npair, 2 * CHUNK, PW), BF16),
                        pltpu.VMEM((npair, 8, PW), F32)],
        compiler_params=_params("parallel", "parallel", "arbitrary"),
        name="wkv_rev" if reverse else "wkv_fwd",
    )(r, k, v, tw, aw, w2p, a2p, w0, a0, kk, ka, s0)


def _rwkv_out_kernel(yf_ref, yb_ref, r_ref, k_ref, v_ref, gs_ref, aw_ref, res_ref, a2f_ref, a2b_ref,
                     g2_ref, wo_ref, a0_ref, ka_ref, rk_ref, lnw_ref, lnb_ref, gate_ref, g_ref, sh_ref,
                     sc_ref, o_ref, h_ref, og_scr, *, rows):
    tm, D = o_ref.shape[1:]
    PW = 2 * HEAD
    same = _seg_ones(PW).astype(BF16)
    same2 = jnp.concatenate([same, same], axis=0)

    def head_sum(x, pieces):
        if pieces == 1:
            return _dot(x.astype(BF16), same)
        return _dot(jnp.concatenate(_split3(x)[:2], axis=1), same2)

    for r0 in range(0, tm, rows):
        rs = slice(r0, r0 + rows)
        aw = aw_ref[0, rs, :]
        a_f = _sigmoid(a0_ref[0:1, :] + _dot(aw[:, :LANES], a2f_ref[...]))
        a_b = _sigmoid(a0_ref[1:2, :] + _dot(aw[:, LANES:], a2b_ref[...]))
        g = _dot(gs_ref[0, rs, :], g2_ref[...])
        for p in range(D // PW):
            sl = slice(p * PW, (p + 1) * PW)
            ksum = k_ref[0, rs, sl].astype(F32) * (2.0 + (a_f[:, sl] + a_b[:, sl] - 2.0) * ka_ref[:, sl])
            y = yf_ref[0, rs, sl] + yb_ref[0, rs, sl]
            yc = y - head_sum(y, 2) * (1.0 / HEAD)
            var = head_sum(yc * yc, 1) * (1.0 / HEAD)
            o = yc * lax.rsqrt(var + GN_EPS) * lnw_ref[:, sl] + lnb_ref[:, sl]
            bonus = (head_sum(r_ref[0, rs, sl].astype(F32) * ksum * rk_ref[:, sl], 1)
                     * v_ref[0, rs, sl].astype(F32))
            og_scr[rs, sl] = ((o + bonus) * g[:, sl]).astype(og_scr.dtype)
        x1 = res_ref[0, rs, :] + gate_ref[0] * _dot(og_scr[rs, :], wo_ref[...])
        o_ref[0, rs, :] = x1
        h_ref[0, rs, :] = _norm_mod(x1, g_ref[...], sh_ref[0], sc_ref[0]).astype(h_ref.dtype)


def _rwkv_out(yf, yb, r, k, v, gs, aw, res, a2fp, a2bp, g2, wo, a0, ka, rk, lnw, lnb, gate, g, shift,
              scale, name):
    Bn, T, D = res.shape
    tm = _tile(T, 256, 16)
    rows = _tile(tm, 128, 16)
    G = gs.shape[-1]
    row = pl.BlockSpec((1, tm, D), lambda b, i: (b, i, 0))
    vec = pl.BlockSpec((1, D), lambda b, i: (0, 0))
    full = lambda arr: pl.BlockSpec(arr.shape, lambda b, i: (0,) * arr.ndim)
    return pl.pallas_call(
        functools.partial(_rwkv_out_kernel, rows=rows),
        grid=(Bn, T // tm),
        in_specs=[row, row, row, row, row,
                  pl.BlockSpec((1, tm, G), lambda b, i: (b, i, 0)),
                  pl.BlockSpec((1, tm, 2 * LANES), lambda b, i: (b, i, 0)),
                  row, full(a2fp), full(a2bp), full(g2), full(wo), full(a0),
                  vec, vec, vec, vec,
                  pl.BlockSpec((1, 1, D), _bsel(gate)), vec,
                  pl.BlockSpec((1, 1, D), _bsel(shift)),
                  pl.BlockSpec((1, 1, D), _bsel(scale))],
        out_specs=[row, row],
        out_shape=[jax.ShapeDtypeStruct((Bn, T, D), F32), jax.ShapeDtypeStruct((Bn, T, D), BF16)],
        scratch_shapes=[pltpu.VMEM((tm, D), BF16)],
        compiler_params=_params("parallel", "parallel"),
        name=name,
    )(yf, yb, r, k, v, gs, aw, res, a2fp, a2bp, g2, wo, a0, ka, rk, lnw, lnb, gate, g[None], shift, scale)


def _conv_kernel(gb_ref, z_ref, cw_ref, o_ref):
    z = z_ref[0].astype(F32)
    T = z.shape[0]
    t = lax.broadcasted_iota(jnp.int32, (T, 1), 0)
    zp = jnp.where(t == 0, 0.0, pltpu.roll(z, 1, 0))
    zn = jnp.where(t == T - 1, 0.0, pltpu.roll(z, T - 1, 0))
    conv = zp * cw_ref[0:1, :] + z * cw_ref[1:2, :] + zn * cw_ref[2:3, :]
    o_ref[0] = (gb_ref[0] * conv).astype(o_ref.dtype)


def _conv(gb, z, cw):
    Bn, T, D = z.shape
    tn = _tile(D, 512, LANES)
    tok = pl.BlockSpec((1, T, tn), lambda b, j: (b, 0, j))
    return pl.pallas_call(
        _conv_kernel,
        grid=(Bn, D // tn),
        in_specs=[tok, tok, pl.BlockSpec((3, tn), lambda b, j: (0, j))],
        out_specs=tok,
        out_shape=jax.ShapeDtypeStruct((Bn, T, D), BF16),
        compiler_params=_params("parallel", "parallel"),
        name="short_conv",
    )(gb, z, cw)


def _pad_rows(w, rows):
    return jnp.pad(w, ((0, rows - w.shape[0]), (0, 0)))


def _pad_cols(w, cols):
    return jnp.pad(w, ((0, 0), (0, cols - w.shape[1])))


def _split_mod(mod_rows, D):
    return [mod_rows[:, m * D:(m + 1) * D][:, None, :] for m in range(6)]


def _ffn_branch(t1, h2, mods, ffn, tag):
    w13, w2, layer, casts = ffn
    act, wdn = _ffn_up(h2, w13, w2, layer, name="ffn_up_" + tag)
    return _mm_res(act, wdn, t1, mods[5], name="ffn_down_" + tag, casts=casts)


def _rwkv_layer(x, ctx, mods_x, mods_c, g1, g2n, mix, wr, wk, wv, wo, w0, w1, w2, a0, a1, a2,
                lg1, lg2, k_k, k_a, r_k, ln_w, ln_b, ffn):
    D = x.shape[-1]
    H = D // HEAD
    w1cat = jnp.concatenate([_pad_cols(w1[0], LANES), _pad_cols(w1[1], LANES)], axis=1).astype(BF16)
    a1cat = jnp.concatenate([_pad_cols(a1[0], LANES), _pad_cols(a1[1], LANES)], axis=1).astype(BF16)
    w2p = [_pad_rows(w2[d], LANES).astype(BF16) for d in range(2)]
    a2p = [_pad_rows(a2[d], LANES).astype(BF16) for d in range(2)]
    wr, wk, wv, wo = (t.astype(BF16) for t in (wr, wk, wv, wo))
    lg1, lg2 = lg1.astype(BF16), lg2.astype(BF16)
    rk = r_k.reshape(1, D)

    xr, xw, xk, xv, xa, xg = _prep_ctx(ctx, g1, mods_c[0], mods_c[1], mix)
    lora_c = dict(tw=_mm(xw, w1cat, BF16, act="tanh", name="lora_w_c"),
                  aw=_mm(xa, a1cat, BF16, name="lora_a_c"),
                  gs=_mm(xg, lg1, BF16, act="sigmoid", name="lora_g_c"))
    ins = {"c": (xr, xk, xv, lora_c["tw"], lora_c["aw"], lora_c["gs"]),
           "x": _prep_latent(x, g1, mods_x[0], mods_x[1], mix, w1cat, a1cat, lg1)}
    sets = {}
    for tag, (xr, xk, xv, tw, aw, gs) in ins.items():
        sets[tag] = dict(r=_mm(xr, wr, BF16, name="proj_r_" + tag),
                         k=_mm(xk, wk, BF16, name="proj_k_" + tag),
                         v=_mm(xv, wv, BF16, name="proj_v_" + tag), tw=tw, aw=aw, gs=gs)

    ys = {"c": [], "x": []}
    zero_state = jnp.zeros((x.shape[0], H // 2, 2 * HEAD, 2 * HEAD), F32)
    for d in range(2):
        state = zero_state
        for tag in ("c", "x"):
            s = sets[tag]
            y, state = _wkv(s["r"], s["k"], s["v"], s["tw"], s["aw"], w2p[d], a2p[d],
                            w0[d][None], a0[d][None], k_k[None], k_a[None], state, d)
            ys[tag].append(y)

    outs = []
    for tag, tok, mods in (("c", ctx, mods_c), ("x", x, mods_x)):
        s = sets[tag]
        t1, h2 = _rwkv_out(ys[tag][0], ys[tag][1], s["r"], s["k"], s["v"], s["gs"], s["aw"], tok,
                           a2p[0], a2p[1], lg2, wo, a0, k_a[None], rk, ln_w[None], ln_b[None],
                           mods[2], g2n, mods[3], mods[4], name="rwkv_out_" + tag)
        outs.append(_ffn_branch(t1, h2, mods, ffn, tag))
    (ctx_out, _), (x_out, cast_out) = outs
    return x_out, ctx_out, cast_out


def _conv_layer(x, mods, g1, g2n, w_in, conv_w, w_out, ffn):
    gb, z = _norm_mm(x, g1, mods[0], mods[1], w_in.astype(BF16), 3, _conv_in_combine, (BF16, BF16),
                     name="conv_in")
    p = _conv(gb, z, conv_w)
    t1, h2 = _mm_res_norm(p, w_out.astype(BF16), x, mods[2], g2n, mods[3], mods[4], name="conv_out")
    return _ffn_branch(t1, h2, mods, ffn, "x")


def kernel(x, c, ctx, c_ctx, norm1_g, norm2_g, ada_w, ada_b, rw_mix, rw_wr, rw_wk, rw_wv, rw_wo,
           rw_w0, rw_w1, rw_w2, rw_a0, rw_a1, rw_a2, rw_g1, rw_g2, rw_kk, rw_ka, rw_rk, rw_lnw,
           rw_lnb, sc_win, sc_conv, sc_wout, ffn_w13, ffn_w2, final_g):
    B, T, D = x.shape
    depth = norm1_g.shape[0]
    rows = -(-(B + 1) // 8) * 8
    cond = jnp.zeros((rows, D), F32).at[:B].set(c).at[B].set(c_ctx)
    conv_w = {}
    for i in range(depth):
        last = i == depth - 1
        j = i // 2
        mod = _ada(cond, ada_w, ada_b, i)
        mods_x = _split_mod(mod[:B], D)
        mods_c = _split_mod(mod[B:B + 1], D)
        nxt = (i + 1) // 2
        casts = (sc_win[nxt], sc_wout[nxt]) if (not last and i % 2 == 0) else ()
        ffn = (ffn_w13, ffn_w2, i, casts)
        if i % 2 == 0:
            x, ctx, cast_out = _rwkv_layer(
                x, ctx, mods_x, mods_c, norm1_g[i], norm2_g[i], rw_mix[j], rw_wr[j], rw_wk[j],
                rw_wv[j], rw_wo[j], rw_w0[j], rw_w1[j], rw_w2[j], rw_a0[j], rw_a1[j], rw_a2[j],
                rw_g1[j], rw_g2[j], rw_kk[j], rw_ka[j], rw_rk[j], rw_lnw[j], rw_lnb[j], ffn)
            if casts:
                conv_w[nxt] = cast_out
        else:
            w_in, w_out = conv_w.get(j, (sc_win[j], sc_wout[j]))
            if not last:
                ctx, _ = _conv_layer(ctx, mods_c, norm1_g[i], norm2_g[i], w_in, sc_conv[j], w_out,
                                     (ffn_w13, ffn_w2, i, ()))
            x, _ = _conv_layer(x, mods_x, norm1_g[i], norm2_g[i], w_in, sc_conv[j], w_out, ffn)
    zeros = jnp.zeros((1, 1, D), F32)
    return _norm(x, final_g, zeros, zeros)
```

```python
import functools
import math

import jax
import jax.numpy as jnp
from jax import lax
from jax.experimental import pallas as pl
from jax.experimental.pallas import tpu as pltpu

HEAD = 64
GRID_W = 64
CHUNK = 64
NORM_EPS = 1e-6
GN_EPS = 64e-5
LANES = 128
VMEM_LIMIT = 56 * 1024 * 1024

F32 = jnp.float32
BF16 = jnp.bfloat16


def _params(*sem):
    return pltpu.CompilerParams(dimension_semantics=sem, vmem_limit_bytes=VMEM_LIMIT)


def _tile(n, pref, mult):
    t = min(pref, n)
    t -= t % mult
    while t >= mult:
        if n % t == 0:
            return t
        t -= mult
    return n


def _sigmoid(x):
    return 1.0 / (1.0 + jnp.exp(-x))


def _norm_mod(x, g, shift, scale, cols=None):
    rs = lax.rsqrt(jnp.mean(x * x, axis=-1, keepdims=True) + NORM_EPS)
    if cols is not None:
        x, g, shift, scale = (t[:, cols[0]:cols[1]] for t in (x, g, shift, scale))
    return (x * rs) * (g * (1.0 + scale)) + shift


def _dot(a, b):
    return jnp.dot(a, b, preferred_element_type=F32)


def _dot_t(a, b, ca, cb):
    return lax.dot_general(a, b, (((ca,), (cb,)), ((), ())), preferred_element_type=F32)


def _ada_kernel(c_ref, w_ref, b_ref, o_ref):
    c = c_ref[...]
    s = c * _sigmoid(c)
    o_ref[...] = _dot(s.astype(BF16), w_ref[0].astype(BF16)) + b_ref[0]


def _ada(cond, w, b, layer):
    R, D = cond.shape
    N = w.shape[2]
    tn = _tile(N, 1024, LANES)
    return pl.pallas_call(
        _ada_kernel,
        grid=(N // tn,),
        in_specs=[pl.BlockSpec((R, D), lambda j: (0, 0)),
                  pl.BlockSpec((1, D, tn), lambda j: (layer, 0, j)),
                  pl.BlockSpec((1, 1, tn), lambda j: (layer, 0, j))],
        out_specs=pl.BlockSpec((R, tn), lambda j: (0, j)),
        out_shape=jax.ShapeDtypeStruct((R, N), F32),
        compiler_params=_params("parallel"),
        name="ada_mod",
    )(cond, w, b[:, None, :])


def _norm_kernel(x_ref, g_ref, sh_ref, sc_ref, o_ref):
    o_ref[0] = _norm_mod(x_ref[0], g_ref[...], sh_ref[0], sc_ref[0])


def _bsel(arr):
    if arr.shape[0] == 1:
        return lambda b, *_: (0, 0, 0)
    return lambda b, *_: (b, 0, 0)


def _norm(x, g, shift, scale):
    Bn, T, D = x.shape
    tm = _tile(T, 512, 8)
    return pl.pallas_call(
        _norm_kernel,
        grid=(Bn, T // tm),
        in_specs=[pl.BlockSpec((1, tm, D), lambda b, i: (b, i, 0)),
                  pl.BlockSpec((1, D), lambda b, i: (0, 0)),
                  pl.BlockSpec((1, 1, D), _bsel(shift)),
                  pl.BlockSpec((1, 1, D), _bsel(scale))],
        out_specs=pl.BlockSpec((1, tm, D), lambda b, i: (b, i, 0)),
        out_shape=jax.ShapeDtypeStruct((Bn, T, D), F32),
        compiler_params=_params("parallel", "parallel"),
        name="norm",
    )(x, g[None], shift, scale)


def _write_mix(out_refs, mix_ref, h, shifted, c0, c1):
    xx = shifted - h
    for m, o_ref in enumerate(out_refs):
        o_ref[0, :, c0:c1] = (h + xx * mix_ref[m:m + 1, c0:c1]).astype(o_ref.dtype)


def _prep_latent_kernel(x_ref, xu_ref, xd_ref, g_ref, sh_ref, sc_ref, mix_ref, w1_ref, a1_ref, g1_ref,
                        *rest, ncast):
    cast_in, rest = rest[:ncast], rest[ncast:]
    xr_ref, xk_ref, xv_ref, tw_ref, aw_ref, gs_ref = rest[:6]
    cast_out, (h_scr, xm_scr) = rest[6:6 + ncast], rest[6 + ncast:]
    for src, dst in zip(cast_in, cast_out):
        dst[...] = src[...].astype(BF16)
    i = pl.program_id(1)
    n = pl.num_programs(1)
    g, sh, sc = g_ref[...], sh_ref[0], sc_ref[0]
    tm, D = x_ref.shape[1:]
    q = D // 4
    W = GRID_W
    h_scr[0:W, 2 * q:3 * q] = jnp.where(i > 0, _norm_mod(xu_ref[0], g, sh, sc, (2 * q, 3 * q)), 0.0)
    h_scr[W - 8:W, 0:q] = jnp.zeros((8, q), F32)
    for r0 in range(0, tm, W):
        h_scr[W + r0:2 * W + r0, :] = _norm_mod(x_ref[0, r0:r0 + W, :], g, sh, sc)
    h_scr[W + tm:, 3 * q:] = jnp.where(i < n - 1, _norm_mod(xd_ref[0], g, sh, sc, (3 * q, D)), 0.0)
    h_scr[W + tm:W + tm + 8, q:2 * q] = jnp.zeros((8, q), F32)
    wide = {0: xr_ref, 2: xk_ref, 3: xv_ref}
    lora = {1: 0, 4: 1, 5: 2}
    lora_w = (w1_ref, a1_ref, g1_ref)
    acc = [None] * 3
    R = 32
    row = lax.broadcasted_iota(jnp.int32, (R, 1), 0)
    for k, off in enumerate((-1, 1, -W, W)):
        c0, c1 = k * q, (k + 1) * q
        mixk = [mix_ref[m:m + 1, c0:c1] for m in range(6)]
        for r0 in range(0, tm, R):
            h = h_scr[W + r0:W + r0 + R, c0:c1]
            s = h_scr[W + r0 + off:W + r0 + off + R, c0:c1]
            if off == -1 and r0 % W == 0:
                s = jnp.where(row == 0, 0.0, s)
            if off == 1 and (r0 + R) % W == 0:
                s = jnp.where(row == R - 1, 0.0, s)
            xx = s - h
            for m in range(6):
                xm = (h + xx * mixk[m]).astype(BF16)
                if m in wide:
                    wide[m][0, r0:r0 + R, c0:c1] = xm
                else:
                    xm_scr[lora[m], r0:r0 + R, c0:c1] = xm
        for l in range(3):
            part = _dot(xm_scr[l, :, c0:c1], lora_w[l][c0:c1, :])
            acc[l] = part if acc[l] is None else acc[l] + part
    tw_ref[0] = jnp.tanh(acc[0]).astype(tw_ref.dtype)
    aw_ref[0] = acc[1].astype(aw_ref.dtype)
    gs_ref[0] = _sigmoid(acc[2]).astype(gs_ref.dtype)


def _prep_latent(x, g, shift, scale, mix, w1cat, a1cat, lg1, casts=()):
    Bn, T, D = x.shape
    rows_per_tile = _tile(T // GRID_W, 4, 1)
    tm = rows_per_tile * GRID_W
    nrow = T // GRID_W
    ni = T // tm
    steps = Bn * ni
    tok = pl.BlockSpec((1, tm, D), lambda b, i: (b, i, 0))
    full = lambda arr: pl.BlockSpec(arr.shape, lambda b, i: (0,) * arr.ndim)
    small = lambda arr: pl.BlockSpec((1, tm, arr.shape[1]), lambda b, i: (b, i, 0))
    cast_specs = []
    for cw in casts:
        rows = cw.shape[0] // steps
        assert rows * steps == cw.shape[0] and rows % 16 == 0, (cw.shape, steps)
        cast_specs.append(pl.BlockSpec((rows, cw.shape[1]), lambda b, i: (b * ni + i, 0)))
    outs = pl.pallas_call(
        functools.partial(_prep_latent_kernel, ncast=len(casts)),
        grid=(Bn, ni),
        in_specs=[tok,
                  pl.BlockSpec((1, GRID_W, D),
                               lambda b, i: (b, jnp.maximum(i * rows_per_tile - 1, 0), 0)),
                  pl.BlockSpec((1, GRID_W, D),
                               lambda b, i: (b, jnp.minimum((i + 1) * rows_per_tile, nrow - 1), 0)),
                  pl.BlockSpec((1, D), lambda b, i: (0, 0)),
                  pl.BlockSpec((1, 1, D), _bsel(shift)),
                  pl.BlockSpec((1, 1, D), _bsel(scale)),
                  pl.BlockSpec((6, D), lambda b, i: (0, 0)),
                  full(w1cat), full(a1cat), full(lg1)] + cast_specs,
        out_specs=[tok, tok, tok, small(w1cat), small(a1cat), small(lg1)] + cast_specs,
        out_shape=[jax.ShapeDtypeStruct((Bn, T, D), BF16)] * 3
        + [jax.ShapeDtypeStruct((Bn, T, w.shape[1]), BF16) for w in (w1cat, a1cat, lg1)]
        + [jax.ShapeDtypeStruct(cw.shape, BF16) for cw in casts],
        scratch_shapes=[pltpu.VMEM((tm + 2 * GRID_W, D), F32), pltpu.VMEM((3, tm, D), BF16)],
        compiler_params=_params("arbitrary", "arbitrary"),
        name="prep_latent",
    )(x, x, x, g[None], shift, scale, mix, w1cat, a1cat, lg1, *casts)
    return tuple(outs[:6]), tuple(outs[6:])


def _prep_ctx_kernel(x_ref, g_ref, sh_ref, sc_ref, mix_ref, *out_refs):
    h = _norm_mod(x_ref[0], g_ref[...], sh_ref[0], sc_ref[0])
    L, D = h.shape
    half = D // 2
    t = lax.broadcasted_iota(jnp.int32, (L, 1), 0)
    h0, h1 = h[:, :half], h[:, half:]
    prev = jnp.where(t == 0, 0.0, pltpu.roll(h0, 1, 0))
    nxt = jnp.where(t == L - 1, 0.0, pltpu.roll(h1, L - 1, 0))
    _write_mix(out_refs, mix_ref, h0, prev, 0, half)
    _write_mix(out_refs, mix_ref, h1, nxt, half, D)


def _prep_ctx(x, g, shift, scale, mix):
    Bn, L, D = x.shape
    return pl.pallas_call(
        _prep_ctx_kernel,
        grid=(Bn,),
        in_specs=[pl.BlockSpec((1, L, D), lambda b: (b, 0, 0)),
                  pl.BlockSpec((1, D), lambda b: (0, 0)),
                  pl.BlockSpec((1, 1, D), _bsel(shift)),
                  pl.BlockSpec((1, 1, D), _bsel(scale)),
                  pl.BlockSpec((6, D), lambda b: (0, 0))],
        out_specs=[pl.BlockSpec((1, L, D), lambda b: (b, 0, 0))] * 6,
        out_shape=[jax.ShapeDtypeStruct((Bn, L, D), BF16)] * 6,
        compiler_params=_params("parallel"),
        name="prep_ctx",
    )(x, g[None], shift, scale, mix)


def _mm_kernel(a_ref, w_ref, o_ref, *, act):
    acc = _dot(a_ref[0], w_ref[...])
    if act == "tanh":
        acc = jnp.tanh(acc)
    elif act == "sigmoid":
        acc = _sigmoid(acc)
    o_ref[0] = acc.astype(o_ref.dtype)


def _mm(a, w, out_dtype, act=None, name="mm"):
    Bn, T, K = a.shape
    N = w.shape[1]
    M = Bn * T
    tm = _tile(M, 1024, 16)
    tn = _tile(N, 2048, LANES)
    out = pl.pallas_call(
        functools.partial(_mm_kernel, act=act),
        grid=(1, M // tm, N // tn),
        in_specs=[pl.BlockSpec((1, tm, K), lambda b, i, j: (b, i, 0)),
                  pl.BlockSpec((K, tn), lambda b, i, j: (0, j))],
        out_specs=pl.BlockSpec((1, tm, tn), lambda b, i, j: (b, i, j)),
        out_shape=jax.ShapeDtypeStruct((1, M, N), out_dtype),
        compiler_params=_params("parallel", "parallel", "parallel"),
        name=name,
    )(a.reshape(1, M, K), w)
    return out.reshape(Bn, T, N)


def _mm_res_kernel(a_ref, w_ref, res_ref, gate_ref, *rest):
    ncast = (len(rest) - 1) // 2
    cast_in, o_ref, cast_out = rest[:ncast], rest[ncast], rest[ncast + 1:]
    o_ref[0] = res_ref[0] + gate_ref[0] * _dot(a_ref[0], w_ref[...])
    for src, dst in zip(cast_in, cast_out):
        dst[...] = src[...].astype(BF16)


def _mm_res(a, w, res, gate, name="mm_res", casts=()):
    Bn, T, K = a.shape
    N = w.shape[1]
    tm = _tile(T, 1024, 16)
    tn = _tile(N, 1024 if K <= 2048 else 512, LANES)
    ni, nj = T // tm, N // tn
    steps = Bn * ni * nj
    gsel = _bsel(gate)
    step = lambda b, i, j: (b * ni + i) * nj + j
    cast_specs = []
    for cw in casts:
        rows = cw.shape[0] // steps
        assert rows * steps == cw.shape[0] and rows % 16 == 0, (cw.shape, steps)
        cast_specs.append(pl.BlockSpec((rows, cw.shape[1]), lambda b, i, j: (step(b, i, j), 0)))
    tile = pl.BlockSpec((1, tm, tn), lambda b, i, j: (b, i, j))
    outs = pl.pallas_call(
        _mm_res_kernel,
        grid=(Bn, ni, nj),
        in_specs=[pl.BlockSpec((1, tm, K), lambda b, i, j: (b, i, 0)),
                  pl.BlockSpec((K, tn), lambda b, i, j: (0, j)),
                  tile,
                  pl.BlockSpec((1, 1, tn), lambda b, i, j: gsel(b)[:2] + (j,))] + cast_specs,
        out_specs=[tile] + cast_specs,
        out_shape=[jax.ShapeDtypeStruct((Bn, T, N), F32)]
        + [jax.ShapeDtypeStruct(cw.shape, BF16) for cw in casts],
        compiler_params=_params("arbitrary", "arbitrary", "arbitrary"),
        name=name,
    )(a, w, res, gate, *casts)
    return outs[0], tuple(outs[1:])


def _mm_res_norm_kernel(a_ref, w_ref, res_ref, gate_ref, g_ref, sh_ref, sc_ref, o_ref, h_ref, *, rows):
    for r0 in range(0, a_ref.shape[1], rows):
        rs = slice(r0, r0 + rows)
        x1 = res_ref[0, rs, :] + gate_ref[0] * _dot(a_ref[0, rs, :], w_ref[...])
        o_ref[0, rs, :] = x1
        h_ref[0, rs, :] = _norm_mod(x1, g_ref[...], sh_ref[0], sc_ref[0]).astype(h_ref.dtype)


def _mm_res_norm(a, w, res, gate, g, shift, scale, name):
    Bn, T, K = a.shape
    N = w.shape[1]
    tm = _tile(T, 512, 16)
    rows = _tile(tm, 256, 16)
    row = pl.BlockSpec((1, tm, N), lambda b, i: (b, i, 0))
    return pl.pallas_call(
        functools.partial(_mm_res_norm_kernel, rows=rows),
        grid=(Bn, T // tm),
        in_specs=[pl.BlockSpec((1, tm, K), lambda b, i: (b, i, 0)),
                  pl.BlockSpec((K, N), lambda b, i: (0, 0)),
                  row,
                  pl.BlockSpec((1, 1, N), _bsel(gate)),
                  pl.BlockSpec((1, N), lambda b, i: (0, 0)),
                  pl.BlockSpec((1, 1, N), _bsel(shift)),
                  pl.BlockSpec((1, 1, N), _bsel(scale))],
        out_specs=[row, row],
        out_shape=[jax.ShapeDtypeStruct((Bn, T, N), F32), jax.ShapeDtypeStruct((Bn, T, N), BF16)],
        compiler_params=_params("parallel", "parallel"),
        name=name,
    )(a, w, res, gate, g[None], shift, scale)


def _ffn_up_kernel(h_ref, wa_ref, wb_ref, w2_ref, o_ref, w2o_ref, w_scr, *, rows):
    @pl.when((pl.program_id(1) == 0) & (pl.program_id(2) == 0))
    def _():
        w_scr[0] = wa_ref[0].astype(BF16)
        w_scr[1] = wb_ref[0].astype(BF16)

    for r0 in range(0, h_ref.shape[1], rows):
        rs = slice(r0, r0 + rows)
        h = h_ref[0, rs, :]
        a = _dot(h, w_scr[0])
        o_ref[0, rs, :] = (a * _sigmoid(a) * _dot(h, w_scr[1])).astype(o_ref.dtype)
    w2o_ref[...] = w2_ref[0].astype(BF16)


def _ffn_up(h, w13, w2, layer, name):
    Bn, T, D = h.shape
    F = w13.shape[2] // 2
    tm = _tile(T, 2048, 16)
    rows = _tile(tm, 512, 16)
    tn = _tile(F, 512, LANES)
    nj, ni = F // tn, T // tm
    steps = nj * Bn * ni
    F2, D2 = w2.shape[1:]
    rows2 = F2 // steps
    assert rows2 * steps == F2 and rows2 % 16 == 0, (F2, steps)
    step = lambda j, b, i: (j * Bn + b) * ni + i
    return pl.pallas_call(
        functools.partial(_ffn_up_kernel, rows=rows),
        grid=(nj, Bn, ni),
        in_specs=[pl.BlockSpec((1, tm, D), lambda j, b, i: (b, i, 0)),
                  pl.BlockSpec((1, D, tn), lambda j, b, i: (layer, 0, j)),
                  pl.BlockSpec((1, D, tn), lambda j, b, i: (layer, 0, j + nj)),
                  pl.BlockSpec((1, rows2, D2), lambda j, b, i: (layer, step(j, b, i), 0))],
        out_specs=[pl.BlockSpec((1, tm, tn), lambda j, b, i: (b, i, j)),
                   pl.BlockSpec((rows2, D2), lambda j, b, i: (step(j, b, i), 0))],
        out_shape=[jax.ShapeDtypeStruct((Bn, T, F), BF16), jax.ShapeDtypeStruct((F2, D2), BF16)],
        scratch_shapes=[pltpu.VMEM((2, D, tn), BF16)],
        compiler_params=_params("arbitrary", "arbitrary", "arbitrary"),
        name=name,
    )(h, w13, w13, w2)


def _conv_in_combine(gb, gc, u):
    return gb, gc * u


def _norm_mm_kernel(x_ref, g_ref, sh_ref, sc_ref, *rest, nw, combine, rows):
    w_refs, out_refs, h_scr = rest[:nw], rest[nw:-1], rest[-1]

    def emit(rs, h):
        outs = combine(*[_dot(h, w_ref[...]) for w_ref in w_refs])
        for o_ref, o in zip(out_refs, outs):
            o_ref[0, rs, :] = o.astype(o_ref.dtype)

    @pl.when(pl.program_id(2) == 0)
    def _():
        for r0 in range(0, x_ref.shape[1], rows):
            rs = slice(r0, r0 + rows)
            h = _norm_mod(x_ref[0, rs, :], g_ref[...], sh_ref[0], sc_ref[0]).astype(BF16)
            h_scr[rs, :] = h
            emit(rs, h)

    @pl.when(pl.program_id(2) > 0)
    def _():
        emit(slice(None), h_scr[...])


def _norm_mm(x, g, shift, scale, w, nw, combine, out_dtypes, name):
    Bn, T, D = x.shape
    N = w.shape[1] // nw
    tm = _tile(T, 1024, 16)
    rows = _tile(tm, 256, 16)
    tn = _tile(N, 512, LANES)
    nj = N // tn
    w_specs = [pl.BlockSpec((D, tn), functools.partial(lambda b, i, j, m: (0, j + m * nj), m=m))
               for m in range(nw)]
    return pl.pallas_call(
        functools.partial(_norm_mm_kernel, nw=nw, combine=combine, rows=rows),
        grid=(Bn, T // tm, nj),
        in_specs=[pl.BlockSpec((1, tm, D), lambda b, i, j: (b, i, 0)),
                  pl.BlockSpec((1, D), lambda b, i, j: (0, 0)),
                  pl.BlockSpec((1, 1, D), _bsel(shift)),
                  pl.BlockSpec((1, 1, D), _bsel(scale))] + w_specs,
        out_specs=[pl.BlockSpec((1, tm, tn), lambda b, i, j: (b, i, j))] * len(out_dtypes),
        out_shape=[jax.ShapeDtypeStruct((Bn, T, N), dt) for dt in out_dtypes],
        scratch_shapes=[pltpu.VMEM((tm, D), BF16)],
        compiler_params=_params("parallel", "parallel", "arbitrary"),
        name=name,
    )(x, g[None], shift, scale, *([w] * nw))


def _seg_ones(width):
    shift = HEAD.bit_length() - 1
    r = lax.shift_right_logical(lax.broadcasted_iota(jnp.int32, (width, width), 0), shift)
    c = lax.shift_right_logical(lax.broadcasted_iota(jnp.int32, (width, width), 1), shift)
    return (r == c).astype(F32)


def _split3(x):
    hi = x.astype(BF16)
    r1 = x - hi.astype(F32)
    mid = r1.astype(BF16)
    lo = (r1 - mid.astype(F32)).astype(BF16)
    return hi, mid, lo


def _wkv_kernel(r_ref, k_ref, v_ref, tw_ref, aw_ref, w2_ref, a2_ref, w0_ref, a0_ref, kk_ref, ka_ref,
                s0_ref, y_ref, sout_ref, s_scr, x_scr, r2_scr, bv_scr, vb_scr, z_scr, wt_scr,
                *, reverse, npair):
    c = pl.program_id(2)
    C = CHUNK
    PW = 2 * HEAD

    @pl.when(c == 0)
    def _():
        s_scr[...] = s0_ref[0]
        x_scr[...] = jnp.zeros_like(x_scr)
        r2_scr[...] = jnp.zeros_like(r2_scr)
        bv_scr[...] = jnp.zeros_like(bv_scr)
        vb_scr[...] = jnp.zeros_like(vb_scr)
        z_scr[...] = jnp.zeros_like(z_scr)
        wt_scr[...] = jnp.ones_like(wt_scr)

    rr = lax.shift_right_logical(lax.broadcasted_iota(jnp.int32, (PW, PW), 0), HEAD.bit_length() - 1)
    cc = lax.shift_right_logical(lax.broadcasted_iota(jnp.int32, (PW, PW), 1), HEAD.bit_length() - 1)
    same = rr == cc
    same_bf = same.astype(BF16)

    def bd(x):
        xb = x.astype(BF16)
        return jnp.concatenate([xb, xb], axis=0) * same_bf

    t2 = lax.broadcasted_iota(jnp.int32, (C, PW), 0)
    s2 = lax.broadcasted_iota(jnp.int32, (C, PW), 1) & (HEAD - 1)
    before = (s2 > t2) if reverse else (s2 < t2)
    upto = before | (s2 == t2)
    pairs = range(npair)
    sls = [slice(p * PW, (p + 1) * PW) for p in pairs]

    r, k, v = (t[0].astype(F32) for t in (r_ref, k_ref, v_ref))
    z = w0_ref[...] + _dot(tw_ref[0], w2_ref[...])
    a_pre = a0_ref[...] + _dot(aw_ref[0], a2_ref[...])

    S = [s_scr[p] for p in pairs]
    X = [x_scr[p] for p in pairs]
    G = [_dot_t(X[p], r2_scr[p], 1, 1) for p in pairs]
    XS = [_dot_t(X[p], S[p].astype(BF16), 1, 1) for p in pairs]
    u = [XS[p][:C] + _dot(jnp.where(before, G[p][:C, PW:], 0.0).astype(BF16), bv_scr[p]) for p in pairs]
    P = [jnp.where(before, G[p][:C, :PW], 0.0) for p in pairs]

    lw = -math.exp(-0.5) * _sigmoid(z)
    kkv = k * kk_ref[...]
    kk2 = kkv * kkv
    t_i = lax.broadcasted_iota(jnp.int32, (C, C), 0)
    s_i = lax.broadcasted_iota(jnp.int32, (C, C), 1)
    tri = ((s_i >= t_i) if reverse else (s_i <= t_i)).astype(BF16)
    cum = _dot(jnp.concatenate([tri, tri], axis=1),
               jnp.concatenate(_split3(lw)[:2], axis=0))
    same2 = jnp.concatenate([same_bf, same_bf], axis=0)
    ss = [_dot(jnp.concatenate(_split3(kk2[:, sl])[:2], axis=1), same2) for sl in sls]

    n_sq = C.bit_length() - 1
    for j in range(n_sq):
        Pb = [P[p].astype(BF16) for p in pairs]
        if j < n_sq - 1:
            PU = [_dot(Pb[p], jnp.concatenate([bd(P[p]), bd(u[p])], axis=1)) for p in pairs]
            P = [PU[p][:, :PW] for p in pairs]
            u = [u[p] + PU[p][:, PW:] for p in pairs]
        else:
            u = [u[p] + _dot(Pb[p], bd(u[p])) for p in pairs]
    for p in pairs:
        R = jnp.concatenate([jnp.where(upto, G[p][C:, :PW], 0.0),
                             jnp.where(upto, G[p][C:, PW:], 0.0)], axis=1).astype(BF16)
        y_ref[0, :, sls[p]] = XS[p][C:] + _dot(R, jnp.concatenate([bd(u[p]), bv_scr[p]], axis=0))
    for p in pairs:
        UV = jnp.concatenate([u[p].astype(BF16), vb_scr[p]], axis=0)
        dS = _dot_t(UV, z_scr[p], 0, 0)
        s_scr[p] = (S[p] + jnp.where(same, dS, 0.0)) * wt_scr[p, 0:1, :]

    a_sig = _sigmoid(a_pre)
    kd = k * (1.0 + (a_sig - 1.0) * ka_ref[...])
    e_pos = jnp.exp(cum)
    e_neg = jnp.exp(-cum)
    e_prev = jnp.exp(cum - lw)
    last = 0 if reverse else C - 1
    for p, sl in zip(pairs, sls):
        kkn = kkv[:, sl] * lax.rsqrt(jnp.maximum(ss[p], 1e-24))
        at = (-kkn) * e_prev[:, sl]
        bt = (kkn * a_sig[:, sl]) * e_neg[:, sl]
        rt = r[:, sl] * e_pos[:, sl]
        kt = kd[:, sl] * e_neg[:, sl]
        x_scr[p] = jnp.concatenate([at, rt], axis=0).astype(BF16)
        r2_scr[p] = jnp.concatenate([bd(bt), bd(kt)], axis=0)
        bv_scr[p] = bd(v[:, sl])
        vb_scr[p] = v[:, sl].astype(BF16)
        z_scr[p] = jnp.concatenate([bt, kt], axis=0).astype(BF16)
        wt_scr[p] = jnp.broadcast_to(e_pos[last:last + 1, sl], wt_scr.shape[1:])

    @pl.when(c == pl.num_programs(2) - 1)
    def _():
        sout_ref[0] = s_scr[...]


def _wkv(r, k, v, tw, aw, w2p, a2p, w0, a0, kk, ka, s0, d):
    Bn, T, D = r.shape
    PW = 2 * HEAD
    npairs = D // PW
    npair = _tile(npairs, 16, 1)
    hw = npair * PW
    nc = T // CHUNK
    reverse = d == 1
    pos = (lambda j: nc - 1 - j) if reverse else (lambda j: j)
    cin = lambda c: pos(jnp.minimum(c, nc - 1))
    cout = lambda c: pos(jnp.maximum(c - 1, 0))
    tok = pl.BlockSpec((1, CHUNK, hw), lambda b, g, c: (b, cin(c), g))
    lora = pl.BlockSpec((1, CHUNK, LANES), lambda b, g, c: (b, cin(c), d))
    lw2 = pl.BlockSpec((LANES, hw), lambda b, g, c: (0, g))
    vec = pl.BlockSpec((1, hw), lambda b, g, c: (0, g))
    st = pl.BlockSpec((1, npair, PW, PW), lambda b, g, c: (b, g, 0, 0))
    return pl.pallas_call(
        functools.partial(_wkv_kernel, reverse=reverse, npair=npair),
        grid=(Bn, npairs // npair, nc + 1),
        in_specs=[tok, tok, tok, lora, lora, lw2, lw2, vec, vec, vec, vec, st],
        out_specs=[pl.BlockSpec((1, CHUNK, hw), lambda b, g, c: (b, cout(c), g)), st],
        out_shape=[jax.ShapeDtypeStruct((Bn, T, D), F32),
                   jax.ShapeDtypeStruct((Bn, npairs, PW, PW), F32)],
        scratch_shapes=[pltpu.VMEM((npair, PW, PW), F32),
                        pltpu.VMEM((npair, 2 * CHUNK, PW), BF16),
                        pltpu.VMEM((npair, 2 * PW, PW), BF16),
                        pltpu.VMEM((npair, PW, PW), BF16),
                        pltpu.VMEM((npair, CHUNK, PW), BF16),
                        pltpu.VMEM((npair, 2 * CHUNK, PW), BF16),
                        pltpu.VMEM((npair, 8, PW), F32)],
        compiler_params=_params("parallel", "parallel", "arbitrary"),
        name="wkv_rev" if reverse else "wkv_fwd",
    )(r, k, v, tw, aw, w2p, a2p, w0, a0, kk, ka, s0)


def _rwkv_out_kernel(yf_ref, yb_ref, r_ref, k_ref, v_ref, gs_ref, aw_ref, res_ref, a2f_ref, a2b_ref,
                     g2_ref, wo_ref, a0_ref, ka_ref, rk_ref, lnw_ref, lnb_ref, gate_ref, g_ref, sh_ref,
                     sc_ref, o_ref, h_ref, og_scr, *, rows):
    tm, D = o_ref.shape[1:]
    PW = 2 * HEAD
    same = _seg_ones(PW).astype(BF16)
    same2 = jnp.concatenate([same, same], axis=0)

    def head_sum(x, pieces):
        if pieces == 1:
            return _dot(x.astype(BF16), same)
        return _dot(jnp.concatenate(_split3(x)[:2], axis=1), same2)

    for r0 in range(0, tm, rows):
        rs = slice(r0, r0 + rows)
        aw = aw_ref[0, rs, :]
        a_f = _sigmoid(a0_ref[0:1, :] + _dot(aw[:, :LANES], a2f_ref[...]))
        a_b = _sigmoid(a0_ref[1:2, :] + _dot(aw[:, LANES:], a2b_ref[...]))
        g = _dot(gs_ref[0, rs, :], g2_ref[...])
        for p in range(D // PW):
            sl = slice(p * PW, (p + 1) * PW)
            ksum = k_ref[0, rs, sl].astype(F32) * (2.0 + (a_f[:, sl] + a_b[:, sl] - 2.0) * ka_ref[:, sl])
            y = yf_ref[0, rs, sl] + yb_ref[0, rs, sl]
            yc = y - head_sum(y, 2) * (1.0 / HEAD)
            var = head_sum(yc * yc, 1) * (1.0 / HEAD)
            o = yc * lax.rsqrt(var + GN_EPS) * lnw_ref[:, sl] + lnb_ref[:, sl]
            bonus = (head_sum(r_ref[0, rs, sl].astype(F32) * ksum * rk_ref[:, sl], 1)
                     * v_ref[0, rs, sl].astype(F32))
            og_scr[rs, sl] = ((o + bonus) * g[:, sl]).astype(og_scr.dtype)
        x1 = res_ref[0, rs, :] + gate_ref[0] * _dot(og_scr[rs, :], wo_ref[...])
        o_ref[0, rs, :] = x1
        h_ref[0, rs, :] = _norm_mod(x1, g_ref[...], sh_ref[0], sc_ref[0]).astype(h_ref.dtype)


def _rwkv_out(yf, yb, r, k, v, gs, aw, res, a2fp, a2bp, g2, wo, a0, ka, rk, lnw, lnb, gate, g, shift,
              scale, name):
    Bn, T, D = res.shape
    tm = _tile(T, 256, 16)
    rows = _tile(tm, 128, 16)
    G = gs.shape[-1]
    row = pl.BlockSpec((1, tm, D), lambda b, i: (b, i, 0))
    vec = pl.BlockSpec((1, D), lambda b, i: (0, 0))
    full = lambda arr: pl.BlockSpec(arr.shape, lambda b, i: (0,) * arr.ndim)
    return pl.pallas_call(
        functools.partial(_rwkv_out_kernel, rows=rows),
        grid=(Bn, T // tm),
        in_specs=[row, row, row, row, row,
                  pl.BlockSpec((1, tm, G), lambda b, i: (b, i, 0)),
                  pl.BlockSpec((1, tm, 2 * LANES), lambda b, i: (b, i, 0)),
                  row, full(a2fp), full(a2bp), full(g2), full(wo), full(a0),
                  vec, vec, vec, vec,
                  pl.BlockSpec((1, 1, D), _bsel(gate)), vec,
                  pl.BlockSpec((1, 1, D), _bsel(shift)),
                  pl.BlockSpec((1, 1, D), _bsel(scale))],
        out_specs=[row, row],
        out_shape=[jax.ShapeDtypeStruct((Bn, T, D), F32), jax.ShapeDtypeStruct((Bn, T, D), BF16)],
        scratch_shapes=[pltpu.VMEM((tm, D), BF16)],
        compiler_params=_params("parallel", "parallel"),
        name=name,
    )(yf, yb, r, k, v, gs, aw, res, a2fp, a2bp, g2, wo, a0, ka, rk, lnw, lnb, gate, g[None], shift, scale)


def _conv_kernel(gb_ref, z_ref, cw_ref, o_ref):
    z = z_ref[0].astype(F32)
    T = z.shape[0]
    t = lax.broadcasted_iota(jnp.int32, (T, 1), 0)
    zp = jnp.where(t == 0, 0.0, pltpu.roll(z, 1, 0))
    zn = jnp.where(t == T - 1, 0.0, pltpu.roll(z, T - 1, 0))
    conv = zp * cw_ref[0:1, :] + z * cw_ref[1:2, :] + zn * cw_ref[2:3, :]
    o_ref[0] = (gb_ref[0] * conv).astype(o_ref.dtype)


def _conv(gb, z, cw):
    Bn, T, D = z.shape
    tn = _tile(D, 512, LANES)
    tok = pl.BlockSpec((1, T, tn), lambda b, j: (b, 0, j))
    return pl.pallas_call(
        _conv_kernel,
        grid=(Bn, D // tn),
        in_specs=[tok, tok, pl.BlockSpec((3, tn), lambda b, j: (0, j))],
        out_specs=tok,
        out_shape=jax.ShapeDtypeStruct((Bn, T, D), BF16),
        compiler_params=_params("parallel", "parallel"),
        name="short_conv",
    )(gb, z, cw)


def _pad_rows(w, rows):
    return jnp.pad(w, ((0, rows - w.shape[0]), (0, 0)))


def _pad_cols(w, cols):
    return jnp.pad(w, ((0, 0), (0, cols - w.shape[1])))


def _split_mod(mod_rows, D):
    return [mod_rows[:, m * D:(m + 1) * D][:, None, :] for m in range(6)]


def _ffn_branch(t1, h2, mods, ffn, tag):
    w13, w2, layer, casts = ffn
    act, wdn = _ffn_up(h2, w13, w2, layer, name="ffn_up_" + tag)
    return _mm_res(act, wdn, t1, mods[5], name="ffn_down_" + tag, casts=casts)


def _rwkv_layer(x, ctx, mods_x, mods_c, g1, g2n, mix, wr, wk, wv, wo, w0, w1, w2, a0, a1, a2,
                lg1, lg2, k_k, k_a, r_k, ln_w, ln_b, ffn):
    D = x.shape[-1]
    H = D // HEAD
    w1cat = jnp.concatenate([_pad_cols(w1[0], LANES), _pad_cols(w1[1], LANES)], axis=1).astype(BF16)
    a1cat = jnp.concatenate([_pad_cols(a1[0], LANES), _pad_cols(a1[1], LANES)], axis=1).astype(BF16)
    w2p = [_pad_rows(w2[d], LANES).astype(BF16) for d in range(2)]
    a2p = [_pad_rows(a2[d], LANES).astype(BF16) for d in range(2)]
    lg1, lg2 = lg1.astype(BF16), lg2.astype(BF16)
    rk = r_k.reshape(1, D)

    xr, xw, xk, xv, xa, xg = _prep_ctx(ctx, g1, mods_c[0], mods_c[1], mix)
    lora_c = dict(tw=_mm(xw, w1cat, BF16, act="tanh", name="lora_w_c"),
                  aw=_mm(xa, a1cat, BF16, name="lora_a_c"),
                  gs=_mm(xg, lg1, BF16, act="sigmoid", name="lora_g_c"))
    ins_x, (wr, wk, wv, wo) = _prep_latent(x, g1, mods_x[0], mods_x[1], mix, w1cat, a1cat, lg1,
                                           casts=(wr, wk, wv, wo))
    ins = {"c": (xr, xk, xv, lora_c["tw"], lora_c["aw"], lora_c["gs"]), "x": ins_x}
    sets = {}
    for tag, (xr, xk, xv, tw, aw, gs) in ins.items():
        sets[tag] = dict(r=_mm(xr, wr, BF16, name="proj_r_" + tag),
                         k=_mm(xk, wk, BF16, name="proj_k_" + tag),
                         v=_mm(xv, wv, BF16, name="proj_v_" + tag), tw=tw, aw=aw, gs=gs)

    ys = {"c": [], "x": []}
    zero_state = jnp.zeros((x.shape[0], H // 2, 2 * HEAD, 2 * HEAD), F32)
    for d in range(2):
        state = zero_state
        for tag in ("c", "x"):
            s = sets[tag]
            y, state = _wkv(s["r"], s["k"], s["v"], s["tw"], s["aw"], w2p[d], a2p[d],
                            w0[d][None], a0[d][None], k_k[None], k_a[None], state, d)
            ys[tag].append(y)

    outs = []
    for tag, tok, mods in (("c", ctx, mods_c), ("x", x, mods_x)):
        s = sets[tag]
        t1, h2 = _rwkv_out(ys[tag][0], ys[tag][1], s["r"], s["k"], s["v"], s["gs"], s["aw"], tok,
                           a2p[0], a2p[1], lg2, wo, a0, k_a[None], rk, ln_w[None], ln_b[None],
                           mods[2], g2n, mods[3], mods[4], name="rwkv_out_" + tag)
        outs.append(_ffn_branch(t1, h2, mods, ffn, tag))
    (ctx_out, _), (x_out, cast_out) = outs
    return x_out, ctx_out, cast_out


def _conv_layer(x, mods, g1, g2n, w_in, conv_w, w_out, ffn):
    gb, z = _norm_mm(x, g1, mods[0], mods[1], w_in.astype(BF16), 3, _conv_in_combine, (BF16, BF16),
                     name="conv_in")
    p = _conv(gb, z, conv_w)
    t1, h2 = _mm_res_norm(p, w_out.astype(BF16), x, mods[2], g2n, mods[3], mods[4], name="conv_out")
    return _ffn_branch(t1, h2, mods, ffn, "x")


def kernel(x, c, ctx, c_ctx, norm1_g, norm2_g, ada_w, ada_b, rw_mix, rw_wr, rw_wk, rw_wv, rw_wo,
           rw_w0, rw_w1, rw_w2, rw_a0, rw_a1, rw_a2, rw_g1, rw_g2, rw_kk, rw_ka, rw_rk, rw_lnw,
           rw_lnb, sc_win, sc_conv, sc_wout, ffn_w13, ffn_w2, final_g):
    B, T, D = x.shape
    depth = norm1_g.shape[0]
    rows = -(-(B + 1) // 8) * 8
    cond = jnp.zeros((rows, D), F32).at[:B].set(c).at[B].set(c_ctx)
    conv_w = {}
    for i in range(depth):
        last = i == depth - 1
        j = i // 2
        mod = _ada(cond, ada_w, ada_b, i)
        mods_x = _split_mod(mod[:B], D)
        mods_c = _split_mod(mod[B:B + 1], D)
        nxt = (i + 1) // 2
        casts = (sc_win[nxt], sc_wout[nxt]) if (not last and i % 2 == 0) else ()
        ffn = (ffn_w13, ffn_w2, i, casts)
        if i % 2 == 0:
            x, ctx, cast_out = _rwkv_layer(
                x, ctx, mods_x, mods_c, norm1_g[i], norm2_g[i], rw_mix[j], rw_wr[j], rw_wk[j],
                rw_wv[j], rw_wo[j], rw_w0[j], rw_w1[j], rw_w2[j], rw_a0[j], rw_a1[j], rw_a2[j],
                rw_g1[j], rw_g2[j], rw_kk[j], rw_ka[j], rw_rk[j], rw_lnw[j], rw_lnb[j], ffn)
            if casts:
                conv_w[nxt] = cast_out
        else:
            w_in, w_out = conv_w.get(j, (sc_win[j], sc_wout[j]))
            if not last:
                ctx, _ = _conv_layer(ctx, mods_c, norm1_g[i], norm2_g[i], w_in, sc_conv[j], w_out,
                                     (ffn_w13, ffn_w2, i, ()))
            x, _ = _conv_layer(x, mods_x, norm1_g[i], norm2_g[i], w_in, sc_conv[j], w_out, ffn)
    zeros = jnp.zeros((1, 1, D), F32)
    return _norm(x, final_g, zeros, zeros)
```

```python
import functools
import math

import jax
import jax.numpy as jnp
from jax import lax
from jax.experimental import pallas as pl
from jax.experimental.pallas import tpu as pltpu

HEAD = 64
GRID_W = 64
CHUNK = 64
NORM_EPS = 1e-6
GN_EPS = 64e-5
LANES = 128
SUBLANES = 8
ROWS_BF16 = 16
VMEM_LIMIT = 56 * 1024 * 1024

F32 = jnp.float32
BF16 = jnp.bfloat16


def _params(*sem):
    return pltpu.CompilerParams(dimension_semantics=sem, vmem_limit_bytes=VMEM_LIMIT)


def _tile(n, pref, mult):
    t = min(pref, n)
    t -= t % mult
    while t >= mult:
        if n % t == 0:
            return t
        t -= mult
    return n


def _sigmoid(x):
    return 1.0 / (1.0 + jnp.exp(-x))


def _norm_mod(x, g, shift, scale, cols=None):
    rs = lax.rsqrt(jnp.mean(x * x, axis=-1, keepdims=True) + NORM_EPS)
    if cols is not None:
        x, g, shift, scale = (t[:, cols[0]:cols[1]] for t in (x, g, shift, scale))
    return (x * rs) * (g * (1.0 + scale)) + shift


def _dot(a, b):
    return jnp.dot(a, b, preferred_element_type=F32)


def _dot_t(a, b, ca, cb):
    return lax.dot_general(a, b, (((ca,), (cb,)), ((), ())), preferred_element_type=F32)


def _ada_kernel(c_ref, w_ref, b_ref, o_ref):
    c = c_ref[...]
    s = c * _sigmoid(c)
    o_ref[...] = _dot(s.astype(BF16), w_ref[0].astype(BF16)) + b_ref[0]


def _ada(cond, w, b, layer):
    R, D = cond.shape
    N = w.shape[2]
    tn = _tile(N, 1024, LANES)
    return pl.pallas_call(
        _ada_kernel,
        grid=(N // tn,),
        in_specs=[pl.BlockSpec((R, D), lambda j: (0, 0)),
                  pl.BlockSpec((1, D, tn), lambda j: (layer, 0, j)),
                  pl.BlockSpec((1, 1, tn), lambda j: (layer, 0, j))],
        out_specs=pl.BlockSpec((R, tn), lambda j: (0, j)),
        out_shape=jax.ShapeDtypeStruct((R, N), F32),
        compiler_params=_params("parallel"),
        name="ada_mod",
    )(cond, w, b[:, None, :])


def _norm_kernel(x_ref, g_ref, sh_ref, sc_ref, o_ref):
    o_ref[0] = _norm_mod(x_ref[0], g_ref[...], sh_ref[0], sc_ref[0])


def _bsel(arr):
    if arr.shape[0] == 1:
        return lambda b, *_: (0, 0, 0)
    return lambda b, *_: (b, 0, 0)


def _norm(x, g, shift, scale):
    Bn, T, D = x.shape
    tm = _tile(T, 512, SUBLANES)
    return pl.pallas_call(
        _norm_kernel,
        grid=(Bn, T // tm),
        in_specs=[pl.BlockSpec((1, tm, D), lambda b, i: (b, i, 0)),
                  pl.BlockSpec((1, D), lambda b, i: (0, 0)),
                  pl.BlockSpec((1, 1, D), _bsel(shift)),
                  pl.BlockSpec((1, 1, D), _bsel(scale))],
        out_specs=pl.BlockSpec((1, tm, D), lambda b, i: (b, i, 0)),
        out_shape=jax.ShapeDtypeStruct((Bn, T, D), F32),
        compiler_params=_params("parallel", "parallel"),
        name="norm",
    )(x, g[None], shift, scale)


def _write_mix(out_refs, mix_ref, h, shifted, c0, c1):
    xx = shifted - h
    for m, o_ref in enumerate(out_refs):
        o_ref[0, :, c0:c1] = (h + xx * mix_ref[m:m + 1, c0:c1]).astype(o_ref.dtype)


def _prep_latent_kernel(x_ref, xu_ref, xd_ref, g_ref, sh_ref, sc_ref, mix_ref, w1_ref, a1_ref, g1_ref,
                        xr_ref, xk_ref, xv_ref, tw_ref, aw_ref, gs_ref, h_scr, xm_scr):
    i = pl.program_id(1)
    n = pl.num_programs(1)
    g, sh, sc = g_ref[...], sh_ref[0], sc_ref[0]
    tm, D = x_ref.shape[1:]
    q = D // 4
    W = GRID_W
    h_scr[0:W, 2 * q:3 * q] = jnp.where(i > 0, _norm_mod(xu_ref[0], g, sh, sc, (2 * q, 3 * q)), 0.0)
    h_scr[W - SUBLANES:W, 0:q] = jnp.zeros((SUBLANES, q), F32)
    for r0 in range(0, tm, W):
        h_scr[W + r0:2 * W + r0, :] = _norm_mod(x_ref[0, r0:r0 + W, :], g, sh, sc)
    h_scr[W + tm:, 3 * q:] = jnp.where(i < n - 1, _norm_mod(xd_ref[0], g, sh, sc, (3 * q, D)), 0.0)
    h_scr[W + tm:W + tm + SUBLANES, q:2 * q] = jnp.zeros((SUBLANES, q), F32)
    wide = {0: xr_ref, 2: xk_ref, 3: xv_ref}
    lora = {1: 0, 4: 1, 5: 2}
    lora_w = (w1_ref, a1_ref, g1_ref)
    acc = [None] * 3
    R = 32
    row = lax.broadcasted_iota(jnp.int32, (R, 1), 0)
    for k, off in enumerate((-1, 1, -W, W)):
        c0, c1 = k * q, (k + 1) * q
        mixk = [mix_ref[m:m + 1, c0:c1] for m in range(6)]
        for r0 in range(0, tm, R):
            h = h_scr[W + r0:W + r0 + R, c0:c1]
            s = h_scr[W + r0 + off:W + r0 + off + R, c0:c1]
            if off == -1 and r0 % W == 0:
                s = jnp.where(row == 0, 0.0, s)
            if off == 1 and (r0 + R) % W == 0:
                s = jnp.where(row == R - 1, 0.0, s)
            xx = s - h
            for m in range(6):
                xm = (h + xx * mixk[m]).astype(BF16)
                if m in wide:
                    wide[m][0, r0:r0 + R, c0:c1] = xm
                else:
                    xm_scr[lora[m], r0:r0 + R, c0:c1] = xm
        for l in range(3):
            part = _dot(xm_scr[l, :, c0:c1], lora_w[l][c0:c1, :])
            acc[l] = part if acc[l] is None else acc[l] + part
    tw_ref[0] = jnp.tanh(acc[0]).astype(tw_ref.dtype)
    aw_ref[0] = acc[1].astype(aw_ref.dtype)
    gs_ref[0] = _sigmoid(acc[2]).astype(gs_ref.dtype)


def _prep_latent(x, g, shift, scale, mix, w1cat, a1cat, lg1):
    Bn, T, D = x.shape
    rows_per_tile = _tile(T // GRID_W, 4, 1)
    tm = rows_per_tile * GRID_W
    nrow = T // GRID_W
    tok = pl.BlockSpec((1, tm, D), lambda b, i: (b, i, 0))
    full = lambda arr: pl.BlockSpec(arr.shape, lambda b, i: (0,) * arr.ndim)
    small = lambda arr: pl.BlockSpec((1, tm, arr.shape[1]), lambda b, i: (b, i, 0))
    return pl.pallas_call(
        _prep_latent_kernel,
        grid=(Bn, T // tm),
        in_specs=[tok,
                  pl.BlockSpec((1, GRID_W, D),
                               lambda b, i: (b, jnp.maximum(i * rows_per_tile - 1, 0), 0)),
                  pl.BlockSpec((1, GRID_W, D),
                               lambda b, i: (b, jnp.minimum((i + 1) * rows_per_tile, nrow - 1), 0)),
                  pl.BlockSpec((1, D), lambda b, i: (0, 0)),
                  pl.BlockSpec((1, 1, D), _bsel(shift)),
                  pl.BlockSpec((1, 1, D), _bsel(scale)),
                  pl.BlockSpec((6, D), lambda b, i: (0, 0)),
                  full(w1cat), full(a1cat), full(lg1)],
        out_specs=[tok, tok, tok, small(w1cat), small(a1cat), small(lg1)],
        out_shape=[jax.ShapeDtypeStruct((Bn, T, D), BF16)] * 3
        + [jax.ShapeDtypeStruct((Bn, T, w.shape[1]), BF16) for w in (w1cat, a1cat, lg1)],
        scratch_shapes=[pltpu.VMEM((tm + 2 * GRID_W, D), F32), pltpu.VMEM((3, tm, D), BF16)],
        compiler_params=_params("parallel", "parallel"),
        name="prep_latent",
    )(x, x, x, g[None], shift, scale, mix, w1cat, a1cat, lg1)


def _prep_ctx_kernel(x_ref, g_ref, sh_ref, sc_ref, mix_ref, *out_refs):
    h = _norm_mod(x_ref[0], g_ref[...], sh_ref[0], sc_ref[0])
    L, D = h.shape
    half = D // 2
    t = lax.broadcasted_iota(jnp.int32, (L, 1), 0)
    h0, h1 = h[:, :half], h[:, half:]
    prev = jnp.where(t == 0, 0.0, pltpu.roll(h0, 1, 0))
    nxt = jnp.where(t == L - 1, 0.0, pltpu.roll(h1, L - 1, 0))
    _write_mix(out_refs, mix_ref, h0, prev, 0, half)
    _write_mix(out_refs, mix_ref, h1, nxt, half, D)


def _prep_ctx(x, g, shift, scale, mix):
    Bn, L, D = x.shape
    return pl.pallas_call(
        _prep_ctx_kernel,
        grid=(Bn,),
        in_specs=[pl.BlockSpec((1, L, D), lambda b: (b, 0, 0)),
                  pl.BlockSpec((1, D), lambda b: (0, 0)),
                  pl.BlockSpec((1, 1, D), _bsel(shift)),
                  pl.BlockSpec((1, 1, D), _bsel(scale)),
                  pl.BlockSpec((6, D), lambda b: (0, 0))],
        out_specs=[pl.BlockSpec((1, L, D), lambda b: (b, 0, 0))] * 6,
        out_shape=[jax.ShapeDtypeStruct((Bn, L, D), BF16)] * 6,
        compiler_params=_params("parallel"),
        name="prep_ctx",
    )(x, g[None], shift, scale, mix)


def _mm_kernel(a_ref, w_ref, o_ref, *, act):
    acc = _dot(a_ref[0], w_ref[...])
    if act == "tanh":
        acc = jnp.tanh(acc)
    elif act == "sigmoid":
        acc = _sigmoid(acc)
    o_ref[0] = acc.astype(o_ref.dtype)


def _mm(a, w, out_dtype, act=None, name="mm"):
    Bn, T, K = a.shape
    N = w.shape[1]
    M = Bn * T
    tm = _tile(M, 1024, ROWS_BF16)
    tn = _tile(N, 2048, LANES)
    out = pl.pallas_call(
        functools.partial(_mm_kernel, act=act),
        grid=(1, M // tm, N // tn),
        in_specs=[pl.BlockSpec((1, tm, K), lambda b, i, j: (b, i, 0)),
                  pl.BlockSpec((K, tn), lambda b, i, j: (0, j))],
        out_specs=pl.BlockSpec((1, tm, tn), lambda b, i, j: (b, i, j)),
        out_shape=jax.ShapeDtypeStruct((1, M, N), out_dtype),
        compiler_params=_params("parallel", "parallel", "parallel"),
        name=name,
    )(a.reshape(1, M, K), w)
    return out.reshape(Bn, T, N)


def _mm_res_kernel(a_ref, w_ref, res_ref, gate_ref, *rest):
    ncast = (len(rest) - 1) // 2
    cast_in, o_ref, cast_out = rest[:ncast], rest[ncast], rest[ncast + 1:]
    o_ref[0] = res_ref[0] + gate_ref[0] * _dot(a_ref[0], w_ref[...])
    for src, dst in zip(cast_in, cast_out):
        dst[...] = src[...].astype(BF16)


def _mm_res(a, w, res, gate, name="mm_res", casts=()):
    Bn, T, K = a.shape
    N = w.shape[1]
    tm = _tile(T, 1024, ROWS_BF16)
    tn = _tile(N, 1024 if K <= 2048 else 512, LANES)
    ni, nj = T // tm, N // tn
    steps = Bn * ni * nj
    gsel = _bsel(gate)
    step = lambda b, i, j: (b * ni + i) * nj + j
    cast_specs = []
    for cw in casts:
        rows = cw.shape[0] // steps
        assert rows * steps == cw.shape[0] and rows % ROWS_BF16 == 0, (cw.shape, steps)
        cast_specs.append(pl.BlockSpec((rows, cw.shape[1]), lambda b, i, j: (step(b, i, j), 0)))
    tile = pl.BlockSpec((1, tm, tn), lambda b, i, j: (b, i, j))
    outs = pl.pallas_call(
        _mm_res_kernel,
        grid=(Bn, ni, nj),
        in_specs=[pl.BlockSpec((1, tm, K), lambda b, i, j: (b, i, 0)),
                  pl.BlockSpec((K, tn), lambda b, i, j: (0, j)),
                  tile,
                  pl.BlockSpec((1, 1, tn), lambda b, i, j: gsel(b)[:2] + (j,))] + cast_specs,
        out_specs=[tile] + cast_specs,
        out_shape=[jax.ShapeDtypeStruct((Bn, T, N), F32)]
        + [jax.ShapeDtypeStruct(cw.shape, BF16) for cw in casts],
        compiler_params=_params("arbitrary", "arbitrary", "arbitrary"),
        name=name,
    )(a, w, res, gate, *casts)
    return outs[0], tuple(outs[1:])


def _mm_res_norm_kernel(a_ref, w_ref, res_ref, gate_ref, g_ref, sh_ref, sc_ref, o_ref, h_ref, *, rows):
    for r0 in range(0, a_ref.shape[1], rows):
        rs = slice(r0, r0 + rows)
        x1 = res_ref[0, rs, :] + gate_ref[0] * _dot(a_ref[0, rs, :], w_ref[...])
        o_ref[0, rs, :] = x1
        h_ref[0, rs, :] = _norm_mod(x1, g_ref[...], sh_ref[0], sc_ref[0]).astype(h_ref.dtype)


def _mm_res_norm(a, w, res, gate, g, shift, scale, name):
    Bn, T, K = a.shape
    N = w.shape[1]
    tm = _tile(T, 512, ROWS_BF16)
    rows = _tile(tm, 256, ROWS_BF16)
    row = pl.BlockSpec((1, tm, N), lambda b, i: (b, i, 0))
    return pl.pallas_call(
        functools.partial(_mm_res_norm_kernel, rows=rows),
        grid=(Bn, T // tm),
        in_specs=[pl.BlockSpec((1, tm, K), lambda b, i: (b, i, 0)),
                  pl.BlockSpec((K, N), lambda b, i: (0, 0)),
                  row,
                  pl.BlockSpec((1, 1, N), _bsel(gate)),
                  pl.BlockSpec((1, N), lambda b, i: (0, 0)),
                  pl.BlockSpec((1, 1, N), _bsel(shift)),
                  pl.BlockSpec((1, 1, N), _bsel(scale))],
        out_specs=[row, row],
        out_shape=[jax.ShapeDtypeStruct((Bn, T, N), F32), jax.ShapeDtypeStruct((Bn, T, N), BF16)],
        compiler_params=_params("parallel", "parallel"),
        name=name,
    )(a, w, res, gate, g[None], shift, scale)


def _ffn_up_kernel(h_ref, wa_ref, wb_ref, w2_ref, o_ref, w2o_ref, w_scr, *, rows):
    @pl.when((pl.program_id(1) == 0) & (pl.program_id(2) == 0))
    def _():
        w_scr[0] = wa_ref[0].astype(BF16)
        w_scr[1] = wb_ref[0].astype(BF16)

    for r0 in range(0, h_ref.shape[1], rows):
        rs = slice(r0, r0 + rows)
        h = h_ref[0, rs, :]
        a = _dot(h, w_scr[0])
        o_ref[0, rs, :] = (a * _sigmoid(a) * _dot(h, w_scr[1])).astype(o_ref.dtype)
    w2o_ref[...] = w2_ref[0].astype(BF16)


def _ffn_up(h, w13, w2, layer, name):
    Bn, T, D = h.shape
    F = w13.shape[2] // 2
    tm = _tile(T, 2048, ROWS_BF16)
    rows = _tile(tm, 512, ROWS_BF16)
    tn = _tile(F, 512, LANES)
    nj, ni = F // tn, T // tm
    steps = nj * Bn * ni
    F2, D2 = w2.shape[1:]
    rows2 = F2 // steps
    assert rows2 * steps == F2 and rows2 % ROWS_BF16 == 0, (F2, steps)
    step = lambda j, b, i: (j * Bn + b) * ni + i
    return pl.pallas_call(
        functools.partial(_ffn_up_kernel, rows=rows),
        grid=(nj, Bn, ni),
        in_specs=[pl.BlockSpec((1, tm, D), lambda j, b, i: (b, i, 0)),
                  pl.BlockSpec((1, D, tn), lambda j, b, i: (layer, 0, j)),
                  pl.BlockSpec((1, D, tn), lambda j, b, i: (layer, 0, j + nj)),
                  pl.BlockSpec((1, rows2, D2), lambda j, b, i: (layer, step(j, b, i), 0))],
        out_specs=[pl.BlockSpec((1, tm, tn), lambda j, b, i: (b, i, j)),
                   pl.BlockSpec((rows2, D2), lambda j, b, i: (step(j, b, i), 0))],
        out_shape=[jax.ShapeDtypeStruct((Bn, T, F), BF16), jax.ShapeDtypeStruct((F2, D2), BF16)],
        scratch_shapes=[pltpu.VMEM((2, D, tn), BF16)],
        compiler_params=_params("arbitrary", "arbitrary", "arbitrary"),
        name=name,
    )(h, w13, w13, w2)


def _conv_in_combine(gb, gc, u):
    return gb, gc * u


def _norm_mm_kernel(x_ref, g_ref, sh_ref, sc_ref, *rest, nw, combine, rows):
    w_refs, out_refs, h_scr = rest[:nw], rest[nw:-1], rest[-1]

    def emit(rs, h):
        outs = combine(*[_dot(h, w_ref[...]) for w_ref in w_refs])
        for o_ref, o in zip(out_refs, outs):
            o_ref[0, rs, :] = o.astype(o_ref.dtype)

    @pl.when(pl.program_id(2) == 0)
    def _():
        for r0 in range(0, x_ref.shape[1], rows):
            rs = slice(r0, r0 + rows)
            h = _norm_mod(x_ref[0, rs, :], g_ref[...], sh_ref[0], sc_ref[0]).astype(BF16)
            h_scr[rs, :] = h
            emit(rs, h)

    @pl.when(pl.program_id(2) > 0)
    def _():
        emit(slice(None), h_scr[...])


def _norm_mm(x, g, shift, scale, w, nw, combine, out_dtypes, name):
    Bn, T, D = x.shape
    N = w.shape[1] // nw
    tm = _tile(T, 1024, ROWS_BF16)
    rows = _tile(tm, 256, ROWS_BF16)
    tn = _tile(N, 512, LANES)
    nj = N // tn
    w_specs = [pl.BlockSpec((D, tn), functools.partial(lambda b, i, j, m: (0, j + m * nj), m=m))
               for m in range(nw)]
    return pl.pallas_call(
        functools.partial(_norm_mm_kernel, nw=nw, combine=combine, rows=rows),
        grid=(Bn, T // tm, nj),
        in_specs=[pl.BlockSpec((1, tm, D), lambda b, i, j: (b, i, 0)),
                  pl.BlockSpec((1, D), lambda b, i, j: (0, 0)),
                  pl.BlockSpec((1, 1, D), _bsel(shift)),
                  pl.BlockSpec((1, 1, D), _bsel(scale))] + w_specs,
        out_specs=[pl.BlockSpec((1, tm, tn), lambda b, i, j: (b, i, j))] * len(out_dtypes),
        out_shape=[jax.ShapeDtypeStruct((Bn, T, N), dt) for dt in out_dtypes],
        scratch_shapes=[pltpu.VMEM((tm, D), BF16)],
        compiler_params=_params("parallel", "parallel", "arbitrary"),
        name=name,
    )(x, g[None], shift, scale, *([w] * nw))


def _seg_ones(width):
    shift = HEAD.bit_length() - 1
    r = lax.shift_right_logical(lax.broadcasted_iota(jnp.int32, (width, width), 0), shift)
    c = lax.shift_right_logical(lax.broadcasted_iota(jnp.int32, (width, width), 1), shift)
    return (r == c).astype(F32)


def _split3(x):
    hi = x.astype(BF16)
    r1 = x - hi.astype(F32)
    mid = r1.astype(BF16)
    lo = (r1 - mid.astype(F32)).astype(BF16)
    return hi, mid, lo


def _wkv_kernel(r_ref, k_ref, v_ref, tw_ref, aw_ref, w2_ref, a2_ref, w0_ref, a0_ref, kk_ref, ka_ref,
                s0_ref, y_ref, sout_ref, s_scr, x_scr, r2_scr, bv_scr, vb_scr, z_scr, wt_scr,
                *, reverse, npair):
    c = pl.program_id(2)
    C = CHUNK
    PW = 2 * HEAD

    @pl.when(c == 0)
    def _():
        s_scr[...] = s0_ref[0]
        x_scr[...] = jnp.zeros_like(x_scr)
        r2_scr[...] = jnp.zeros_like(r2_scr)
        bv_scr[...] = jnp.zeros_like(bv_scr)
        vb_scr[...] = jnp.zeros_like(vb_scr)
        z_scr[...] = jnp.zeros_like(z_scr)
        wt_scr[...] = jnp.ones_like(wt_scr)

    rr = lax.shift_right_logical(lax.broadcasted_iota(jnp.int32, (PW, PW), 0), HEAD.bit_length() - 1)
    cc = lax.shift_right_logical(lax.broadcasted_iota(jnp.int32, (PW, PW), 1), HEAD.bit_length() - 1)
    same = rr == cc
    same_bf = same.astype(BF16)

    def bd(x):
        xb = x.astype(BF16)
        return jnp.concatenate([xb, xb], axis=0) * same_bf

    t2 = lax.broadcasted_iota(jnp.int32, (C, PW), 0)
    s2 = lax.broadcasted_iota(jnp.int32, (C, PW), 1) & (HEAD - 1)
    before = (s2 > t2) if reverse else (s2 < t2)
    upto = before | (s2 == t2)
    pairs = range(npair)
    sls = [slice(p * PW, (p + 1) * PW) for p in pairs]

    r, k = r_ref[0].astype(F32), k_ref[0].astype(F32)
    z = w0_ref[...] + _dot(tw_ref[0], w2_ref[...])
    a_pre = a0_ref[...] + _dot(aw_ref[0], a2_ref[...])

    S = [s_scr[p] for p in pairs]
    X = [x_scr[p] for p in pairs]
    G = [_dot_t(X[p], r2_scr[p], 1, 1) for p in pairs]
    XS = [_dot_t(X[p], S[p].astype(BF16), 1, 1) for p in pairs]
    u = [XS[p][:C] + _dot(jnp.where(before, G[p][:C, PW:], 0.0).astype(BF16), bv_scr[p]) for p in pairs]
    P = [jnp.where(before, G[p][:C, :PW], 0.0) for p in pairs]

    lw = -math.exp(-0.5) * _sigmoid(z)
    kkv = k * kk_ref[...]
    kk2 = kkv * kkv
    t_i = lax.broadcasted_iota(jnp.int32, (C, C), 0)
    s_i = lax.broadcasted_iota(jnp.int32, (C, C), 1)
    tri = ((s_i >= t_i) if reverse else (s_i <= t_i)).astype(BF16)
    cum = _dot(jnp.concatenate([tri, tri], axis=1),
               jnp.concatenate(_split3(lw)[:2], axis=0))
    same2 = jnp.concatenate([same_bf, same_bf], axis=0)
    ss = [_dot(jnp.concatenate(_split3(kk2[:, sl])[:2], axis=1), same2) for sl in sls]

    n_sq = C.bit_length() - 1
    for j in range(n_sq):
        Pb = [P[p].astype(BF16) for p in pairs]
        if j < n_sq - 1:
            PU = [_dot(Pb[p], jnp.concatenate([bd(P[p]), bd(u[p])], axis=1)) for p in pairs]
            P = [PU[p][:, :PW] for p in pairs]
            u = [u[p] + PU[p][:, PW:] for p in pairs]
        else:
            u = [u[p] + _dot(Pb[p], bd(u[p])) for p in pairs]
    for p in pairs:
        R = jnp.concatenate([jnp.where(upto, G[p][C:, :PW], 0.0),
                             jnp.where(upto, G[p][C:, PW:], 0.0)], axis=1).astype(BF16)
        y_ref[0, :, sls[p]] = XS[p][C:] + _dot(R, jnp.concatenate([bd(u[p]), bv_scr[p]], axis=0))
    for p in pairs:
        UV = jnp.concatenate([u[p].astype(BF16), vb_scr[p]], axis=0)
        dS = _dot_t(UV, z_scr[p], 0, 0)
        s_scr[p] = (S[p] + jnp.where(same, dS, 0.0)) * wt_scr[p, 0:1, :]

    a_sig = _sigmoid(a_pre)
    kd = k * (1.0 + (a_sig - 1.0) * ka_ref[...])
    e_pos = jnp.exp(cum)
    e_neg = jnp.exp(-cum)
    e_prev = jnp.exp(cum - lw)
    last = 0 if reverse else C - 1
    for p, sl in zip(pairs, sls):
        kkn = kkv[:, sl] * lax.rsqrt(jnp.maximum(ss[p], 1e-24))
        at = (-kkn) * e_prev[:, sl]
        bt = (kkn * a_sig[:, sl]) * e_neg[:, sl]
        rt = r[:, sl] * e_pos[:, sl]
        kt = kd[:, sl] * e_neg[:, sl]
        x_scr[p] = jnp.concatenate([at, rt], axis=0).astype(BF16)
        r2_scr[p] = jnp.concatenate([bd(bt), bd(kt)], axis=0)
        bv_scr[p] = bd(v_ref[0, :, sl])
        vb_scr[p] = v_ref[0, :, sl]
        z_scr[p] = jnp.concatenate([bt, kt], axis=0).astype(BF16)
        wt_scr[p] = jnp.broadcast_to(e_pos[last:last + 1, sl], wt_scr.shape[1:])

    @pl.when(c == pl.num_programs(2) - 1)
    def _():
        sout_ref[0] = s_scr[...]


def _wkv(r, k, v, tw, aw, w2p, a2p, w0, a0, kk, ka, s0, d):
    Bn, T, D = r.shape
    PW = 2 * HEAD
    npairs = D // PW
    npair = _tile(npairs, 16, 1)
    hw = npair * PW
    nc = T // CHUNK
    reverse = d == 1
    pos = (lambda j: nc - 1 - j) if reverse else (lambda j: j)
    cin = lambda c: pos(jnp.minimum(c, nc - 1))
    cout = lambda c: pos(jnp.maximum(c - 1, 0))
    tok = pl.BlockSpec((1, CHUNK, hw), lambda b, g, c: (b, cin(c), g))
    lora = pl.BlockSpec((1, CHUNK, LANES), lambda b, g, c: (b, cin(c), d))
    lw2 = pl.BlockSpec((LANES, hw), lambda b, g, c: (0, g))
    vec = pl.BlockSpec((1, hw), lambda b, g, c: (0, g))
    st = pl.BlockSpec((1, npair, PW, PW), lambda b, g, c: (b, g, 0, 0))
    return pl.pallas_call(
        functools.partial(_wkv_kernel, reverse=reverse, npair=npair),
        grid=(Bn, npairs // npair, nc + 1),
        in_specs=[tok, tok, tok, lora, lora, lw2, lw2, vec, vec, vec, vec, st],
        out_specs=[pl.BlockSpec((1, CHUNK, hw), lambda b, g, c: (b, cout(c), g)), st],
        out_shape=[jax.ShapeDtypeStruct((Bn, T, D), F32),
                   jax.ShapeDtypeStruct((Bn, npairs, PW, PW), F32)],
        scratch_shapes=[pltpu.VMEM((npair, PW, PW), F32),
                        pltpu.VMEM((npair, 2 * CHUNK, PW), BF16),
                        pltpu.VMEM((npair, 2 * PW, PW), BF16),
                        pltpu.VMEM((npair, PW, PW), BF16),
                        pltpu.VMEM((npair, CHUNK, PW), BF16),
                        pltpu.VMEM((npair, 2 * CHUNK, PW), BF16),
                        pltpu.VMEM((npair, SUBLANES, PW), F32)],
        compiler_params=_params("parallel", "parallel", "arbitrary"),
        name="wkv_rev" if reverse else "wkv_fwd",
    )(r, k, v, tw, aw, w2p, a2p, w0, a0, kk, ka, s0)


def _rwkv_out_kernel(yf_ref, yb_ref, r_ref, k_ref, v_ref, gs_ref, aw_ref, res_ref, a2f_ref, a2b_ref,
                     g2_ref, wo_ref, a0_ref, ka_ref, rk_ref, lnw_ref, lnb_ref, gate_ref, g_ref, sh_ref,
                     sc_ref, o_ref, h_ref, og_scr, *, rows):
    tm, D = o_ref.shape[1:]
    PW = 2 * HEAD
    same = _seg_ones(PW).astype(BF16)
    same2 = jnp.concatenate([same, same], axis=0)

    def head_sum(x, pieces):
        if pieces == 1:
            return _dot(x.astype(BF16), same)
        return _dot(jnp.concatenate(_split3(x)[:2], axis=1), same2)

    for r0 in range(0, tm, rows):
        rs = slice(r0, r0 + rows)
        aw = aw_ref[0, rs, :]
        a_f = _sigmoid(a0_ref[0:1, :] + _dot(aw[:, :LANES], a2f_ref[...]))
        a_b = _sigmoid(a0_ref[1:2, :] + _dot(aw[:, LANES:], a2b_ref[...]))
        g = _dot(gs_ref[0, rs, :], g2_ref[...])
        for p in range(D // PW):
            sl = slice(p * PW, (p + 1) * PW)
            ksum = k_ref[0, rs, sl].astype(F32) * (2.0 + (a_f[:, sl] + a_b[:, sl] - 2.0) * ka_ref[:, sl])
            y = yf_ref[0, rs, sl] + yb_ref[0, rs, sl]
            yc = y - head_sum(y, 2) * (1.0 / HEAD)
            var = head_sum(yc * yc, 1) * (1.0 / HEAD)
            o = yc * lax.rsqrt(var + GN_EPS) * lnw_ref[:, sl] + lnb_ref[:, sl]
            bonus = (head_sum(r_ref[0, rs, sl].astype(F32) * ksum * rk_ref[:, sl], 1)
                     * v_ref[0, rs, sl].astype(F32))
            og_scr[rs, sl] = ((o + bonus) * g[:, sl]).astype(og_scr.dtype)
        x1 = res_ref[0, rs, :] + gate_ref[0] * _dot(og_scr[rs, :], wo_ref[...])
        o_ref[0, rs, :] = x1
        h_ref[0, rs, :] = _norm_mod(x1, g_ref[...], sh_ref[0], sc_ref[0]).astype(h_ref.dtype)


def _rwkv_out(yf, yb, r, k, v, gs, aw, res, a2fp, a2bp, g2, wo, a0, ka, rk, lnw, lnb, gate, g, shift,
              scale, name):
    Bn, T, D = res.shape
    tm = _tile(T, 256, ROWS_BF16)
    rows = _tile(tm, 128, ROWS_BF16)
    G = gs.shape[-1]
    row = pl.BlockSpec((1, tm, D), lambda b, i: (b, i, 0))
    vec = pl.BlockSpec((1, D), lambda b, i: (0, 0))
    full = lambda arr: pl.BlockSpec(arr.shape, lambda b, i: (0,) * arr.ndim)
    return pl.pallas_call(
        functools.partial(_rwkv_out_kernel, rows=rows),
        grid=(Bn, T // tm),
        in_specs=[row, row, row, row, row,
                  pl.BlockSpec((1, tm, G), lambda b, i: (b, i, 0)),
                  pl.BlockSpec((1, tm, 2 * LANES), lambda b, i: (b, i, 0)),
                  row, full(a2fp), full(a2bp), full(g2), full(wo), full(a0),
                  vec, vec, vec, vec,
                  pl.BlockSpec((1, 1, D), _bsel(gate)), vec,
                  pl.BlockSpec((1, 1, D), _bsel(shift)),
                  pl.BlockSpec((1, 1, D), _bsel(scale))],
        out_specs=[row, row],
        out_shape=[jax.ShapeDtypeStruct((Bn, T, D), F32), jax.ShapeDtypeStruct((Bn, T, D), BF16)],
        scratch_shapes=[pltpu.VMEM((tm, D), BF16)],
        compiler_params=_params("parallel", "parallel"),
        name=name,
    )(yf, yb, r, k, v, gs, aw, res, a2fp, a2bp, g2, wo, a0, ka, rk, lnw, lnb, gate, g[None], shift, scale)


def _conv_kernel(gb_ref, z_ref, cw_ref, o_ref):
    z = z_ref[0].astype(F32)
    T = z.shape[0]
    t = lax.broadcasted_iota(jnp.int32, (T, 1), 0)
    zp = jnp.where(t == 0, 0.0, pltpu.roll(z, 1, 0))
    zn = jnp.where(t == T - 1, 0.0, pltpu.roll(z, T - 1, 0))
    conv = zp * cw_ref[0:1, :] + z * cw_ref[1:2, :] + zn * cw_ref[2:3, :]
    o_ref[0] = (gb_ref[0] * conv).astype(o_ref.dtype)


def _conv(gb, z, cw):
    Bn, T, D = z.shape
    tn = _tile(D, 512, LANES)
    tok = pl.BlockSpec((1, T, tn), lambda b, j: (b, 0, j))
    return pl.pallas_call(
        _conv_kernel,
        grid=(Bn, D // tn),
        in_specs=[tok, tok, pl.BlockSpec((3, tn), lambda b, j: (0, j))],
        out_specs=tok,
        out_shape=jax.ShapeDtypeStruct((Bn, T, D), BF16),
        compiler_params=_params("parallel", "parallel"),
        name="short_conv",
    )(gb, z, cw)


def _pad_rows(w, rows):
    return jnp.pad(w, ((0, rows - w.shape[0]), (0, 0)))


def _pad_cols(w, cols):
    return jnp.pad(w, ((0, 0), (0, cols - w.shape[1])))


def _split_mod(mod_rows, D):
    return [mod_rows[:, m * D:(m + 1) * D][:, None, :] for m in range(6)]


def _ffn_branch(t1, h2, mods, ffn, tag):
    w13, w2, layer, casts = ffn
    act, wdn = _ffn_up(h2, w13, w2, layer, name="ffn_up_" + tag)
    return _mm_res(act, wdn, t1, mods[5], name="ffn_down_" + tag, casts=casts)


def _rwkv_layer(x, ctx, mods_x, mods_c, g1, g2n, mix, wr, wk, wv, wo, w0, w1, w2, a0, a1, a2,
                lg1, lg2, k_k, k_a, r_k, ln_w, ln_b, ffn):
    D = x.shape[-1]
    H = D // HEAD
    w1cat = jnp.concatenate([_pad_cols(w1[0], LANES), _pad_cols(w1[1], LANES)], axis=1).astype(BF16)
    a1cat = jnp.concatenate([_pad_cols(a1[0], LANES), _pad_cols(a1[1], LANES)], axis=1).astype(BF16)
    w2p = [_pad_rows(w2[d], LANES).astype(BF16) for d in range(2)]
    a2p = [_pad_rows(a2[d], LANES).astype(BF16) for d in range(2)]
    wr, wk, wv, wo = (t.astype(BF16) for t in (wr, wk, wv, wo))
    lg1, lg2 = lg1.astype(BF16), lg2.astype(BF16)
    rk = r_k.reshape(1, D)

    xr, xw, xk, xv, xa, xg = _prep_ctx(ctx, g1, mods_c[0], mods_c[1], mix)
    lora_c = dict(tw=_mm(xw, w1cat, BF16, act="tanh", name="lora_w_c"),
                  aw=_mm(xa, a1cat, BF16, name="lora_a_c"),
                  gs=_mm(xg, lg1, BF16, act="sigmoid", name="lora_g_c"))
    ins = {"c": (xr, xk, xv, lora_c["tw"], lora_c["aw"], lora_c["gs"]),
           "x": _prep_latent(x, g1, mods_x[0], mods_x[1], mix, w1cat, a1cat, lg1)}
    sets = {}
    for tag, (xr, xk, xv, tw, aw, gs) in ins.items():
        sets[tag] = dict(r=_mm(xr, wr, BF16, name="proj_r_" + tag),
                         k=_mm(xk, wk, BF16, name="proj_k_" + tag),
                         v=_mm(xv, wv, BF16, name="proj_v_" + tag), tw=tw, aw=aw, gs=gs)

    ys = {"c": [], "x": []}
    zero_state = jnp.zeros((x.shape[0], H // 2, 2 * HEAD, 2 * HEAD), F32)
    for d in range(2):
        state = zero_state
        for tag in ("c", "x"):
            s = sets[tag]
            y, state = _wkv(s["r"], s["k"], s["v"], s["tw"], s["aw"], w2p[d], a2p[d],
                            w0[d][None], a0[d][None], k_k[None], k_a[None], state, d)
            ys[tag].append(y)

    outs = []
    for tag, tok, mods in (("c", ctx, mods_c), ("x", x, mods_x)):
        s = sets[tag]
        t1, h2 = _rwkv_out(ys[tag][0], ys[tag][1], s["r"], s["k"], s["v"], s["gs"], s["aw"], tok,
                           a2p[0], a2p[1], lg2, wo, a0, k_a[None], rk, ln_w[None], ln_b[None],
                           mods[2], g2n, mods[3], mods[4], name="rwkv_out_" + tag)
        outs.append(_ffn_branch(t1, h2, mods, ffn, tag))
    (ctx_out, _), (x_out, cast_out) = outs
    return x_out, ctx_out, cast_out


def _conv_layer(x, mods, g1, g2n, w_in, conv_w, w_out, ffn):
    gb, z = _norm_mm(x, g1, mods[0], mods[1], w_in.astype(BF16), 3, _conv_in_combine, (BF16, BF16),
                     name="conv_in")
    p = _conv(gb, z, conv_w)
    t1, h2 = _mm_res_norm(p, w_out.astype(BF16), x, mods[2], g2n, mods[3], mods[4], name="conv_out")
    return _ffn_branch(t1, h2, mods, ffn, "x")


def kernel(x, c, ctx, c_ctx, norm1_g, norm2_g, ada_w, ada_b, rw_mix, rw_wr, rw_wk, rw_wv, rw_wo,
           rw_w0, rw_w1, rw_w2, rw_a0, rw_a1, rw_a2, rw_g1, rw_g2, rw_kk, rw_ka, rw_rk, rw_lnw,
           rw_lnb, sc_win, sc_conv, sc_wout, ffn_w13, ffn_w2, final_g):
    B, T, D = x.shape
    depth = norm1_g.shape[0]
    rows = -(-(B + 1) // 8) * 8
    cond = jnp.zeros((rows, D), F32).at[:B].set(c).at[B].set(c_ctx)
    conv_w = {}
    for i in range(depth):
        last = i == depth - 1
        j = i // 2
        mod = _ada(cond, ada_w, ada_b, i)
        mods_x = _split_mod(mod[:B], D)
        mods_c = _split_mod(mod[B:B + 1], D)
        nxt = (i + 1) // 2
        casts = (sc_win[nxt], sc_wout[nxt]) if (not last and i % 2 == 0) else ()
        ffn = (ffn_w13, ffn_w2, i, casts)
        if i % 2 == 0:
            x, ctx, cast_out = _rwkv_layer(
                x, ctx, mods_x, mods_c, norm1_g[i], norm2_g[i], rw_mix[j], rw_wr[j], rw_wk[j],
                rw_wv[j], rw_wo[j], rw_w0[j], rw_w1[j], rw_w2[j], rw_a0[j], rw_a1[j], rw_a2[j],
                rw_g1[j], rw_g2[j], rw_kk[j], rw_ka[j], rw_rk[j], rw_lnw[j], rw_lnb[j], ffn)
            if casts:
                conv_w[nxt] = cast_out
        else:
            w_in, w_out = conv_w.get(j, (sc_win[j], sc_wout[j]))
            if not last:
                ctx, _ = _conv_layer(ctx, mods_c, norm1_g[i], norm2_g[i], w_in, sc_conv[j], w_out,
                                     (ffn_w13, ffn_w2, i, ()))
            x, _ = _conv_layer(x, mods_x, norm1_g[i], norm2_g[i], w_in, sc_conv[j], w_out, ffn)
    zeros = jnp.zeros((1, 1, D), F32)
    return _norm(x, final_g, zeros, zeros)
```

```python
import functools
import math

import jax
import jax.numpy as jnp
from jax import lax
from jax.experimental import pallas as pl
from jax.experimental.pallas import tpu as pltpu

HEAD = 64
GRID_W = 64
CHUNK = 64
NORM_EPS = 1e-6
GN_EPS = 64e-5
LANES = 128
SUBLANES = 8
ROWS_BF16 = 16
VMEM_LIMIT = 56 * 1024 * 1024

F32 = jnp.float32
BF16 = jnp.bfloat16


def _params(*sem):
    return pltpu.CompilerParams(dimension_semantics=sem, vmem_limit_bytes=VMEM_LIMIT)


def _tile(n, pref, mult):
    t = min(pref, n)
    t -= t % mult
    while t >= mult:
        if n % t == 0:
            return t
        t -= mult
    return n


def _sigmoid(x):
    return 1.0 / (1.0 + jnp.exp(-x))


def _norm_mod(x, g, shift, scale, cols=None):
    rs = lax.rsqrt(jnp.mean(x * x, axis=-1, keepdims=True) + NORM_EPS)
    if cols is not None:
        x, g, shift, scale = (t[:, cols[0]:cols[1]] for t in (x, g, shift, scale))
    return (x * rs) * (g * (1.0 + scale)) + shift


def _dot(a, b):
    return jnp.dot(a, b, preferred_element_type=F32)


def _dot_t(a, b, ca, cb):
    return lax.dot_general(a, b, (((ca,), (cb,)), ((), ())), preferred_element_type=F32)


def _ada_kernel(c_ref, w_ref, b_ref, o_ref):
    c = c_ref[...]
    s = c * _sigmoid(c)
    o_ref[...] = _dot(s.astype(BF16), w_ref[0].astype(BF16)) + b_ref[0]


def _ada(cond, w, b, layer):
    R, D = cond.shape
    N = w.shape[2]
    tn = _tile(N, 1024, LANES)
    return pl.pallas_call(
        _ada_kernel,
        grid=(N // tn,),
        in_specs=[pl.BlockSpec((R, D), lambda j: (0, 0)),
                  pl.BlockSpec((1, D, tn), lambda j: (layer, 0, j)),
                  pl.BlockSpec((1, 1, tn), lambda j: (layer, 0, j))],
        out_specs=pl.BlockSpec((R, tn), lambda j: (0, j)),
        out_shape=jax.ShapeDtypeStruct((R, N), F32),
        compiler_params=_params("parallel"),
        name="ada_mod",
    )(cond, w, b[:, None, :])


def _norm_kernel(x_ref, g_ref, sh_ref, sc_ref, o_ref):
    o_ref[0] = _norm_mod(x_ref[0], g_ref[...], sh_ref[0], sc_ref[0])


def _bsel(arr):
    if arr.shape[0] == 1:
        return lambda b, *_: (0, 0, 0)
    return lambda b, *_: (b, 0, 0)


def _norm(x, g, shift, scale):
    Bn, T, D = x.shape
    tm = _tile(T, 512, SUBLANES)
    return pl.pallas_call(
        _norm_kernel,
        grid=(Bn, T // tm),
        in_specs=[pl.BlockSpec((1, tm, D), lambda b, i: (b, i, 0)),
                  pl.BlockSpec((1, D), lambda b, i: (0, 0)),
                  pl.BlockSpec((1, 1, D), _bsel(shift)),
                  pl.BlockSpec((1, 1, D), _bsel(scale))],
        out_specs=pl.BlockSpec((1, tm, D), lambda b, i: (b, i, 0)),
        out_shape=jax.ShapeDtypeStruct((Bn, T, D), F32),
        compiler_params=_params("parallel", "parallel"),
        name="norm",
    )(x, g[None], shift, scale)


def _write_mix(out_refs, mix_ref, h, shifted, c0, c1):
    xx = shifted - h
    for m, o_ref in enumerate(out_refs):
        o_ref[0, :, c0:c1] = (h + xx * mix_ref[m:m + 1, c0:c1]).astype(o_ref.dtype)


def _prep_latent_kernel(x_ref, xu_ref, xd_ref, g_ref, sh_ref, sc_ref, mix_ref, w1_ref, a1_ref, g1_ref,
                        xr_ref, xk_ref, xv_ref, tw_ref, aw_ref, gs_ref, h_scr, xm_scr):
    i = pl.program_id(1)
    n = pl.num_programs(1)
    g, sh, sc = g_ref[...], sh_ref[0], sc_ref[0]
    tm, D = x_ref.shape[1:]
    q = D // 4
    W = GRID_W
    h_scr[0:W, 2 * q:3 * q] = jnp.where(i > 0, _norm_mod(xu_ref[0], g, sh, sc, (2 * q, 3 * q)), 0.0)
    h_scr[W - SUBLANES:W, 0:q] = jnp.zeros((SUBLANES, q), F32)
    for r0 in range(0, tm, W):
        h_scr[W + r0:2 * W + r0, :] = _norm_mod(x_ref[0, r0:r0 + W, :], g, sh, sc)
    h_scr[W + tm:, 3 * q:] = jnp.where(i < n - 1, _norm_mod(xd_ref[0], g, sh, sc, (3 * q, D)), 0.0)
    h_scr[W + tm:W + tm + SUBLANES, q:2 * q] = jnp.zeros((SUBLANES, q), F32)
    wide = {0: xr_ref, 2: xk_ref, 3: xv_ref}
    lora = {1: 0, 4: 1, 5: 2}
    lora_w = (w1_ref, a1_ref, g1_ref)
    acc = [None] * 3
    R = 32
    row = lax.broadcasted_iota(jnp.int32, (R, 1), 0)
    for k, off in enumerate((-1, 1, -W, W)):
        c0, c1 = k * q, (k + 1) * q
        mixk = [mix_ref[m:m + 1, c0:c1] for m in range(6)]
        for r0 in range(0, tm, R):
            h = h_scr[W + r0:W + r0 + R, c0:c1]
            s = h_scr[W + r0 + off:W + r0 + off + R, c0:c1]
            if off == -1 and r0 % W == 0:
                s = jnp.where(row == 0, 0.0, s)
            if off == 1 and (r0 + R) % W == 0:
                s = jnp.where(row == R - 1, 0.0, s)
            xx = s - h
            for m in range(6):
                xm = (h + xx * mixk[m]).astype(BF16)
                if m in wide:
                    wide[m][0, r0:r0 + R, c0:c1] = xm
                else:
                    xm_scr[lora[m], r0:r0 + R, c0:c1] = xm
        for l in range(3):
            part = _dot(xm_scr[l, :, c0:c1], lora_w[l][c0:c1, :])
            acc[l] = part if acc[l] is None else acc[l] + part
    tw_ref[0] = jnp.tanh(acc[0]).astype(tw_ref.dtype)
    aw_ref[0] = acc[1].astype(aw_ref.dtype)
    gs_ref[0] = _sigmoid(acc[2]).astype(gs_ref.dtype)


def _prep_latent(x, g, shift, scale, mix, w1cat, a1cat, lg1):
    Bn, T, D = x.shape
    rows_per_tile = _tile(T // GRID_W, 4, 1)
    tm = rows_per_tile * GRID_W
    nrow = T // GRID_W
    tok = pl.BlockSpec((1, tm, D), lambda b, i: (b, i, 0))
    full = lambda arr: pl.BlockSpec(arr.shape, lambda b, i: (0,) * arr.ndim)
    small = lambda arr: pl.BlockSpec((1, tm, arr.shape[1]), lambda b, i: (b, i, 0))
    return pl.pallas_call(
        _prep_latent_kernel,
        grid=(Bn, T // tm),
        in_specs=[tok,
                  pl.BlockSpec((1, GRID_W, D),
                               lambda b, i: (b, jnp.maximum(i * rows_per_tile - 1, 0), 0)),
                  pl.BlockSpec((1, GRID_W, D),
                               lambda b, i: (b, jnp.minimum((i + 1) * rows_per_tile, nrow - 1), 0)),
                  pl.BlockSpec((1, D), lambda b, i: (0, 0)),
                  pl.BlockSpec((1, 1, D), _bsel(shift)),
                  pl.BlockSpec((1, 1, D), _bsel(scale)),
                  pl.BlockSpec((6, D), lambda b, i: (0, 0)),
                  full(w1cat), full(a1cat), full(lg1)],
        out_specs=[tok, tok, tok, small(w1cat), small(a1cat), small(lg1)],
        out_shape=[jax.ShapeDtypeStruct((Bn, T, D), BF16)] * 3
        + [jax.ShapeDtypeStruct((Bn, T, w.shape[1]), BF16) for w in (w1cat, a1cat, lg1)],
        scratch_shapes=[pltpu.VMEM((tm + 2 * GRID_W, D), F32), pltpu.VMEM((3, tm, D), BF16)],
        compiler_params=_params("parallel", "parallel"),
        name="prep_latent",
    )(x, x, x, g[None], shift, scale, mix, w1cat, a1cat, lg1)


def _prep_ctx_kernel(x_ref, g_ref, sh_ref, sc_ref, mix_ref, *out_refs):
    h = _norm_mod(x_ref[0], g_ref[...], sh_ref[0], sc_ref[0])
    L, D = h.shape
    half = D // 2
    t = lax.broadcasted_iota(jnp.int32, (L, 1), 0)
    h0, h1 = h[:, :half], h[:, half:]
    prev = jnp.where(t == 0, 0.0, pltpu.roll(h0, 1, 0))
    nxt = jnp.where(t == L - 1, 0.0, pltpu.roll(h1, L - 1, 0))
    _write_mix(out_refs, mix_ref, h0, prev, 0, half)
    _write_mix(out_refs, mix_ref, h1, nxt, half, D)


def _prep_ctx(x, g, shift, scale, mix):
    Bn, L, D = x.shape
    return pl.pallas_call(
        _prep_ctx_kernel,
        grid=(Bn,),
        in_specs=[pl.BlockSpec((1, L, D), lambda b: (b, 0, 0)),
                  pl.BlockSpec((1, D), lambda b: (0, 0)),
                  pl.BlockSpec((1, 1, D), _bsel(shift)),
                  pl.BlockSpec((1, 1, D), _bsel(scale)),
                  pl.BlockSpec((6, D), lambda b: (0, 0))],
        out_specs=[pl.BlockSpec((1, L, D), lambda b: (b, 0, 0))] * 6,
        out_shape=[jax.ShapeDtypeStruct((Bn, L, D), BF16)] * 6,
        compiler_params=_params("parallel"),
        name="prep_ctx",
    )(x, g[None], shift, scale, mix)


def _mm_kernel(a_ref, w_ref, o_ref, *, act):
    acc = _dot(a_ref[0], w_ref[...])
    if act == "tanh":
        acc = jnp.tanh(acc)
    elif act == "sigmoid":
        acc = _sigmoid(acc)
    o_ref[0] = acc.astype(o_ref.dtype)


def _mm(a, w, out_dtype, act=None, name="mm"):
    Bn, T, K = a.shape
    N = w.shape[1]
    M = Bn * T
    tm = _tile(M, 1024, ROWS_BF16)
    tn = _tile(N, 2048, LANES)
    out = pl.pallas_call(
        functools.partial(_mm_kernel, act=act),
        grid=(1, M // tm, N // tn),
        in_specs=[pl.BlockSpec((1, tm, K), lambda b, i, j: (b, i, 0)),
                  pl.BlockSpec((K, tn), lambda b, i, j: (0, j))],
        out_specs=pl.BlockSpec((1, tm, tn), lambda b, i, j: (b, i, j)),
        out_shape=jax.ShapeDtypeStruct((1, M, N), out_dtype),
        compiler_params=_params("parallel", "parallel", "parallel"),
        name=name,
    )(a.reshape(1, M, K), w)
    return out.reshape(Bn, T, N)


def _mm_res_kernel(a_ref, w_ref, res_ref, gate_ref, *rest):
    ncast = (len(rest) - 1) // 2
    cast_in, o_ref, cast_out = rest[:ncast], rest[ncast], rest[ncast + 1:]
    o_ref[0] = res_ref[0] + gate_ref[0] * _dot(a_ref[0], w_ref[...])
    for src, dst in zip(cast_in, cast_out):
        dst[...] = src[...].astype(BF16)


def _mm_res(a, w, res, gate, name="mm_res", casts=()):
    Bn, T, K = a.shape
    N = w.shape[1]
    tm = _tile(T, 1024, ROWS_BF16)
    tn = _tile(N, 1024 if K <= 2048 else 512, LANES)
    ni, nj = T // tm, N // tn
    steps = Bn * ni * nj
    gsel = _bsel(gate)
    step = lambda b, i, j: (b * ni + i) * nj + j
    cast_specs = []
    for cw in casts:
        rows = cw.shape[0] // steps
        assert rows * steps == cw.shape[0] and rows % ROWS_BF16 == 0, (cw.shape, steps)
        cast_specs.append(pl.BlockSpec((rows, cw.shape[1]), lambda b, i, j: (step(b, i, j), 0)))
    tile = pl.BlockSpec((1, tm, tn), lambda b, i, j: (b, i, j))
    outs = pl.pallas_call(
        _mm_res_kernel,
        grid=(Bn, ni, nj),
        in_specs=[pl.BlockSpec((1, tm, K), lambda b, i, j: (b, i, 0)),
                  pl.BlockSpec((K, tn), lambda b, i, j: (0, j)),
                  tile,
                  pl.BlockSpec((1, 1, tn), lambda b, i, j: gsel(b)[:2] + (j,))] + cast_specs,
        out_specs=[tile] + cast_specs,
        out_shape=[jax.ShapeDtypeStruct((Bn, T, N), F32)]
        + [jax.ShapeDtypeStruct(cw.shape, BF16) for cw in casts],
        compiler_params=_params("arbitrary", "arbitrary", "arbitrary"),
        name=name,
    )(a, w, res, gate, *casts)
    return outs[0], tuple(outs[1:])


def _mm_res_norm_kernel(a_ref, w_ref, res_ref, gate_ref, g_ref, sh_ref, sc_ref, o_ref, h_ref, *, rows):
    for r0 in range(0, a_ref.shape[1], rows):
        rs = slice(r0, r0 + rows)
        x1 = res_ref[0, rs, :] + gate_ref[0] * _dot(a_ref[0, rs, :], w_ref[...])
        o_ref[0, rs, :] = x1
        h_ref[0, rs, :] = _norm_mod(x1, g_ref[...], sh_ref[0], sc_ref[0]).astype(h_ref.dtype)


def _mm_res_norm(a, w, res, gate, g, shift, scale, name):
    Bn, T, K = a.shape
    N = w.shape[1]
    tm = _tile(T, 512, ROWS_BF16)
    rows = _tile(tm, 256, ROWS_BF16)
    row = pl.BlockSpec((1, tm, N), lambda b, i: (b, i, 0))
    return pl.pallas_call(
        functools.partial(_mm_res_norm_kernel, rows=rows),
        grid=(Bn, T // tm),
        in_specs=[pl.BlockSpec((1, tm, K), lambda b, i: (b, i, 0)),
                  pl.BlockSpec((K, N), lambda b, i: (0, 0)),
                  row,
                  pl.BlockSpec((1, 1, N), _bsel(gate)),
                  pl.BlockSpec((1, N), lambda b, i: (0, 0)),
                  pl.BlockSpec((1, 1, N), _bsel(shift)),
                  pl.BlockSpec((1, 1, N), _bsel(scale))],
        out_specs=[row, row],
        out_shape=[jax.ShapeDtypeStruct((Bn, T, N), F32), jax.ShapeDtypeStruct((Bn, T, N), BF16)],
        compiler_params=_params("parallel", "parallel"),
        name=name,
    )(a, w, res, gate, g[None], shift, scale)


def _ffn_up_kernel(h_ref, wa_ref, wb_ref, w2_ref, o_ref, w2o_ref, w_scr, *, rows):
    @pl.when((pl.program_id(1) == 0) & (pl.program_id(2) == 0))
    def _():
        w_scr[0] = wa_ref[0].astype(BF16)
        w_scr[1] = wb_ref[0].astype(BF16)

    for r0 in range(0, h_ref.shape[1], rows):
        rs = slice(r0, r0 + rows)
        h = h_ref[0, rs, :]
        a = _dot(h, w_scr[0])
        o_ref[0, rs, :] = (a * _sigmoid(a) * _dot(h, w_scr[1])).astype(o_ref.dtype)
    w2o_ref[...] = w2_ref[0].astype(BF16)


def _ffn_up(h, w13, w2, layer, name):
    Bn, T, D = h.shape
    F = w13.shape[2] // 2
    tm = _tile(T, 2048, ROWS_BF16)
    rows = _tile(tm, 512, ROWS_BF16)
    tn = _tile(F, 512, LANES)
    nj, ni = F // tn, T // tm
    steps = nj * Bn * ni
    F2, D2 = w2.shape[1:]
    rows2 = F2 // steps
    assert rows2 * steps == F2 and rows2 % ROWS_BF16 == 0, (F2, steps)
    step = lambda j, b, i: (j * Bn + b) * ni + i
    return pl.pallas_call(
        functools.partial(_ffn_up_kernel, rows=rows),
        grid=(nj, Bn, ni),
        in_specs=[pl.BlockSpec((1, tm, D), lambda j, b, i: (b, i, 0)),
                  pl.BlockSpec((1, D, tn), lambda j, b, i: (layer, 0, j)),
                  pl.BlockSpec((1, D, tn), lambda j, b, i: (layer, 0, j + nj)),
                  pl.BlockSpec((1, rows2, D2), lambda j, b, i: (layer, step(j, b, i), 0))],
        out_specs=[pl.BlockSpec((1, tm, tn), lambda j, b, i: (b, i, j)),
                   pl.BlockSpec((rows2, D2), lambda j, b, i: (step(j, b, i), 0))],
        out_shape=[jax.ShapeDtypeStruct((Bn, T, F), BF16), jax.ShapeDtypeStruct((F2, D2), BF16)],
        scratch_shapes=[pltpu.VMEM((2, D, tn), BF16)],
        compiler_params=_params("arbitrary", "arbitrary", "arbitrary"),
        name=name,
    )(h, w13, w13, w2)


def _conv_in_combine(gb, gc, u):
    return gb, gc * u


def _norm_mm_kernel(x_ref, g_ref, sh_ref, sc_ref, *rest, nw, combine, rows):
    w_refs, out_refs, h_scr = rest[:nw], rest[nw:-1], rest[-1]

    def emit(rs, h):
        outs = combine(*[_dot(h, w_ref[...]) for w_ref in w_refs])
        for o_ref, o in zip(out_refs, outs):
            o_ref[0, rs, :] = o.astype(o_ref.dtype)

    @pl.when(pl.program_id(2) == 0)
    def _():
        for r0 in range(0, x_ref.shape[1], rows):
            rs = slice(r0, r0 + rows)
            h = _norm_mod(x_ref[0, rs, :], g_ref[...], sh_ref[0], sc_ref[0]).astype(BF16)
            h_scr[rs, :] = h
            emit(rs, h)

    @pl.when(pl.program_id(2) > 0)
    def _():
        emit(slice(None), h_scr[...])


def _norm_mm(x, g, shift, scale, w, nw, combine, out_dtypes, name):
    Bn, T, D = x.shape
    N = w.shape[1] // nw
    tm = _tile(T, 1024, ROWS_BF16)
    rows = _tile(tm, 256, ROWS_BF16)
    tn = _tile(N, 512, LANES)
    nj = N // tn
    w_specs = [pl.BlockSpec((D, tn), functools.partial(lambda b, i, j, m: (0, j + m * nj), m=m))
               for m in range(nw)]
    return pl.pallas_call(
        functools.partial(_norm_mm_kernel, nw=nw, combine=combine, rows=rows),
        grid=(Bn, T // tm, nj),
        in_specs=[pl.BlockSpec((1, tm, D), lambda b, i, j: (b, i, 0)),
                  pl.BlockSpec((1, D), lambda b, i, j: (0, 0)),
                  pl.BlockSpec((1, 1, D), _bsel(shift)),
                  pl.BlockSpec((1, 1, D), _bsel(scale))] + w_specs,
        out_specs=[pl.BlockSpec((1, tm, tn), lambda b, i, j: (b, i, j))] * len(out_dtypes),
        out_shape=[jax.ShapeDtypeStruct((Bn, T, N), dt) for dt in out_dtypes],
        scratch_shapes=[pltpu.VMEM((tm, D), BF16)],
        compiler_params=_params("parallel", "parallel", "arbitrary"),
        name=name,
    )(x, g[None], shift, scale, *([w] * nw))


def _seg_ones(width):
    shift = HEAD.bit_length() - 1
    r = lax.shift_right_logical(lax.broadcasted_iota(jnp.int32, (width, width), 0), shift)
    c = lax.shift_right_logical(lax.broadcasted_iota(jnp.int32, (width, width), 1), shift)
    return (r == c).astype(F32)


def _split3(x):
    hi = x.astype(BF16)
    r1 = x - hi.astype(F32)
    mid = r1.astype(BF16)
    lo = (r1 - mid.astype(F32)).astype(BF16)
    return hi, mid, lo


def _wkv_kernel(r_ref, k_ref, v_ref, tw_ref, aw_ref, w2_ref, a2_ref, w0_ref, a0_ref, kk_ref, ka_ref,
                s0_ref, y_ref, sout_ref, s_scr, x_scr, r2_scr, bv_scr, vb_scr, z_scr, wt_scr,
                *, reverse, npair):
    c = pl.program_id(2)
    C = CHUNK
    PW = 2 * HEAD

    @pl.when(c == 0)
    def _():
        s_scr[...] = s0_ref[0]
        x_scr[...] = jnp.zeros_like(x_scr)
        r2_scr[...] = jnp.zeros_like(r2_scr)
        bv_scr[...] = jnp.zeros_like(bv_scr)
        vb_scr[...] = jnp.zeros_like(vb_scr)
        z_scr[...] = jnp.zeros_like(z_scr)
        wt_scr[...] = jnp.ones_like(wt_scr)

    rr = lax.shift_right_logical(lax.broadcasted_iota(jnp.int32, (PW, PW), 0), HEAD.bit_length() - 1)
    cc = lax.shift_right_logical(lax.broadcasted_iota(jnp.int32, (PW, PW), 1), HEAD.bit_length() - 1)
    same = rr == cc
    same_bf = same.astype(BF16)

    def bd(x):
        xb = x.astype(BF16)
        return jnp.concatenate([xb, xb], axis=0) * same_bf

    t2 = lax.broadcasted_iota(jnp.int32, (C, PW), 0)
    s2 = lax.broadcasted_iota(jnp.int32, (C, PW), 1) & (HEAD - 1)
    before = (s2 > t2) if reverse else (s2 < t2)
    upto = before | (s2 == t2)
    pairs = range(npair)
    sls = [slice(p * PW, (p + 1) * PW) for p in pairs]

    r, k, v = (t[0].astype(F32) for t in (r_ref, k_ref, v_ref))
    z = w0_ref[...] + _dot(tw_ref[0], w2_ref[...])
    a_pre = a0_ref[...] + _dot(aw_ref[0], a2_ref[...])

    S = [s_scr[p] for p in pairs]
    X = [x_scr[p] for p in pairs]
    G = [_dot_t(X[p], r2_scr[p], 1, 1) for p in pairs]
    XS = [_dot_t(X[p], S[p].astype(BF16), 1, 1) for p in pairs]
    u = [XS[p][:C] + _dot(jnp.where(before, G[p][:C, PW:], 0.0).astype(BF16), bv_scr[p]) for p in pairs]
    P = [jnp.where(before, G[p][:C, :PW], 0.0) for p in pairs]

    lw = -math.exp(-0.5) * _sigmoid(z)
    kkv = k * kk_ref[...]
    kk2 = kkv * kkv
    t_i = lax.broadcasted_iota(jnp.int32, (C, C), 0)
    s_i = lax.broadcasted_iota(jnp.int32, (C, C), 1)
    tri = ((s_i >= t_i) if reverse else (s_i <= t_i)).astype(BF16)
    cum = _dot(jnp.concatenate([tri, tri], axis=1),
               jnp.concatenate(_split3(lw)[:2], axis=0))
    same2 = jnp.concatenate([same_bf, same_bf], axis=0)
    ss = [_dot(jnp.concatenate(_split3(kk2[:, sl])[:2], axis=1), same2) for sl in sls]

    n_sq = C.bit_length() - 1
    for j in range(n_sq):
        Pb = [P[p].astype(BF16) for p in pairs]
        if j < n_sq - 1:
            PU = [_dot(Pb[p], jnp.concatenate([bd(P[p]), bd(u[p])], axis=1)) for p in pairs]
            P = [PU[p][:, :PW] for p in pairs]
            u = [u[p] + PU[p][:, PW:] for p in pairs]
        else:
            u = [u[p] + _dot(Pb[p], bd(u[p])) for p in pairs]
    for p in pairs:
        R = jnp.concatenate([jnp.where(upto, G[p][C:, :PW], 0.0),
                             jnp.where(upto, G[p][C:, PW:], 0.0)], axis=1).astype(BF16)
        y_ref[0, :, sls[p]] = XS[p][C:] + _dot(R, jnp.concatenate([bd(u[p]), bv_scr[p]], axis=0))
    for p in pairs:
        UV = jnp.concatenate([u[p].astype(BF16), vb_scr[p]], axis=0)
        dS = _dot_t(UV, z_scr[p], 0, 0)
        s_scr[p] = (S[p] + jnp.where(same, dS, 0.0)) * wt_scr[p, 0:1, :]

    a_sig = _sigmoid(a_pre)
    kd = k * (1.0 + (a_sig - 1.0) * ka_ref[...])
    e_pos = jnp.exp(cum)
    e_neg = jnp.exp(-cum)
    e_prev = jnp.exp(cum - lw)
    last = 0 if reverse else C - 1
    for p, sl in zip(pairs, sls):
        kkn = kkv[:, sl] * lax.rsqrt(jnp.maximum(ss[p], 1e-24))
        at = (-kkn) * e_prev[:, sl]
        bt = (kkn * a_sig[:, sl]) * e_neg[:, sl]
        rt = r[:, sl] * e_pos[:, sl]
        kt = kd[:, sl] * e_neg[:, sl]
        x_scr[p] = jnp.concatenate([at, rt], axis=0).astype(BF16)
        r2_scr[p] = jnp.concatenate([bd(bt), bd(kt)], axis=0)
        bv_scr[p] = bd(v[:, sl])
        vb_scr[p] = v[:, sl].astype(BF16)
        z_scr[p] = jnp.concatenate([bt, kt], axis=0).astype(BF16)
        wt_scr[p] = jnp.broadcast_to(e_pos[last:last + 1, sl], wt_scr.shape[1:])

    @pl.when(c == pl.num_programs(2) - 1)
    def _():
        sout_ref[0] = s_scr[...]


def _wkv(r, k, v, tw, aw, w2p, a2p, w0, a0, kk, ka, s0, d):
    Bn, T, D = r.shape
    PW = 2 * HEAD
    npairs = D // PW
    npair = _tile(npairs, 16, 1)
    hw = npair * PW
    nc = T // CHUNK
    reverse = d == 1
    pos = (lambda j: nc - 1 - j) if reverse else (lambda j: j)
    cin = lambda c: pos(jnp.minimum(c, nc - 1))
    cout = lambda c: pos(jnp.maximum(c - 1, 0))
    tok = pl.BlockSpec((1, CHUNK, hw), lambda b, g, c: (b, cin(c), g))
    lora = pl.BlockSpec((1, CHUNK, LANES), lambda b, g, c: (b, cin(c), d))
    lw2 = pl.BlockSpec((LANES, hw), lambda b, g, c: (0, g))
    vec = pl.BlockSpec((1, hw), lambda b, g, c: (0, g))
    st = pl.BlockSpec((1, npair, PW, PW), lambda b, g, c: (b, g, 0, 0))
    return pl.pallas_call(
        functools.partial(_wkv_kernel, reverse=reverse, npair=npair),
        grid=(Bn, npairs // npair, nc + 1),
        in_specs=[tok, tok, tok, lora, lora, lw2, lw2, vec, vec, vec, vec, st],
        out_specs=[pl.BlockSpec((1, CHUNK, hw), lambda b, g, c: (b, cout(c), g)), st],
        out_shape=[jax.ShapeDtypeStruct((Bn, T, D), F32),
                   jax.ShapeDtypeStruct((Bn, npairs, PW, PW), F32)],
        scratch_shapes=[pltpu.VMEM((npair, PW, PW), F32),
                        pltpu.VMEM((npair, 2 * CHUNK, PW), BF16),
                        pltpu.VMEM((npair, 2 * PW, PW), BF16),
                        pltpu.VMEM((npair, PW, PW), BF16),
                        pltpu.VMEM((npair, CHUNK, PW), BF16),
                        pltpu.VMEM((npair, 2 * CHUNK, PW), BF16),
                        pltpu.VMEM((npair, SUBLANES, PW), F32)],
        compiler_params=_params("parallel", "parallel", "arbitrary"),
        name="wkv_rev" if reverse else "wkv_fwd",
    )(r, k, v, tw, aw, w2p, a2p, w0, a0, kk, ka, s0)


def _rwkv_out_kernel(yf_ref, yb_ref, r_ref, k_ref, v_ref, gs_ref, aw_ref, res_ref, a2f_ref, a2b_ref,
                     g2_ref, wo_ref, a0_ref, ka_ref, rk_ref, lnw_ref, lnb_ref, gate_ref, g_ref, sh_ref,
                     sc_ref, o_ref, h_ref, og_scr, *, rows):
    tm, D = o_ref.shape[1:]
    PW = 2 * HEAD
    same = _seg_ones(PW).astype(BF16)
    same2 = jnp.concatenate([same, same], axis=0)

    def head_sum(x, pieces):
        if pieces == 1:
            return _dot(x.astype(BF16), same)
        return _dot(jnp.concatenate(_split3(x)[:2], axis=1), same2)

    for r0 in range(0, tm, rows):
        rs = slice(r0, r0 + rows)
        aw = aw_ref[0, rs, :]
        a_f = _sigmoid(a0_ref[0:1, :] + _dot(aw[:, :LANES], a2f_ref[...]))
        a_b = _sigmoid(a0_ref[1:2, :] + _dot(aw[:, LANES:], a2b_ref[...]))
        g = _dot(gs_ref[0, rs, :], g2_ref[...])
        for p in range(D // PW):
            sl = slice(p * PW, (p + 1) * PW)
            ksum = k_ref[0, rs, sl].astype(F32) * (2.0 + (a_f[:, sl] + a_b[:, sl] - 2.0) * ka_ref[:, sl])
            y = yf_ref[0, rs, sl] + yb_ref[0, rs, sl]
            yc = y - head_sum(y, 2) * (1.0 / HEAD)
            var = head_sum(yc * yc, 1) * (1.0 / HEAD)
            o = yc * lax.rsqrt(var + GN_EPS) * lnw_ref[:, sl] + lnb_ref[:, sl]
            bonus = (head_sum(r_ref[0, rs, sl].astype(F32) * ksum * rk_ref[:, sl], 1)
                     * v_ref[0, rs, sl].astype(F32))
            og_scr[rs, sl] = ((o + bonus) * g[:, sl]).astype(og_scr.dtype)
        x1 = res_ref[0, rs, :] + gate_ref[0] * _dot(og_scr[rs, :], wo_ref[...])
        o_ref[0, rs, :] = x1
        h_ref[0, rs, :] = _norm_mod(x1, g_ref[...], sh_ref[0], sc_ref[0]).astype(h_ref.dtype)


def _rwkv_out(yf, yb, r, k, v, gs, aw, res, a2fp, a2bp, g2, wo, a0, ka, rk, lnw, lnb, gate, g, shift,
              scale, name):
    Bn, T, D = res.shape
    tm = _tile(T, 256, ROWS_BF16)
    rows = _tile(tm, 256, ROWS_BF16)
    G = gs.shape[-1]
    row = pl.BlockSpec((1, tm, D), lambda b, i: (b, i, 0))
    vec = pl.BlockSpec((1, D), lambda b, i: (0, 0))
    full = lambda arr: pl.BlockSpec(arr.shape, lambda b, i: (0,) * arr.ndim)
    return pl.pallas_call(
        functools.partial(_rwkv_out_kernel, rows=rows),
        grid=(Bn, T // tm),
        in_specs=[row, row, row, row, row,
                  pl.BlockSpec((1, tm, G), lambda b, i: (b, i, 0)),
                  pl.BlockSpec((1, tm, 2 * LANES), lambda b, i: (b, i, 0)),
                  row, full(a2fp), full(a2bp), full(g2), full(wo), full(a0),
                  vec, vec, vec, vec,
                  pl.BlockSpec((1, 1, D), _bsel(gate)), vec,
                  pl.BlockSpec((1, 1, D), _bsel(shift)),
                  pl.BlockSpec((1, 1, D), _bsel(scale))],
        out_specs=[row, row],
        out_shape=[jax.ShapeDtypeStruct((Bn, T, D), F32), jax.ShapeDtypeStruct((Bn, T, D), BF16)],
        scratch_shapes=[pltpu.VMEM((tm, D), BF16)],
        compiler_params=_params("parallel", "parallel"),
        name=name,
    )(yf, yb, r, k, v, gs, aw, res, a2fp, a2bp, g2, wo, a0, ka, rk, lnw, lnb, gate, g[None], shift, scale)


def _conv_kernel(gb_ref, z_ref, cw_ref, o_ref):
    z = z_ref[0].astype(F32)
    T = z.shape[0]
    t = lax.broadcasted_iota(jnp.int32, (T, 1), 0)
    zp = jnp.where(t == 0, 0.0, pltpu.roll(z, 1, 0))
    zn = jnp.where(t == T - 1, 0.0, pltpu.roll(z, T - 1, 0))
    conv = zp * cw_ref[0:1, :] + z * cw_ref[1:2, :] + zn * cw_ref[2:3, :]
    o_ref[0] = (gb_ref[0] * conv).astype(o_ref.dtype)


def _conv(gb, z, cw):
    Bn, T, D = z.shape
    tn = _tile(D, 512, LANES)
    tok = pl.BlockSpec((1, T, tn), lambda b, j: (b, 0, j))
    return pl.pallas_call(
        _conv_kernel,
        grid=(Bn, D // tn),
        in_specs=[tok, tok, pl.BlockSpec((3, tn), lambda b, j: (0, j))],
        out_specs=tok,
        out_shape=jax.ShapeDtypeStruct((Bn, T, D), BF16),
        compiler_params=_params("parallel", "parallel"),
        name="short_conv",
    )(gb, z, cw)


def _pad_rows(w, rows):
    return jnp.pad(w, ((0, rows - w.shape[0]), (0, 0)))


def _pad_cols(w, cols):
    return jnp.pad(w, ((0, 0), (0, cols - w.shape[1])))


def _split_mod(mod_rows, D):
    return [mod_rows[:, m * D:(m + 1) * D][:, None, :] for m in range(6)]


def _ffn_branch(t1, h2, mods, ffn, tag):
    w13, w2, layer, casts = ffn
    act, wdn = _ffn_up(h2, w13, w2, layer, name="ffn_up_" + tag)
    return _mm_res(act, wdn, t1, mods[5], name="ffn_down_" + tag, casts=casts)


def _rwkv_layer(x, ctx, mods_x, mods_c, g1, g2n, mix, wr, wk, wv, wo, w0, w1, w2, a0, a1, a2,
                lg1, lg2, k_k, k_a, r_k, ln_w, ln_b, ffn):
    D = x.shape[-1]
    H = D // HEAD
    w1cat = jnp.concatenate([_pad_cols(w1[0], LANES), _pad_cols(w1[1], LANES)], axis=1).astype(BF16)
    a1cat = jnp.concatenate([_pad_cols(a1[0], LANES), _pad_cols(a1[1], LANES)], axis=1).astype(BF16)
    w2p = [_pad_rows(w2[d], LANES).astype(BF16) for d in range(2)]
    a2p = [_pad_rows(a2[d], LANES).astype(BF16) for d in range(2)]
    wr, wk, wv, wo = (t.astype(BF16) for t in (wr, wk, wv, wo))
    lg1, lg2 = lg1.astype(BF16), lg2.astype(BF16)
    rk = r_k.reshape(1, D)

    xr, xw, xk, xv, xa, xg = _prep_ctx(ctx, g1, mods_c[0], mods_c[1], mix)
    lora_c = dict(tw=_mm(xw, w1cat, BF16, act="tanh", name="lora_w_c"),
                  aw=_mm(xa, a1cat, BF16, name="lora_a_c"),
                  gs=_mm(xg, lg1, BF16, act="sigmoid", name="lora_g_c"))
    ins = {"c": (xr, xk, xv, lora_c["tw"], lora_c["aw"], lora_c["gs"]),
           "x": _prep_latent(x, g1, mods_x[0], mods_x[1], mix, w1cat, a1cat, lg1)}
    sets = {}
    for tag, (xr, xk, xv, tw, aw, gs) in ins.items():
        sets[tag] = dict(r=_mm(xr, wr, BF16, name="proj_r_" + tag),
                         k=_mm(xk, wk, BF16, name="proj_k_" + tag),
                         v=_mm(xv, wv, BF16, name="proj_v_" + tag), tw=tw, aw=aw, gs=gs)

    ys = {"c": [], "x": []}
    zero_state = jnp.zeros((x.shape[0], H // 2, 2 * HEAD, 2 * HEAD), F32)
    for d in range(2):
        state = zero_state
        for tag in ("c", "x"):
            s = sets[tag]
            y, state = _wkv(s["r"], s["k"], s["v"], s["tw"], s["aw"], w2p[d], a2p[d],
                            w0[d][None], a0[d][None], k_k[None], k_a[None], state, d)
            ys[tag].append(y)

    outs = []
    for tag, tok, mods in (("c", ctx, mods_c), ("x", x, mods_x)):
        s = sets[tag]
        t1, h2 = _rwkv_out(ys[tag][0], ys[tag][1], s["r"], s["k"], s["v"], s["gs"], s["aw"], tok,
                           a2p[0], a2p[1], lg2, wo, a0, k_a[None], rk, ln_w[None], ln_b[None],
                           mods[2], g2n, mods[3], mods[4], name="rwkv_out_" + tag)
        outs.append(_ffn_branch(t1, h2, mods, ffn, tag))
    (ctx_out, _), (x_out, cast_out) = outs
    return x_out, ctx_out, cast_out


def _conv_layer(x, mods, g1, g2n, w_in, conv_w, w_out, ffn):
    gb, z = _norm_mm(x, g1, mods[0], mods[1], w_in.astype(BF16), 3, _conv_in_combine, (BF16, BF16),
                     name="conv_in")
    p = _conv(gb, z, conv_w)
    t1, h2 = _mm_res_norm(p, w_out.astype(BF16), x, mods[2], g2n, mods[3], mods[4], name="conv_out")
    return _ffn_branch(t1, h2, mods, ffn, "x")


def kernel(x, c, ctx, c_ctx, norm1_g, norm2_g, ada_w, ada_b, rw_mix, rw_wr, rw_wk, rw_wv, rw_wo,
           rw_w0, rw_w1, rw_w2, rw_a0, rw_a1, rw_a2, rw_g1, rw_g2, rw_kk, rw_ka, rw_rk, rw_lnw,
           rw_lnb, sc_win, sc_conv, sc_wout, ffn_w13, ffn_w2, final_g):
    B, T, D = x.shape
    depth = norm1_g.shape[0]
    rows = -(-(B + 1) // 8) * 8
    cond = jnp.zeros((rows, D), F32).at[:B].set(c).at[B].set(c_ctx)
    conv_w = {}
    for i in range(depth):
        last = i == depth - 1
        j = i // 2
        mod = _ada(cond, ada_w, ada_b, i)
        mods_x = _split_mod(mod[:B], D)
        mods_c = _split_mod(mod[B:B + 1], D)
        nxt = (i + 1) // 2
        casts = (sc_win[nxt], sc_wout[nxt]) if (not last and i % 2 == 0) else ()
        ffn = (ffn_w13, ffn_w2, i, casts)
        if i % 2 == 0:
            x, ctx, cast_out = _rwkv_layer(
                x, ctx, mods_x, mods_c, norm1_g[i], norm2_g[i], rw_mix[j], rw_wr[j], rw_wk[j],
                rw_wv[j], rw_wo[j], rw_w0[j], rw_w1[j], rw_w2[j], rw_a0[j], rw_a1[j], rw_a2[j],
                rw_g1[j], rw_g2[j], rw_kk[j], rw_ka[j], rw_rk[j], rw_lnw[j], rw_lnb[j], ffn)
            if casts:
                conv_w[nxt] = cast_out
        else:
            w_in, w_out = conv_w.get(j, (sc_win[j], sc_wout[j]))
            if not last:
                ctx, _ = _conv_layer(ctx, mods_c, norm1_g[i], norm2_g[i], w_in, sc_conv[j], w_out,
                                     (ffn_w13, ffn_w2, i, ()))
            x, _ = _conv_layer(x, mods_x, norm1_g[i], norm2_g[i], w_in, sc_conv[j], w_out, ffn)
    zeros = jnp.zeros((1, 1, D), F32)
    return _norm(x, final_g, zeros, zeros)
```

```python
import functools
import math

import jax
import jax.numpy as jnp
from jax import lax
from jax.experimental import pallas as pl
from jax.experimental.pallas import tpu as pltpu

HEAD = 64
GRID_W = 64
CHUNK = 64
NORM_EPS = 1e-6
GN_EPS = 64e-5
LANES = 128
SUBLANES = 8
ROWS_BF16 = 16
VMEM_LIMIT = 56 * 1024 * 1024

F32 = jnp.float32
BF16 = jnp.bfloat16


def _params(*sem):
    return pltpu.CompilerParams(dimension_semantics=sem, vmem_limit_bytes=VMEM_LIMIT)


def _tile(n, pref, mult):
    t = min(pref, n)
    t -= t % mult
    while t >= mult:
        if n % t == 0:
            return t
        t -= mult
    return n


def _sigmoid(x):
    return 1.0 / (1.0 + jnp.exp(-x))


def _norm_mod(x, g, shift, scale, cols=None):
    rs = lax.rsqrt(jnp.mean(x * x, axis=-1, keepdims=True) + NORM_EPS)
    if cols is not None:
        x, g, shift, scale = (t[:, cols[0]:cols[1]] for t in (x, g, shift, scale))
    return (x * rs) * (g * (1.0 + scale)) + shift


def _dot(a, b):
    return jnp.dot(a, b, preferred_element_type=F32)


def _dot_t(a, b, ca, cb):
    return lax.dot_general(a, b, (((ca,), (cb,)), ((), ())), preferred_element_type=F32)


def _ada_kernel(c_ref, w_ref, b_ref, o_ref):
    c = c_ref[...]
    s = c * _sigmoid(c)
    o_ref[...] = _dot(s.astype(BF16), w_ref[0].astype(BF16)) + b_ref[0]


def _ada(cond, w, b, layer):
    R, D = cond.shape
    N = w.shape[2]
    tn = _tile(N, 1024, LANES)
    return pl.pallas_call(
        _ada_kernel,
        grid=(N // tn,),
        in_specs=[pl.BlockSpec((R, D), lambda j: (0, 0)),
                  pl.BlockSpec((1, D, tn), lambda j: (layer, 0, j)),
                  pl.BlockSpec((1, 1, tn), lambda j: (layer, 0, j))],
        out_specs=pl.BlockSpec((R, tn), lambda j: (0, j)),
        out_shape=jax.ShapeDtypeStruct((R, N), F32),
        compiler_params=_params("parallel"),
        name="ada_mod",
    )(cond, w, b[:, None, :])


def _norm_kernel(x_ref, g_ref, sh_ref, sc_ref, o_ref):
    o_ref[0] = _norm_mod(x_ref[0], g_ref[...], sh_ref[0], sc_ref[0])


def _bsel(arr):
    if arr.shape[0] == 1:
        return lambda b, *_: (0, 0, 0)
    return lambda b, *_: (b, 0, 0)


def _norm(x, g, shift, scale):
    Bn, T, D = x.shape
    tm = _tile(T, 512, SUBLANES)
    return pl.pallas_call(
        _norm_kernel,
        grid=(Bn, T // tm),
        in_specs=[pl.BlockSpec((1, tm, D), lambda b, i: (b, i, 0)),
                  pl.BlockSpec((1, D), lambda b, i: (0, 0)),
                  pl.BlockSpec((1, 1, D), _bsel(shift)),
                  pl.BlockSpec((1, 1, D), _bsel(scale))],
        out_specs=pl.BlockSpec((1, tm, D), lambda b, i: (b, i, 0)),
        out_shape=jax.ShapeDtypeStruct((Bn, T, D), F32),
        compiler_params=_params("parallel", "parallel"),
        name="norm",
    )(x, g[None], shift, scale)


def _write_mix(out_refs, mix_ref, h, shifted, c0, c1):
    xx = shifted - h
    for m, o_ref in enumerate(out_refs):
        o_ref[0, :, c0:c1] = (h + xx * mix_ref[m:m + 1, c0:c1]).astype(o_ref.dtype)


def _prep_latent_kernel(x_ref, xu_ref, xd_ref, g_ref, sh_ref, sc_ref, mix_ref, w1_ref, a1_ref, g1_ref,
                        xr_ref, xk_ref, xv_ref, tw_ref, aw_ref, gs_ref, h_scr, xm_scr):
    i = pl.program_id(1)
    n = pl.num_programs(1)
    g, sh, sc = g_ref[...], sh_ref[0], sc_ref[0]
    tm, D = x_ref.shape[1:]
    q = D // 4
    W = GRID_W
    h_scr[0:W, 2 * q:3 * q] = jnp.where(i > 0, _norm_mod(xu_ref[0], g, sh, sc, (2 * q, 3 * q)), 0.0)
    h_scr[W - SUBLANES:W, 0:q] = jnp.zeros((SUBLANES, q), F32)
    for r0 in range(0, tm, W):
        h_scr[W + r0:2 * W + r0, :] = _norm_mod(x_ref[0, r0:r0 + W, :], g, sh, sc)
    h_scr[W + tm:, 3 * q:] = jnp.where(i < n - 1, _norm_mod(xd_ref[0], g, sh, sc, (3 * q, D)), 0.0)
    h_scr[W + tm:W + tm + SUBLANES, q:2 * q] = jnp.zeros((SUBLANES, q), F32)
    wide = {0: xr_ref, 2: xk_ref, 3: xv_ref}
    lora = {1: 0, 4: 1, 5: 2}
    lora_w = (w1_ref, a1_ref, g1_ref)
    acc = [None] * 3
    R = 32
    row = lax.broadcasted_iota(jnp.int32, (R, 1), 0)
    for k, off in enumerate((-1, 1, -W, W)):
        c0, c1 = k * q, (k + 1) * q
        mixk = [mix_ref[m:m + 1, c0:c1] for m in range(6)]
        for r0 in range(0, tm, R):
            h = h_scr[W + r0:W + r0 + R, c0:c1]
            s = h_scr[W + r0 + off:W + r0 + off + R, c0:c1]
            if off == -1 and r0 % W == 0:
                s = jnp.where(row == 0, 0.0, s)
            if off == 1 and (r0 + R) % W == 0:
                s = jnp.where(row == R - 1, 0.0, s)
            xx = s - h
            for m in range(6):
                xm = (h + xx * mixk[m]).astype(BF16)
                if m in wide:
                    wide[m][0, r0:r0 + R, c0:c1] = xm
                else:
                    xm_scr[lora[m], r0:r0 + R, c0:c1] = xm
        for l in range(3):
            part = _dot(xm_scr[l, :, c0:c1], lora_w[l][c0:c1, :])
            acc[l] = part if acc[l] is None else acc[l] + part
    tw_ref[0] = jnp.tanh(acc[0]).astype(tw_ref.dtype)
    aw_ref[0] = acc[1].astype(aw_ref.dtype)
    gs_ref[0] = _sigmoid(acc[2]).astype(gs_ref.dtype)


def _prep_latent(x, g, shift, scale, mix, w1cat, a1cat, lg1):
    Bn, T, D = x.shape
    rows_per_tile = _tile(T // GRID_W, 4, 1)
    tm = rows_per_tile * GRID_W
    nrow = T // GRID_W
    tok = pl.BlockSpec((1, tm, D), lambda b, i: (b, i, 0))
    full = lambda arr: pl.BlockSpec(arr.shape, lambda b, i: (0,) * arr.ndim)
    small = lambda arr: pl.BlockSpec((1, tm, arr.shape[1]), lambda b, i: (b, i, 0))
    return pl.pallas_call(
        _prep_latent_kernel,
        grid=(Bn, T // tm),
        in_specs=[tok,
                  pl.BlockSpec((1, GRID_W, D),
                               lambda b, i: (b, jnp.maximum(i * rows_per_tile - 1, 0), 0)),
                  pl.BlockSpec((1, GRID_W, D),
                               lambda b, i: (b, jnp.minimum((i + 1) * rows_per_tile, nrow - 1), 0)),
                  pl.BlockSpec((1, D), lambda b, i: (0, 0)),
                  pl.BlockSpec((1, 1, D), _bsel(shift)),
                  pl.BlockSpec((1, 1, D), _bsel(scale)),
                  pl.BlockSpec((6, D), lambda b, i: (0, 0)),
                  full(w1cat), full(a1cat), full(lg1)],
        out_specs=[tok, tok, tok, small(w1cat), small(a1cat), small(lg1)],
        out_shape=[jax.ShapeDtypeStruct((Bn, T, D), BF16)] * 3
        + [jax.ShapeDtypeStruct((Bn, T, w.shape[1]), BF16) for w in (w1cat, a1cat, lg1)],
        scratch_shapes=[pltpu.VMEM((tm + 2 * GRID_W, D), F32), pltpu.VMEM((3, tm, D), BF16)],
        compiler_params=_params("parallel", "parallel"),
        name="prep_latent",
    )(x, x, x, g[None], shift, scale, mix, w1cat, a1cat, lg1)


def _prep_ctx_kernel(x_ref, g_ref, sh_ref, sc_ref, mix_ref, *out_refs):
    h = _norm_mod(x_ref[0], g_ref[...], sh_ref[0], sc_ref[0])
    L, D = h.shape
    half = D // 2
    t = lax.broadcasted_iota(jnp.int32, (L, 1), 0)
    h0, h1 = h[:, :half], h[:, half:]
    prev = jnp.where(t == 0, 0.0, pltpu.roll(h0, 1, 0))
    nxt = jnp.where(t == L - 1, 0.0, pltpu.roll(h1, L - 1, 0))
    _write_mix(out_refs, mix_ref, h0, prev, 0, half)
    _write_mix(out_refs, mix_ref, h1, nxt, half, D)


def _prep_ctx(x, g, shift, scale, mix):
    Bn, L, D = x.shape
    return pl.pallas_call(
        _prep_ctx_kernel,
        grid=(Bn,),
        in_specs=[pl.BlockSpec((1, L, D), lambda b: (b, 0, 0)),
                  pl.BlockSpec((1, D), lambda b: (0, 0)),
                  pl.BlockSpec((1, 1, D), _bsel(shift)),
                  pl.BlockSpec((1, 1, D), _bsel(scale)),
                  pl.BlockSpec((6, D), lambda b: (0, 0))],
        out_specs=[pl.BlockSpec((1, L, D), lambda b: (b, 0, 0))] * 6,
        out_shape=[jax.ShapeDtypeStruct((Bn, L, D), BF16)] * 6,
        compiler_params=_params("parallel"),
        name="prep_ctx",
    )(x, g[None], shift, scale, mix)


def _mm_kernel(a_ref, w_ref, o_ref, *, act):
    acc = _dot(a_ref[0], w_ref[...])
    if act == "tanh":
        acc = jnp.tanh(acc)
    elif act == "sigmoid":
        acc = _sigmoid(acc)
    o_ref[0] = acc.astype(o_ref.dtype)


def _mm(a, w, out_dtype, act=None, name="mm"):
    Bn, T, K = a.shape
    N = w.shape[1]
    M = Bn * T
    tm = _tile(M, 1024, ROWS_BF16)
    tn = _tile(N, 2048, LANES)
    out = pl.pallas_call(
        functools.partial(_mm_kernel, act=act),
        grid=(1, M // tm, N // tn),
        in_specs=[pl.BlockSpec((1, tm, K), lambda b, i, j: (b, i, 0)),
                  pl.BlockSpec((K, tn), lambda b, i, j: (0, j))],
        out_specs=pl.BlockSpec((1, tm, tn), lambda b, i, j: (b, i, j)),
        out_shape=jax.ShapeDtypeStruct((1, M, N), out_dtype),
        compiler_params=_params("parallel", "parallel", "parallel"),
        name=name,
    )(a.reshape(1, M, K), w)
    return out.reshape(Bn, T, N)


def _mm_res_kernel(a_ref, w_ref, res_ref, gate_ref, *rest):
    ncast = (len(rest) - 1) // 2
    cast_in, o_ref, cast_out = rest[:ncast], rest[ncast], rest[ncast + 1:]
    o_ref[0] = res_ref[0] + gate_ref[0] * _dot(a_ref[0], w_ref[...])
    for src, dst in zip(cast_in, cast_out):
        dst[...] = src[...].astype(BF16)


def _mm_res(a, w, res, gate, name="mm_res", casts=()):
    Bn, T, K = a.shape
    N = w.shape[1]
    tm = _tile(T, 1024, ROWS_BF16)
    tn = _tile(N, 1024 if K <= 2048 else 512, LANES)
    ni, nj = T // tm, N // tn
    steps = Bn * ni * nj
    gsel = _bsel(gate)
    step = lambda b, i, j: (b * ni + i) * nj + j
    cast_specs = []
    for cw in casts:
        rows = cw.shape[0] // steps
        assert rows * steps == cw.shape[0] and rows % ROWS_BF16 == 0, (cw.shape, steps)
        cast_specs.append(pl.BlockSpec((rows, cw.shape[1]), lambda b, i, j: (step(b, i, j), 0)))
    tile = pl.BlockSpec((1, tm, tn), lambda b, i, j: (b, i, j))
    outs = pl.pallas_call(
        _mm_res_kernel,
        grid=(Bn, ni, nj),
        in_specs=[pl.BlockSpec((1, tm, K), lambda b, i, j: (b, i, 0)),
                  pl.BlockSpec((K, tn), lambda b, i, j: (0, j)),
                  tile,
                  pl.BlockSpec((1, 1, tn), lambda b, i, j: gsel(b)[:2] + (j,))] + cast_specs,
        out_specs=[tile] + cast_specs,
        out_shape=[jax.ShapeDtypeStruct((Bn, T, N), F32)]
        + [jax.ShapeDtypeStruct(cw.shape, BF16) for cw in casts],
        compiler_params=_params("arbitrary", "arbitrary", "arbitrary"),
        name=name,
    )(a, w, res, gate, *casts)
    return outs[0], tuple(outs[1:])


def _mm_res_norm_kernel(a_ref, w_ref, res_ref, gate_ref, g_ref, sh_ref, sc_ref, o_ref, h_ref, *, rows):
    for r0 in range(0, a_ref.shape[1], rows):
        rs = slice(r0, r0 + rows)
        x1 = res_ref[0, rs, :] + gate_ref[0] * _dot(a_ref[0, rs, :], w_ref[...])
        o_ref[0, rs, :] = x1
        h_ref[0, rs, :] = _norm_mod(x1, g_ref[...], sh_ref[0], sc_ref[0]).astype(h_ref.dtype)


def _mm_res_norm(a, w, res, gate, g, shift, scale, name):
    Bn, T, K = a.shape
    N = w.shape[1]
    tm = _tile(T, 512, ROWS_BF16)
    rows = _tile(tm, 256, ROWS_BF16)
    row = pl.BlockSpec((1, tm, N), lambda b, i: (b, i, 0))
    return pl.pallas_call(
        functools.partial(_mm_res_norm_kernel, rows=rows),
        grid=(Bn, T // tm),
        in_specs=[pl.BlockSpec((1, tm, K), lambda b, i: (b, i, 0)),
                  pl.BlockSpec((K, N), lambda b, i: (0, 0)),
                  row,
                  pl.BlockSpec((1, 1, N), _bsel(gate)),
                  pl.BlockSpec((1, N), lambda b, i: (0, 0)),
                  pl.BlockSpec((1, 1, N), _bsel(shift)),
                  pl.BlockSpec((1, 1, N), _bsel(scale))],
        out_specs=[row, row],
        out_shape=[jax.ShapeDtypeStruct((Bn, T, N), F32), jax.ShapeDtypeStruct((Bn, T, N), BF16)],
        compiler_params=_params("parallel", "parallel"),
        name=name,
    )(a, w, res, gate, g[None], shift, scale)


def _ffn_up_kernel(h_ref, wa_ref, wb_ref, w2_ref, o_ref, w2o_ref, w_scr, *, rows):
    @pl.when((pl.program_id(1) == 0) & (pl.program_id(2) == 0))
    def _():
        w_scr[0] = wa_ref[0].astype(BF16)
        w_scr[1] = wb_ref[0].astype(BF16)

    for r0 in range(0, h_ref.shape[1], rows):
        rs = slice(r0, r0 + rows)
        h = h_ref[0, rs, :]
        a = _dot(h, w_scr[0])
        o_ref[0, rs, :] = (a * _sigmoid(a) * _dot(h, w_scr[1])).astype(o_ref.dtype)
    w2o_ref[...] = w2_ref[0].astype(BF16)


def _ffn_up(h, w13, w2, layer, name):
    Bn, T, D = h.shape
    F = w13.shape[2] // 2
    tm = _tile(T, 2048, ROWS_BF16)
    rows = _tile(tm, 512, ROWS_BF16)
    tn = _tile(F, 512, LANES)
    nj, ni = F // tn, T // tm
    steps = nj * Bn * ni
    F2, D2 = w2.shape[1:]
    rows2 = F2 // steps
    assert rows2 * steps == F2 and rows2 % ROWS_BF16 == 0, (F2, steps)
    step = lambda j, b, i: (j * Bn + b) * ni + i
    return pl.pallas_call(
        functools.partial(_ffn_up_kernel, rows=rows),
        grid=(nj, Bn, ni),
        in_specs=[pl.BlockSpec((1, tm, D), lambda j, b, i: (b, i, 0)),
                  pl.BlockSpec((1, D, tn), lambda j, b, i: (layer, 0, j)),
                  pl.BlockSpec((1, D, tn), lambda j, b, i: (layer, 0, j + nj)),
                  pl.BlockSpec((1, rows2, D2), lambda j, b, i: (layer, step(j, b, i), 0))],
        out_specs=[pl.BlockSpec((1, tm, tn), lambda j, b, i: (b, i, j)),
                   pl.BlockSpec((rows2, D2), lambda j, b, i: (step(j, b, i), 0))],
        out_shape=[jax.ShapeDtypeStruct((Bn, T, F), BF16), jax.ShapeDtypeStruct((F2, D2), BF16)],
        scratch_shapes=[pltpu.VMEM((2, D, tn), BF16)],
        compiler_params=_params("arbitrary", "arbitrary", "arbitrary"),
        name=name,
    )(h, w13, w13, w2)


def _conv_in_combine(gb, gc, u):
    return gb, gc * u


def _norm_mm_kernel(x_ref, g_ref, sh_ref, sc_ref, *rest, nw, combine, rows):
    w_refs, out_refs, h_scr = rest[:nw], rest[nw:-1], rest[-1]

    def emit(rs, h):
        outs = combine(*[_dot(h, w_ref[...]) for w_ref in w_refs])
        for o_ref, o in zip(out_refs, outs):
            o_ref[0, rs, :] = o.astype(o_ref.dtype)

    @pl.when(pl.program_id(2) == 0)
    def _():
        for r0 in range(0, x_ref.shape[1], rows):
            rs = slice(r0, r0 + rows)
            h = _norm_mod(x_ref[0, rs, :], g_ref[...], sh_ref[0], sc_ref[0]).astype(BF16)
            h_scr[rs, :] = h
            emit(rs, h)

    @pl.when(pl.program_id(2) > 0)
    def _():
        emit(slice(None), h_scr[...])


def _norm_mm(x, g, shift, scale, w, nw, combine, out_dtypes, name):
    Bn, T, D = x.shape
    N = w.shape[1] // nw
    tm = _tile(T, 1024, ROWS_BF16)
    rows = _tile(tm, 256, ROWS_BF16)
    tn = _tile(N, 512, LANES)
    nj = N // tn
    w_specs = [pl.BlockSpec((D, tn), functools.partial(lambda b, i, j, m: (0, j + m * nj), m=m))
               for m in range(nw)]
    return pl.pallas_call(
        functools.partial(_norm_mm_kernel, nw=nw, combine=combine, rows=rows),
        grid=(Bn, T // tm, nj),
        in_specs=[pl.BlockSpec((1, tm, D), lambda b, i, j: (b, i, 0)),
                  pl.BlockSpec((1, D), lambda b, i, j: (0, 0)),
                  pl.BlockSpec((1, 1, D), _bsel(shift)),
                  pl.BlockSpec((1, 1, D), _bsel(scale))] + w_specs,
        out_specs=[pl.BlockSpec((1, tm, tn), lambda b, i, j: (b, i, j))] * len(out_dtypes),
        out_shape=[jax.ShapeDtypeStruct((Bn, T, N), dt) for dt in out_dtypes],
        scratch_shapes=[pltpu.VMEM((tm, D), BF16)],
        compiler_params=_params("parallel", "parallel", "arbitrary"),
        name=name,
    )(x, g[None], shift, scale, *([w] * nw))


def _seg_ones(width):
    shift = HEAD.bit_length() - 1
    r = lax.shift_right_logical(lax.broadcasted_iota(jnp.int32, (width, width), 0), shift)
    c = lax.shift_right_logical(lax.broadcasted_iota(jnp.int32, (width, width), 1), shift)
    return (r == c).astype(F32)


def _split3(x):
    hi = x.astype(BF16)
    r1 = x - hi.astype(F32)
    mid = r1.astype(BF16)
    lo = (r1 - mid.astype(F32)).astype(BF16)
    return hi, mid, lo


def _wkv_kernel(r_ref, k_ref, v_ref, tw_ref, aw_ref, w2_ref, a2_ref, w0_ref, a0_ref, kk_ref, ka_ref,
                s0_ref, y_ref, sout_ref, s_scr, x_scr, r2_scr, bv_scr, vb_scr, z_scr, wt_scr,
                *, reverse, npair):
    c = pl.program_id(2)
    C = CHUNK
    PW = 2 * HEAD

    @pl.when(c == 0)
    def _():
        s_scr[...] = s0_ref[0]
        x_scr[...] = jnp.zeros_like(x_scr)
        r2_scr[...] = jnp.zeros_like(r2_scr)
        bv_scr[...] = jnp.zeros_like(bv_scr)
        vb_scr[...] = jnp.zeros_like(vb_scr)
        z_scr[...] = jnp.zeros_like(z_scr)
        wt_scr[...] = jnp.ones_like(wt_scr)

    rr = lax.shift_right_logical(lax.broadcasted_iota(jnp.int32, (PW, PW), 0), HEAD.bit_length() - 1)
    cc = lax.shift_right_logical(lax.broadcasted_iota(jnp.int32, (PW, PW), 1), HEAD.bit_length() - 1)
    same = rr == cc
    same_bf = same.astype(BF16)

    def bd(x):
        xb = x.astype(BF16)
        return jnp.concatenate([xb, xb], axis=0) * same_bf

    t2 = lax.broadcasted_iota(jnp.int32, (C, PW), 0)
    s2 = lax.broadcasted_iota(jnp.int32, (C, PW), 1) & (HEAD - 1)
    before = (s2 > t2) if reverse else (s2 < t2)
    upto = before | (s2 == t2)
    pairs = range(npair)
    sls = [slice(p * PW, (p + 1) * PW) for p in pairs]

    r, k, v = (t[0].astype(F32) for t in (r_ref, k_ref, v_ref))
    z = w0_ref[...] + _dot(tw_ref[0], w2_ref[...])
    a_pre = a0_ref[...] + _dot(aw_ref[0], a2_ref[...])

    S = [s_scr[p] for p in pairs]
    X = [x_scr[p] for p in pairs]
    G = [_dot_t(X[p], r2_scr[p], 1, 1) for p in pairs]
    XS = [_dot_t(X[p], S[p].astype(BF16), 1, 1) for p in pairs]
    u = [XS[p][:C] + _dot(jnp.where(before, G[p][:C, PW:], 0.0).astype(BF16), bv_scr[p]) for p in pairs]
    P = [jnp.where(before, G[p][:C, :PW], 0.0) for p in pairs]

    lw = -math.exp(-0.5) * _sigmoid(z)
    kkv = k * kk_ref[...]
    kk2 = kkv * kkv
    t_i = lax.broadcasted_iota(jnp.int32, (C, C), 0)
    s_i = lax.broadcasted_iota(jnp.int32, (C, C), 1)
    tri = ((s_i >= t_i) if reverse else (s_i <= t_i)).astype(BF16)
    cum = _dot(jnp.concatenate([tri, tri], axis=1),
               jnp.concatenate(_split3(lw)[:2], axis=0))
    ss = [_dot(kk2[:, sl].astype(BF16), same_bf) for sl in sls]

    n_sq = C.bit_length() - 1
    for j in range(n_sq):
        Pb = [P[p].astype(BF16) for p in pairs]
        if j < n_sq - 1:
            PU = [_dot(Pb[p], jnp.concatenate([bd(P[p]), bd(u[p])], axis=1)) for p in pairs]
            P = [PU[p][:, :PW] for p in pairs]
            u = [u[p] + PU[p][:, PW:] for p in pairs]
        else:
            u = [u[p] + _dot(Pb[p], bd(u[p])) for p in pairs]
    for p in pairs:
        R = jnp.concatenate([jnp.where(upto, G[p][C:, :PW], 0.0),
                             jnp.where(upto, G[p][C:, PW:], 0.0)], axis=1).astype(BF16)
        y_ref[0, :, sls[p]] = XS[p][C:] + _dot(R, jnp.concatenate([bd(u[p]), bv_scr[p]], axis=0))
    for p in pairs:
        UV = jnp.concatenate([u[p].astype(BF16), vb_scr[p]], axis=0)
        dS = _dot_t(UV, z_scr[p], 0, 0)
        s_scr[p] = (S[p] + jnp.where(same, dS, 0.0)) * wt_scr[p, 0:1, :]

    a_sig = _sigmoid(a_pre)
    kd = k * (1.0 + (a_sig - 1.0) * ka_ref[...])
    e_pos = jnp.exp(cum)
    e_neg = jnp.exp(-cum)
    e_prev = jnp.exp(cum - lw)
    last = 0 if reverse else C - 1
    for p, sl in zip(pairs, sls):
        kkn = kkv[:, sl] * lax.rsqrt(jnp.maximum(ss[p], 1e-24))
        at = (-kkn) * e_prev[:, sl]
        bt = (kkn * a_sig[:, sl]) * e_neg[:, sl]
        rt = r[:, sl] * e_pos[:, sl]
        kt = kd[:, sl] * e_neg[:, sl]
        x_scr[p] = jnp.concatenate([at, rt], axis=0).astype(BF16)
        r2_scr[p] = jnp.concatenate([bd(bt), bd(kt)], axis=0)
        bv_scr[p] = bd(v[:, sl])
        vb_scr[p] = v[:, sl].astype(BF16)
        z_scr[p] = jnp.concatenate([bt, kt], axis=0).astype(BF16)
        wt_scr[p] = jnp.broadcast_to(e_pos[last:last + 1, sl], wt_scr.shape[1:])

    @pl.when(c == pl.num_programs(2) - 1)
    def _():
        sout_ref[0] = s_scr[...]


def _wkv(r, k, v, tw, aw, w2p, a2p, w0, a0, kk, ka, s0, d):
    Bn, T, D = r.shape
    PW = 2 * HEAD
    npairs = D // PW
    npair = _tile(npairs, 16, 1)
    hw = npair * PW
    nc = T // CHUNK
    reverse = d == 1
    pos = (lambda j: nc - 1 - j) if reverse else (lambda j: j)
    cin = lambda c: pos(jnp.minimum(c, nc - 1))
    cout = lambda c: pos(jnp.maximum(c - 1, 0))
    tok = pl.BlockSpec((1, CHUNK, hw), lambda b, g, c: (b, cin(c), g))
    lora = pl.BlockSpec((1, CHUNK, LANES), lambda b, g, c: (b, cin(c), d))
    lw2 = pl.BlockSpec((LANES, hw), lambda b, g, c: (0, g))
    vec = pl.BlockSpec((1, hw), lambda b, g, c: (0, g))
    st = pl.BlockSpec((1, npair, PW, PW), lambda b, g, c: (b, g, 0, 0))
    return pl.pallas_call(
        functools.partial(_wkv_kernel, reverse=reverse, npair=npair),
        grid=(Bn, npairs // npair, nc + 1),
        in_specs=[tok, tok, tok, lora, lora, lw2, lw2, vec, vec, vec, vec, st],
        out_specs=[pl.BlockSpec((1, CHUNK, hw), lambda b, g, c: (b, cout(c), g)), st],
        out_shape=[jax.ShapeDtypeStruct((Bn, T, D), F32),
                   jax.ShapeDtypeStruct((Bn, npairs, PW, PW), F32)],
        scratch_shapes=[pltpu.VMEM((npair, PW, PW), F32),
                        pltpu.VMEM((npair, 2 * CHUNK, PW), BF16),
                        pltpu.VMEM((npair, 2 * PW, PW), BF16),
                        pltpu.VMEM((npair, PW, PW), BF16),
                        pltpu.VMEM((npair, CHUNK, PW), BF16),
                        pltpu.VMEM((npair, 2 * CHUNK, PW), BF16),
                        pltpu.VMEM((npair, SUBLANES, PW), F32)],
        compiler_params=_params("parallel", "parallel", "arbitrary"),
        name="wkv_rev" if reverse else "wkv_fwd",
    )(r, k, v, tw, aw, w2p, a2p, w0, a0, kk, ka, s0)


def _rwkv_out_kernel(yf_ref, yb_ref, r_ref, k_ref, v_ref, gs_ref, aw_ref, res_ref, a2f_ref, a2b_ref,
                     g2_ref, wo_ref, a0_ref, ka_ref, rk_ref, lnw_ref, lnb_ref, gate_ref, g_ref, sh_ref,
                     sc_ref, o_ref, h_ref, og_scr, *, rows):
    tm, D = o_ref.shape[1:]
    PW = 2 * HEAD
    same = _seg_ones(PW).astype(BF16)
    same2 = jnp.concatenate([same, same], axis=0)

    def head_sum(x, pieces):
        if pieces == 1:
            return _dot(x.astype(BF16), same)
        return _dot(jnp.concatenate(_split3(x)[:2], axis=1), same2)

    for r0 in range(0, tm, rows):
        rs = slice(r0, r0 + rows)
        aw = aw_ref[0, rs, :]
        a_f = _sigmoid(a0_ref[0:1, :] + _dot(aw[:, :LANES], a2f_ref[...]))
        a_b = _sigmoid(a0_ref[1:2, :] + _dot(aw[:, LANES:], a2b_ref[...]))
        g = _dot(gs_ref[0, rs, :], g2_ref[...])
        for p in range(D // PW):
            sl = slice(p * PW, (p + 1) * PW)
            ksum = k_ref[0, rs, sl].astype(F32) * (2.0 + (a_f[:, sl] + a_b[:, sl] - 2.0) * ka_ref[:, sl])
            y = yf_ref[0, rs, sl] + yb_ref[0, rs, sl]
            yc = y - head_sum(y, 2) * (1.0 / HEAD)
            var = head_sum(yc * yc, 1) * (1.0 / HEAD)
            o = yc * lax.rsqrt(var + GN_EPS) * lnw_ref[:, sl] + lnb_ref[:, sl]
            bonus = (head_sum(r_ref[0, rs, sl].astype(F32) * ksum * rk_ref[:, sl], 1)
                     * v_ref[0, rs, sl].astype(F32))
            og_scr[rs, sl] = ((o + bonus) * g[:, sl]).astype(og_scr.dtype)
        x1 = res_ref[0, rs, :] + gate_ref[0] * _dot(og_scr[rs, :], wo_ref[...])
        o_ref[0, rs, :] = x1
        h_ref[0, rs, :] = _norm_mod(x1, g_ref[...], sh_ref[0], sc_ref[0]).astype(h_ref.dtype)


def _rwkv_out(yf, yb, r, k, v, gs, aw, res, a2fp, a2bp, g2, wo, a0, ka, rk, lnw, lnb, gate, g, shift,
              scale, name):
    Bn, T, D = res.shape
    tm = _tile(T, 256, ROWS_BF16)
    rows = _tile(tm, 256, ROWS_BF16)
    G = gs.shape[-1]
    row = pl.BlockSpec((1, tm, D), lambda b, i: (b, i, 0))
    vec = pl.BlockSpec((1, D), lambda b, i: (0, 0))
    full = lambda arr: pl.BlockSpec(arr.shape, lambda b, i: (0,) * arr.ndim)
    return pl.pallas_call(
        functools.partial(_rwkv_out_kernel, rows=rows),
        grid=(Bn, T // tm),
        in_specs=[row, row, row, row, row,
                  pl.BlockSpec((1, tm, G), lambda b, i: (b, i, 0)),
                  pl.BlockSpec((1, tm, 2 * LANES), lambda b, i: (b, i, 0)),
                  row, full(a2fp), full(a2bp), full(g2), full(wo), full(a0),
                  vec, vec, vec, vec,
                  pl.BlockSpec((1, 1, D), _bsel(gate)), vec,
                  pl.BlockSpec((1, 1, D), _bsel(shift)),
                  pl.BlockSpec((1, 1, D), _bsel(scale))],
        out_specs=[row, row],
        out_shape=[jax.ShapeDtypeStruct((Bn, T, D), F32), jax.ShapeDtypeStruct((Bn, T, D), BF16)],
        scratch_shapes=[pltpu.VMEM((tm, D), BF16)],
        compiler_params=_params("parallel", "parallel"),
        name=name,
    )(yf, yb, r, k, v, gs, aw, res, a2fp, a2bp, g2, wo, a0, ka, rk, lnw, lnb, gate, g[None], shift, scale)


def _conv_kernel(gb_ref, z_ref, cw_ref, o_ref):
    z = z_ref[0].astype(F32)
    T = z.shape[0]
    t = lax.broadcasted_iota(jnp.int32, (T, 1), 0)
    zp = jnp.where(t == 0, 0.0, pltpu.roll(z, 1, 0))
    zn = jnp.where(t == T - 1, 0.0, pltpu.roll(z, T - 1, 0))
    conv = zp * cw_ref[0:1, :] + z * cw_ref[1:2, :] + zn * cw_ref[2:3, :]
    o_ref[0] = (gb_ref[0] * conv).astype(o_ref.dtype)


def _conv(gb, z, cw):
    Bn, T, D = z.shape
    tn = _tile(D, 512, LANES)
    tok = pl.BlockSpec((1, T, tn), lambda b, j: (b, 0, j))
    return pl.pallas_call(
        _conv_kernel,
        grid=(Bn, D // tn),
        in_specs=[tok, tok, pl.BlockSpec((3, tn), lambda b, j: (0, j))],
        out_specs=tok,
        out_shape=jax.ShapeDtypeStruct((Bn, T, D), BF16),
        compiler_params=_params("parallel", "parallel"),
        name="short_conv",
    )(gb, z, cw)


def _pad_rows(w, rows):
    return jnp.pad(w, ((0, rows - w.shape[0]), (0, 0)))


def _pad_cols(w, cols):
    return jnp.pad(w, ((0, 0), (0, cols - w.shape[1])))


def _split_mod(mod_rows, D):
    return [mod_rows[:, m * D:(m + 1) * D][:, None, :] for m in range(6)]


def _ffn_branch(t1, h2, mods, ffn, tag):
    w13, w2, layer, casts = ffn
    act, wdn = _ffn_up(h2, w13, w2, layer, name="ffn_up_" + tag)
    return _mm_res(act, wdn, t1, mods[5], name="ffn_down_" + tag, casts=casts)


def _rwkv_layer(x, ctx, mods_x, mods_c, g1, g2n, mix, wr, wk, wv, wo, w0, w1, w2, a0, a1, a2,
                lg1, lg2, k_k, k_a, r_k, ln_w, ln_b, ffn):
    D = x.shape[-1]
    H = D // HEAD
    w1cat = jnp.concatenate([_pad_cols(w1[0], LANES), _pad_cols(w1[1], LANES)], axis=1).astype(BF16)
    a1cat = jnp.concatenate([_pad_cols(a1[0], LANES), _pad_cols(a1[1], LANES)], axis=1).astype(BF16)
    w2p = [_pad_rows(w2[d], LANES).astype(BF16) for d in range(2)]
    a2p = [_pad_rows(a2[d], LANES).astype(BF16) for d in range(2)]
    wr, wk, wv, wo = (t.astype(BF16) for t in (wr, wk, wv, wo))
    lg1, lg2 = lg1.astype(BF16), lg2.astype(BF16)
    rk = r_k.reshape(1, D)

    xr, xw, xk, xv, xa, xg = _prep_ctx(ctx, g1, mods_c[0], mods_c[1], mix)
    lora_c = dict(tw=_mm(xw, w1cat, BF16, act="tanh", name="lora_w_c"),
                  aw=_mm(xa, a1cat, BF16, name="lora_a_c"),
                  gs=_mm(xg, lg1, BF16, act="sigmoid", name="lora_g_c"))
    ins = {"c": (xr, xk, xv, lora_c["tw"], lora_c["aw"], lora_c["gs"]),
           "x": _prep_latent(x, g1, mods_x[0], mods_x[1], mix, w1cat, a1cat, lg1)}
    sets = {}
    for tag, (xr, xk, xv, tw, aw, gs) in ins.items():
        sets[tag] = dict(r=_mm(xr, wr, BF16, name="proj_r_" + tag),
                         k=_mm(xk, wk, BF16, name="proj_k_" + tag),
                         v=_mm(xv, wv, BF16, name="proj_v_" + tag), tw=tw, aw=aw, gs=gs)

    ys = {"c": [], "x": []}
    zero_state = jnp.zeros((x.shape[0], H // 2, 2 * HEAD, 2 * HEAD), F32)
    for d in range(2):
        state = zero_state
        for tag in ("c", "x"):
            s = sets[tag]
            y, state = _wkv(s["r"], s["k"], s["v"], s["tw"], s["aw"], w2p[d], a2p[d],
                            w0[d][None], a0[d][None], k_k[None], k_a[None], state, d)
            ys[tag].append(y)

    outs = []
    for tag, tok, mods in (("c", ctx, mods_c), ("x", x, mods_x)):
        s = sets[tag]
        t1, h2 = _rwkv_out(ys[tag][0], ys[tag][1], s["r"], s["k"], s["v"], s["gs"], s["aw"], tok,
                           a2p[0], a2p[1], lg2, wo, a0, k_a[None], rk, ln_w[None], ln_b[None],
                           mods[2], g2n, mods[3], mods[4], name="rwkv_out_" + tag)
        outs.append(_ffn_branch(t1, h2, mods, ffn, tag))
    (ctx_out, _), (x_out, cast_out) = outs
    return x_out, ctx_out, cast_out


def _conv_layer(x, mods, g1, g2n, w_in, conv_w, w_out, ffn):
    gb, z = _norm_mm(x, g1, mods[0], mods[1], w_in.astype(BF16), 3, _conv_in_combine, (BF16, BF16),
                     name="conv_in")
    p = _conv(gb, z, conv_w)
    t1, h2 = _mm_res_norm(p, w_out.astype(BF16), x, mods[2], g2n, mods[3], mods[4], name="conv_out")
    return _ffn_branch(t1, h2, mods, ffn, "x")


def kernel(x, c, ctx, c_ctx, norm1_g, norm2_g, ada_w, ada_b, rw_mix, rw_wr, rw_wk, rw_wv, rw_wo,
           rw_w0, rw_w1, rw_w2, rw_a0, rw_a1, rw_a2, rw_g1, rw_g2, rw_kk, rw_ka, rw_rk, rw_lnw,
           rw_lnb, sc_win, sc_conv, sc_wout, ffn_w13, ffn_w2, final_g):
    B, T, D = x.shape
    depth = norm1_g.shape[0]
    rows = -(-(B + 1) // 8) * 8
    cond = jnp.zeros((rows, D), F32).at[:B].set(c).at[B].set(c_ctx)
    conv_w = {}
    for i in range(depth):
        last = i == depth - 1
        j = i // 2
        mod = _ada(cond, ada_w, ada_b, i)
        mods_x = _split_mod(mod[:B], D)
        mods_c = _split_mod(mod[B:B + 1], D)
        nxt = (i + 1) // 2
        casts = (sc_win[nxt], sc_wout[nxt]) if (not last and i % 2 == 0) else ()
        ffn = (ffn_w13, ffn_w2, i, casts)
        if i % 2 == 0:
            x, ctx, cast_out = _rwkv_layer(
                x, ctx, mods_x, mods_c, norm1_g[i], norm2_g[i], rw_mix[j], rw_wr[j], rw_wk[j],
                rw_wv[j], rw_wo[j], rw_w0[j], rw_w1[j], rw_w2[j], rw_a0[j], rw_a1[j], rw_a2[j],
                rw_g1[j], rw_g2[j], rw_kk[j], rw_ka[j], rw_rk[j], rw_lnw[j], rw_lnb[j], ffn)
            if casts:
                conv_w[nxt] = cast_out
        else:
            w_in, w_out = conv_w.get(j, (sc_win[j], sc_wout[j]))
            if not last:
                ctx, _ = _conv_layer(ctx, mods_c, norm1_g[i], norm2_g[i], w_in, sc_conv[j], w_out,
                                     (ffn_w13, ffn_w2, i, ()))
            x, _ = _conv_layer(x, mods_x, norm1_g[i], norm2_g[i], w_in, sc_conv[j], w_out, ffn)
    zeros = jnp.zeros((1, 1, D), F32)
    return _norm(x, final_g, zeros, zeros)
```

```python
import functools
import math

import jax
import jax.numpy as jnp
from jax import lax
from jax.experimental import pallas as pl
from jax.experimental.pallas import tpu as pltpu

HEAD = 64
GRID_W = 64
CHUNK = 64
NORM_EPS = 1e-6
GN_EPS = 64e-5
LANES = 128
SUBLANES = 8
ROWS_BF16 = 16
VMEM_LIMIT = 56 * 1024 * 1024

F32 = jnp.float32
BF16 = jnp.bfloat16


def _params(*sem):
    return pltpu.CompilerParams(dimension_semantics=sem, vmem_limit_bytes=VMEM_LIMIT)


def _tile(n, pref, mult):
    t = min(pref, n)
    t -= t % mult
    while t >= mult:
        if n % t == 0:
            return t
        t -= mult
    return n


def _sigmoid(x):
    return 1.0 / (1.0 + jnp.exp(-x))


def _norm_mod(x, g, shift, scale, cols=None):
    rs = lax.rsqrt(jnp.mean(x * x, axis=-1, keepdims=True) + NORM_EPS)
    if cols is not None:
        x, g, shift, scale = (t[:, cols[0]:cols[1]] for t in (x, g, shift, scale))
    return (x * rs) * (g * (1.0 + scale)) + shift


def _dot(a, b):
    return jnp.dot(a, b, preferred_element_type=F32)


def _dot_t(a, b, ca, cb):
    return lax.dot_general(a, b, (((ca,), (cb,)), ((), ())), preferred_element_type=F32)


def _ada_kernel(c_ref, w_ref, b_ref, o_ref):
    c = c_ref[...]
    s = c * _sigmoid(c)
    o_ref[...] = _dot(s.astype(BF16), w_ref[0].astype(BF16)) + b_ref[0]


def _ada(cond, w, b, layer):
    R, D = cond.shape
    N = w.shape[2]
    tn = _tile(N, 1024, LANES)
    return pl.pallas_call(
        _ada_kernel,
        grid=(N // tn,),
        in_specs=[pl.BlockSpec((R, D), lambda j: (0, 0)),
                  pl.BlockSpec((1, D, tn), lambda j: (layer, 0, j)),
                  pl.BlockSpec((1, 1, tn), lambda j: (layer, 0, j))],
        out_specs=pl.BlockSpec((R, tn), lambda j: (0, j)),
        out_shape=jax.ShapeDtypeStruct((R, N), F32),
        compiler_params=_params("parallel"),
        name="ada_mod",
    )(cond, w, b[:, None, :])


def _norm_kernel(x_ref, g_ref, sh_ref, sc_ref, o_ref):
    o_ref[0] = _norm_mod(x_ref[0], g_ref[...], sh_ref[0], sc_ref[0])


def _bsel(arr):
    if arr.shape[0] == 1:
        return lambda b, *_: (0, 0, 0)
    return lambda b, *_: (b, 0, 0)


def _norm(x, g, shift, scale):
    Bn, T, D = x.shape
    tm = _tile(T, 512, SUBLANES)
    return pl.pallas_call(
        _norm_kernel,
        grid=(Bn, T // tm),
        in_specs=[pl.BlockSpec((1, tm, D), lambda b, i: (b, i, 0)),
                  pl.BlockSpec((1, D), lambda b, i: (0, 0)),
                  pl.BlockSpec((1, 1, D), _bsel(shift)),
                  pl.BlockSpec((1, 1, D), _bsel(scale))],
        out_specs=pl.BlockSpec((1, tm, D), lambda b, i: (b, i, 0)),
        out_shape=jax.ShapeDtypeStruct((Bn, T, D), F32),
        compiler_params=_params("parallel", "parallel"),
        name="norm",
    )(x, g[None], shift, scale)


def _write_mix(out_refs, mix_ref, h, shifted, c0, c1):
    xx = shifted - h
    for m, o_ref in enumerate(out_refs):
        o_ref[0, :, c0:c1] = (h + xx * mix_ref[m:m + 1, c0:c1]).astype(o_ref.dtype)


def _prep_latent_kernel(x_ref, xu_ref, xd_ref, g_ref, sh_ref, sc_ref, mix_ref, w1_ref, a1_ref, g1_ref,
                        xr_ref, xk_ref, xv_ref, tw_ref, aw_ref, gs_ref, h_scr, xm_scr):
    i = pl.program_id(1)
    n = pl.num_programs(1)
    g, sh, sc = g_ref[...], sh_ref[0], sc_ref[0]
    tm, D = x_ref.shape[1:]
    q = D // 4
    W = GRID_W
    h_scr[0:W, 2 * q:3 * q] = jnp.where(i > 0, _norm_mod(xu_ref[0], g, sh, sc, (2 * q, 3 * q)), 0.0)
    h_scr[W - SUBLANES:W, 0:q] = jnp.zeros((SUBLANES, q), F32)
    for r0 in range(0, tm, W):
        h_scr[W + r0:2 * W + r0, :] = _norm_mod(x_ref[0, r0:r0 + W, :], g, sh, sc)
    h_scr[W + tm:, 3 * q:] = jnp.where(i < n - 1, _norm_mod(xd_ref[0], g, sh, sc, (3 * q, D)), 0.0)
    h_scr[W + tm:W + tm + SUBLANES, q:2 * q] = jnp.zeros((SUBLANES, q), F32)
    wide = {0: xr_ref, 2: xk_ref, 3: xv_ref}
    lora = {1: 0, 4: 1, 5: 2}
    lora_w = (w1_ref, a1_ref, g1_ref)
    acc = [None] * 3
    R = 32
    row = lax.broadcasted_iota(jnp.int32, (R, 1), 0)
    for k, off in enumerate((-1, 1, -W, W)):
        c0, c1 = k * q, (k + 1) * q
        mixk = [mix_ref[m:m + 1, c0:c1] for m in range(6)]
        for r0 in range(0, tm, R):
            h = h_scr[W + r0:W + r0 + R, c0:c1]
            s = h_scr[W + r0 + off:W + r0 + off + R, c0:c1]
            if off == -1 and r0 % W == 0:
                s = jnp.where(row == 0, 0.0, s)
            if off == 1 and (r0 + R) % W == 0:
                s = jnp.where(row == R - 1, 0.0, s)
            xx = s - h
            for m in range(6):
                xm = (h + xx * mixk[m]).astype(BF16)
                if m in wide:
                    wide[m][0, r0:r0 + R, c0:c1] = xm
                else:
                    xm_scr[lora[m], r0:r0 + R, c0:c1] = xm
        for l in range(3):
            part = _dot(xm_scr[l, :, c0:c1], lora_w[l][c0:c1, :])
            acc[l] = part if acc[l] is None else acc[l] + part
    tw_ref[0] = jnp.tanh(acc[0]).astype(tw_ref.dtype)
    aw_ref[0] = acc[1].astype(aw_ref.dtype)
    gs_ref[0] = _sigmoid(acc[2]).astype(gs_ref.dtype)


def _prep_latent(x, g, shift, scale, mix, w1cat, a1cat, lg1):
    Bn, T, D = x.shape
    rows_per_tile = _tile(T // GRID_W, 4, 1)
    tm = rows_per_tile * GRID_W
    nrow = T // GRID_W
    tok = pl.BlockSpec((1, tm, D), lambda b, i: (b, i, 0))
    full = lambda arr: pl.BlockSpec(arr.shape, lambda b, i: (0,) * arr.ndim)
    small = lambda arr: pl.BlockSpec((1, tm, arr.shape[1]), lambda b, i: (b, i, 0))
    return pl.pallas_call(
        _prep_latent_kernel,
        grid=(Bn, T // tm),
        in_specs=[tok,
                  pl.BlockSpec((1, GRID_W, D),
                               lambda b, i: (b, jnp.maximum(i * rows_per_tile - 1, 0), 0)),
                  pl.BlockSpec((1, GRID_W, D),
                               lambda b, i: (b, jnp.minimum((i + 1) * rows_per_tile, nrow - 1), 0)),
                  pl.BlockSpec((1, D), lambda b, i: (0, 0)),
                  pl.BlockSpec((1, 1, D), _bsel(shift)),
                  pl.BlockSpec((1, 1, D), _bsel(scale)),
                  pl.BlockSpec((6, D), lambda b, i: (0, 0)),
                  full(w1cat), full(a1cat), full(lg1)],
        out_specs=[tok, tok, tok, small(w1cat), small(a1cat), small(lg1)],
        out_shape=[jax.ShapeDtypeStruct((Bn, T, D), BF16)] * 3
        + [jax.ShapeDtypeStruct((Bn, T, w.shape[1]), BF16) for w in (w1cat, a1cat, lg1)],
        scratch_shapes=[pltpu.VMEM((tm + 2 * GRID_W, D), F32), pltpu.VMEM((3, tm, D), BF16)],
        compiler_params=_params("parallel", "parallel"),
        name="prep_latent",
    )(x, x, x, g[None], shift, scale, mix, w1cat, a1cat, lg1)


def _prep_ctx_kernel(x_ref, g_ref, sh_ref, sc_ref, mix_ref, *out_refs):
    h = _norm_mod(x_ref[0], g_ref[...], sh_ref[0], sc_ref[0])
    L, D = h.shape
    half = D // 2
    t = lax.broadcasted_iota(jnp.int32, (L, 1), 0)
    h0, h1 = h[:, :half], h[:, half:]
    prev = jnp.where(t == 0, 0.0, pltpu.roll(h0, 1, 0))
    nxt = jnp.where(t == L - 1, 0.0, pltpu.roll(h1, L - 1, 0))
    _write_mix(out_refs, mix_ref, h0, prev, 0, half)
    _write_mix(out_refs, mix_ref, h1, nxt, half, D)


def _prep_ctx(x, g, shift, scale, mix):
    Bn, L, D = x.shape
    return pl.pallas_call(
        _prep_ctx_kernel,
        grid=(Bn,),
        in_specs=[pl.BlockSpec((1, L, D), lambda b: (b, 0, 0)),
                  pl.BlockSpec((1, D), lambda b: (0, 0)),
                  pl.BlockSpec((1, 1, D), _bsel(shift)),
                  pl.BlockSpec((1, 1, D), _bsel(scale)),
                  pl.BlockSpec((6, D), lambda b: (0, 0))],
        out_specs=[pl.BlockSpec((1, L, D), lambda b: (b, 0, 0))] * 6,
        out_shape=[jax.ShapeDtypeStruct((Bn, L, D), BF16)] * 6,
        compiler_params=_params("parallel"),
        name="prep_ctx",
    )(x, g[None], shift, scale, mix)


def _mm_kernel(a_ref, w_ref, o_ref, *, act):
    acc = _dot(a_ref[0], w_ref[...])
    if act == "tanh":
        acc = jnp.tanh(acc)
    elif act == "sigmoid":
        acc = _sigmoid(acc)
    o_ref[0] = acc.astype(o_ref.dtype)


def _mm(a, w, out_dtype, act=None, name="mm"):
    Bn, T, K = a.shape
    N = w.shape[1]
    M = Bn * T
    tm = _tile(M, 1024, ROWS_BF16)
    tn = _tile(N, 2048, LANES)
    out = pl.pallas_call(
        functools.partial(_mm_kernel, act=act),
        grid=(1, M // tm, N // tn),
        in_specs=[pl.BlockSpec((1, tm, K), lambda b, i, j: (b, i, 0)),
                  pl.BlockSpec((K, tn), lambda b, i, j: (0, j))],
        out_specs=pl.BlockSpec((1, tm, tn), lambda b, i, j: (b, i, j)),
        out_shape=jax.ShapeDtypeStruct((1, M, N), out_dtype),
        compiler_params=_params("parallel", "parallel", "parallel"),
        name=name,
    )(a.reshape(1, M, K), w)
    return out.reshape(Bn, T, N)


def _mm_res_kernel(a_ref, w_ref, res_ref, gate_ref, *rest):
    ncast = (len(rest) - 1) // 2
    cast_in, o_ref, cast_out = rest[:ncast], rest[ncast], rest[ncast + 1:]
    o_ref[0] = res_ref[0] + gate_ref[0] * _dot(a_ref[0], w_ref[...])
    for src, dst in zip(cast_in, cast_out):
        dst[...] = src[...].astype(BF16)


def _mm_res(a, w, res, gate, name="mm_res", casts=()):
    Bn, T, K = a.shape
    N = w.shape[1]
    tm = _tile(T, 1024, ROWS_BF16)
    tn = _tile(N, 1024 if K <= 2048 else 512, LANES)
    ni, nj = T // tm, N // tn
    steps = Bn * ni * nj
    gsel = _bsel(gate)
    step = lambda b, i, j: (b * ni + i) * nj + j
    cast_specs = []
    for cw in casts:
        rows = cw.shape[0] // steps
        assert rows * steps == cw.shape[0] and rows % ROWS_BF16 == 0, (cw.shape, steps)
        cast_specs.append(pl.BlockSpec((rows, cw.shape[1]), lambda b, i, j: (step(b, i, j), 0)))
    tile = pl.BlockSpec((1, tm, tn), lambda b, i, j: (b, i, j))
    outs = pl.pallas_call(
        _mm_res_kernel,
        grid=(Bn, ni, nj),
        in_specs=[pl.BlockSpec((1, tm, K), lambda b, i, j: (b, i, 0)),
                  pl.BlockSpec((K, tn), lambda b, i, j: (0, j)),
                  tile,
                  pl.BlockSpec((1, 1, tn), lambda b, i, j: gsel(b)[:2] + (j,))] + cast_specs,
        out_specs=[tile] + cast_specs,
        out_shape=[jax.ShapeDtypeStruct((Bn, T, N), F32)]
        + [jax.ShapeDtypeStruct(cw.shape, BF16) for cw in casts],
        compiler_params=_params("arbitrary", "arbitrary", "arbitrary"),
        name=name,
    )(a, w, res, gate, *casts)
    return outs[0], tuple(outs[1:])


def _mm_res_norm_kernel(a_ref, w_ref, res_ref, gate_ref, g_ref, sh_ref, sc_ref, o_ref, h_ref, *, rows):
    for r0 in range(0, a_ref.shape[1], rows):
        rs = slice(r0, r0 + rows)
        x1 = res_ref[0, rs, :] + gate_ref[0] * _dot(a_ref[0, rs, :], w_ref[...])
        o_ref[0, rs, :] = x1
        h_ref[0, rs, :] = _norm_mod(x1, g_ref[...], sh_ref[0], sc_ref[0]).astype(h_ref.dtype)


def _mm_res_norm(a, w, res, gate, g, shift, scale, name):
    Bn, T, K = a.shape
    N = w.shape[1]
    tm = _tile(T, 512, ROWS_BF16)
    rows = _tile(tm, 256, ROWS_BF16)
    row = pl.BlockSpec((1, tm, N), lambda b, i: (b, i, 0))
    return pl.pallas_call(
        functools.partial(_mm_res_norm_kernel, rows=rows),
        grid=(Bn, T // tm),
        in_specs=[pl.BlockSpec((1, tm, K), lambda b, i: (b, i, 0)),
                  pl.BlockSpec((K, N), lambda b, i: (0, 0)),
                  row,
                  pl.BlockSpec((1, 1, N), _bsel(gate)),
                  pl.BlockSpec((1, N), lambda b, i: (0, 0)),
                  pl.BlockSpec((1, 1, N), _bsel(shift)),
                  pl.BlockSpec((1, 1, N), _bsel(scale))],
        out_specs=[row, row],
        out_shape=[jax.ShapeDtypeStruct((Bn, T, N), F32), jax.ShapeDtypeStruct((Bn, T, N), BF16)],
        compiler_params=_params("parallel", "parallel"),
        name=name,
    )(a, w, res, gate, g[None], shift, scale)


def _ffn_up_kernel(h_ref, wa_ref, wb_ref, w2_ref, o_ref, w2o_ref, w_scr, *, rows):
    @pl.when((pl.program_id(1) == 0) & (pl.program_id(2) == 0))
    def _():
        w_scr[0] = wa_ref[0].astype(BF16)
        w_scr[1] = wb_ref[0].astype(BF16)

    for r0 in range(0, h_ref.shape[1], rows):
        rs = slice(r0, r0 + rows)
        h = h_ref[0, rs, :]
        a = _dot(h, w_scr[0])
        o_ref[0, rs, :] = (a * _sigmoid(a) * _dot(h, w_scr[1])).astype(o_ref.dtype)
    w2o_ref[...] = w2_ref[0].astype(BF16)


def _ffn_up(h, w13, w2, layer, name):
    Bn, T, D = h.shape
    F = w13.shape[2] // 2
    tm = _tile(T, 2048, ROWS_BF16)
    rows = _tile(tm, 512, ROWS_BF16)
    tn = _tile(F, 512, LANES)
    nj, ni = F // tn, T // tm
    steps = nj * Bn * ni
    F2, D2 = w2.shape[1:]
    rows2 = F2 // steps
    assert rows2 * steps == F2 and rows2 % ROWS_BF16 == 0, (F2, steps)
    step = lambda j, b, i: (j * Bn + b) * ni + i
    return pl.pallas_call(
        functools.partial(_ffn_up_kernel, rows=rows),
        grid=(nj, Bn, ni),
        in_specs=[pl.BlockSpec((1, tm, D), lambda j, b, i: (b, i, 0)),
                  pl.BlockSpec((1, D, tn), lambda j, b, i: (layer, 0, j)),
                  pl.BlockSpec((1, D, tn), lambda j, b, i: (layer, 0, j + nj)),
                  pl.BlockSpec((1, rows2, D2), lambda j, b, i: (layer, step(j, b, i), 0))],
        out_specs=[pl.BlockSpec((1, tm, tn), lambda j, b, i: (b, i, j)),
                   pl.BlockSpec((rows2, D2), lambda j, b, i: (step(j, b, i), 0))],
        out_shape=[jax.ShapeDtypeStruct((Bn, T, F), BF16), jax.ShapeDtypeStruct((F2, D2), BF16)],
        scratch_shapes=[pltpu.VMEM((2, D, tn), BF16)],
        compiler_params=_params("arbitrary", "arbitrary", "arbitrary"),
        name=name,
    )(h, w13, w13, w2)


def _conv_in_combine(gb, gc, u):
    return gb, gc * u


def _norm_mm_kernel(x_ref, g_ref, sh_ref, sc_ref, *rest, nw, combine, rows):
    w_refs, out_refs, h_scr = rest[:nw], rest[nw:-1], rest[-1]

    def emit(rs, h):
        outs = combine(*[_dot(h, w_ref[...]) for w_ref in w_refs])
        for o_ref, o in zip(out_refs, outs):
            o_ref[0, rs, :] = o.astype(o_ref.dtype)

    @pl.when(pl.program_id(2) == 0)
    def _():
        for r0 in range(0, x_ref.shape[1], rows):
            rs = slice(r0, r0 + rows)
            h = _norm_mod(x_ref[0, rs, :], g_ref[...], sh_ref[0], sc_ref[0]).astype(BF16)
            h_scr[rs, :] = h
            emit(rs, h)

    @pl.when(pl.program_id(2) > 0)
    def _():
        emit(slice(None), h_scr[...])


def _norm_mm(x, g, shift, scale, w, nw, combine, out_dtypes, name):
    Bn, T, D = x.shape
    N = w.shape[1] // nw
    tm = _tile(T, 1024, ROWS_BF16)
    rows = _tile(tm, 256, ROWS_BF16)
    tn = _tile(N, 512, LANES)
    nj = N // tn
    w_specs = [pl.BlockSpec((D, tn), functools.partial(lambda b, i, j, m: (0, j + m * nj), m=m))
               for m in range(nw)]
    return pl.pallas_call(
        functools.partial(_norm_mm_kernel, nw=nw, combine=combine, rows=rows),
        grid=(Bn, T // tm, nj),
        in_specs=[pl.BlockSpec((1, tm, D), lambda b, i, j: (b, i, 0)),
                  pl.BlockSpec((1, D), lambda b, i, j: (0, 0)),
                  pl.BlockSpec((1, 1, D), _bsel(shift)),
                  pl.BlockSpec((1, 1, D), _bsel(scale))] + w_specs,
        out_specs=[pl.BlockSpec((1, tm, tn), lambda b, i, j: (b, i, j))] * len(out_dtypes),
        out_shape=[jax.ShapeDtypeStruct((Bn, T, N), dt) for dt in out_dtypes],
        scratch_shapes=[pltpu.VMEM((tm, D), BF16)],
        compiler_params=_params("parallel", "parallel", "arbitrary"),
        name=name,
    )(x, g[None], shift, scale, *([w] * nw))


def _seg_ones(width):
    shift = HEAD.bit_length() - 1
    r = lax.shift_right_logical(lax.broadcasted_iota(jnp.int32, (width, width), 0), shift)
    c = lax.shift_right_logical(lax.broadcasted_iota(jnp.int32, (width, width), 1), shift)
    return (r == c).astype(F32)


def _split3(x):
    hi = x.astype(BF16)
    r1 = x - hi.astype(F32)
    mid = r1.astype(BF16)
    lo = (r1 - mid.astype(F32)).astype(BF16)
    return hi, mid, lo


def _wkv_step(r_ref, k_ref, v_ref, tw_ref, aw_ref, w2_ref, a2_ref, w0_ref, a0_ref, kk_ref, ka_ref,
                s0_ref, y_ref, sout_ref, s_scr, x_scr, r2_scr, bv_scr, vb_scr, z_scr, wt_scr,
                *, reverse, npair, chain):
    C = CHUNK
    PW = 2 * HEAD

    rr = lax.shift_right_logical(lax.broadcasted_iota(jnp.int32, (PW, PW), 0), HEAD.bit_length() - 1)
    cc = lax.shift_right_logical(lax.broadcasted_iota(jnp.int32, (PW, PW), 1), HEAD.bit_length() - 1)
    same = rr == cc
    same_bf = same.astype(BF16)

    def bd(x):
        xb = x.astype(BF16)
        return jnp.concatenate([xb, xb], axis=0) * same_bf

    t2 = lax.broadcasted_iota(jnp.int32, (C, PW), 0)
    s2 = lax.broadcasted_iota(jnp.int32, (C, PW), 1) & (HEAD - 1)
    before = (s2 > t2) if reverse else (s2 < t2)
    upto = before | (s2 == t2)
    pairs = range(npair)
    sls = [slice(p * PW, (p + 1) * PW) for p in pairs]

    r, k, v = (t[0].astype(F32) for t in (r_ref, k_ref, v_ref))
    z = w0_ref[...] + _dot(tw_ref[0], w2_ref[...])
    a_pre = a0_ref[...] + _dot(aw_ref[0], a2_ref[...])

    if chain:
        S = [s_scr[p] for p in pairs]
        X = [x_scr[p] for p in pairs]
        G = [_dot_t(X[p], r2_scr[p], 1, 1) for p in pairs]
        XS = [_dot_t(X[p], S[p].astype(BF16), 1, 1) for p in pairs]
        u = [XS[p][:C] + _dot(jnp.where(before, G[p][:C, PW:], 0.0).astype(BF16), bv_scr[p]) for p in pairs]
        P = [jnp.where(before, G[p][:C, :PW], 0.0) for p in pairs]

    lw = -math.exp(-0.5) * _sigmoid(z)
    kkv = k * kk_ref[...]
    kk2 = kkv * kkv
    t_i = lax.broadcasted_iota(jnp.int32, (C, C), 0)
    s_i = lax.broadcasted_iota(jnp.int32, (C, C), 1)
    tri = ((s_i >= t_i) if reverse else (s_i <= t_i)).astype(BF16)
    cum = _dot(jnp.concatenate([tri, tri], axis=1),
               jnp.concatenate(_split3(lw)[:2], axis=0))
    ss = [_dot(kk2[:, sl].astype(BF16), same_bf) for sl in sls]

    if chain:
        n_sq = C.bit_length() - 1
        for j in range(n_sq):
            Pb = [P[p].astype(BF16) for p in pairs]
            if j < n_sq - 1:
                PU = [_dot(Pb[p], jnp.concatenate([bd(P[p]), bd(u[p])], axis=1)) for p in pairs]
                P = [PU[p][:, :PW] for p in pairs]
                u = [u[p] + PU[p][:, PW:] for p in pairs]
            else:
                u = [u[p] + _dot(Pb[p], bd(u[p])) for p in pairs]
        for p in pairs:
            R = jnp.concatenate([jnp.where(upto, G[p][C:, :PW], 0.0),
                                 jnp.where(upto, G[p][C:, PW:], 0.0)], axis=1).astype(BF16)
            y_ref[0, :, sls[p]] = XS[p][C:] + _dot(R, jnp.concatenate([bd(u[p]), bv_scr[p]], axis=0))
        for p in pairs:
            UV = jnp.concatenate([u[p].astype(BF16), vb_scr[p]], axis=0)
            dS = _dot_t(UV, z_scr[p], 0, 0)
            s_scr[p] = (S[p] + jnp.where(same, dS, 0.0)) * wt_scr[p, 0:1, :]

    a_sig = _sigmoid(a_pre)
    kd = k * (1.0 + (a_sig - 1.0) * ka_ref[...])
    e_pos = jnp.exp(cum)
    e_neg = jnp.exp(-cum)
    e_prev = jnp.exp(cum - lw)
    last = 0 if reverse else C - 1
    for p, sl in zip(pairs, sls):
        kkn = kkv[:, sl] * lax.rsqrt(jnp.maximum(ss[p], 1e-24))
        at = (-kkn) * e_prev[:, sl]
        bt = (kkn * a_sig[:, sl]) * e_neg[:, sl]
        rt = r[:, sl] * e_pos[:, sl]
        kt = kd[:, sl] * e_neg[:, sl]
        x_scr[p] = jnp.concatenate([at, rt], axis=0).astype(BF16)
        r2_scr[p] = jnp.concatenate([bd(bt), bd(kt)], axis=0)
        bv_scr[p] = bd(v[:, sl])
        vb_scr[p] = v[:, sl].astype(BF16)
        z_scr[p] = jnp.concatenate([bt, kt], axis=0).astype(BF16)
        wt_scr[p] = jnp.broadcast_to(e_pos[last:last + 1, sl], wt_scr.shape[1:])


def _wkv_kernel(*refs, reverse, npair):
    s0_ref, sout_ref, s_scr = refs[11], refs[13], refs[14]
    c = pl.program_id(2)

    @pl.when(c == 0)
    def _():
        s_scr[...] = s0_ref[0]
        _wkv_step(*refs, reverse=reverse, npair=npair, chain=False)

    @pl.when(c > 0)
    def _():
        _wkv_step(*refs, reverse=reverse, npair=npair, chain=True)

    @pl.when(c == pl.num_programs(2) - 1)
    def _():
        sout_ref[0] = s_scr[...]


def _wkv(r, k, v, tw, aw, w2p, a2p, w0, a0, kk, ka, s0, d):
    Bn, T, D = r.shape
    PW = 2 * HEAD
    npairs = D // PW
    npair = _tile(npairs, 16, 1)
    hw = npair * PW
    nc = T // CHUNK
    reverse = d == 1
    pos = (lambda j: nc - 1 - j) if reverse else (lambda j: j)
    cin = lambda c: pos(jnp.minimum(c, nc - 1))
    cout = lambda c: pos(jnp.maximum(c - 1, 0))
    tok = pl.BlockSpec((1, CHUNK, hw), lambda b, g, c: (b, cin(c), g))
    lora = pl.BlockSpec((1, CHUNK, LANES), lambda b, g, c: (b, cin(c), d))
    lw2 = pl.BlockSpec((LANES, hw), lambda b, g, c: (0, g))
    vec = pl.BlockSpec((1, hw), lambda b, g, c: (0, g))
    st = pl.BlockSpec((1, npair, PW, PW), lambda b, g, c: (b, g, 0, 0))
    return pl.pallas_call(
        functools.partial(_wkv_kernel, reverse=reverse, npair=npair),
        grid=(Bn, npairs // npair, nc + 1),
        in_specs=[tok, tok, tok, lora, lora, lw2, lw2, vec, vec, vec, vec, st],
        out_specs=[pl.BlockSpec((1, CHUNK, hw), lambda b, g, c: (b, cout(c), g)), st],
        out_shape=[jax.ShapeDtypeStruct((Bn, T, D), F32),
                   jax.ShapeDtypeStruct((Bn, npairs, PW, PW), F32)],
        scratch_shapes=[pltpu.VMEM((npair, PW, PW), F32),
                        pltpu.VMEM((npair, 2 * CHUNK, PW), BF16),
                        pltpu.VMEM((npair, 2 * PW, PW), BF16),
                        pltpu.VMEM((npair, PW, PW), BF16),
                        pltpu.VMEM((npair, CHUNK, PW), BF16),
                        pltpu.VMEM((npair, 2 * CHUNK, PW), BF16),
                        pltpu.VMEM((npair, SUBLANES, PW), F32)],
        compiler_params=_params("parallel", "parallel", "arbitrary"),
        name="wkv_rev" if reverse else "wkv_fwd",
    )(r, k, v, tw, aw, w2p, a2p, w0, a0, kk, ka, s0)


def _rwkv_out_kernel(yf_ref, yb_ref, r_ref, k_ref, v_ref, gs_ref, aw_ref, res_ref, a2f_ref, a2b_ref,
                     g2_ref, wo_ref, a0_ref, ka_ref, rk_ref, lnw_ref, lnb_ref, gate_ref, g_ref, sh_ref,
                     sc_ref, o_ref, h_ref, og_scr, *, rows):
    tm, D = o_ref.shape[1:]
    PW = 2 * HEAD
    same = _seg_ones(PW).astype(BF16)
    same2 = jnp.concatenate([same, same], axis=0)

    def head_sum(x, pieces):
        if pieces == 1:
            return _dot(x.astype(BF16), same)
        return _dot(jnp.concatenate(_split3(x)[:2], axis=1), same2)

    for r0 in range(0, tm, rows):
        rs = slice(r0, r0 + rows)
        aw = aw_ref[0, rs, :]
        a_f = _sigmoid(a0_ref[0:1, :] + _dot(aw[:, :LANES], a2f_ref[...]))
        a_b = _sigmoid(a0_ref[1:2, :] + _dot(aw[:, LANES:], a2b_ref[...]))
        g = _dot(gs_ref[0, rs, :], g2_ref[...])
        for p in range(D // PW):
            sl = slice(p * PW, (p + 1) * PW)
            ksum = k_ref[0, rs, sl].astype(F32) * (2.0 + (a_f[:, sl] + a_b[:, sl] - 2.0) * ka_ref[:, sl])
            y = yf_ref[0, rs, sl] + yb_ref[0, rs, sl]
            yc = y - head_sum(y, 2) * (1.0 / HEAD)
            var = head_sum(yc * yc, 1) * (1.0 / HEAD)
            o = yc * lax.rsqrt(var + GN_EPS) * lnw_ref[:, sl] + lnb_ref[:, sl]
            bonus = (head_sum(r_ref[0, rs, sl].astype(F32) * ksum * rk_ref[:, sl], 1)
                     * v_ref[0, rs, sl].astype(F32))
            og_scr[rs, sl] = ((o + bonus) * g[:, sl]).astype(og_scr.dtype)
        x1 = res_ref[0, rs, :] + gate_ref[0] * _dot(og_scr[rs, :], wo_ref[...])
        o_ref[0, rs, :] = x1
        h_ref[0, rs, :] = _norm_mod(x1, g_ref[...], sh_ref[0], sc_ref[0]).astype(h_ref.dtype)


def _rwkv_out(yf, yb, r, k, v, gs, aw, res, a2fp, a2bp, g2, wo, a0, ka, rk, lnw, lnb, gate, g, shift,
              scale, name):
    Bn, T, D = res.shape
    tm = _tile(T, 256, ROWS_BF16)
    rows = _tile(tm, 256, ROWS_BF16)
    G = gs.shape[-1]
    row = pl.BlockSpec((1, tm, D), lambda b, i: (b, i, 0))
    vec = pl.BlockSpec((1, D), lambda b, i: (0, 0))
    full = lambda arr: pl.BlockSpec(arr.shape, lambda b, i: (0,) * arr.ndim)
    return pl.pallas_call(
        functools.partial(_rwkv_out_kernel, rows=rows),
        grid=(Bn, T // tm),
        in_specs=[row, row, row, row, row,
                  pl.BlockSpec((1, tm, G), lambda b, i: (b, i, 0)),
                  pl.BlockSpec((1, tm, 2 * LANES), lambda b, i: (b, i, 0)),
                  row, full(a2fp), full(a2bp), full(g2), full(wo), full(a0),
                  vec, vec, vec, vec,
                  pl.BlockSpec((1, 1, D), _bsel(gate)), vec,
                  pl.BlockSpec((1, 1, D), _bsel(shift)),
                  pl.BlockSpec((1, 1, D), _bsel(scale))],
        out_specs=[row, row],
        out_shape=[jax.ShapeDtypeStruct((Bn, T, D), F32), jax.ShapeDtypeStruct((Bn, T, D), BF16)],
        scratch_shapes=[pltpu.VMEM((tm, D), BF16)],
        compiler_params=_params("parallel", "parallel"),
        name=name,
    )(yf, yb, r, k, v, gs, aw, res, a2fp, a2bp, g2, wo, a0, ka, rk, lnw, lnb, gate, g[None], shift, scale)


def _conv_kernel(gb_ref, z_ref, cw_ref, o_ref):
    z = z_ref[0].astype(F32)
    T = z.shape[0]
    t = lax.broadcasted_iota(jnp.int32, (T, 1), 0)
    zp = jnp.where(t == 0, 0.0, pltpu.roll(z, 1, 0))
    zn = jnp.where(t == T - 1, 0.0, pltpu.roll(z, T - 1, 0))
    conv = zp * cw_ref[0:1, :] + z * cw_ref[1:2, :] + zn * cw_ref[2:3, :]
    o_ref[0] = (gb_ref[0] * conv).astype(o_ref.dtype)


def _conv(gb, z, cw):
    Bn, T, D = z.shape
    tn = _tile(D, 512, LANES)
    tok = pl.BlockSpec((1, T, tn), lambda b, j: (b, 0, j))
    return pl.pallas_call(
        _conv_kernel,
        grid=(Bn, D // tn),
        in_specs=[tok, tok, pl.BlockSpec((3, tn), lambda b, j: (0, j))],
        out_specs=tok,
        out_shape=jax.ShapeDtypeStruct((Bn, T, D), BF16),
        compiler_params=_params("parallel", "parallel"),
        name="short_conv",
    )(gb, z, cw)


def _pad_rows(w, rows):
    return jnp.pad(w, ((0, rows - w.shape[0]), (0, 0)))


def _pad_cols(w, cols):
    return jnp.pad(w, ((0, 0), (0, cols - w.shape[1])))


def _split_mod(mod_rows, D):
    return [mod_rows[:, m * D:(m + 1) * D][:, None, :] for m in range(6)]


def _ffn_branch(t1, h2, mods, ffn, tag):
    w13, w2, layer, casts = ffn
    act, wdn = _ffn_up(h2, w13, w2, layer, name="ffn_up_" + tag)
    return _mm_res(act, wdn, t1, mods[5], name="ffn_down_" + tag, casts=casts)


def _rwkv_layer(x, ctx, mods_x, mods_c, g1, g2n, mix, wr, wk, wv, wo, w0, w1, w2, a0, a1, a2,
                lg1, lg2, k_k, k_a, r_k, ln_w, ln_b, ffn):
    D = x.shape[-1]
    H = D // HEAD
    w1cat = jnp.concatenate([_pad_cols(w1[0], LANES), _pad_cols(w1[1], LANES)], axis=1).astype(BF16)
    a1cat = jnp.concatenate([_pad_cols(a1[0], LANES), _pad_cols(a1[1], LANES)], axis=1).astype(BF16)
    w2p = [_pad_rows(w2[d], LANES).astype(BF16) for d in range(2)]
    a2p = [_pad_rows(a2[d], LANES).astype(BF16) for d in range(2)]
    wr, wk, wv, wo = (t.astype(BF16) for t in (wr, wk, wv, wo))
    lg1, lg2 = lg1.astype(BF16), lg2.astype(BF16)
    rk = r_k.reshape(1, D)

    xr, xw, xk, xv, xa, xg = _prep_ctx(ctx, g1, mods_c[0], mods_c[1], mix)
    lora_c = dict(tw=_mm(xw, w1cat, BF16, act="tanh", name="lora_w_c"),
                  aw=_mm(xa, a1cat, BF16, name="lora_a_c"),
                  gs=_mm(xg, lg1, BF16, act="sigmoid", name="lora_g_c"))
    ins = {"c": (xr, xk, xv, lora_c["tw"], lora_c["aw"], lora_c["gs"]),
           "x": _prep_latent(x, g1, mods_x[0], mods_x[1], mix, w1cat, a1cat, lg1)}
    sets = {}
    for tag, (xr, xk, xv, tw, aw, gs) in ins.items():
        sets[tag] = dict(r=_mm(xr, wr, BF16, name="proj_r_" + tag),
                         k=_mm(xk, wk, BF16, name="proj_k_" + tag),
                         v=_mm(xv, wv, BF16, name="proj_v_" + tag), tw=tw, aw=aw, gs=gs)

    ys = {"c": [], "x": []}
    zero_state = jnp.zeros((x.shape[0], H // 2, 2 * HEAD, 2 * HEAD), F32)
    for d in range(2):
        state = zero_state
        for tag in ("c", "x"):
            s = sets[tag]
            y, state = _wkv(s["r"], s["k"], s["v"], s["tw"], s["aw"], w2p[d], a2p[d],
                            w0[d][None], a0[d][None], k_k[None], k_a[None], state, d)
            ys[tag].append(y)

    outs = []
    for tag, tok, mods in (("c", ctx, mods_c), ("x", x, mods_x)):
        s = sets[tag]
        t1, h2 = _rwkv_out(ys[tag][0], ys[tag][1], s["r"], s["k"], s["v"], s["gs"], s["aw"], tok,
                           a2p[0], a2p[1], lg2, wo, a0, k_a[None], rk, ln_w[None], ln_b[None],
                           mods[2], g2n, mods[3], mods[4], name="rwkv_out_" + tag)
        outs.append(_ffn_branch(t1, h2, mods, ffn, tag))
    (ctx_out, _), (x_out, cast_out) = outs
    return x_out, ctx_out, cast_out


def _conv_layer(x, mods, g1, g2n, w_in, conv_w, w_out, ffn):
    gb, z = _norm_mm(x, g1, mods[0], mods[1], w_in.astype(BF16), 3, _conv_in_combine, (BF16, BF16),
                     name="conv_in")
    p = _conv(gb, z, conv_w)
    t1, h2 = _mm_res_norm(p, w_out.astype(BF16), x, mods[2], g2n, mods[3], mods[4], name="conv_out")
    return _ffn_branch(t1, h2, mods, ffn, "x")


def kernel(x, c, ctx, c_ctx, norm1_g, norm2_g, ada_w, ada_b, rw_mix, rw_wr, rw_wk, rw_wv, rw_wo,
           rw_w0, rw_w1, rw_w2, rw_a0, rw_a1, rw_a2, rw_g1, rw_g2, rw_kk, rw_ka, rw_rk, rw_lnw,
           rw_lnb, sc_win, sc_conv, sc_wout, ffn_w13, ffn_w2, final_g):
    B, T, D = x.shape
    depth = norm1_g.shape[0]
    rows = -(-(B + 1) // 8) * 8
    cond = jnp.zeros((rows, D), F32).at[:B].set(c).at[B].set(c_ctx)
    conv_w = {}
    for i in range(depth):
        last = i == depth - 1
        j = i // 2
        mod = _ada(cond, ada_w, ada_b, i)
        mods_x = _split_mod(mod[:B], D)
        mods_c = _split_mod(mod[B:B + 1], D)
        nxt = (i + 1) // 2
        casts = (sc_win[nxt], sc_wout[nxt]) if (not last and i % 2 == 0) else ()
        ffn = (ffn_w13, ffn_w2, i, casts)
        if i % 2 == 0:
            x, ctx, cast_out = _rwkv_layer(
                x, ctx, mods_x, mods_c, norm1_g[i], norm2_g[i], rw_mix[j], rw_wr[j], rw_wk[j],
                rw_wv[j], rw_wo[j], rw_w0[j], rw_w1[j], rw_w2[j], rw_a0[j], rw_a1[j], rw_a2[j],
                rw_g1[j], rw_g2[j], rw_kk[j], rw_ka[j], rw_rk[j], rw_lnw[j], rw_lnb[j], ffn)
            if casts:
                conv_w[nxt] = cast_out
        else:
            w_in, w_out = conv_w.get(j, (sc_win[j], sc_wout[j]))
            if not last:
                ctx, _ = _conv_layer(ctx, mods_c, norm1_g[i], norm2_g[i], w_in, sc_conv[j], w_out,
                                     (ffn_w13, ffn_w2, i, ()))
            x, _ = _conv_layer(x, mods_x, norm1_g[i], norm2_g[i], w_in, sc_conv[j], w_out, ffn)
    zeros = jnp.zeros((1, 1, D), F32)
    return _norm(x, final_g, zeros, zeros)
```

```python
import functools
import math

import jax
import jax.numpy as jnp
from jax import lax
from jax.experimental import pallas as pl
from jax.experimental.pallas import tpu as pltpu

HEAD = 64
GRID_W = 64
CHUNK = 64
NORM_EPS = 1e-6
GN_EPS = 64e-5
LANES = 128
SUBLANES = 8
ROWS_BF16 = 16
VMEM_LIMIT = 56 * 1024 * 1024

F32 = jnp.float32
BF16 = jnp.bfloat16


def _params(*sem):
    return pltpu.CompilerParams(dimension_semantics=sem, vmem_limit_bytes=VMEM_LIMIT)


def _tile(n, pref, mult):
    t = min(pref, n)
    t -= t % mult
    while t >= mult:
        if n % t == 0:
            return t
        t -= mult
    return n


def _sigmoid(x):
    return 1.0 / (1.0 + jnp.exp(-x))


def _norm_mod(x, g, shift, scale, cols=None):
    rs = lax.rsqrt(jnp.mean(x * x, axis=-1, keepdims=True) + NORM_EPS)
    if cols is not None:
        x, g, shift, scale = (t[:, cols[0]:cols[1]] for t in (x, g, shift, scale))
    return (x * rs) * (g * (1.0 + scale)) + shift


def _dot(a, b):
    return jnp.dot(a, b, preferred_element_type=F32)


def _dot_t(a, b, ca, cb):
    return lax.dot_general(a, b, (((ca,), (cb,)), ((), ())), preferred_element_type=F32)


def _ada_kernel(c_ref, w_ref, b_ref, o_ref):
    c = c_ref[...]
    s = c * _sigmoid(c)
    o_ref[...] = _dot(s.astype(BF16), w_ref[0].astype(BF16)) + b_ref[0]


def _ada(cond, w, b, layer):
    R, D = cond.shape
    N = w.shape[2]
    tn = _tile(N, 1024, LANES)
    return pl.pallas_call(
        _ada_kernel,
        grid=(N // tn,),
        in_specs=[pl.BlockSpec((R, D), lambda j: (0, 0)),
                  pl.BlockSpec((1, D, tn), lambda j: (layer, 0, j)),
                  pl.BlockSpec((1, 1, tn), lambda j: (layer, 0, j))],
        out_specs=pl.BlockSpec((R, tn), lambda j: (0, j)),
        out_shape=jax.ShapeDtypeStruct((R, N), F32),
        compiler_params=_params("parallel"),
        name="ada_mod",
    )(cond, w, b[:, None, :])


def _norm_kernel(x_ref, g_ref, sh_ref, sc_ref, o_ref):
    o_ref[0] = _norm_mod(x_ref[0], g_ref[...], sh_ref[0], sc_ref[0])


def _bsel(arr):
    if arr.shape[0] == 1:
        return lambda b, *_: (0, 0, 0)
    return lambda b, *_: (b, 0, 0)


def _norm(x, g, shift, scale):
    Bn, T, D = x.shape
    tm = _tile(T, 512, SUBLANES)
    return pl.pallas_call(
        _norm_kernel,
        grid=(Bn, T // tm),
        in_specs=[pl.BlockSpec((1, tm, D), lambda b, i: (b, i, 0)),
                  pl.BlockSpec((1, D), lambda b, i: (0, 0)),
                  pl.BlockSpec((1, 1, D), _bsel(shift)),
                  pl.BlockSpec((1, 1, D), _bsel(scale))],
        out_specs=pl.BlockSpec((1, tm, D), lambda b, i: (b, i, 0)),
        out_shape=jax.ShapeDtypeStruct((Bn, T, D), F32),
        compiler_params=_params("parallel", "parallel"),
        name="norm",
    )(x, g[None], shift, scale)


def _write_mix(out_refs, mix_ref, h, shifted, c0, c1):
    xx = shifted - h
    for m, o_ref in enumerate(out_refs):
        o_ref[0, :, c0:c1] = (h + xx * mix_ref[m:m + 1, c0:c1]).astype(o_ref.dtype)


def _prep_latent_kernel(x_ref, xu_ref, xd_ref, g_ref, sh_ref, sc_ref, mix_ref, w1_ref, a1_ref, g1_ref,
                        xr_ref, xk_ref, xv_ref, tw_ref, aw_ref, gs_ref, h_scr, xm_scr):
    i = pl.program_id(1)
    n = pl.num_programs(1)
    g, sh, sc = g_ref[...], sh_ref[0], sc_ref[0]
    tm, D = x_ref.shape[1:]
    q = D // 4
    W = GRID_W
    h_scr[0:W, 2 * q:3 * q] = jnp.where(i > 0, _norm_mod(xu_ref[0], g, sh, sc, (2 * q, 3 * q)), 0.0)
    h_scr[W - SUBLANES:W, 0:q] = jnp.zeros((SUBLANES, q), F32)
    for r0 in range(0, tm, W):
        h_scr[W + r0:2 * W + r0, :] = _norm_mod(x_ref[0, r0:r0 + W, :], g, sh, sc)
    h_scr[W + tm:, 3 * q:] = jnp.where(i < n - 1, _norm_mod(xd_ref[0], g, sh, sc, (3 * q, D)), 0.0)
    h_scr[W + tm:W + tm + SUBLANES, q:2 * q] = jnp.zeros((SUBLANES, q), F32)
    wide = {0: xr_ref, 2: xk_ref, 3: xv_ref}
    lora = {1: 0, 4: 1, 5: 2}
    lora_w = (w1_ref, a1_ref, g1_ref)
    acc = [None] * 3
    R = 32
    row = lax.broadcasted_iota(jnp.int32, (R, 1), 0)
    for k, off in enumerate((-1, 1, -W, W)):
        c0, c1 = k * q, (k + 1) * q
        mixk = [mix_ref[m:m + 1, c0:c1] for m in range(6)]
        for r0 in range(0, tm, R):
            h = h_scr[W + r0:W + r0 + R, c0:c1]
            s = h_scr[W + r0 + off:W + r0 + off + R, c0:c1]
            if off == -1 and r0 % W == 0:
                s = jnp.where(row == 0, 0.0, s)
            if off == 1 and (r0 + R) % W == 0:
                s = jnp.where(row == R - 1, 0.0, s)
            xx = s - h
            for m in range(6):
                xm = (h + xx * mixk[m]).astype(BF16)
                if m in wide:
                    wide[m][0, r0:r0 + R, c0:c1] = xm
                else:
                    xm_scr[lora[m], r0:r0 + R, c0:c1] = xm
        for l in range(3):
            part = _dot(xm_scr[l, :, c0:c1], lora_w[l][c0:c1, :])
            acc[l] = part if acc[l] is None else acc[l] + part
    tw_ref[0] = jnp.tanh(acc[0]).astype(tw_ref.dtype)
    aw_ref[0] = acc[1].astype(aw_ref.dtype)
    gs_ref[0] = _sigmoid(acc[2]).astype(gs_ref.dtype)


def _prep_latent(x, g, shift, scale, mix, w1cat, a1cat, lg1):
    Bn, T, D = x.shape
    rows_per_tile = _tile(T // GRID_W, 4, 1)
    tm = rows_per_tile * GRID_W
    nrow = T // GRID_W
    tok = pl.BlockSpec((1, tm, D), lambda b, i: (b, i, 0))
    full = lambda arr: pl.BlockSpec(arr.shape, lambda b, i: (0,) * arr.ndim)
    small = lambda arr: pl.BlockSpec((1, tm, arr.shape[1]), lambda b, i: (b, i, 0))
    return pl.pallas_call(
        _prep_latent_kernel,
        grid=(Bn, T // tm),
        in_specs=[tok,
                  pl.BlockSpec((1, GRID_W, D),
                               lambda b, i: (b, jnp.maximum(i * rows_per_tile - 1, 0), 0)),
                  pl.BlockSpec((1, GRID_W, D),
                               lambda b, i: (b, jnp.minimum((i + 1) * rows_per_tile, nrow - 1), 0)),
                  pl.BlockSpec((1, D), lambda b, i: (0, 0)),
                  pl.BlockSpec((1, 1, D), _bsel(shift)),
                  pl.BlockSpec((1, 1, D), _bsel(scale)),
                  pl.BlockSpec((6, D), lambda b, i: (0, 0)),
                  full(w1cat), full(a1cat), full(lg1)],
        out_specs=[tok, tok, tok, small(w1cat), small(a1cat), small(lg1)],
        out_shape=[jax.ShapeDtypeStruct((Bn, T, D), BF16)] * 3
        + [jax.ShapeDtypeStruct((Bn, T, w.shape[1]), BF16) for w in (w1cat, a1cat, lg1)],
        scratch_shapes=[pltpu.VMEM((tm + 2 * GRID_W, D), F32), pltpu.VMEM((3, tm, D), BF16)],
        compiler_params=_params("parallel", "parallel"),
        name="prep_latent",
    )(x, x, x, g[None], shift, scale, mix, w1cat, a1cat, lg1)


def _prep_ctx_kernel(x_ref, g_ref, sh_ref, sc_ref, mix_ref, *out_refs):
    h = _norm_mod(x_ref[0], g_ref[...], sh_ref[0], sc_ref[0])
    L, D = h.shape
    half = D // 2
    t = lax.broadcasted_iota(jnp.int32, (L, 1), 0)
    h0, h1 = h[:, :half], h[:, half:]
    prev = jnp.where(t == 0, 0.0, pltpu.roll(h0, 1, 0))
    nxt = jnp.where(t == L - 1, 0.0, pltpu.roll(h1, L - 1, 0))
    _write_mix(out_refs, mix_ref, h0, prev, 0, half)
    _write_mix(out_refs, mix_ref, h1, nxt, half, D)


def _prep_ctx(x, g, shift, scale, mix):
    Bn, L, D = x.shape
    return pl.pallas_call(
        _prep_ctx_kernel,
        grid=(Bn,),
        in_specs=[pl.BlockSpec((1, L, D), lambda b: (b, 0, 0)),
                  pl.BlockSpec((1, D), lambda b: (0, 0)),
                  pl.BlockSpec((1, 1, D), _bsel(shift)),
                  pl.BlockSpec((1, 1, D), _bsel(scale)),
                  pl.BlockSpec((6, D), lambda b: (0, 0))],
        out_specs=[pl.BlockSpec((1, L, D), lambda b: (b, 0, 0))] * 6,
        out_shape=[jax.ShapeDtypeStruct((Bn, L, D), BF16)] * 6,
        compiler_params=_params("parallel"),
        name="prep_ctx",
    )(x, g[None], shift, scale, mix)


def _mm_kernel(a_ref, w_ref, o_ref, *, act):
    acc = _dot(a_ref[0], w_ref[...])
    if act == "tanh":
        acc = jnp.tanh(acc)
    elif act == "sigmoid":
        acc = _sigmoid(acc)
    o_ref[0] = acc.astype(o_ref.dtype)


def _mm(a, w, out_dtype, act=None, name="mm"):
    Bn, T, K = a.shape
    N = w.shape[1]
    M = Bn * T
    tm = _tile(M, 1024, ROWS_BF16)
    tn = _tile(N, 2048, LANES)
    out = pl.pallas_call(
        functools.partial(_mm_kernel, act=act),
        grid=(1, M // tm, N // tn),
        in_specs=[pl.BlockSpec((1, tm, K), lambda b, i, j: (b, i, 0)),
                  pl.BlockSpec((K, tn), lambda b, i, j: (0, j))],
        out_specs=pl.BlockSpec((1, tm, tn), lambda b, i, j: (b, i, j)),
        out_shape=jax.ShapeDtypeStruct((1, M, N), out_dtype),
        compiler_params=_params("parallel", "parallel", "parallel"),
        name=name,
    )(a.reshape(1, M, K), w)
    return out.reshape(Bn, T, N)


def _mm_res_kernel(a_ref, w_ref, res_ref, gate_ref, *rest):
    ncast = (len(rest) - 1) // 2
    cast_in, o_ref, cast_out = rest[:ncast], rest[ncast], rest[ncast + 1:]
    o_ref[0] = res_ref[0] + gate_ref[0] * _dot(a_ref[0], w_ref[...])
    for src, dst in zip(cast_in, cast_out):
        dst[...] = src[...].astype(BF16)


def _mm_res(a, w, res, gate, name="mm_res", casts=()):
    Bn, T, K = a.shape
    N = w.shape[1]
    tm = _tile(T, 1024, ROWS_BF16)
    tn = _tile(N, 1024 if K <= 2048 else 512, LANES)
    ni, nj = T // tm, N // tn
    steps = Bn * ni * nj
    gsel = _bsel(gate)
    step = lambda b, i, j: (b * ni + i) * nj + j
    cast_specs = []
    for cw in casts:
        rows = cw.shape[0] // steps
        assert rows * steps == cw.shape[0] and rows % ROWS_BF16 == 0, (cw.shape, steps)
        cast_specs.append(pl.BlockSpec((rows, cw.shape[1]), lambda b, i, j: (step(b, i, j), 0)))
    tile = pl.BlockSpec((1, tm, tn), lambda b, i, j: (b, i, j))
    outs = pl.pallas_call(
        _mm_res_kernel,
        grid=(Bn, ni, nj),
        in_specs=[pl.BlockSpec((1, tm, K), lambda b, i, j: (b, i, 0)),
                  pl.BlockSpec((K, tn), lambda b, i, j: (0, j)),
                  tile,
                  pl.BlockSpec((1, 1, tn), lambda b, i, j: gsel(b)[:2] + (j,))] + cast_specs,
        out_specs=[tile] + cast_specs,
        out_shape=[jax.ShapeDtypeStruct((Bn, T, N), F32)]
        + [jax.ShapeDtypeStruct(cw.shape, BF16) for cw in casts],
        compiler_params=_params("arbitrary", "arbitrary", "arbitrary"),
        name=name,
    )(a, w, res, gate, *casts)
    return outs[0], tuple(outs[1:])


def _mm_res_norm_kernel(a_ref, w_ref, res_ref, gate_ref, g_ref, sh_ref, sc_ref, o_ref, h_ref, *, rows):
    for r0 in range(0, a_ref.shape[1], rows):
        rs = slice(r0, r0 + rows)
        x1 = res_ref[0, rs, :] + gate_ref[0] * _dot(a_ref[0, rs, :], w_ref[...])
        o_ref[0, rs, :] = x1
        h_ref[0, rs, :] = _norm_mod(x1, g_ref[...], sh_ref[0], sc_ref[0]).astype(h_ref.dtype)


def _mm_res_norm(a, w, res, gate, g, shift, scale, name):
    Bn, T, K = a.shape
    N = w.shape[1]
    tm = _tile(T, 512, ROWS_BF16)
    rows = _tile(tm, 256, ROWS_BF16)
    row = pl.BlockSpec((1, tm, N), lambda b, i: (b, i, 0))
    return pl.pallas_call(
        functools.partial(_mm_res_norm_kernel, rows=rows),
        grid=(Bn, T // tm),
        in_specs=[pl.BlockSpec((1, tm, K), lambda b, i: (b, i, 0)),
                  pl.BlockSpec((K, N), lambda b, i: (0, 0)),
                  row,
                  pl.BlockSpec((1, 1, N), _bsel(gate)),
                  pl.BlockSpec((1, N), lambda b, i: (0, 0)),
                  pl.BlockSpec((1, 1, N), _bsel(shift)),
                  pl.BlockSpec((1, 1, N), _bsel(scale))],
        out_specs=[row, row],
        out_shape=[jax.ShapeDtypeStruct((Bn, T, N), F32), jax.ShapeDtypeStruct((Bn, T, N), BF16)],
        compiler_params=_params("parallel", "parallel"),
        name=name,
    )(a, w, res, gate, g[None], shift, scale)


def _ffn_up_kernel(h_ref, wa_ref, wb_ref, w2_ref, o_ref, w2o_ref, w_scr, *, rows):
    @pl.when((pl.program_id(1) == 0) & (pl.program_id(2) == 0))
    def _():
        w_scr[0] = wa_ref[0].astype(BF16)
        w_scr[1] = wb_ref[0].astype(BF16)

    for r0 in range(0, h_ref.shape[1], rows):
        rs = slice(r0, r0 + rows)
        h = h_ref[0, rs, :]
        a = _dot(h, w_scr[0])
        o_ref[0, rs, :] = (a * _sigmoid(a) * _dot(h, w_scr[1])).astype(o_ref.dtype)
    w2o_ref[...] = w2_ref[0].astype(BF16)


def _ffn_up(h, w13, w2, layer, name):
    Bn, T, D = h.shape
    F = w13.shape[2] // 2
    tm = _tile(T, 2048, ROWS_BF16)
    rows = _tile(tm, 512, ROWS_BF16)
    tn = _tile(F, 512, LANES)
    nj, ni = F // tn, T // tm
    steps = nj * Bn * ni
    F2, D2 = w2.shape[1:]
    rows2 = F2 // steps
    assert rows2 * steps == F2 and rows2 % ROWS_BF16 == 0, (F2, steps)
    step = lambda j, b, i: (j * Bn + b) * ni + i
    return pl.pallas_call(
        functools.partial(_ffn_up_kernel, rows=rows),
        grid=(nj, Bn, ni),
        in_specs=[pl.BlockSpec((1, tm, D), lambda j, b, i: (b, i, 0)),
                  pl.BlockSpec((1, D, tn), lambda j, b, i: (layer, 0, j)),
                  pl.BlockSpec((1, D, tn), lambda j, b, i: (layer, 0, j + nj)),
                  pl.BlockSpec((1, rows2, D2), lambda j, b, i: (layer, step(j, b, i), 0))],
        out_specs=[pl.BlockSpec((1, tm, tn), lambda j, b, i: (b, i, j)),
                   pl.BlockSpec((rows2, D2), lambda j, b, i: (step(j, b, i), 0))],
        out_shape=[jax.ShapeDtypeStruct((Bn, T, F), BF16), jax.ShapeDtypeStruct((F2, D2), BF16)],
        scratch_shapes=[pltpu.VMEM((2, D, tn), BF16)],
        compiler_params=_params("arbitrary", "arbitrary", "arbitrary"),
        name=name,
    )(h, w13, w13, w2)


def _conv_in_combine(gb, gc, u):
    return gb, gc * u


def _norm_mm_kernel(x_ref, g_ref, sh_ref, sc_ref, *rest, nw, combine, rows):
    w_refs, out_refs, h_scr = rest[:nw], rest[nw:-1], rest[-1]

    def emit(rs, h):
        outs = combine(*[_dot(h, w_ref[...]) for w_ref in w_refs])
        for o_ref, o in zip(out_refs, outs):
            o_ref[0, rs, :] = o.astype(o_ref.dtype)

    @pl.when(pl.program_id(2) == 0)
    def _():
        for r0 in range(0, x_ref.shape[1], rows):
            rs = slice(r0, r0 + rows)
            h = _norm_mod(x_ref[0, rs, :], g_ref[...], sh_ref[0], sc_ref[0]).astype(BF16)
            h_scr[rs, :] = h
            emit(rs, h)

    @pl.when(pl.program_id(2) > 0)
    def _():
        emit(slice(None), h_scr[...])


def _norm_mm(x, g, shift, scale, w, nw, combine, out_dtypes, name):
    Bn, T, D = x.shape
    N = w.shape[1] // nw
    tm = _tile(T, 1024, ROWS_BF16)
    rows = _tile(tm, 256, ROWS_BF16)
    tn = _tile(N, 512, LANES)
    nj = N // tn
    w_specs = [pl.BlockSpec((D, tn), functools.partial(lambda b, i, j, m: (0, j + m * nj), m=m))
               for m in range(nw)]
    return pl.pallas_call(
        functools.partial(_norm_mm_kernel, nw=nw, combine=combine, rows=rows),
        grid=(Bn, T // tm, nj),
        in_specs=[pl.BlockSpec((1, tm, D), lambda b, i, j: (b, i, 0)),
                  pl.BlockSpec((1, D), lambda b, i, j: (0, 0)),
                  pl.BlockSpec((1, 1, D), _bsel(shift)),
                  pl.BlockSpec((1, 1, D), _bsel(scale))] + w_specs,
        out_specs=[pl.BlockSpec((1, tm, tn), lambda b, i, j: (b, i, j))] * len(out_dtypes),
        out_shape=[jax.ShapeDtypeStruct((Bn, T, N), dt) for dt in out_dtypes],
        scratch_shapes=[pltpu.VMEM((tm, D), BF16)],
        compiler_params=_params("parallel", "parallel", "arbitrary"),
        name=name,
    )(x, g[None], shift, scale, *([w] * nw))


def _seg_ones(width):
    shift = HEAD.bit_length() - 1
    r = lax.shift_right_logical(lax.broadcasted_iota(jnp.int32, (width, width), 0), shift)
    c = lax.shift_right_logical(lax.broadcasted_iota(jnp.int32, (width, width), 1), shift)
    return (r == c).astype(F32)


def _split3(x):
    hi = x.astype(BF16)
    r1 = x - hi.astype(F32)
    mid = r1.astype(BF16)
    lo = (r1 - mid.astype(F32)).astype(BF16)
    return hi, mid, lo


def _wkv_step(r_ref, k_ref, v_ref, tw_ref, aw_ref, w2_ref, a2_ref, w0_ref, a0_ref, kk_ref, ka_ref,
                s0_ref, y_ref, sout_ref, s_scr, x_scr, r2_scr, bv_scr, vb_scr, z_scr, wt_scr,
                *, reverse, npair, chain, slot, rows):
    C = CHUNK
    PW = 2 * HEAD

    rr = lax.shift_right_logical(lax.broadcasted_iota(jnp.int32, (PW, PW), 0), HEAD.bit_length() - 1)
    cc = lax.shift_right_logical(lax.broadcasted_iota(jnp.int32, (PW, PW), 1), HEAD.bit_length() - 1)
    same = rr == cc
    same_bf = same.astype(BF16)

    def bd(x):
        xb = x.astype(BF16)
        return jnp.concatenate([xb, xb], axis=0) * same_bf

    t2 = lax.broadcasted_iota(jnp.int32, (C, PW), 0)
    s2 = lax.broadcasted_iota(jnp.int32, (C, PW), 1) & (HEAD - 1)
    before = (s2 > t2) if reverse else (s2 < t2)
    upto = before | (s2 == t2)
    pairs = range(npair)
    sls = [slice(p * PW, (p + 1) * PW) for p in pairs]

    r, k, v = (t[0, rows, :].astype(F32) for t in (r_ref, k_ref, v_ref))
    z = w0_ref[...] + _dot(tw_ref[0, rows, :], w2_ref[...])
    a_pre = a0_ref[...] + _dot(aw_ref[0, rows, :], a2_ref[...])

    if chain:
        S = [s_scr[p] for p in pairs]
        X = [x_scr[slot, p] for p in pairs]
        G = [_dot_t(X[p], r2_scr[slot, p], 1, 1) for p in pairs]
        XS = [_dot_t(X[p], S[p].astype(BF16), 1, 1) for p in pairs]
        u = [XS[p][:C] + _dot(jnp.where(before, G[p][:C, PW:], 0.0).astype(BF16), bv_scr[slot, p]) for p in pairs]
        P = [jnp.where(before, G[p][:C, :PW], 0.0) for p in pairs]

    lw = -math.exp(-0.5) * _sigmoid(z)
    kkv = k * kk_ref[...]
    kk2 = kkv * kkv
    t_i = lax.broadcasted_iota(jnp.int32, (C, C), 0)
    s_i = lax.broadcasted_iota(jnp.int32, (C, C), 1)
    tri = ((s_i >= t_i) if reverse else (s_i <= t_i)).astype(BF16)
    cum = _dot(jnp.concatenate([tri, tri], axis=1),
               jnp.concatenate(_split3(lw)[:2], axis=0))
    ss = [_dot(kk2[:, sl].astype(BF16), same_bf) for sl in sls]

    if chain:
        n_sq = C.bit_length() - 1
        for j in range(n_sq):
            Pb = [P[p].astype(BF16) for p in pairs]
            if j < n_sq - 1:
                PU = [_dot(Pb[p], jnp.concatenate([bd(P[p]), bd(u[p])], axis=1)) for p in pairs]
                P = [PU[p][:, :PW] for p in pairs]
                u = [u[p] + PU[p][:, PW:] for p in pairs]
            else:
                u = [u[p] + _dot(Pb[p], bd(u[p])) for p in pairs]
        for p in pairs:
            R = jnp.concatenate([jnp.where(upto, G[p][C:, :PW], 0.0),
                                 jnp.where(upto, G[p][C:, PW:], 0.0)], axis=1).astype(BF16)
            y_ref[0, rows, sls[p]] = XS[p][C:] + _dot(R, jnp.concatenate([bd(u[p]), bv_scr[slot, p]], axis=0))
        for p in pairs:
            UV = jnp.concatenate([u[p].astype(BF16), vb_scr[slot, p]], axis=0)
            dS = _dot_t(UV, z_scr[slot, p], 0, 0)
            s_scr[p] = (S[p] + jnp.where(same, dS, 0.0)) * wt_scr[slot, p, 0:1, :]

    a_sig = _sigmoid(a_pre)
    kd = k * (1.0 + (a_sig - 1.0) * ka_ref[...])
    e_pos = jnp.exp(cum)
    e_neg = jnp.exp(-cum)
    e_prev = jnp.exp(cum - lw)
    last = 0 if reverse else C - 1
    for p, sl in zip(pairs, sls):
        kkn = kkv[:, sl] * lax.rsqrt(jnp.maximum(ss[p], 1e-24))
        at = (-kkn) * e_prev[:, sl]
        bt = (kkn * a_sig[:, sl]) * e_neg[:, sl]
        rt = r[:, sl] * e_pos[:, sl]
        kt = kd[:, sl] * e_neg[:, sl]
        x_scr[slot, p] = jnp.concatenate([at, rt], axis=0).astype(BF16)
        r2_scr[slot, p] = jnp.concatenate([bd(bt), bd(kt)], axis=0)
        bv_scr[slot, p] = bd(v[:, sl])
        vb_scr[slot, p] = v[:, sl].astype(BF16)
        z_scr[slot, p] = jnp.concatenate([bt, kt], axis=0).astype(BF16)
        wt_scr[slot, p] = jnp.broadcast_to(e_pos[last:last + 1, sl], wt_scr.shape[2:])


def _wkv_kernel(*refs, reverse, npair):
    s0_ref, sout_ref, s_scr = refs[11], refs[13], refs[14]
    c = pl.program_id(2)
    halves = [slice(h * CHUNK, (h + 1) * CHUNK) for h in ((1, 0) if reverse else (0, 1))]

    @pl.when(c == 0)
    def _():
        s_scr[...] = s0_ref[0]
        for slot, rows in enumerate(halves):
            _wkv_step(*refs, reverse=reverse, npair=npair, chain=False, slot=slot, rows=rows)

    @pl.when(c > 0)
    def _():
        for slot, rows in enumerate(halves):
            _wkv_step(*refs, reverse=reverse, npair=npair, chain=True, slot=slot, rows=rows)

    @pl.when(c == pl.num_programs(2) - 1)
    def _():
        sout_ref[0] = s_scr[...]


def _wkv(r, k, v, tw, aw, w2p, a2p, w0, a0, kk, ka, s0, d):
    Bn, T, D = r.shape
    PW = 2 * HEAD
    npairs = D // PW
    npair = _tile(npairs, 16, 1)
    hw = npair * PW
    rows = 2 * CHUNK
    nc = T // rows
    assert nc * rows == T, T
    reverse = d == 1
    pos = (lambda j: nc - 1 - j) if reverse else (lambda j: j)
    cin = lambda c: pos(jnp.minimum(c, nc - 1))
    cout = lambda c: pos(jnp.maximum(c - 1, 0))
    tok = pl.BlockSpec((1, rows, hw), lambda b, g, c: (b, cin(c), g))
    lora = pl.BlockSpec((1, rows, LANES), lambda b, g, c: (b, cin(c), d))
    lw2 = pl.BlockSpec((LANES, hw), lambda b, g, c: (0, g))
    vec = pl.BlockSpec((1, hw), lambda b, g, c: (0, g))
    st = pl.BlockSpec((1, npair, PW, PW), lambda b, g, c: (b, g, 0, 0))
    return pl.pallas_call(
        functools.partial(_wkv_kernel, reverse=reverse, npair=npair),
        grid=(Bn, npairs // npair, nc + 1),
        in_specs=[tok, tok, tok, lora, lora, lw2, lw2, vec, vec, vec, vec, st],
        out_specs=[pl.BlockSpec((1, rows, hw), lambda b, g, c: (b, cout(c), g)), st],
        out_shape=[jax.ShapeDtypeStruct((Bn, T, D), F32),
                   jax.ShapeDtypeStruct((Bn, npairs, PW, PW), F32)],
        scratch_shapes=[pltpu.VMEM((npair, PW, PW), F32),
                        pltpu.VMEM((2, npair, 2 * CHUNK, PW), BF16),
                        pltpu.VMEM((2, npair, 2 * PW, PW), BF16),
                        pltpu.VMEM((2, npair, PW, PW), BF16),
                        pltpu.VMEM((2, npair, CHUNK, PW), BF16),
                        pltpu.VMEM((2, npair, 2 * CHUNK, PW), BF16),
                        pltpu.VMEM((2, npair, SUBLANES, PW), F32)],
        compiler_params=_params("parallel", "parallel", "arbitrary"),
        name="wkv_rev" if reverse else "wkv_fwd",
    )(r, k, v, tw, aw, w2p, a2p, w0, a0, kk, ka, s0)


def _rwkv_out_kernel(yf_ref, yb_ref, r_ref, k_ref, v_ref, gs_ref, aw_ref, res_ref, a2f_ref, a2b_ref,
                     g2_ref, wo_ref, a0_ref, ka_ref, rk_ref, lnw_ref, lnb_ref, gate_ref, g_ref, sh_ref,
                     sc_ref, o_ref, h_ref, og_scr, *, rows):
    tm, D = o_ref.shape[1:]
    PW = 2 * HEAD
    same = _seg_ones(PW).astype(BF16)
    same2 = jnp.concatenate([same, same], axis=0)

    def head_sum(x, pieces):
        if pieces == 1:
            return _dot(x.astype(BF16), same)
        return _dot(jnp.concatenate(_split3(x)[:2], axis=1), same2)

    for r0 in range(0, tm, rows):
        rs = slice(r0, r0 + rows)
        aw = aw_ref[0, rs, :]
        a_f = _sigmoid(a0_ref[0:1, :] + _dot(aw[:, :LANES], a2f_ref[...]))
        a_b = _sigmoid(a0_ref[1:2, :] + _dot(aw[:, LANES:], a2b_ref[...]))
        g = _dot(gs_ref[0, rs, :], g2_ref[...])
        for p in range(D // PW):
            sl = slice(p * PW, (p + 1) * PW)
            ksum = k_ref[0, rs, sl].astype(F32) * (2.0 + (a_f[:, sl] + a_b[:, sl] - 2.0) * ka_ref[:, sl])
            y = yf_ref[0, rs, sl] + yb_ref[0, rs, sl]
            yc = y - head_sum(y, 2) * (1.0 / HEAD)
            var = head_sum(yc * yc, 1) * (1.0 / HEAD)
            o = yc * lax.rsqrt(var + GN_EPS) * lnw_ref[:, sl] + lnb_ref[:, sl]
            bonus = (head_sum(r_ref[0, rs, sl].astype(F32) * ksum * rk_ref[:, sl], 1)
                     * v_ref[0, rs, sl].astype(F32))
            og_scr[rs, sl] = ((o + bonus) * g[:, sl]).astype(og_scr.dtype)
        x1 = res_ref[0, rs, :] + gate_ref[0] * _dot(og_scr[rs, :], wo_ref[...])
        o_ref[0, rs, :] = x1
        h_ref[0, rs, :] = _norm_mod(x1, g_ref[...], sh_ref[0], sc_ref[0]).astype(h_ref.dtype)


def _rwkv_out(yf, yb, r, k, v, gs, aw, res, a2fp, a2bp, g2, wo, a0, ka, rk, lnw, lnb, gate, g, shift,
              scale, name):
    Bn, T, D = res.shape
    tm = _tile(T, 256, ROWS_BF16)
    rows = _tile(tm, 256, ROWS_BF16)
    G = gs.shape[-1]
    row = pl.BlockSpec((1, tm, D), lambda b, i: (b, i, 0))
    vec = pl.BlockSpec((1, D), lambda b, i: (0, 0))
    full = lambda arr: pl.BlockSpec(arr.shape, lambda b, i: (0,) * arr.ndim)
    return pl.pallas_call(
        functools.partial(_rwkv_out_kernel, rows=rows),
        grid=(Bn, T // tm),
        in_specs=[row, row, row, row, row,
                  pl.BlockSpec((1, tm, G), lambda b, i: (b, i, 0)),
                  pl.BlockSpec((1, tm, 2 * LANES), lambda b, i: (b, i, 0)),
                  row, full(a2fp), full(a2bp), full(g2), full(wo), full(a0),
                  vec, vec, vec, vec,
                  pl.BlockSpec((1, 1, D), _bsel(gate)), vec,
                  pl.BlockSpec((1, 1, D), _bsel(shift)),
                  pl.BlockSpec((1, 1, D), _bsel(scale))],
        out_specs=[row, row],
        out_shape=[jax.ShapeDtypeStruct((Bn, T, D), F32), jax.ShapeDtypeStruct((Bn, T, D), BF16)],
        scratch_shapes=[pltpu.VMEM((tm, D), BF16)],
        compiler_params=_params("parallel", "parallel"),
        name=name,
    )(yf, yb, r, k, v, gs, aw, res, a2fp, a2bp, g2, wo, a0, ka, rk, lnw, lnb, gate, g[None], shift, scale)


def _conv_kernel(gb_ref, z_ref, cw_ref, o_ref):
    z = z_ref[0].astype(F32)
    T = z.shape[0]
    t = lax.broadcasted_iota(jnp.int32, (T, 1), 0)
    zp = jnp.where(t == 0, 0.0, pltpu.roll(z, 1, 0))
    zn = jnp.where(t == T - 1, 0.0, pltpu.roll(z, T - 1, 0))
    conv = zp * cw_ref[0:1, :] + z * cw_ref[1:2, :] + zn * cw_ref[2:3, :]
    o_ref[0] = (gb_ref[0] * conv).astype(o_ref.dtype)


def _conv(gb, z, cw):
    Bn, T, D = z.shape
    tn = _tile(D, 512, LANES)
    tok = pl.BlockSpec((1, T, tn), lambda b, j: (b, 0, j))
    return pl.pallas_call(
        _conv_kernel,
        grid=(Bn, D // tn),
        in_specs=[tok, tok, pl.BlockSpec((3, tn), lambda b, j: (0, j))],
        out_specs=tok,
        out_shape=jax.ShapeDtypeStruct((Bn, T, D), BF16),
        compiler_params=_params("parallel", "parallel"),
        name="short_conv",
    )(gb, z, cw)


def _pad_rows(w, rows):
    return jnp.pad(w, ((0, rows - w.shape[0]), (0, 0)))


def _pad_cols(w, cols):
    return jnp.pad(w, ((0, 0), (0, cols - w.shape[1])))


def _split_mod(mod_rows, D):
    return [mod_rows[:, m * D:(m + 1) * D][:, None, :] for m in range(6)]


def _ffn_branch(t1, h2, mods, ffn, tag):
    w13, w2, layer, casts = ffn
    act, wdn = _ffn_up(h2, w13, w2, layer, name="ffn_up_" + tag)
    return _mm_res(act, wdn, t1, mods[5], name="ffn_down_" + tag, casts=casts)


def _rwkv_layer(x, ctx, mods_x, mods_c, g1, g2n, mix, wr, wk, wv, wo, w0, w1, w2, a0, a1, a2,
                lg1, lg2, k_k, k_a, r_k, ln_w, ln_b, ffn):
    D = x.shape[-1]
    H = D // HEAD
    w1cat = jnp.concatenate([_pad_cols(w1[0], LANES), _pad_cols(w1[1], LANES)], axis=1).astype(BF16)
    a1cat = jnp.concatenate([_pad_cols(a1[0], LANES), _pad_cols(a1[1], LANES)], axis=1).astype(BF16)
    w2p = [_pad_rows(w2[d], LANES).astype(BF16) for d in range(2)]
    a2p = [_pad_rows(a2[d], LANES).astype(BF16) for d in range(2)]
    wr, wk, wv, wo = (t.astype(BF16) for t in (wr, wk, wv, wo))
    lg1, lg2 = lg1.astype(BF16), lg2.astype(BF16)
    rk = r_k.reshape(1, D)

    xr, xw, xk, xv, xa, xg = _prep_ctx(ctx, g1, mods_c[0], mods_c[1], mix)
    lora_c = dict(tw=_mm(xw, w1cat, BF16, act="tanh", name="lora_w_c"),
                  aw=_mm(xa, a1cat, BF16, name="lora_a_c"),
                  gs=_mm(xg, lg1, BF16, act="sigmoid", name="lora_g_c"))
    ins = {"c": (xr, xk, xv, lora_c["tw"], lora_c["aw"], lora_c["gs"]),
           "x": _prep_latent(x, g1, mods_x[0], mods_x[1], mix, w1cat, a1cat, lg1)}
    sets = {}
    for tag, (xr, xk, xv, tw, aw, gs) in ins.items():
        sets[tag] = dict(r=_mm(xr, wr, BF16, name="proj_r_" + tag),
                         k=_mm(xk, wk, BF16, name="proj_k_" + tag),
                         v=_mm(xv, wv, BF16, name="proj_v_" + tag), tw=tw, aw=aw, gs=gs)

    ys = {"c": [], "x": []}
    zero_state = jnp.zeros((x.shape[0], H // 2, 2 * HEAD, 2 * HEAD), F32)
    for d in range(2):
        state = zero_state
        for tag in ("c", "x"):
            s = sets[tag]
            y, state = _wkv(s["r"], s["k"], s["v"], s["tw"], s["aw"], w2p[d], a2p[d],
                            w0[d][None], a0[d][None], k_k[None], k_a[None], state, d)
            ys[tag].append(y)

    outs = []
    for tag, tok, mods in (("c", ctx, mods_c), ("x", x, mods_x)):
        s = sets[tag]
        t1, h2 = _rwkv_out(ys[tag][0], ys[tag][1], s["r"], s["k"], s["v"], s["gs"], s["aw"], tok,
                           a2p[0], a2p[1], lg2, wo, a0, k_a[None], rk, ln_w[None], ln_b[None],
                           mods[2], g2n, mods[3], mods[4], name="rwkv_out_" + tag)
        outs.append(_ffn_branch(t1, h2, mods, ffn, tag))
    (ctx_out, _), (x_out, cast_out) = outs
    return x_out, ctx_out, cast_out


def _conv_layer(x, mods, g1, g2n, w_in, conv_w, w_out, ffn):
    gb, z = _norm_mm(x, g1, mods[0], mods[1], w_in.astype(BF16), 3, _conv_in_combine, (BF16, BF16),
                     name="conv_in")
    p = _conv(gb, z, conv_w)
    t1, h2 = _mm_res_norm(p, w_out.astype(BF16), x, mods[2], g2n, mods[3], mods[4], name="conv_out")
    return _ffn_branch(t1, h2, mods, ffn, "x")


def kernel(x, c, ctx, c_ctx, norm1_g, norm2_g, ada_w, ada_b, rw_mix, rw_wr, rw_wk, rw_wv, rw_wo,
           rw_w0, rw_w1, rw_w2, rw_a0, rw_a1, rw_a2, rw_g1, rw_g2, rw_kk, rw_ka, rw_rk, rw_lnw,
           rw_lnb, sc_win, sc_conv, sc_wout, ffn_w13, ffn_w2, final_g):
    B, T, D = x.shape
    depth = norm1_g.shape[0]
    rows = -(-(B + 1) // 8) * 8
    cond = jnp.zeros((rows, D), F32).at[:B].set(c).at[B].set(c_ctx)
    conv_w = {}
    for i in range(depth):
        last = i == depth - 1
        j = i // 2
        mod = _ada(cond, ada_w, ada_b, i)
        mods_x = _split_mod(mod[:B], D)
        mods_c = _split_mod(mod[B:B + 1], D)
        nxt = (i + 1) // 2
        casts = (sc_win[nxt], sc_wout[nxt]) if (not last and i % 2 == 0) else ()
        ffn = (ffn_w13, ffn_w2, i, casts)
        if i % 2 == 0:
            x, ctx, cast_out = _rwkv_layer(
                x, ctx, mods_x, mods_c, norm1_g[i], norm2_g[i], rw_mix[j], rw_wr[j], rw_wk[j],
                rw_wv[j], rw_wo[j], rw_w0[j], rw_w1[j], rw_w2[j], rw_a0[j], rw_a1[j], rw_a2[j],
                rw_g1[j], rw_g2[j], rw_kk[j], rw_ka[j], rw_rk[j], rw_lnw[j], rw_lnb[j], ffn)
            if casts:
                conv_w[nxt] = cast_out
        else:
            w_in, w_out = conv_w.get(j, (sc_win[j], sc_wout[j]))
            if not last:
                ctx, _ = _conv_layer(ctx, mods_c, norm1_g[i], norm2_g[i], w_in, sc_conv[j], w_out,
                                     (ffn_w13, ffn_w2, i, ()))
            x, _ = _conv_layer(x, mods_x, norm1_g[i], norm2_g[i], w_in, sc_conv[j], w_out, ffn)
    zeros = jnp.zeros((1, 1, D), F32)
    return _norm(x, final_g, zeros, zeros)
```

```python
import functools
import math

import jax
import jax.numpy as jnp
from jax import lax
from jax.experimental import pallas as pl
from jax.experimental.pallas import tpu as pltpu

HEAD = 64
GRID_W = 64
CHUNK = 64
STEP_CHUNKS = 4
NORM_EPS = 1e-6
GN_EPS = 64e-5
LANES = 128
SUBLANES = 8
ROWS_BF16 = 16
VMEM_LIMIT = 56 * 1024 * 1024

F32 = jnp.float32
BF16 = jnp.bfloat16


def _params(*sem):
    return pltpu.CompilerParams(dimension_semantics=sem, vmem_limit_bytes=VMEM_LIMIT)


def _tile(n, pref, mult):
    t = min(pref, n)
    t -= t % mult
    while t >= mult:
        if n % t == 0:
            return t
        t -= mult
    return n


def _sigmoid(x):
    return 1.0 / (1.0 + jnp.exp(-x))


def _norm_mod(x, g, shift, scale, cols=None):
    rs = lax.rsqrt(jnp.mean(x * x, axis=-1, keepdims=True) + NORM_EPS)
    if cols is not None:
        x, g, shift, scale = (t[:, cols[0]:cols[1]] for t in (x, g, shift, scale))
    return (x * rs) * (g * (1.0 + scale)) + shift


def _dot(a, b):
    return jnp.dot(a, b, preferred_element_type=F32)


def _dot_t(a, b, ca, cb):
    return lax.dot_general(a, b, (((ca,), (cb,)), ((), ())), preferred_element_type=F32)


def _ada_kernel(c_ref, w_ref, b_ref, o_ref):
    c = c_ref[...]
    s = c * _sigmoid(c)
    o_ref[...] = _dot(s.astype(BF16), w_ref[0].astype(BF16)) + b_ref[0]


def _ada(cond, w, b, layer):
    R, D = cond.shape
    N = w.shape[2]
    tn = _tile(N, 1024, LANES)
    return pl.pallas_call(
        _ada_kernel,
        grid=(N // tn,),
        in_specs=[pl.BlockSpec((R, D), lambda j: (0, 0)),
                  pl.BlockSpec((1, D, tn), lambda j: (layer, 0, j)),
                  pl.BlockSpec((1, 1, tn), lambda j: (layer, 0, j))],
        out_specs=pl.BlockSpec((R, tn), lambda j: (0, j)),
        out_shape=jax.ShapeDtypeStruct((R, N), F32),
        compiler_params=_params("parallel"),
        name="ada_mod",
    )(cond, w, b[:, None, :])


def _norm_kernel(x_ref, g_ref, sh_ref, sc_ref, o_ref):
    o_ref[0] = _norm_mod(x_ref[0], g_ref[...], sh_ref[0], sc_ref[0])


def _bsel(arr):
    if arr.shape[0] == 1:
        return lambda b, *_: (0, 0, 0)
    return lambda b, *_: (b, 0, 0)


def _norm(x, g, shift, scale):
    Bn, T, D = x.shape
    tm = _tile(T, 512, SUBLANES)
    return pl.pallas_call(
        _norm_kernel,
        grid=(Bn, T // tm),
        in_specs=[pl.BlockSpec((1, tm, D), lambda b, i: (b, i, 0)),
                  pl.BlockSpec((1, D), lambda b, i: (0, 0)),
                  pl.BlockSpec((1, 1, D), _bsel(shift)),
                  pl.BlockSpec((1, 1, D), _bsel(scale))],
        out_specs=pl.BlockSpec((1, tm, D), lambda b, i: (b, i, 0)),
        out_shape=jax.ShapeDtypeStruct((Bn, T, D), F32),
        compiler_params=_params("parallel", "parallel"),
        name="norm",
    )(x, g[None], shift, scale)


def _write_mix(out_refs, mix_ref, h, shifted, c0, c1):
    xx = shifted - h
    for m, o_ref in enumerate(out_refs):
        o_ref[0, :, c0:c1] = (h + xx * mix_ref[m:m + 1, c0:c1]).astype(o_ref.dtype)


def _prep_latent_kernel(x_ref, xu_ref, xd_ref, g_ref, sh_ref, sc_ref, mix_ref, w1_ref, a1_ref, g1_ref,
                        xr_ref, xk_ref, xv_ref, tw_ref, aw_ref, gs_ref, h_scr, xm_scr):
    i = pl.program_id(1)
    n = pl.num_programs(1)
    g, sh, sc = g_ref[...], sh_ref[0], sc_ref[0]
    tm, D = x_ref.shape[1:]
    q = D // 4
    W = GRID_W
    h_scr[0:W, 2 * q:3 * q] = jnp.where(i > 0, _norm_mod(xu_ref[0], g, sh, sc, (2 * q, 3 * q)), 0.0)
    h_scr[W - SUBLANES:W, 0:q] = jnp.zeros((SUBLANES, q), F32)
    for r0 in range(0, tm, W):
        h_scr[W + r0:2 * W + r0, :] = _norm_mod(x_ref[0, r0:r0 + W, :], g, sh, sc)
    h_scr[W + tm:, 3 * q:] = jnp.where(i < n - 1, _norm_mod(xd_ref[0], g, sh, sc, (3 * q, D)), 0.0)
    h_scr[W + tm:W + tm + SUBLANES, q:2 * q] = jnp.zeros((SUBLANES, q), F32)
    wide = {0: xr_ref, 2: xk_ref, 3: xv_ref}
    lora = {1: 0, 4: 1, 5: 2}
    lora_w = (w1_ref, a1_ref, g1_ref)
    acc = [None] * 3
    R = 32
    row = lax.broadcasted_iota(jnp.int32, (R, 1), 0)
    for k, off in enumerate((-1, 1, -W, W)):
        c0, c1 = k * q, (k + 1) * q
        mixk = [mix_ref[m:m + 1, c0:c1] for m in range(6)]
        for r0 in range(0, tm, R):
            h = h_scr[W + r0:W + r0 + R, c0:c1]
            s = h_scr[W + r0 + off:W + r0 + off + R, c0:c1]
            if off == -1 and r0 % W == 0:
                s = jnp.where(row == 0, 0.0, s)
            if off == 1 and (r0 + R) % W == 0:
                s = jnp.where(row == R - 1, 0.0, s)
            xx = s - h
            for m in range(6):
                xm = (h + xx * mixk[m]).astype(BF16)
                if m in wide:
                    wide[m][0, r0:r0 + R, c0:c1] = xm
                else:
                    xm_scr[lora[m], r0:r0 + R, c0:c1] = xm
        for l in range(3):
            part = _dot(xm_scr[l, :, c0:c1], lora_w[l][c0:c1, :])
            acc[l] = part if acc[l] is None else acc[l] + part
    tw_ref[0] = jnp.tanh(acc[0]).astype(tw_ref.dtype)
    aw_ref[0] = acc[1].astype(aw_ref.dtype)
    gs_ref[0] = _sigmoid(acc[2]).astype(gs_ref.dtype)


def _prep_latent(x, g, shift, scale, mix, w1cat, a1cat, lg1):
    Bn, T, D = x.shape
    rows_per_tile = _tile(T // GRID_W, 4, 1)
    tm = rows_per_tile * GRID_W
    nrow = T // GRID_W
    tok = pl.BlockSpec((1, tm, D), lambda b, i: (b, i, 0))
    full = lambda arr: pl.BlockSpec(arr.shape, lambda b, i: (0,) * arr.ndim)
    small = lambda arr: pl.BlockSpec((1, tm, arr.shape[1]), lambda b, i: (b, i, 0))
    return pl.pallas_call(
        _prep_latent_kernel,
        grid=(Bn, T // tm),
        in_specs=[tok,
                  pl.BlockSpec((1, GRID_W, D),
                               lambda b, i: (b, jnp.maximum(i * rows_per_tile - 1, 0), 0)),
                  pl.BlockSpec((1, GRID_W, D),
                               lambda b, i: (b, jnp.minimum((i + 1) * rows_per_tile, nrow - 1), 0)),
                  pl.BlockSpec((1, D), lambda b, i: (0, 0)),
                  pl.BlockSpec((1, 1, D), _bsel(shift)),
                  pl.BlockSpec((1, 1, D), _bsel(scale)),
                  pl.BlockSpec((6, D), lambda b, i: (0, 0)),
                  full(w1cat), full(a1cat), full(lg1)],
        out_specs=[tok, tok, tok, small(w1cat), small(a1cat), small(lg1)],
        out_shape=[jax.ShapeDtypeStruct((Bn, T, D), BF16)] * 3
        + [jax.ShapeDtypeStruct((Bn, T, w.shape[1]), BF16) for w in (w1cat, a1cat, lg1)],
        scratch_shapes=[pltpu.VMEM((tm + 2 * GRID_W, D), F32), pltpu.VMEM((3, tm, D), BF16)],
        compiler_params=_params("parallel", "parallel"),
        name="prep_latent",
    )(x, x, x, g[None], shift, scale, mix, w1cat, a1cat, lg1)


def _prep_ctx_kernel(x_ref, g_ref, sh_ref, sc_ref, mix_ref, *out_refs):
    h = _norm_mod(x_ref[0], g_ref[...], sh_ref[0], sc_ref[0])
    L, D = h.shape
    half = D // 2
    t = lax.broadcasted_iota(jnp.int32, (L, 1), 0)
    h0, h1 = h[:, :half], h[:, half:]
    prev = jnp.where(t == 0, 0.0, pltpu.roll(h0, 1, 0))
    nxt = jnp.where(t == L - 1, 0.0, pltpu.roll(h1, L - 1, 0))
    _write_mix(out_refs, mix_ref, h0, prev, 0, half)
    _write_mix(out_refs, mix_ref, h1, nxt, half, D)


def _prep_ctx(x, g, shift, scale, mix):
    Bn, L, D = x.shape
    return pl.pallas_call(
        _prep_ctx_kernel,
        grid=(Bn,),
        in_specs=[pl.BlockSpec((1, L, D), lambda b: (b, 0, 0)),
                  pl.BlockSpec((1, D), lambda b: (0, 0)),
                  pl.BlockSpec((1, 1, D), _bsel(shift)),
                  pl.BlockSpec((1, 1, D), _bsel(scale)),
                  pl.BlockSpec((6, D), lambda b: (0, 0))],
        out_specs=[pl.BlockSpec((1, L, D), lambda b: (b, 0, 0))] * 6,
        out_shape=[jax.ShapeDtypeStruct((Bn, L, D), BF16)] * 6,
        compiler_params=_params("parallel"),
        name="prep_ctx",
    )(x, g[None], shift, scale, mix)


def _mm_kernel(a_ref, w_ref, o_ref, *, act):
    acc = _dot(a_ref[0], w_ref[...])
    if act == "tanh":
        acc = jnp.tanh(acc)
    elif act == "sigmoid":
        acc = _sigmoid(acc)
    o_ref[0] = acc.astype(o_ref.dtype)


def _mm(a, w, out_dtype, act=None, name="mm"):
    Bn, T, K = a.shape
    N = w.shape[1]
    M = Bn * T
    tm = _tile(M, 1024, ROWS_BF16)
    tn = _tile(N, 2048, LANES)
    out = pl.pallas_call(
        functools.partial(_mm_kernel, act=act),
        grid=(1, M // tm, N // tn),
        in_specs=[pl.BlockSpec((1, tm, K), lambda b, i, j: (b, i, 0)),
                  pl.BlockSpec((K, tn), lambda b, i, j: (0, j))],
        out_specs=pl.BlockSpec((1, tm, tn), lambda b, i, j: (b, i, j)),
        out_shape=jax.ShapeDtypeStruct((1, M, N), out_dtype),
        compiler_params=_params("parallel", "parallel", "parallel"),
        name=name,
    )(a.reshape(1, M, K), w)
    return out.reshape(Bn, T, N)


def _mm_res_kernel(a_ref, w_ref, res_ref, gate_ref, *rest):
    ncast = (len(rest) - 1) // 2
    cast_in, o_ref, cast_out = rest[:ncast], rest[ncast], rest[ncast + 1:]
    o_ref[0] = res_ref[0] + gate_ref[0] * _dot(a_ref[0], w_ref[...])
    for src, dst in zip(cast_in, cast_out):
        dst[...] = src[...].astype(BF16)


def _mm_res(a, w, res, gate, name="mm_res", casts=()):
    Bn, T, K = a.shape
    N = w.shape[1]
    tm = _tile(T, 1024, ROWS_BF16)
    tn = _tile(N, 1024 if K <= 2048 else 512, LANES)
    ni, nj = T // tm, N // tn
    steps = Bn * ni * nj
    gsel = _bsel(gate)
    step = lambda b, i, j: (b * ni + i) * nj + j
    cast_specs = []
    for cw in casts:
        rows = cw.shape[0] // steps
        assert rows * steps == cw.shape[0] and rows % ROWS_BF16 == 0, (cw.shape, steps)
        cast_specs.append(pl.BlockSpec((rows, cw.shape[1]), lambda b, i, j: (step(b, i, j), 0)))
    tile = pl.BlockSpec((1, tm, tn), lambda b, i, j: (b, i, j))
    outs = pl.pallas_call(
        _mm_res_kernel,
        grid=(Bn, ni, nj),
        in_specs=[pl.BlockSpec((1, tm, K), lambda b, i, j: (b, i, 0)),
                  pl.BlockSpec((K, tn), lambda b, i, j: (0, j)),
                  tile,
                  pl.BlockSpec((1, 1, tn), lambda b, i, j: gsel(b)[:2] + (j,))] + cast_specs,
        out_specs=[tile] + cast_specs,
        out_shape=[jax.ShapeDtypeStruct((Bn, T, N), F32)]
        + [jax.ShapeDtypeStruct(cw.shape, BF16) for cw in casts],
        compiler_params=_params("arbitrary", "arbitrary", "arbitrary"),
        name=name,
    )(a, w, res, gate, *casts)
    return outs[0], tuple(outs[1:])


def _mm_res_norm_kernel(a_ref, w_ref, res_ref, gate_ref, g_ref, sh_ref, sc_ref, o_ref, h_ref, *, rows):
    for r0 in range(0, a_ref.shape[1], rows):
        rs = slice(r0, r0 + rows)
        x1 = res_ref[0, rs, :] + gate_ref[0] * _dot(a_ref[0, rs, :], w_ref[...])
        o_ref[0, rs, :] = x1
        h_ref[0, rs, :] = _norm_mod(x1, g_ref[...], sh_ref[0], sc_ref[0]).astype(h_ref.dtype)


def _mm_res_norm(a, w, res, gate, g, shift, scale, name):
    Bn, T, K = a.shape
    N = w.shape[1]
    tm = _tile(T, 512, ROWS_BF16)
    rows = _tile(tm, 256, ROWS_BF16)
    row = pl.BlockSpec((1, tm, N), lambda b, i: (b, i, 0))
    return pl.pallas_call(
        functools.partial(_mm_res_norm_kernel, rows=rows),
        grid=(Bn, T // tm),
        in_specs=[pl.BlockSpec((1, tm, K), lambda b, i: (b, i, 0)),
                  pl.BlockSpec((K, N), lambda b, i: (0, 0)),
                  row,
                  pl.BlockSpec((1, 1, N), _bsel(gate)),
                  pl.BlockSpec((1, N), lambda b, i: (0, 0)),
                  pl.BlockSpec((1, 1, N), _bsel(shift)),
                  pl.BlockSpec((1, 1, N), _bsel(scale))],
        out_specs=[row, row],
        out_shape=[jax.ShapeDtypeStruct((Bn, T, N), F32), jax.ShapeDtypeStruct((Bn, T, N), BF16)],
        compiler_params=_params("parallel", "parallel"),
        name=name,
    )(a, w, res, gate, g[None], shift, scale)


def _ffn_up_kernel(h_ref, wa_ref, wb_ref, w2_ref, o_ref, w2o_ref, w_scr, *, rows):
    @pl.when((pl.program_id(1) == 0) & (pl.program_id(2) == 0))
    def _():
        w_scr[0] = wa_ref[0].astype(BF16)
        w_scr[1] = wb_ref[0].astype(BF16)

    for r0 in range(0, h_ref.shape[1], rows):
        rs = slice(r0, r0 + rows)
        h = h_ref[0, rs, :]
        a = _dot(h, w_scr[0])
        o_ref[0, rs, :] = (a * _sigmoid(a) * _dot(h, w_scr[1])).astype(o_ref.dtype)
    w2o_ref[...] = w2_ref[0].astype(BF16)


def _ffn_up(h, w13, w2, layer, name):
    Bn, T, D = h.shape
    F = w13.shape[2] // 2
    tm = _tile(T, 2048, ROWS_BF16)
    rows = _tile(tm, 512, ROWS_BF16)
    tn = _tile(F, 512, LANES)
    nj, ni = F // tn, T // tm
    steps = nj * Bn * ni
    F2, D2 = w2.shape[1:]
    rows2 = F2 // steps
    assert rows2 * steps == F2 and rows2 % ROWS_BF16 == 0, (F2, steps)
    step = lambda j, b, i: (j * Bn + b) * ni + i
    return pl.pallas_call(
        functools.partial(_ffn_up_kernel, rows=rows),
        grid=(nj, Bn, ni),
        in_specs=[pl.BlockSpec((1, tm, D), lambda j, b, i: (b, i, 0)),
                  pl.BlockSpec((1, D, tn), lambda j, b, i: (layer, 0, j)),
                  pl.BlockSpec((1, D, tn), lambda j, b, i: (layer, 0, j + nj)),
                  pl.BlockSpec((1, rows2, D2), lambda j, b, i: (layer, step(j, b, i), 0))],
        out_specs=[pl.BlockSpec((1, tm, tn), lambda j, b, i: (b, i, j)),
                   pl.BlockSpec((rows2, D2), lambda j, b, i: (step(j, b, i), 0))],
        out_shape=[jax.ShapeDtypeStruct((Bn, T, F), BF16), jax.ShapeDtypeStruct((F2, D2), BF16)],
        scratch_shapes=[pltpu.VMEM((2, D, tn), BF16)],
        compiler_params=_params("arbitrary", "arbitrary", "arbitrary"),
        name=name,
    )(h, w13, w13, w2)


def _conv_in_combine(gb, gc, u):
    return gb, gc * u


def _norm_mm_kernel(x_ref, g_ref, sh_ref, sc_ref, *rest, nw, combine, rows):
    w_refs, out_refs, h_scr = rest[:nw], rest[nw:-1], rest[-1]

    def emit(rs, h):
        outs = combine(*[_dot(h, w_ref[...]) for w_ref in w_refs])
        for o_ref, o in zip(out_refs, outs):
            o_ref[0, rs, :] = o.astype(o_ref.dtype)

    @pl.when(pl.program_id(2) == 0)
    def _():
        for r0 in range(0, x_ref.shape[1], rows):
            rs = slice(r0, r0 + rows)
            h = _norm_mod(x_ref[0, rs, :], g_ref[...], sh_ref[0], sc_ref[0]).astype(BF16)
            h_scr[rs, :] = h
            emit(rs, h)

    @pl.when(pl.program_id(2) > 0)
    def _():
        emit(slice(None), h_scr[...])


def _norm_mm(x, g, shift, scale, w, nw, combine, out_dtypes, name):
    Bn, T, D = x.shape
    N = w.shape[1] // nw
    tm = _tile(T, 1024, ROWS_BF16)
    rows = _tile(tm, 256, ROWS_BF16)
    tn = _tile(N, 512, LANES)
    nj = N // tn
    w_specs = [pl.BlockSpec((D, tn), functools.partial(lambda b, i, j, m: (0, j + m * nj), m=m))
               for m in range(nw)]
    return pl.pallas_call(
        functools.partial(_norm_mm_kernel, nw=nw, combine=combine, rows=rows),
        grid=(Bn, T // tm, nj),
        in_specs=[pl.BlockSpec((1, tm, D), lambda b, i, j: (b, i, 0)),
                  pl.BlockSpec((1, D), lambda b, i, j: (0, 0)),
                  pl.BlockSpec((1, 1, D), _bsel(shift)),
                  pl.BlockSpec((1, 1, D), _bsel(scale))] + w_specs,
        out_specs=[pl.BlockSpec((1, tm, tn), lambda b, i, j: (b, i, j))] * len(out_dtypes),
        out_shape=[jax.ShapeDtypeStruct((Bn, T, N), dt) for dt in out_dtypes],
        scratch_shapes=[pltpu.VMEM((tm, D), BF16)],
        compiler_params=_params("parallel", "parallel", "arbitrary"),
        name=name,
    )(x, g[None], shift, scale, *([w] * nw))


def _seg_ones(width):
    shift = HEAD.bit_length() - 1
    r = lax.shift_right_logical(lax.broadcasted_iota(jnp.int32, (width, width), 0), shift)
    c = lax.shift_right_logical(lax.broadcasted_iota(jnp.int32, (width, width), 1), shift)
    return (r == c).astype(F32)


def _split3(x):
    hi = x.astype(BF16)
    r1 = x - hi.astype(F32)
    mid = r1.astype(BF16)
    lo = (r1 - mid.astype(F32)).astype(BF16)
    return hi, mid, lo


def _wkv_step(r_ref, k_ref, v_ref, tw_ref, aw_ref, w2_ref, a2_ref, w0_ref, a0_ref, kk_ref, ka_ref,
                s0_ref, y_ref, sout_ref, s_scr, x_scr, r2_scr, bv_scr, vb_scr, z_scr, wt_scr,
                *, reverse, npair, chain, slot, rows):
    C = CHUNK
    PW = 2 * HEAD

    rr = lax.shift_right_logical(lax.broadcasted_iota(jnp.int32, (PW, PW), 0), HEAD.bit_length() - 1)
    cc = lax.shift_right_logical(lax.broadcasted_iota(jnp.int32, (PW, PW), 1), HEAD.bit_length() - 1)
    same = rr == cc
    same_bf = same.astype(BF16)

    def bd(x):
        xb = x.astype(BF16)
        return jnp.concatenate([xb, xb], axis=0) * same_bf

    t2 = lax.broadcasted_iota(jnp.int32, (C, PW), 0)
    s2 = lax.broadcasted_iota(jnp.int32, (C, PW), 1) & (HEAD - 1)
    before = (s2 > t2) if reverse else (s2 < t2)
    upto = before | (s2 == t2)
    pairs = range(npair)
    sls = [slice(p * PW, (p + 1) * PW) for p in pairs]

    r, k, v = (t[0, rows, :].astype(F32) for t in (r_ref, k_ref, v_ref))
    z = w0_ref[...] + _dot(tw_ref[0, rows, :], w2_ref[...])
    a_pre = a0_ref[...] + _dot(aw_ref[0, rows, :], a2_ref[...])

    if chain:
        S = [s_scr[p] for p in pairs]
        X = [x_scr[slot, p] for p in pairs]
        G = [_dot_t(X[p], r2_scr[slot, p], 1, 1) for p in pairs]
        XS = [_dot_t(X[p], S[p].astype(BF16), 1, 1) for p in pairs]
        u = [XS[p][:C] + _dot(jnp.where(before, G[p][:C, PW:], 0.0).astype(BF16), bv_scr[slot, p]) for p in pairs]
        P = [jnp.where(before, G[p][:C, :PW], 0.0) for p in pairs]

    lw = -math.exp(-0.5) * _sigmoid(z)
    kkv = k * kk_ref[...]
    kk2 = kkv * kkv
    t_i = lax.broadcasted_iota(jnp.int32, (C, C), 0)
    s_i = lax.broadcasted_iota(jnp.int32, (C, C), 1)
    tri = ((s_i >= t_i) if reverse else (s_i <= t_i)).astype(BF16)
    cum = _dot(jnp.concatenate([tri, tri], axis=1),
               jnp.concatenate(_split3(lw)[:2], axis=0))
    ss = [_dot(kk2[:, sl].astype(BF16), same_bf) for sl in sls]

    if chain:
        n_sq = C.bit_length() - 1
        for j in range(n_sq):
            Pb = [P[p].astype(BF16) for p in pairs]
            if j < n_sq - 1:
                PU = [_dot(Pb[p], jnp.concatenate([bd(P[p]), bd(u[p])], axis=1)) for p in pairs]
                P = [PU[p][:, :PW] for p in pairs]
                u = [u[p] + PU[p][:, PW:] for p in pairs]
            else:
                u = [u[p] + _dot(Pb[p], bd(u[p])) for p in pairs]
        for p in pairs:
            R = jnp.concatenate([jnp.where(upto, G[p][C:, :PW], 0.0),
                                 jnp.where(upto, G[p][C:, PW:], 0.0)], axis=1).astype(BF16)
            y_ref[0, rows, sls[p]] = XS[p][C:] + _dot(R, jnp.concatenate([bd(u[p]), bv_scr[slot, p]], axis=0))
        for p in pairs:
            UV = jnp.concatenate([u[p].astype(BF16), vb_scr[slot, p]], axis=0)
            dS = _dot_t(UV, z_scr[slot, p], 0, 0)
            s_scr[p] = (S[p] + jnp.where(same, dS, 0.0)) * wt_scr[slot, p, 0:1, :]

    a_sig = _sigmoid(a_pre)
    kd = k * (1.0 + (a_sig - 1.0) * ka_ref[...])
    e_pos = jnp.exp(cum)
    e_neg = jnp.exp(-cum)
    e_prev = jnp.exp(cum - lw)
    last = 0 if reverse else C - 1
    for p, sl in zip(pairs, sls):
        kkn = kkv[:, sl] * lax.rsqrt(jnp.maximum(ss[p], 1e-24))
        at = (-kkn) * e_prev[:, sl]
        bt = (kkn * a_sig[:, sl]) * e_neg[:, sl]
        rt = r[:, sl] * e_pos[:, sl]
        kt = kd[:, sl] * e_neg[:, sl]
        x_scr[slot, p] = jnp.concatenate([at, rt], axis=0).astype(BF16)
        r2_scr[slot, p] = jnp.concatenate([bd(bt), bd(kt)], axis=0)
        bv_scr[slot, p] = bd(v[:, sl])
        vb_scr[slot, p] = v[:, sl].astype(BF16)
        z_scr[slot, p] = jnp.concatenate([bt, kt], axis=0).astype(BF16)
        wt_scr[slot, p] = jnp.broadcast_to(e_pos[last:last + 1, sl], wt_scr.shape[2:])


def _wkv_kernel(*refs, reverse, npair):
    s0_ref, sout_ref, s_scr = refs[11], refs[13], refs[14]
    c = pl.program_id(2)
    order = range(STEP_CHUNKS - 1, -1, -1) if reverse else range(STEP_CHUNKS)
    halves = [slice(h * CHUNK, (h + 1) * CHUNK) for h in order]

    @pl.when(c == 0)
    def _():
        s_scr[...] = s0_ref[0]
        for slot, rows in enumerate(halves):
            _wkv_step(*refs, reverse=reverse, npair=npair, chain=False, slot=slot, rows=rows)

    @pl.when(c > 0)
    def _():
        for slot, rows in enumerate(halves):
            _wkv_step(*refs, reverse=reverse, npair=npair, chain=True, slot=slot, rows=rows)

    @pl.when(c == pl.num_programs(2) - 1)
    def _():
        sout_ref[0] = s_scr[...]


def _wkv(r, k, v, tw, aw, w2p, a2p, w0, a0, kk, ka, s0, d):
    Bn, T, D = r.shape
    PW = 2 * HEAD
    npairs = D // PW
    npair = _tile(npairs, 16, 1)
    hw = npair * PW
    rows = STEP_CHUNKS * CHUNK
    nc = T // rows
    assert nc * rows == T, T
    reverse = d == 1
    pos = (lambda j: nc - 1 - j) if reverse else (lambda j: j)
    cin = lambda c: pos(jnp.minimum(c, nc - 1))
    cout = lambda c: pos(jnp.maximum(c - 1, 0))
    tok = pl.BlockSpec((1, rows, hw), lambda b, g, c: (b, cin(c), g))
    lora = pl.BlockSpec((1, rows, LANES), lambda b, g, c: (b, cin(c), d))
    lw2 = pl.BlockSpec((LANES, hw), lambda b, g, c: (0, g))
    vec = pl.BlockSpec((1, hw), lambda b, g, c: (0, g))
    st = pl.BlockSpec((1, npair, PW, PW), lambda b, g, c: (b, g, 0, 0))
    return pl.pallas_call(
        functools.partial(_wkv_kernel, reverse=reverse, npair=npair),
        grid=(Bn, npairs // npair, nc + 1),
        in_specs=[tok, tok, tok, lora, lora, lw2, lw2, vec, vec, vec, vec, st],
        out_specs=[pl.BlockSpec((1, rows, hw), lambda b, g, c: (b, cout(c), g)), st],
        out_shape=[jax.ShapeDtypeStruct((Bn, T, D), F32),
                   jax.ShapeDtypeStruct((Bn, npairs, PW, PW), F32)],
        scratch_shapes=[pltpu.VMEM((npair, PW, PW), F32),
                        pltpu.VMEM((STEP_CHUNKS, npair, 2 * CHUNK, PW), BF16),
                        pltpu.VMEM((STEP_CHUNKS, npair, 2 * PW, PW), BF16),
                        pltpu.VMEM((STEP_CHUNKS, npair, PW, PW), BF16),
                        pltpu.VMEM((STEP_CHUNKS, npair, CHUNK, PW), BF16),
                        pltpu.VMEM((STEP_CHUNKS, npair, 2 * CHUNK, PW), BF16),
                        pltpu.VMEM((STEP_CHUNKS, npair, SUBLANES, PW), F32)],
        compiler_params=_params("parallel", "parallel", "arbitrary"),
        name="wkv_rev" if reverse else "wkv_fwd",
    )(r, k, v, tw, aw, w2p, a2p, w0, a0, kk, ka, s0)


def _rwkv_out_kernel(yf_ref, yb_ref, r_ref, k_ref, v_ref, gs_ref, aw_ref, res_ref, a2f_ref, a2b_ref,
                     g2_ref, wo_ref, a0_ref, ka_ref, rk_ref, lnw_ref, lnb_ref, gate_ref, g_ref, sh_ref,
                     sc_ref, o_ref, h_ref, og_scr, *, rows):
    tm, D = o_ref.shape[1:]
    PW = 2 * HEAD
    same = _seg_ones(PW).astype(BF16)
    same2 = jnp.concatenate([same, same], axis=0)

    def head_sum(x, pieces):
        if pieces == 1:
            return _dot(x.astype(BF16), same)
        return _dot(jnp.concatenate(_split3(x)[:2], axis=1), same2)

    for r0 in range(0, tm, rows):
        rs = slice(r0, r0 + rows)
        aw = aw_ref[0, rs, :]
        a_f = _sigmoid(a0_ref[0:1, :] + _dot(aw[:, :LANES], a2f_ref[...]))
        a_b = _sigmoid(a0_ref[1:2, :] + _dot(aw[:, LANES:], a2b_ref[...]))
        g = _dot(gs_ref[0, rs, :], g2_ref[...])
        for p in range(D // PW):
            sl = slice(p * PW, (p + 1) * PW)
            ksum = k_ref[0, rs, sl].astype(F32) * (2.0 + (a_f[:, sl] + a_b[:, sl] - 2.0) * ka_ref[:, sl])
            y = yf_ref[0, rs, sl] + yb_ref[0, rs, sl]
            yc = y - head_sum(y, 2) * (1.0 / HEAD)
            var = head_sum(yc * yc, 1) * (1.0 / HEAD)
            o = yc * lax.rsqrt(var + GN_EPS) * lnw_ref[:, sl] + lnb_ref[:, sl]
            bonus = (head_sum(r_ref[0, rs, sl].astype(F32) * ksum * rk_ref[:, sl], 1)
                     * v_ref[0, rs, sl].astype(F32))
            og_scr[rs, sl] = ((o + bonus) * g[:, sl]).astype(og_scr.dtype)
        x1 = res_ref[0, rs, :] + gate_ref[0] * _dot(og_scr[rs, :], wo_ref[...])
        o_ref[0, rs, :] = x1
        h_ref[0, rs, :] = _norm_mod(x1, g_ref[...], sh_ref[0], sc_ref[0]).astype(h_ref.dtype)


def _rwkv_out(yf, yb, r, k, v, gs, aw, res, a2fp, a2bp, g2, wo, a0, ka, rk, lnw, lnb, gate, g, shift,
              scale, name):
    Bn, T, D = res.shape
    tm = _tile(T, 256, ROWS_BF16)
    rows = _tile(tm, 256, ROWS_BF16)
    G = gs.shape[-1]
    row = pl.BlockSpec((1, tm, D), lambda b, i: (b, i, 0))
    vec = pl.BlockSpec((1, D), lambda b, i: (0, 0))
    full = lambda arr: pl.BlockSpec(arr.shape, lambda b, i: (0,) * arr.ndim)
    return pl.pallas_call(
        functools.partial(_rwkv_out_kernel, rows=rows),
        grid=(Bn, T // tm),
        in_specs=[row, row, row, row, row,
                  pl.BlockSpec((1, tm, G), lambda b, i: (b, i, 0)),
                  pl.BlockSpec((1, tm, 2 * LANES), lambda b, i: (b, i, 0)),
                  row, full(a2fp), full(a2bp), full(g2), full(wo), full(a0),
                  vec, vec, vec, vec,
                  pl.BlockSpec((1, 1, D), _bsel(gate)), vec,
                  pl.BlockSpec((1, 1, D), _bsel(shift)),
                  pl.BlockSpec((1, 1, D), _bsel(scale))],
        out_specs=[row, row],
        out_shape=[jax.ShapeDtypeStruct((Bn, T, D), F32), jax.ShapeDtypeStruct((Bn, T, D), BF16)],
        scratch_shapes=[pltpu.VMEM((tm, D), BF16)],
        compiler_params=_params("parallel", "parallel"),
        name=name,
    )(yf, yb, r, k, v, gs, aw, res, a2fp, a2bp, g2, wo, a0, ka, rk, lnw, lnb, gate, g[None], shift, scale)


def _conv_kernel(gb_ref, z_ref, cw_ref, o_ref):
    z = z_ref[0].astype(F32)
    T = z.shape[0]
    t = lax.broadcasted_iota(jnp.int32, (T, 1), 0)
    zp = jnp.where(t == 0, 0.0, pltpu.roll(z, 1, 0))
    zn = jnp.where(t == T - 1, 0.0, pltpu.roll(z, T - 1, 0))
    conv = zp * cw_ref[0:1, :] + z * cw_ref[1:2, :] + zn * cw_ref[2:3, :]
    o_ref[0] = (gb_ref[0] * conv).astype(o_ref.dtype)


def _conv(gb, z, cw):
    Bn, T, D = z.shape
    tn = _tile(D, 512, LANES)
    tok = pl.BlockSpec((1, T, tn), lambda b, j: (b, 0, j))
    return pl.pallas_call(
        _conv_kernel,
        grid=(Bn, D // tn),
        in_specs=[tok, tok, pl.BlockSpec((3, tn), lambda b, j: (0, j))],
        out_specs=tok,
        out_shape=jax.ShapeDtypeStruct((Bn, T, D), BF16),
        compiler_params=_params("parallel", "parallel"),
        name="short_conv",
    )(gb, z, cw)


def _pad_rows(w, rows):
    return jnp.pad(w, ((0, rows - w.shape[0]), (0, 0)))


def _pad_cols(w, cols):
    return jnp.pad(w, ((0, 0), (0, cols - w.shape[1])))


def _split_mod(mod_rows, D):
    return [mod_rows[:, m * D:(m + 1) * D][:, None, :] for m in range(6)]


def _ffn_branch(t1, h2, mods, ffn, tag):
    w13, w2, layer, casts = ffn
    act, wdn = _ffn_up(h2, w13, w2, layer, name="ffn_up_" + tag)
    return _mm_res(act, wdn, t1, mods[5], name="ffn_down_" + tag, casts=casts)


def _rwkv_layer(x, ctx, mods_x, mods_c, g1, g2n, mix, wr, wk, wv, wo, w0, w1, w2, a0, a1, a2,
                lg1, lg2, k_k, k_a, r_k, ln_w, ln_b, ffn):
    D = x.shape[-1]
    H = D // HEAD
    w1cat = jnp.concatenate([_pad_cols(w1[0], LANES), _pad_cols(w1[1], LANES)], axis=1).astype(BF16)
    a1cat = jnp.concatenate([_pad_cols(a1[0], LANES), _pad_cols(a1[1], LANES)], axis=1).astype(BF16)
    w2p = [_pad_rows(w2[d], LANES).astype(BF16) for d in range(2)]
    a2p = [_pad_rows(a2[d], LANES).astype(BF16) for d in range(2)]
    wr, wk, wv, wo = (t.astype(BF16) for t in (wr, wk, wv, wo))
    lg1, lg2 = lg1.astype(BF16), lg2.astype(BF16)
    rk = r_k.reshape(1, D)

    xr, xw, xk, xv, xa, xg = _prep_ctx(ctx, g1, mods_c[0], mods_c[1], mix)
    lora_c = dict(tw=_mm(xw, w1cat, BF16, act="tanh", name="lora_w_c"),
                  aw=_mm(xa, a1cat, BF16, name="lora_a_c"),
                  gs=_mm(xg, lg1, BF16, act="sigmoid", name="lora_g_c"))
    ins = {"c": (xr, xk, xv, lora_c["tw"], lora_c["aw"], lora_c["gs"]),
           "x": _prep_latent(x, g1, mods_x[0], mods_x[1], mix, w1cat, a1cat, lg1)}
    sets = {}
    for tag, (xr, xk, xv, tw, aw, gs) in ins.items():
        sets[tag] = dict(r=_mm(xr, wr, BF16, name="proj_r_" + tag),
                         k=_mm(xk, wk, BF16, name="proj_k_" + tag),
                         v=_mm(xv, wv, BF16, name="proj_v_" + tag), tw=tw, aw=aw, gs=gs)

    ys = {"c": [], "x": []}
    zero_state = jnp.zeros((x.shape[0], H // 2, 2 * HEAD, 2 * HEAD), F32)
    for d in range(2):
        state = zero_state
        for tag in ("c", "x"):
            s = sets[tag]
            y, state = _wkv(s["r"], s["k"], s["v"], s["tw"], s["aw"], w2p[d], a2p[d],
                            w0[d][None], a0[d][None], k_k[None], k_a[None], state, d)
            ys[tag].append(y)

    outs = []
    for tag, tok, mods in (("c", ctx, mods_c), ("x", x, mods_x)):
        s = sets[tag]
        t1, h2 = _rwkv_out(ys[tag][0], ys[tag][1], s["r"], s["k"], s["v"], s["gs"], s["aw"], tok,
                           a2p[0], a2p[1], lg2, wo, a0, k_a[None], rk, ln_w[None], ln_b[None],
                           mods[2], g2n, mods[3], mods[4], name="rwkv_out_" + tag)
        outs.append(_ffn_branch(t1, h2, mods, ffn, tag))
    (ctx_out, _), (x_out, cast_out) = outs
    return x_out, ctx_out, cast_out


def _conv_layer(x, mods, g1, g2n, w_in, conv_w, w_out, ffn):
    gb, z = _norm_mm(x, g1, mods[0], mods[1], w_in.astype(BF16), 3, _conv_in_combine, (BF16, BF16),
                     name="conv_in")
    p = _conv(gb, z, conv_w)
    t1, h2 = _mm_res_norm(p, w_out.astype(BF16), x, mods[2], g2n, mods[3], mods[4], name="conv_out")
    return _ffn_branch(t1, h2, mods, ffn, "x")


def kernel(x, c, ctx, c_ctx, norm1_g, norm2_g, ada_w, ada_b, rw_mix, rw_wr, rw_wk, rw_wv, rw_wo,
           rw_w0, rw_w1, rw_w2, rw_a0, rw_a1, rw_a2, rw_g1, rw_g2, rw_kk, rw_ka, rw_rk, rw_lnw,
           rw_lnb, sc_win, sc_conv, sc_wout, ffn_w13, ffn_w2, final_g):
    B, T, D = x.shape
    depth = norm1_g.shape[0]
    rows = -(-(B + 1) // 8) * 8
    cond = jnp.zeros((rows, D), F32).at[:B].set(c).at[B].set(c_ctx)
    conv_w = {}
    for i in range(depth):
        last = i == depth - 1
        j = i // 2
        mod = _ada(cond, ada_w, ada_b, i)
        mods_x = _split_mod(mod[:B], D)
        mods_c = _split_mod(mod[B:B + 1], D)
        nxt = (i + 1) // 2
        casts = (sc_win[nxt], sc_wout[nxt]) if (not last and i % 2 == 0) else ()
        ffn = (ffn_w13, ffn_w2, i, casts)
        if i % 2 == 0:
            x, ctx, cast_out = _rwkv_layer(
                x, ctx, mods_x, mods_c, norm1_g[i], norm2_g[i], rw_mix[j], rw_wr[j], rw_wk[j],
                rw_wv[j], rw_wo[j], rw_w0[j], rw_w1[j], rw_w2[j], rw_a0[j], rw_a1[j], rw_a2[j],
                rw_g1[j], rw_g2[j], rw_kk[j], rw_ka[j], rw_rk[j], rw_lnw[j], rw_lnb[j], ffn)
            if casts:
                conv_w[nxt] = cast_out
        else:
            w_in, w_out = conv_w.get(j, (sc_win[j], sc_wout[j]))
            if not last:
                ctx, _ = _conv_layer(ctx, mods_c, norm1_g[i], norm2_g[i], w_in, sc_conv[j], w_out,
                                     (ffn_w13, ffn_w2, i, ()))
            x, _ = _conv_layer(x, mods_x, norm1_g[i], norm2_g[i], w_in, sc_conv[j], w_out, ffn)
    zeros = jnp.zeros((1, 1, D), F32)
    return _norm(x, final_g, zeros, zeros)
```

```python
import functools
import math

import jax
import jax.numpy as jnp
from jax import lax
from jax.experimental import pallas as pl
from jax.experimental.pallas import tpu as pltpu

HEAD = 64
GRID_W = 64
CHUNK = 64
NORM_EPS = 1e-6
GN_EPS = 64e-5
LANES = 128
SUBLANES = 8
ROWS_BF16 = 16
VMEM_LIMIT = 56 * 1024 * 1024

F32 = jnp.float32
BF16 = jnp.bfloat16


def _params(*sem):
    return pltpu.CompilerParams(dimension_semantics=sem, vmem_limit_bytes=VMEM_LIMIT)


def _tile(n, pref, mult):
    t = min(pref, n)
    t -= t % mult
    while t >= mult:
        if n % t == 0:
            return t
        t -= mult
    return n


def _sigmoid(x):
    return 1.0 / (1.0 + jnp.exp(-x))


def _norm_mod(x, g, shift, scale, cols=None):
    rs = lax.rsqrt(jnp.mean(x * x, axis=-1, keepdims=True) + NORM_EPS)
    if cols is not None:
        x, g, shift, scale = (t[:, cols[0]:cols[1]] for t in (x, g, shift, scale))
    return (x * rs) * (g * (1.0 + scale)) + shift


def _dot(a, b):
    return jnp.dot(a, b, preferred_element_type=F32)


def _dot_t(a, b, ca, cb):
    return lax.dot_general(a, b, (((ca,), (cb,)), ((), ())), preferred_element_type=F32)


def _ada_kernel(c_ref, w_ref, b_ref, o_ref):
    c = c_ref[...]
    s = c * _sigmoid(c)
    o_ref[...] = _dot(s.astype(BF16), w_ref[0].astype(BF16)) + b_ref[0]


def _ada(cond, w, b, layer):
    R, D = cond.shape
    N = w.shape[2]
    tn = _tile(N, 1024, LANES)
    return pl.pallas_call(
        _ada_kernel,
        grid=(N // tn,),
        in_specs=[pl.BlockSpec((R, D), lambda j: (0, 0)),
                  pl.BlockSpec((1, D, tn), lambda j: (layer, 0, j)),
                  pl.BlockSpec((1, 1, tn), lambda j: (layer, 0, j))],
        out_specs=pl.BlockSpec((R, tn), lambda j: (0, j)),
        out_shape=jax.ShapeDtypeStruct((R, N), F32),
        compiler_params=_params("parallel"),
        name="ada_mod",
    )(cond, w, b[:, None, :])


def _norm_kernel(x_ref, g_ref, sh_ref, sc_ref, o_ref):
    o_ref[0] = _norm_mod(x_ref[0], g_ref[...], sh_ref[0], sc_ref[0])


def _bsel(arr):
    if arr.shape[0] == 1:
        return lambda b, *_: (0, 0, 0)
    return lambda b, *_: (b, 0, 0)


def _norm(x, g, shift, scale):
    Bn, T, D = x.shape
    tm = _tile(T, 512, SUBLANES)
    return pl.pallas_call(
        _norm_kernel,
        grid=(Bn, T // tm),
        in_specs=[pl.BlockSpec((1, tm, D), lambda b, i: (b, i, 0)),
                  pl.BlockSpec((1, D), lambda b, i: (0, 0)),
                  pl.BlockSpec((1, 1, D), _bsel(shift)),
                  pl.BlockSpec((1, 1, D), _bsel(scale))],
        out_specs=pl.BlockSpec((1, tm, D), lambda b, i: (b, i, 0)),
        out_shape=jax.ShapeDtypeStruct((Bn, T, D), F32),
        compiler_params=_params("parallel", "parallel"),
        name="norm",
    )(x, g[None], shift, scale)


def _write_mix(out_refs, mix_ref, h, shifted, c0, c1):
    xx = shifted - h
    for m, o_ref in enumerate(out_refs):
        o_ref[0, :, c0:c1] = (h + xx * mix_ref[m:m + 1, c0:c1]).astype(o_ref.dtype)


def _prep_latent_kernel(x_ref, xu_ref, xd_ref, g_ref, sh_ref, sc_ref, mix_ref, w1_ref, a1_ref, g1_ref,
                        xr_ref, xk_ref, xv_ref, tw_ref, aw_ref, gs_ref, h_scr, xm_scr):
    i = pl.program_id(1)
    n = pl.num_programs(1)
    g, sh, sc = g_ref[...], sh_ref[0], sc_ref[0]
    tm, D = x_ref.shape[1:]
    q = D // 4
    W = GRID_W
    h_scr[0:W, 2 * q:3 * q] = jnp.where(i > 0, _norm_mod(xu_ref[0], g, sh, sc, (2 * q, 3 * q)), 0.0)
    h_scr[W - SUBLANES:W, 0:q] = jnp.zeros((SUBLANES, q), F32)
    for r0 in range(0, tm, W):
        h_scr[W + r0:2 * W + r0, :] = _norm_mod(x_ref[0, r0:r0 + W, :], g, sh, sc)
    h_scr[W + tm:, 3 * q:] = jnp.where(i < n - 1, _norm_mod(xd_ref[0], g, sh, sc, (3 * q, D)), 0.0)
    h_scr[W + tm:W + tm + SUBLANES, q:2 * q] = jnp.zeros((SUBLANES, q), F32)
    wide = {0: xr_ref, 2: xk_ref, 3: xv_ref}
    lora = {1: 0, 4: 1, 5: 2}
    lora_w = (w1_ref, a1_ref, g1_ref)
    acc = [None] * 3
    R = 32
    row = lax.broadcasted_iota(jnp.int32, (R, 1), 0)
    for k, off in enumerate((-1, 1, -W, W)):
        c0, c1 = k * q, (k + 1) * q
        mixk = [mix_ref[m:m + 1, c0:c1] for m in range(6)]
        for r0 in range(0, tm, R):
            h = h_scr[W + r0:W + r0 + R, c0:c1]
            s = h_scr[W + r0 + off:W + r0 + off + R, c0:c1]
            if off == -1 and r0 % W == 0:
                s = jnp.where(row == 0, 0.0, s)
            if off == 1 and (r0 + R) % W == 0:
                s = jnp.where(row == R - 1, 0.0, s)
            xx = s - h
            for m in range(6):
                xm = (h + xx * mixk[m]).astype(BF16)
                if m in wide:
                    wide[m][0, r0:r0 + R, c0:c1] = xm
                else:
                    xm_scr[lora[m], r0:r0 + R, c0:c1] = xm
        for l in range(3):
            part = _dot(xm_scr[l, :, c0:c1], lora_w[l][c0:c1, :])
            acc[l] = part if acc[l] is None else acc[l] + part
    tw_ref[0] = jnp.tanh(acc[0]).astype(tw_ref.dtype)
    aw_ref[0] = acc[1].astype(aw_ref.dtype)
    gs_ref[0] = _sigmoid(acc[2]).astype(gs_ref.dtype)


def _prep_latent(x, g, shift, scale, mix, w1cat, a1cat, lg1):
    Bn, T, D = x.shape
    rows_per_tile = _tile(T // GRID_W, 4, 1)
    tm = rows_per_tile * GRID_W
    nrow = T // GRID_W
    tok = pl.BlockSpec((1, tm, D), lambda b, i: (b, i, 0))
    full = lambda arr: pl.BlockSpec(arr.shape, lambda b, i: (0,) * arr.ndim)
    small = lambda arr: pl.BlockSpec((1, tm, arr.shape[1]), lambda b, i: (b, i, 0))
    return pl.pallas_call(
        _prep_latent_kernel,
        grid=(Bn, T // tm),
        in_specs=[tok,
                  pl.BlockSpec((1, GRID_W, D),
                               lambda b, i: (b, jnp.maximum(i * rows_per_tile - 1, 0), 0)),
                  pl.BlockSpec((1, GRID_W, D),
                               lambda b, i: (b, jnp.minimum((i + 1) * rows_per_tile, nrow - 1), 0)),
                  pl.BlockSpec((1, D), lambda b, i: (0, 0)),
                  pl.BlockSpec((1, 1, D), _bsel(shift)),
                  pl.BlockSpec((1, 1, D), _bsel(scale)),
                  pl.BlockSpec((6, D), lambda b, i: (0, 0)),
                  full(w1cat), full(a1cat), full(lg1)],
        out_specs=[tok, tok, tok, small(w1cat), small(a1cat), small(lg1)],
        out_shape=[jax.ShapeDtypeStruct((Bn, T, D), BF16)] * 3
        + [jax.ShapeDtypeStruct((Bn, T, w.shape[1]), BF16) for w in (w1cat, a1cat, lg1)],
        scratch_shapes=[pltpu.VMEM((tm + 2 * GRID_W, D), F32), pltpu.VMEM((3, tm, D), BF16)],
        compiler_params=_params("parallel", "parallel"),
        name="prep_latent",
    )(x, x, x, g[None], shift, scale, mix, w1cat, a1cat, lg1)


def _prep_ctx_kernel(x_ref, g_ref, sh_ref, sc_ref, mix_ref, *out_refs):
    h = _norm_mod(x_ref[0], g_ref[...], sh_ref[0], sc_ref[0])
    L, D = h.shape
    half = D // 2
    t = lax.broadcasted_iota(jnp.int32, (L, 1), 0)
    h0, h1 = h[:, :half], h[:, half:]
    prev = jnp.where(t == 0, 0.0, pltpu.roll(h0, 1, 0))
    nxt = jnp.where(t == L - 1, 0.0, pltpu.roll(h1, L - 1, 0))
    _write_mix(out_refs, mix_ref, h0, prev, 0, half)
    _write_mix(out_refs, mix_ref, h1, nxt, half, D)


def _prep_ctx(x, g, shift, scale, mix):
    Bn, L, D = x.shape
    return pl.pallas_call(
        _prep_ctx_kernel,
        grid=(Bn,),
        in_specs=[pl.BlockSpec((1, L, D), lambda b: (b, 0, 0)),
                  pl.BlockSpec((1, D), lambda b: (0, 0)),
                  pl.BlockSpec((1, 1, D), _bsel(shift)),
                  pl.BlockSpec((1, 1, D), _bsel(scale)),
                  pl.BlockSpec((6, D), lambda b: (0, 0))],
        out_specs=[pl.BlockSpec((1, L, D), lambda b: (b, 0, 0))] * 6,
        out_shape=[jax.ShapeDtypeStruct((Bn, L, D), BF16)] * 6,
        compiler_params=_params("parallel"),
        name="prep_ctx",
    )(x, g[None], shift, scale, mix)


def _mm_kernel(a_ref, w_ref, o_ref, *, act):
    acc = _dot(a_ref[0], w_ref[...])
    if act == "tanh":
        acc = jnp.tanh(acc)
    elif act == "sigmoid":
        acc = _sigmoid(acc)
    o_ref[0] = acc.astype(o_ref.dtype)


def _mm(a, w, out_dtype, act=None, name="mm"):
    Bn, T, K = a.shape
    N = w.shape[1]
    M = Bn * T
    tm = _tile(M, 1024, ROWS_BF16)
    tn = _tile(N, 2048, LANES)
    out = pl.pallas_call(
        functools.partial(_mm_kernel, act=act),
        grid=(1, M // tm, N // tn),
        in_specs=[pl.BlockSpec((1, tm, K), lambda b, i, j: (b, i, 0)),
                  pl.BlockSpec((K, tn), lambda b, i, j: (0, j))],
        out_specs=pl.BlockSpec((1, tm, tn), lambda b, i, j: (b, i, j)),
        out_shape=jax.ShapeDtypeStruct((1, M, N), out_dtype),
        compiler_params=_params("parallel", "parallel", "parallel"),
        name=name,
    )(a.reshape(1, M, K), w)
    return out.reshape(Bn, T, N)


def _mm_res_kernel(a_ref, w_ref, res_ref, gate_ref, *rest):
    ncast = (len(rest) - 1) // 2
    cast_in, o_ref, cast_out = rest[:ncast], rest[ncast], rest[ncast + 1:]
    o_ref[0] = res_ref[0] + gate_ref[0] * _dot(a_ref[0], w_ref[...])
    for src, dst in zip(cast_in, cast_out):
        dst[...] = src[...].astype(BF16)


def _mm_res(a, w, res, gate, name="mm_res", casts=()):
    Bn, T, K = a.shape
    N = w.shape[1]
    tm = _tile(T, 1024, ROWS_BF16)
    tn = _tile(N, 1024 if K <= 2048 else 512, LANES)
    ni, nj = T // tm, N // tn
    steps = Bn * ni * nj
    gsel = _bsel(gate)
    step = lambda b, i, j: (b * ni + i) * nj + j
    cast_specs = []
    for cw in casts:
        rows = cw.shape[0] // steps
        assert rows * steps == cw.shape[0] and rows % ROWS_BF16 == 0, (cw.shape, steps)
        cast_specs.append(pl.BlockSpec((rows, cw.shape[1]), lambda b, i, j: (step(b, i, j), 0)))
    tile = pl.BlockSpec((1, tm, tn), lambda b, i, j: (b, i, j))
    outs = pl.pallas_call(
        _mm_res_kernel,
        grid=(Bn, ni, nj),
        in_specs=[pl.BlockSpec((1, tm, K), lambda b, i, j: (b, i, 0)),
                  pl.BlockSpec((K, tn), lambda b, i, j: (0, j)),
                  tile,
                  pl.BlockSpec((1, 1, tn), lambda b, i, j: gsel(b)[:2] + (j,))] + cast_specs,
        out_specs=[tile] + cast_specs,
        out_shape=[jax.ShapeDtypeStruct((Bn, T, N), F32)]
        + [jax.ShapeDtypeStruct(cw.shape, BF16) for cw in casts],
        compiler_params=_params("arbitrary", "arbitrary", "arbitrary"),
        name=name,
    )(a, w, res, gate, *casts)
    return outs[0], tuple(outs[1:])


def _mm_res_norm_kernel(a_ref, w_ref, res_ref, gate_ref, g_ref, sh_ref, sc_ref, o_ref, h_ref, *, rows):
    for r0 in range(0, a_ref.shape[1], rows):
        rs = slice(r0, r0 + rows)
        x1 = res_ref[0, rs, :] + gate_ref[0] * _dot(a_ref[0, rs, :], w_ref[...])
        o_ref[0, rs, :] = x1
        h_ref[0, rs, :] = _norm_mod(x1, g_ref[...], sh_ref[0], sc_ref[0]).astype(h_ref.dtype)


def _mm_res_norm(a, w, res, gate, g, shift, scale, name):
    Bn, T, K = a.shape
    N = w.shape[1]
    tm = _tile(T, 512, ROWS_BF16)
    rows = _tile(tm, 256, ROWS_BF16)
    row = pl.BlockSpec((1, tm, N), lambda b, i: (b, i, 0))
    return pl.pallas_call(
        functools.partial(_mm_res_norm_kernel, rows=rows),
        grid=(Bn, T // tm),
        in_specs=[pl.BlockSpec((1, tm, K), lambda b, i: (b, i, 0)),
                  pl.BlockSpec((K, N), lambda b, i: (0, 0)),
                  row,
                  pl.BlockSpec((1, 1, N), _bsel(gate)),
                  pl.BlockSpec((1, N), lambda b, i: (0, 0)),
                  pl.BlockSpec((1, 1, N), _bsel(shift)),
                  pl.BlockSpec((1, 1, N), _bsel(scale))],
        out_specs=[row, row],
        out_shape=[jax.ShapeDtypeStruct((Bn, T, N), F32), jax.ShapeDtypeStruct((Bn, T, N), BF16)],
        compiler_params=_params("parallel", "parallel"),
        name=name,
    )(a, w, res, gate, g[None], shift, scale)


def _ffn_up_kernel(h_ref, wa_ref, wb_ref, w2_ref, o_ref, w2o_ref, w_scr, *, rows):
    @pl.when((pl.program_id(1) == 0) & (pl.program_id(2) == 0))
    def _():
        w_scr[0] = wa_ref[0].astype(BF16)
        w_scr[1] = wb_ref[0].astype(BF16)

    for r0 in range(0, h_ref.shape[1], rows):
        rs = slice(r0, r0 + rows)
        h = h_ref[0, rs, :]
        a = _dot(h, w_scr[0])
        o_ref[0, rs, :] = (a * _sigmoid(a) * _dot(h, w_scr[1])).astype(o_ref.dtype)
    w2o_ref[...] = w2_ref[0].astype(BF16)


def _ffn_up(h, w13, w2, layer, name):
    Bn, T, D = h.shape
    F = w13.shape[2] // 2
    tm = _tile(T, 2048, ROWS_BF16)
    rows = _tile(tm, 512, ROWS_BF16)
    tn = _tile(F, 512, LANES)
    nj, ni = F // tn, T // tm
    steps = nj * Bn * ni
    F2, D2 = w2.shape[1:]
    rows2 = F2 // steps
    assert rows2 * steps == F2 and rows2 % ROWS_BF16 == 0, (F2, steps)
    step = lambda j, b, i: (j * Bn + b) * ni + i
    return pl.pallas_call(
        functools.partial(_ffn_up_kernel, rows=rows),
        grid=(nj, Bn, ni),
        in_specs=[pl.BlockSpec((1, tm, D), lambda j, b, i: (b, i, 0)),
                  pl.BlockSpec((1, D, tn), lambda j, b, i: (layer, 0, j)),
                  pl.BlockSpec((1, D, tn), lambda j, b, i: (layer, 0, j + nj)),
                  pl.BlockSpec((1, rows2, D2), lambda j, b, i: (layer, step(j, b, i), 0))],
        out_specs=[pl.BlockSpec((1, tm, tn), lambda j, b, i: (b, i, j)),
                   pl.BlockSpec((rows2, D2), lambda j, b, i: (step(j, b, i), 0))],
        out_shape=[jax.ShapeDtypeStruct((Bn, T, F), BF16), jax.ShapeDtypeStruct((F2, D2), BF16)],
        scratch_shapes=[pltpu.VMEM((2, D, tn), BF16)],
        compiler_params=_params("arbitrary", "arbitrary", "arbitrary"),
        name=name,
    )(h, w13, w13, w2)


def _conv_in_combine(gb, gc, u):
    return gb, gc * u


def _norm_mm_kernel(x_ref, g_ref, sh_ref, sc_ref, *rest, nw, combine, rows):
    w_refs, out_refs, h_scr = rest[:nw], rest[nw:-1], rest[-1]

    def emit(rs, h):
        outs = combine(*[_dot(h, w_ref[...]) for w_ref in w_refs])
        for o_ref, o in zip(out_refs, outs):
            o_ref[0, rs, :] = o.astype(o_ref.dtype)

    @pl.when(pl.program_id(2) == 0)
    def _():
        for r0 in range(0, x_ref.shape[1], rows):
            rs = slice(r0, r0 + rows)
            h = _norm_mod(x_ref[0, rs, :], g_ref[...], sh_ref[0], sc_ref[0]).astype(BF16)
            h_scr[rs, :] = h
            emit(rs, h)

    @pl.when(pl.program_id(2) > 0)
    def _():
        emit(slice(None), h_scr[...])


def _norm_mm(x, g, shift, scale, w, nw, combine, out_dtypes, name):
    Bn, T, D = x.shape
    N = w.shape[1] // nw
    tm = _tile(T, 1024, ROWS_BF16)
    rows = _tile(tm, 256, ROWS_BF16)
    tn = _tile(N, 512, LANES)
    nj = N // tn
    w_specs = [pl.BlockSpec((D, tn), functools.partial(lambda b, i, j, m: (0, j + m * nj), m=m))
               for m in range(nw)]
    return pl.pallas_call(
        functools.partial(_norm_mm_kernel, nw=nw, combine=combine, rows=rows),
        grid=(Bn, T // tm, nj),
        in_specs=[pl.BlockSpec((1, tm, D), lambda b, i, j: (b, i, 0)),
                  pl.BlockSpec((1, D), lambda b, i, j: (0, 0)),
                  pl.BlockSpec((1, 1, D), _bsel(shift)),
                  pl.BlockSpec((1, 1, D), _bsel(scale))] + w_specs,
        out_specs=[pl.BlockSpec((1, tm, tn), lambda b, i, j: (b, i, j))] * len(out_dtypes),
        out_shape=[jax.ShapeDtypeStruct((Bn, T, N), dt) for dt in out_dtypes],
        scratch_shapes=[pltpu.VMEM((tm, D), BF16)],
        compiler_params=_params("parallel", "parallel", "arbitrary"),
        name=name,
    )(x, g[None], shift, scale, *([w] * nw))


def _seg_ones(width):
    shift = HEAD.bit_length() - 1
    r = lax.shift_right_logical(lax.broadcasted_iota(jnp.int32, (width, width), 0), shift)
    c = lax.shift_right_logical(lax.broadcasted_iota(jnp.int32, (width, width), 1), shift)
    return (r == c).astype(F32)


def _split3(x):
    hi = x.astype(BF16)
    r1 = x - hi.astype(F32)
    mid = r1.astype(BF16)
    lo = (r1 - mid.astype(F32)).astype(BF16)
    return hi, mid, lo


def _wkv_step(r_ref, k_ref, v_ref, tw_ref, aw_ref, w2_ref, a2_ref, w0_ref, a0_ref, kk_ref, ka_ref,
                s0_ref, y_ref, sout_ref, s_scr, x_scr, r2_scr, bv_scr, vb_scr, z_scr, wt_scr,
                *, reverse, npair, chain, slot, rows):
    C = CHUNK
    PW = 2 * HEAD

    rr = lax.shift_right_logical(lax.broadcasted_iota(jnp.int32, (PW, PW), 0), HEAD.bit_length() - 1)
    cc = lax.shift_right_logical(lax.broadcasted_iota(jnp.int32, (PW, PW), 1), HEAD.bit_length() - 1)
    same = rr == cc
    same_bf = same.astype(BF16)

    def bd(x):
        xb = x.astype(BF16)
        return jnp.concatenate([xb, xb], axis=0) * same_bf

    t2 = lax.broadcasted_iota(jnp.int32, (C, PW), 0)
    s2 = lax.broadcasted_iota(jnp.int32, (C, PW), 1) & (HEAD - 1)
    before = (s2 > t2) if reverse else (s2 < t2)
    upto = before | (s2 == t2)
    pairs = range(npair)
    sls = [slice(p * PW, (p + 1) * PW) for p in pairs]

    r, k, v = (t[0, rows, :].astype(F32) for t in (r_ref, k_ref, v_ref))
    z = w0_ref[...] + _dot(tw_ref[0, rows, :], w2_ref[...])
    a_pre = a0_ref[...] + _dot(aw_ref[0, rows, :], a2_ref[...])

    if chain:
        S = [s_scr[p] for p in pairs]
        X = [x_scr[slot, p] for p in pairs]
        G = [_dot_t(X[p], r2_scr[slot, p], 1, 1) for p in pairs]
        XS = [_dot_t(X[p], S[p].astype(BF16), 1, 1) for p in pairs]
        u = [XS[p][:C] + _dot(jnp.where(before, G[p][:C, PW:], 0.0).astype(BF16), bv_scr[slot, p]) for p in pairs]
        P = [jnp.where(before, G[p][:C, :PW], 0.0) for p in pairs]

    lw = -math.exp(-0.5) * _sigmoid(z)
    kkv = k * kk_ref[...]
    kk2 = kkv * kkv
    t_i = lax.broadcasted_iota(jnp.int32, (C, C), 0)
    s_i = lax.broadcasted_iota(jnp.int32, (C, C), 1)
    tri = ((s_i >= t_i) if reverse else (s_i <= t_i)).astype(BF16)
    cum = _dot(jnp.concatenate([tri, tri], axis=1),
               jnp.concatenate(_split3(lw)[:2], axis=0))
    ss = [_dot(kk2[:, sl].astype(BF16), same_bf) for sl in sls]

    if chain:
        n_sq = C.bit_length() - 1
        for j in range(n_sq):
            Pb = [P[p].astype(BF16) for p in pairs]
            if j < n_sq - 1:
                PU = [_dot(Pb[p], jnp.concatenate([bd(P[p]), bd(u[p])], axis=1)) for p in pairs]
                P = [PU[p][:, :PW] for p in pairs]
                u = [u[p] + PU[p][:, PW:] for p in pairs]
            else:
                u = [u[p] + _dot(Pb[p], bd(u[p])) for p in pairs]
        for p in pairs:
            R = jnp.concatenate([jnp.where(upto, G[p][C:, :PW], 0.0),
                                 jnp.where(upto, G[p][C:, PW:], 0.0)], axis=1).astype(BF16)
            y_ref[0, rows, sls[p]] = XS[p][C:] + _dot(R, jnp.concatenate([bd(u[p]), bv_scr[slot, p]], axis=0))
        for p in pairs:
            UV = jnp.concatenate([u[p].astype(BF16), vb_scr[slot, p]], axis=0)
            dS = _dot_t(UV, z_scr[slot, p], 0, 0)
            s_scr[p] = (S[p] + jnp.where(same, dS, 0.0)) * wt_scr[slot, p, 0:1, :]

    a_sig = _sigmoid(a_pre)
    kd = k * (1.0 + (a_sig - 1.0) * ka_ref[...])
    e_pos = jnp.exp(cum)
    e_neg = jnp.exp(-cum)
    e_prev = jnp.exp(cum - lw)
    last = 0 if reverse else C - 1
    for p, sl in zip(pairs, sls):
        kkn = kkv[:, sl] * lax.rsqrt(jnp.maximum(ss[p], 1e-24))
        at = (-kkn) * e_prev[:, sl]
        bt = (kkn * a_sig[:, sl]) * e_neg[:, sl]
        rt = r[:, sl] * e_pos[:, sl]
        kt = kd[:, sl] * e_neg[:, sl]
        x_scr[slot, p] = jnp.concatenate([at, rt], axis=0).astype(BF16)
        r2_scr[slot, p] = jnp.concatenate([bd(bt), bd(kt)], axis=0)
        bv_scr[slot, p] = bd(v[:, sl])
        vb_scr[slot, p] = v[:, sl].astype(BF16)
        z_scr[slot, p] = jnp.concatenate([bt, kt], axis=0).astype(BF16)
        wt_scr[slot, p] = jnp.broadcast_to(e_pos[last:last + 1, sl], wt_scr.shape[2:])


def _wkv_kernel(*refs, reverse, npair):
    s0_ref, sout_ref, s_scr = refs[11], refs[13], refs[14]
    c = pl.program_id(2)
    halves = [slice(h * CHUNK, (h + 1) * CHUNK) for h in ((1, 0) if reverse else (0, 1))]

    @pl.when(c == 0)
    def _():
        s_scr[...] = s0_ref[0]
        for slot, rows in enumerate(halves):
            _wkv_step(*refs, reverse=reverse, npair=npair, chain=False, slot=slot, rows=rows)

    @pl.when(c > 0)
    def _():
        for slot, rows in enumerate(halves):
            _wkv_step(*refs, reverse=reverse, npair=npair, chain=True, slot=slot, rows=rows)

    @pl.when(c == pl.num_programs(2) - 1)
    def _():
        sout_ref[0] = s_scr[...]


def _wkv(r, k, v, tw, aw, w2p, a2p, w0, a0, kk, ka, s0, d):
    Bn, T, D = r.shape
    PW = 2 * HEAD
    npairs = D // PW
    npair = _tile(npairs, 16, 1)
    hw = npair * PW
    rows = 2 * CHUNK
    nc = T // rows
    assert nc * rows == T, T
    reverse = d == 1
    pos = (lambda j: nc - 1 - j) if reverse else (lambda j: j)
    cin = lambda c: pos(jnp.minimum(c, nc - 1))
    cout = lambda c: pos(jnp.maximum(c - 1, 0))
    tok = pl.BlockSpec((1, rows, hw), lambda b, g, c: (b, cin(c), g))
    lora = pl.BlockSpec((1, rows, LANES), lambda b, g, c: (b, cin(c), d))
    lw2 = pl.BlockSpec((LANES, hw), lambda b, g, c: (0, g))
    vec = pl.BlockSpec((1, hw), lambda b, g, c: (0, g))
    st = pl.BlockSpec((1, npair, PW, PW), lambda b, g, c: (b, g, 0, 0))
    return pl.pallas_call(
        functools.partial(_wkv_kernel, reverse=reverse, npair=npair),
        grid=(Bn, npairs // npair, nc + 1),
        in_specs=[tok, tok, tok, lora, lora, lw2, lw2, vec, vec, vec, vec, st],
        out_specs=[pl.BlockSpec((1, rows, hw), lambda b, g, c: (b, cout(c), g)), st],
        out_shape=[jax.ShapeDtypeStruct((Bn, T, D), F32),
                   jax.ShapeDtypeStruct((Bn, npairs, PW, PW), F32)],
        scratch_shapes=[pltpu.VMEM((npair, PW, PW), F32),
                        pltpu.VMEM((2, npair, 2 * CHUNK, PW), BF16),
                        pltpu.VMEM((2, npair, 2 * PW, PW), BF16),
                        pltpu.VMEM((2, npair, PW, PW), BF16),
                        pltpu.VMEM((2, npair, CHUNK, PW), BF16),
                        pltpu.VMEM((2, npair, 2 * CHUNK, PW), BF16),
                        pltpu.VMEM((2, npair, SUBLANES, PW), F32)],
        compiler_params=_params("parallel", "parallel", "arbitrary"),
        name="wkv_rev" if reverse else "wkv_fwd",
    )(r, k, v, tw, aw, w2p, a2p, w0, a0, kk, ka, s0)


def _rwkv_out_kernel(yf_ref, yb_ref, r_ref, k_ref, v_ref, gs_ref, aw_ref, res_ref, a2f_ref, a2b_ref,
                     g2_ref, wo_ref, a0_ref, ka_ref, rk_ref, lnw_ref, lnb_ref, gate_ref, g_ref, sh_ref,
                     sc_ref, o_ref, h_ref, og_scr, *, rows):
    tm, D = o_ref.shape[1:]
    PW = 2 * HEAD
    same = _seg_ones(PW).astype(BF16)
    same2 = jnp.concatenate([same, same], axis=0)

    def head_sum(x, pieces):
        if pieces == 1:
            return _dot(x.astype(BF16), same)
        return _dot(jnp.concatenate(_split3(x)[:2], axis=1), same2)

    for r0 in range(0, tm, rows):
        rs = slice(r0, r0 + rows)
        aw = aw_ref[0, rs, :]
        a_f = _sigmoid(a0_ref[0:1, :] + _dot(aw[:, :LANES], a2f_ref[...]))
        a_b = _sigmoid(a0_ref[1:2, :] + _dot(aw[:, LANES:], a2b_ref[...]))
        g = _dot(gs_ref[0, rs, :], g2_ref[...])
        for p in range(D // PW):
            sl = slice(p * PW, (p + 1) * PW)
            ksum = k_ref[0, rs, sl].astype(F32) * (2.0 + (a_f[:, sl] + a_b[:, sl] - 2.0) * ka_ref[:, sl])
            y = yf_ref[0, rs, sl] + yb_ref[0, rs, sl]
            yc = y - head_sum(y, 2) * (1.0 / HEAD)
            var = head_sum(yc * yc, 1) * (1.0 / HEAD)
            o = yc * lax.rsqrt(var + GN_EPS) * lnw_ref[:, sl] + lnb_ref[:, sl]
            bonus = (head_sum(r_ref[0, rs, sl].astype(F32) * ksum * rk_ref[:, sl], 1)
                     * v_ref[0, rs, sl].astype(F32))
            og_scr[rs, sl] = ((o + bonus) * g[:, sl]).astype(og_scr.dtype)
        x1 = res_ref[0, rs, :] + gate_ref[0] * _dot(og_scr[rs, :], wo_ref[...])
        o_ref[0, rs, :] = x1
        h_ref[0, rs, :] = _norm_mod(x1, g_ref[...], sh_ref[0], sc_ref[0]).astype(h_ref.dtype)


def _rwkv_out(yf, yb, r, k, v, gs, aw, res, a2fp, a2bp, g2, wo, a0, ka, rk, lnw, lnb, gate, g, shift,
              scale, name):
    Bn, T, D = res.shape
    tm = _tile(T, 256, ROWS_BF16)
    rows = _tile(tm, 256, ROWS_BF16)
    G = gs.shape[-1]
    row = pl.BlockSpec((1, tm, D), lambda b, i: (b, i, 0))
    vec = pl.BlockSpec((1, D), lambda b, i: (0, 0))
    full = lambda arr: pl.BlockSpec(arr.shape, lambda b, i: (0,) * arr.ndim)
    return pl.pallas_call(
        functools.partial(_rwkv_out_kernel, rows=rows),
        grid=(Bn, T // tm),
        in_specs=[row, row, row, row, row,
                  pl.BlockSpec((1, tm, G), lambda b, i: (b, i, 0)),
                  pl.BlockSpec((1, tm, 2 * LANES), lambda b, i: (b, i, 0)),
                  row, full(a2fp), full(a2bp), full(g2), full(wo), full(a0),
                  vec, vec, vec, vec,
                  pl.BlockSpec((1, 1, D), _bsel(gate)), vec,
                  pl.BlockSpec((1, 1, D), _bsel(shift)),
                  pl.BlockSpec((1, 1, D), _bsel(scale))],
        out_specs=[row, row],
        out_shape=[jax.ShapeDtypeStruct((Bn, T, D), F32), jax.ShapeDtypeStruct((Bn, T, D), BF16)],
        scratch_shapes=[pltpu.VMEM((tm, D), BF16)],
        compiler_params=_params("parallel", "parallel"),
        name=name,
    )(yf, yb, r, k, v, gs, aw, res, a2fp, a2bp, g2, wo, a0, ka, rk, lnw, lnb, gate, g[None], shift, scale)


CONV_SLOTS = 3


def _conv_kernel(gb_hbm, z_hbm, cw_ref, o_ref, gbuf, zbuf, sem, *, nj, nsteps):
    tn = o_ref.shape[-1]
    s = pl.program_id(0) * nj + pl.program_id(1)

    def copies(step, slot):
        b, j = step // nj, step % nj
        cols = pl.ds(pl.multiple_of(j * tn, tn), tn)
        return (pltpu.make_async_copy(gb_hbm.at[b, :, cols], gbuf.at[slot], sem.at[0, slot]),
                pltpu.make_async_copy(z_hbm.at[b, :, cols], zbuf.at[slot], sem.at[1, slot]))

    @pl.when(s == 0)
    def _():
        for step in range(min(CONV_SLOTS - 1, nsteps)):
            for cp in copies(step, step):
                cp.start()

    @pl.when(s + CONV_SLOTS - 1 < nsteps)
    def _():
        nxt = s + CONV_SLOTS - 1
        for cp in copies(nxt, nxt % CONV_SLOTS):
            cp.start()

    slot = s % CONV_SLOTS
    for cp in copies(s, slot):
        cp.wait()
    z = zbuf[slot].astype(F32)
    T = z.shape[0]
    t = lax.broadcasted_iota(jnp.int32, (T, 1), 0)
    zp = jnp.where(t == 0, 0.0, pltpu.roll(z, 1, 0))
    zn = jnp.where(t == T - 1, 0.0, pltpu.roll(z, T - 1, 0))
    conv = zp * cw_ref[0:1, :] + z * cw_ref[1:2, :] + zn * cw_ref[2:3, :]
    o_ref[0] = (gbuf[slot] * conv).astype(o_ref.dtype)


def _conv(gb, z, cw):
    Bn, T, D = z.shape
    tn = _tile(D, 512, LANES)
    nj = D // tn
    return pl.pallas_call(
        functools.partial(_conv_kernel, nj=nj, nsteps=Bn * nj),
        grid=(Bn, nj),
        in_specs=[pl.BlockSpec(memory_space=pl.ANY), pl.BlockSpec(memory_space=pl.ANY),
                  pl.BlockSpec((3, tn), lambda b, j: (0, j))],
        out_specs=pl.BlockSpec((1, T, tn), lambda b, j: (b, 0, j)),
        out_shape=jax.ShapeDtypeStruct((Bn, T, D), BF16),
        scratch_shapes=[pltpu.VMEM((CONV_SLOTS, T, tn), gb.dtype), pltpu.VMEM((CONV_SLOTS, T, tn), z.dtype),
                        pltpu.SemaphoreType.DMA((2, CONV_SLOTS))],
        compiler_params=_params("arbitrary", "arbitrary"),
        name="short_conv",
    )(gb, z, cw)


def _pad_rows(w, rows):
    return jnp.pad(w, ((0, rows - w.shape[0]), (0, 0)))


def _pad_cols(w, cols):
    return jnp.pad(w, ((0, 0), (0, cols - w.shape[1])))


def _split_mod(mod_rows, D):
    return [mod_rows[:, m * D:(m + 1) * D][:, None, :] for m in range(6)]


def _ffn_branch(t1, h2, mods, ffn, tag):
    w13, w2, layer, casts = ffn
    act, wdn = _ffn_up(h2, w13, w2, layer, name="ffn_up_" + tag)
    return _mm_res(act, wdn, t1, mods[5], name="ffn_down_" + tag, casts=casts)


def _rwkv_layer(x, ctx, mods_x, mods_c, g1, g2n, mix, wr, wk, wv, wo, w0, w1, w2, a0, a1, a2,
                lg1, lg2, k_k, k_a, r_k, ln_w, ln_b, ffn):
    D = x.shape[-1]
    H = D // HEAD
    w1cat = jnp.concatenate([_pad_cols(w1[0], LANES), _pad_cols(w1[1], LANES)], axis=1).astype(BF16)
    a1cat = jnp.concatenate([_pad_cols(a1[0], LANES), _pad_cols(a1[1], LANES)], axis=1).astype(BF16)
    w2p = [_pad_rows(w2[d], LANES).astype(BF16) for d in range(2)]
    a2p = [_pad_rows(a2[d], LANES).astype(BF16) for d in range(2)]
    wr, wk, wv, wo = (t.astype(BF16) for t in (wr, wk, wv, wo))
    lg1, lg2 = lg1.astype(BF16), lg2.astype(BF16)
    rk = r_k.reshape(1, D)

    xr, xw, xk, xv, xa, xg = _prep_ctx(ctx, g1, mods_c[0], mods_c[1], mix)
    lora_c = dict(tw=_mm(xw, w1cat, BF16, act="tanh", name="lora_w_c"),
                  aw=_mm(xa, a1cat, BF16, name="lora_a_c"),
                  gs=_mm(xg, lg1, BF16, act="sigmoid", name="lora_g_c"))
    ins = {"c": (xr, xk, xv, lora_c["tw"], lora_c["aw"], lora_c["gs"]),
           "x": _prep_latent(x, g1, mods_x[0], mods_x[1], mix, w1cat, a1cat, lg1)}
    sets = {}
    for tag, (xr, xk, xv, tw, aw, gs) in ins.items():
        sets[tag] = dict(r=_mm(xr, wr, BF16, name="proj_r_" + tag),
                         k=_mm(xk, wk, BF16, name="proj_k_" + tag),
                         v=_mm(xv, wv, BF16, name="proj_v_" + tag), tw=tw, aw=aw, gs=gs)

    ys = {"c": [], "x": []}
    zero_state = jnp.zeros((x.shape[0], H // 2, 2 * HEAD, 2 * HEAD), F32)
    for d in range(2):
        state = zero_state
        for tag in ("c", "x"):
            s = sets[tag]
            y, state = _wkv(s["r"], s["k"], s["v"], s["tw"], s["aw"], w2p[d], a2p[d],
                            w0[d][None], a0[d][None], k_k[None], k_a[None], state, d)
            ys[tag].append(y)

    outs = []
    for tag, tok, mods in (("c", ctx, mods_c), ("x", x, mods_x)):
        s = sets[tag]
        t1, h2 = _rwkv_out(ys[tag][0], ys[tag][1], s["r"], s["k"], s["v"], s["gs"], s["aw"], tok,
                           a2p[0], a2p[1], lg2, wo, a0, k_a[None], rk, ln_w[None], ln_b[None],
                           mods[2], g2n, mods[3], mods[4], name="rwkv_out_" + tag)
        outs.append(_ffn_branch(t1, h2, mods, ffn, tag))
    (ctx_out, _), (x_out, cast_out) = outs
    return x_out, ctx_out, cast_out


def _conv_layer(x, mods, g1, g2n, w_in, conv_w, w_out, ffn):
    gb, z = _norm_mm(x, g1, mods[0], mods[1], w_in.astype(BF16), 3, _conv_in_combine, (BF16, BF16),
                     name="conv_in")
    p = _conv(gb, z, conv_w)
    t1, h2 = _mm_res_norm(p, w_out.astype(BF16), x, mods[2], g2n, mods[3], mods[4], name="conv_out")
    return _ffn_branch(t1, h2, mods, ffn, "x")


def kernel(x, c, ctx, c_ctx, norm1_g, norm2_g, ada_w, ada_b, rw_mix, rw_wr, rw_wk, rw_wv, rw_wo,
           rw_w0, rw_w1, rw_w2, rw_a0, rw_a1, rw_a2, rw_g1, rw_g2, rw_kk, rw_ka, rw_rk, rw_lnw,
           rw_lnb, sc_win, sc_conv, sc_wout, ffn_w13, ffn_w2, final_g):
    B, T, D = x.shape
    depth = norm1_g.shape[0]
    rows = -(-(B + 1) // 8) * 8
    cond = jnp.zeros((rows, D), F32).at[:B].set(c).at[B].set(c_ctx)
    conv_w = {}
    for i in range(depth):
        last = i == depth - 1
        j = i // 2
        mod = _ada(cond, ada_w, ada_b, i)
        mods_x = _split_mod(mod[:B], D)
        mods_c = _split_mod(mod[B:B + 1], D)
        nxt = (i + 1) // 2
        casts = (sc_win[nxt], sc_wout[nxt]) if (not last and i % 2 == 0) else ()
        ffn = (ffn_w13, ffn_w2, i, casts)
        if i % 2 == 0:
            x, ctx, cast_out = _rwkv_layer(
                x, ctx, mods_x, mods_c, norm1_g[i], norm2_g[i], rw_mix[j], rw_wr[j], rw_wk[j],
                rw_wv[j], rw_wo[j], rw_w0[j], rw_w1[j], rw_w2[j], rw_a0[j], rw_a1[j], rw_a2[j],
                rw_g1[j], rw_g2[j], rw_kk[j], rw_ka[j], rw_rk[j], rw_lnw[j], rw_lnb[j], ffn)
            if casts:
                conv_w[nxt] = cast_out
        else:
            w_in, w_out = conv_w.get(j, (sc_win[j], sc_wout[j]))
            if not last:
                ctx, _ = _conv_layer(ctx, mods_c, norm1_g[i], norm2_g[i], w_in, sc_conv[j], w_out,
                                     (ffn_w13, ffn_w2, i, ()))
            x, _ = _conv_layer(x, mods_x, norm1_g[i], norm2_g[i], w_in, sc_conv[j], w_out, ffn)
    zeros = jnp.zeros((1, 1, D), F32)
    return _norm(x, final_g, zeros, zeros)
```
